```python
import jax, jax.numpy as jnp
from jax import lax
import numpy as np

D_MODEL = 4096
BATCH = 16
SEQ = 2048
DEPTH = 1

MIX_W = D_MODEL
RWKV_W = MIX_W // 2
RWKV_HEAD = 64
N_RWKV_HEADS = RWKV_W // RWKV_HEAD
DECAY_LORA = 96
AAA_LORA = 96
GATE_LORA = 256
LRU_W = MIX_W - RWKV_W
LRU_BLOCK_W = 128
LRU_BLOCKS = LRU_W // LRU_BLOCK_W
CONV_WIDTH = 4
LRU_C = 8.0
D_FF = ((8 * D_MODEL + 3 * 256 - 1) // (3 * 256)) * 256
RWKV_COLS = 3 * RWKV_W + DECAY_LORA + AAA_LORA + GATE_LORA
IN_COLS = RWKV_COLS + 2 * LRU_W
NORM_EPS = 1e-6
GN_EPS = 64e-5

kernel_name = "hymba_rwkv7_rglru_swiglu"


def _rmsnorm(x, g):
    xf = x.astype(jnp.float32)
    return xf * lax.rsqrt(jnp.mean(xf * xf, axis=-1, keepdims=True) + NORM_EPS) * g.astype(jnp.float32)


def _token_shift(p):
    return jnp.pad(p, ((0, 0), (1, 0), (0, 0)))[:, :-1]


def _rwkv7_scan(r, w, k, v, a, b):
    Bsz, T, H, N = r.shape

    def step(S, inp):
        r_t, w_t, k_t, v_t, a_t, b_t = inp
        sa = jnp.einsum('bhvk,bhk->bhv', S, a_t)
        S = (S * w_t[:, :, None, :] + sa[..., None] * b_t[:, :, None, :]
             + v_t[..., None] * k_t[:, :, None, :])
        y_t = jnp.einsum('bhvk,bhk->bhv', S, r_t)
        return S, y_t

    seq = tuple(jnp.swapaxes(t, 0, 1) for t in (r, w, k, v, a, b))
    S0 = jnp.zeros((Bsz, H, N, N), jnp.float32)
    _, ys = lax.scan(step, S0, seq)
    return jnp.swapaxes(ys, 0, 1)


def _rwkv7_mixer(p, mu, w0, w2, a0, a2, g2, k_k, k_a, r_k, ln_g, ln_b):
    Bsz, T, _ = p.shape
    p = p + (_token_shift(p) - p) * mu
    o1, o2, o3 = RWKV_W, 2 * RWKV_W, 3 * RWKV_W
    r, k, v, wd, ad, gd = jnp.split(p, [o1, o2, o3, o3 + DECAY_LORA, o3 + DECAY_LORA + AAA_LORA], axis=-1)
    w = -jax.nn.softplus(-(w0 + jnp.tanh(wd) @ w2)) - 0.5
    decay = jnp.exp(-jnp.exp(w))
    a = jax.nn.sigmoid(a0 + ad @ a2)
    g = jax.nn.sigmoid(gd) @ g2

    def heads(t):
        return t.reshape(Bsz, T, N_RWKV_HEADS, RWKV_HEAD)

    kk = heads(k * k_k)
    kk = kk * lax.rsqrt(jnp.maximum(jnp.sum(kk * kk, axis=-1, keepdims=True), 1e-24))
    k = k * (1.0 + (a - 1.0) * k_a)
    r, k, v, decay, a = heads(r), heads(k), heads(v), heads(decay), heads(a)
    y = _rwkv7_scan(r, decay, k, v, -kk, kk * a)
    mean = jnp.mean(y, axis=-1, keepdims=True)
    var = jnp.mean(jnp.square(y - mean), axis=-1, keepdims=True)
    y = (y - mean) * lax.rsqrt(var + GN_EPS)
    y = y * ln_g.reshape(N_RWKV_HEADS, RWKV_HEAD) + ln_b.reshape(N_RWKV_HEADS, RWKV_HEAD)
    y = y + jnp.sum(r * k * r_k, axis=-1, keepdims=True) * v
    return y.reshape(Bsz, T, RWKV_W) * g


def _rglru_mixer(p, conv_w, conv_b, wr, br, wi, bi, lam, norm_g):
    Bsz, T, _ = p.shape
    xb, gate = jnp.split(p, [LRU_W], axis=-1)
    xpad = jnp.pad(xb, ((0, 0), (CONV_WIDTH - 1, 0), (0, 0)))
    xc = conv_b + xpad[:, 0:T] * conv_w[0]
    for j in range(1, CONV_WIDTH):
        xc = xc + xpad[:, j:j + T] * conv_w[j]
    xh = xc.reshape(Bsz, T, LRU_BLOCKS, LRU_BLOCK_W)
    rg = jax.nn.sigmoid(jnp.einsum('bthi,hij->bthj', xh, wr).reshape(Bsz, T, LRU_W) + br)
    ig = jax.nn.sigmoid(jnp.einsum('bthi,hij->bthj', xh, wi).reshape(Bsz, T, LRU_W) + bi)
    log_a = -LRU_C * rg * jax.nn.softplus(-lam)
    a = jnp.exp(log_a)
    mult = jnp.sqrt(-jnp.expm1(2.0 * log_a))
    first = (jnp.arange(T) == 0)[None, :, None]
    mult = jnp.where(first, 1.0, mult)
    bx = mult * ig * xc

    def combine(left, right):
        a1, b1 = left
        a2, b2 = right
        return a1 * a2, a2 * b1 + b2

    _, h = lax.associative_scan(combine, (a, bx), axis=1)
    y = h * jax.nn.gelu(gate)
    return _rmsnorm(y, norm_g)


def _fwd_setup_inputs(seed: int = 0) -> dict:
    key = jax.random.key(seed)
    ks = jax.random.split(key, 32)
    f32 = jnp.float32
    L = DEPTH

    def nrm(k, shape, scale):
        return jax.random.normal(k, shape, f32) * scale

    u = jax.random.uniform(ks[20], (L, LRU_W), f32, 0.9, 0.999)
    a_base = u ** (1.0 / LRU_C)
    lru_lambda = jnp.log(a_base) - jnp.log1p(-a_base)
    return {
        "x": nrm(ks[0], (BATCH, SEQ, D_MODEL), 1.0),
        "norm_mix_g": 1.0 + nrm(ks[1], (L, D_MODEL), 0.02),
        "w_in": nrm(ks[2], (L, D_MODEL, IN_COLS), D_MODEL ** -0.5),
        "mu_shift": jax.random.uniform(ks[3], (L, RWKV_COLS), f32),
        "rwkv_w0": jax.random.uniform(ks[4], (L, RWKV_W), f32, -6.0, 0.0),
        "rwkv_w2": nrm(ks[5], (L, DECAY_LORA, RWKV_W), 0.1 * DECAY_LORA ** -0.5),
        "rwkv_a0": nrm(ks[6], (L, RWKV_W), 0.1),
        "rwkv_a2": nrm(ks[7], (L, AAA_LORA, RWKV_W), AAA_LORA ** -0.5),
        "rwkv_g2": nrm(ks[8], (L, GATE_LORA, RWKV_W), GATE_LORA ** -0.5),
        "rwkv_k_k": 0.85 + nrm(ks[9], (L, RWKV_W), 0.02),
        "rwkv_k_a": 1.0 + nrm(ks[10], (L, RWKV_W), 0.02),
        "rwkv_r_k": nrm(ks[11], (L, N_RWKV_HEADS, RWKV_HEAD), 0.1),
        "rwkv_ln_g": 1.0 + nrm(ks[12], (L, RWKV_W), 0.02),
        "rwkv_ln_b": nrm(ks[13], (L, RWKV_W), 0.01),
        "conv_w": nrm(ks[14], (L, CONV_WIDTH, LRU_W), CONV_WIDTH ** -0.5),
        "conv_b": nrm(ks[15], (L, LRU_W), 0.01),
        "lru_wr": nrm(ks[16], (L, LRU_BLOCKS, LRU_BLOCK_W, LRU_BLOCK_W), LRU_BLOCK_W ** -0.5),
        "lru_br": nrm(ks[17], (L, LRU_W), 0.01),
        "lru_wi": nrm(ks[18], (L, LRU_BLOCKS, LRU_BLOCK_W, LRU_BLOCK_W), LRU_BLOCK_W ** -0.5),
        "lru_bi": nrm(ks[19], (L, LRU_W), 0.01),
        "lru_lambda": lru_lambda,
        "lru_norm_g": 1.0 + nrm(ks[21], (L, LRU_W), 0.02),
        "w_out": nrm(ks[22], (L, MIX_W, D_MODEL), MIX_W ** -0.5),
        "norm_ffn_g": 1.0 + nrm(ks[23], (L, D_MODEL), 0.02),
        "ffn_w_gate": nrm(ks[24], (L, D_MODEL, D_FF), D_MODEL ** -0.5),
        "ffn_w_up": nrm(ks[25], (L, D_MODEL, D_FF), D_MODEL ** -0.5),
        "ffn_w_down": nrm(ks[26], (L, D_FF, D_MODEL), D_FF ** -0.5),
        "norm_final_g": 1.0 + nrm(ks[27], (D_MODEL,), 0.02),
    }


def _fwd_reference(x, norm_mix_g, w_in, mu_shift, rwkv_w0, rwkv_w2, rwkv_a0, rwkv_a2, rwkv_g2,
              rwkv_k_k, rwkv_k_a, rwkv_r_k, rwkv_ln_g, rwkv_ln_b, conv_w, conv_b,
              lru_wr, lru_br, lru_wi, lru_bi, lru_lambda, lru_norm_g, w_out,
              norm_ffn_g, ffn_w_gate, ffn_w_up, ffn_w_down, norm_final_g):
    h = x.astype(jnp.float32)
    for l in range(DEPTH):
        u = _rmsnorm(h, norm_mix_g[l])
        p = u @ w_in[l]
        y_a = _rwkv7_mixer(p[..., :RWKV_COLS], mu_shift[l], rwkv_w0[l], rwkv_w2[l], rwkv_a0[l],
                           rwkv_a2[l], rwkv_g2[l], rwkv_k_k[l], rwkv_k_a[l], rwkv_r_k[l],
                           rwkv_ln_g[l], rwkv_ln_b[l])
        y_b = _rglru_mixer(p[..., RWKV_COLS:], conv_w[l], conv_b[l], lru_wr[l], lru_br[l],
                           lru_wi[l], lru_bi[l], lru_lambda[l], lru_norm_g[l])
        h = h + jnp.concatenate([y_a, y_b], axis=-1) @ w_out[l]
        u = _rmsnorm(h, norm_ffn_g[l])
        h = h + (jax.nn.silu(u @ ffn_w_gate[l]) * (u @ ffn_w_up[l])) @ ffn_w_down[l]
    return _rmsnorm(h, norm_final_g).astype(x.dtype)


import jax as _jax
import jax.numpy as _jnp

TWIN_FORMAT = 'train_step'
FWD_PARAMS = ['x', 'norm_mix_g', 'w_in', 'mu_shift', 'rwkv_w0', 'rwkv_w2', 'rwkv_a0', 'rwkv_a2', 'rwkv_g2', 'rwkv_k_k', 'rwkv_k_a', 'rwkv_r_k', 'rwkv_ln_g', 'rwkv_ln_b', 'conv_w', 'conv_b', 'lru_wr', 'lru_br', 'lru_wi', 'lru_bi', 'lru_lambda', 'lru_norm_g', 'w_out', 'norm_ffn_g', 'ffn_w_gate', 'ffn_w_up', 'ffn_w_down', 'norm_final_g']
TWIN_WEIGHTS = ['norm_mix_g', 'w_in', 'mu_shift', 'rwkv_w0', 'rwkv_w2', 'rwkv_a0', 'rwkv_a2', 'rwkv_g2', 'rwkv_k_k', 'rwkv_k_a', 'rwkv_r_k', 'rwkv_ln_g', 'rwkv_ln_b', 'conv_w', 'conv_b', 'lru_wr', 'lru_br', 'lru_wi', 'lru_bi', 'lru_lambda', 'lru_norm_g', 'w_out', 'norm_ffn_g', 'ffn_w_gate', 'ffn_w_up', 'ffn_w_down', 'norm_final_g']
TWIN_DIFF_INPUT = 'x'
TWIN_INPUTS = ['x', 'norm_mix_g', 'w_in', 'mu_shift', 'rwkv_w0', 'rwkv_w2', 'rwkv_a0', 'rwkv_a2', 'rwkv_g2', 'rwkv_k_k', 'rwkv_k_a', 'rwkv_r_k', 'rwkv_ln_g', 'rwkv_ln_b', 'conv_w', 'conv_b', 'lru_wr', 'lru_br', 'lru_wi', 'lru_bi', 'lru_lambda', 'lru_norm_g', 'w_out', 'norm_ffn_g', 'ffn_w_gate', 'ffn_w_up', 'ffn_w_down', 'norm_final_g', 'loss_target', 'm_norm_mix_g', 'm_w_in', 'm_mu_shift', 'm_rwkv_w0', 'm_rwkv_w2', 'm_rwkv_a0', 'm_rwkv_a2', 'm_rwkv_g2', 'm_rwkv_k_k', 'm_rwkv_k_a', 'm_rwkv_r_k', 'm_rwkv_ln_g', 'm_rwkv_ln_b', 'm_conv_w', 'm_conv_b', 'm_lru_wr', 'm_lru_br', 'm_lru_wi', 'm_lru_bi', 'm_lru_lambda', 'm_lru_norm_g', 'm_w_out', 'm_norm_ffn_g', 'm_ffn_w_gate', 'm_ffn_w_up', 'm_ffn_w_down', 'm_norm_final_g', 'v_norm_mix_g', 'v_w_in', 'v_mu_shift', 'v_rwkv_w0', 'v_rwkv_w2', 'v_rwkv_a0', 'v_rwkv_a2', 'v_rwkv_g2', 'v_rwkv_k_k', 'v_rwkv_k_a', 'v_rwkv_r_k', 'v_rwkv_ln_g', 'v_rwkv_ln_b', 'v_conv_w', 'v_conv_b', 'v_lru_wr', 'v_lru_br', 'v_lru_wi', 'v_lru_bi', 'v_lru_lambda', 'v_lru_norm_g', 'v_w_out', 'v_norm_ffn_g', 'v_ffn_w_gate', 'v_ffn_w_up', 'v_ffn_w_down', 'v_norm_final_g']
TWIN_OUTPUTS = ['loss', 'grad_x', 'grad_norm_mix_g', 'grad_w_in', 'grad_mu_shift', 'grad_rwkv_w0', 'grad_rwkv_w2', 'grad_rwkv_a0', 'grad_rwkv_a2', 'grad_rwkv_g2', 'grad_rwkv_k_k', 'grad_rwkv_k_a', 'grad_rwkv_r_k', 'grad_rwkv_ln_g', 'grad_rwkv_ln_b', 'grad_conv_w', 'grad_conv_b', 'grad_lru_wr', 'grad_lru_br', 'grad_lru_wi', 'grad_lru_bi', 'grad_lru_lambda', 'grad_lru_norm_g', 'grad_w_out', 'grad_norm_ffn_g', 'grad_ffn_w_gate', 'grad_ffn_w_up', 'grad_ffn_w_down', 'grad_norm_final_g', 'delta_norm_mix_g', 'delta_w_in', 'delta_mu_shift', 'delta_rwkv_w0', 'delta_rwkv_w2', 'delta_rwkv_a0', 'delta_rwkv_a2', 'delta_rwkv_g2', 'delta_rwkv_k_k', 'delta_rwkv_k_a', 'delta_rwkv_r_k', 'delta_rwkv_ln_g', 'delta_rwkv_ln_b', 'delta_conv_w', 'delta_conv_b', 'delta_lru_wr', 'delta_lru_br', 'delta_lru_wi', 'delta_lru_bi', 'delta_lru_lambda', 'delta_lru_norm_g', 'delta_w_out', 'delta_norm_ffn_g', 'delta_ffn_w_gate', 'delta_ffn_w_up', 'delta_ffn_w_down', 'delta_norm_final_g', 'new_m_norm_mix_g', 'new_m_w_in', 'new_m_mu_shift', 'new_m_rwkv_w0', 'new_m_rwkv_w2', 'new_m_rwkv_a0', 'new_m_rwkv_a2', 'new_m_rwkv_g2', 'new_m_rwkv_k_k', 'new_m_rwkv_k_a', 'new_m_rwkv_r_k', 'new_m_rwkv_ln_g', 'new_m_rwkv_ln_b', 'new_m_conv_w', 'new_m_conv_b', 'new_m_lru_wr', 'new_m_lru_br', 'new_m_lru_wi', 'new_m_lru_bi', 'new_m_lru_lambda', 'new_m_lru_norm_g', 'new_m_w_out', 'new_m_norm_ffn_g', 'new_m_ffn_w_gate', 'new_m_ffn_w_up', 'new_m_ffn_w_down', 'new_m_norm_final_g', 'new_v_norm_mix_g', 'new_v_w_in', 'new_v_mu_shift', 'new_v_rwkv_w0', 'new_v_rwkv_w2', 'new_v_rwkv_a0', 'new_v_rwkv_a2', 'new_v_rwkv_g2', 'new_v_rwkv_k_k', 'new_v_rwkv_k_a', 'new_v_rwkv_r_k', 'new_v_rwkv_ln_g', 'new_v_rwkv_ln_b', 'new_v_conv_w', 'new_v_conv_b', 'new_v_lru_wr', 'new_v_lru_br', 'new_v_lru_wi', 'new_v_lru_bi', 'new_v_lru_lambda', 'new_v_lru_norm_g', 'new_v_w_out', 'new_v_norm_ffn_g', 'new_v_ffn_w_gate', 'new_v_ffn_w_up', 'new_v_ffn_w_down', 'new_v_norm_final_g']
TWIN_LEAF_KINDS = {'loss': 'loss', 'grad_x': 'grad_x', 'grad_norm_mix_g': 'grad_w', 'grad_w_in': 'grad_w', 'grad_mu_shift': 'grad_w', 'grad_rwkv_w0': 'grad_w', 'grad_rwkv_w2': 'grad_w', 'grad_rwkv_a0': 'grad_w', 'grad_rwkv_a2': 'grad_w', 'grad_rwkv_g2': 'grad_w', 'grad_rwkv_k_k': 'grad_w', 'grad_rwkv_k_a': 'grad_w', 'grad_rwkv_r_k': 'grad_w', 'grad_rwkv_ln_g': 'grad_w', 'grad_rwkv_ln_b': 'grad_w', 'grad_conv_w': 'grad_w', 'grad_conv_b': 'grad_w', 'grad_lru_wr': 'grad_w', 'grad_lru_br': 'grad_w', 'grad_lru_wi': 'grad_w', 'grad_lru_bi': 'grad_w', 'grad_lru_lambda': 'grad_w', 'grad_lru_norm_g': 'grad_w', 'grad_w_out': 'grad_w', 'grad_norm_ffn_g': 'grad_w', 'grad_ffn_w_gate': 'grad_w', 'grad_ffn_w_up': 'grad_w', 'grad_ffn_w_down': 'grad_w', 'grad_norm_final_g': 'grad_w', 'delta_norm_mix_g': 'delta_w', 'delta_w_in': 'delta_w', 'delta_mu_shift': 'delta_w', 'delta_rwkv_w0': 'delta_w', 'delta_rwkv_w2': 'delta_w', 'delta_rwkv_a0': 'delta_w', 'delta_rwkv_a2': 'delta_w', 'delta_rwkv_g2': 'delta_w', 'delta_rwkv_k_k': 'delta_w', 'delta_rwkv_k_a': 'delta_w', 'delta_rwkv_r_k': 'delta_w', 'delta_rwkv_ln_g': 'delta_w', 'delta_rwkv_ln_b': 'delta_w', 'delta_conv_w': 'delta_w', 'delta_conv_b': 'delta_w', 'delta_lru_wr': 'delta_w', 'delta_lru_br': 'delta_w', 'delta_lru_wi': 'delta_w', 'delta_lru_bi': 'delta_w', 'delta_lru_lambda': 'delta_w', 'delta_lru_norm_g': 'delta_w', 'delta_w_out': 'delta_w', 'delta_norm_ffn_g': 'delta_w', 'delta_ffn_w_gate': 'delta_w', 'delta_ffn_w_up': 'delta_w', 'delta_ffn_w_down': 'delta_w', 'delta_norm_final_g': 'delta_w', 'new_m_norm_mix_g': 'new_m', 'new_m_w_in': 'new_m', 'new_m_mu_shift': 'new_m', 'new_m_rwkv_w0': 'new_m', 'new_m_rwkv_w2': 'new_m', 'new_m_rwkv_a0': 'new_m', 'new_m_rwkv_a2': 'new_m', 'new_m_rwkv_g2': 'new_m', 'new_m_rwkv_k_k': 'new_m', 'new_m_rwkv_k_a': 'new_m', 'new_m_rwkv_r_k': 'new_m', 'new_m_rwkv_ln_g': 'new_m', 'new_m_rwkv_ln_b': 'new_m', 'new_m_conv_w': 'new_m', 'new_m_conv_b': 'new_m', 'new_m_lru_wr': 'new_m', 'new_m_lru_br': 'new_m', 'new_m_lru_wi': 'new_m', 'new_m_lru_bi': 'new_m', 'new_m_lru_lambda': 'new_m', 'new_m_lru_norm_g': 'new_m', 'new_m_w_out': 'new_m', 'new_m_norm_ffn_g': 'new_m', 'new_m_ffn_w_gate': 'new_m', 'new_m_ffn_w_up': 'new_m', 'new_m_ffn_w_down': 'new_m', 'new_m_norm_final_g': 'new_m', 'new_v_norm_mix_g': 'new_v', 'new_v_w_in': 'new_v', 'new_v_mu_shift': 'new_v', 'new_v_rwkv_w0': 'new_v', 'new_v_rwkv_w2': 'new_v', 'new_v_rwkv_a0': 'new_v', 'new_v_rwkv_a2': 'new_v', 'new_v_rwkv_g2': 'new_v', 'new_v_rwkv_k_k': 'new_v', 'new_v_rwkv_k_a': 'new_v', 'new_v_rwkv_r_k': 'new_v', 'new_v_rwkv_ln_g': 'new_v', 'new_v_rwkv_ln_b': 'new_v', 'new_v_conv_w': 'new_v', 'new_v_conv_b': 'new_v', 'new_v_lru_wr': 'new_v', 'new_v_lru_br': 'new_v', 'new_v_lru_wi': 'new_v', 'new_v_lru_bi': 'new_v', 'new_v_lru_lambda': 'new_v', 'new_v_lru_norm_g': 'new_v', 'new_v_w_out': 'new_v', 'new_v_norm_ffn_g': 'new_v', 'new_v_ffn_w_gate': 'new_v', 'new_v_ffn_w_up': 'new_v', 'new_v_ffn_w_down': 'new_v', 'new_v_norm_final_g': 'new_v'}


def _forward(args):
    return _fwd_reference(*[args[k] for k in FWD_PARAMS])


def _output_shape():
    def fwd():
        inp = _fwd_setup_inputs(0)
        return _fwd_reference(*[inp[k] for k in FWD_PARAMS])
    out = _jax.eval_shape(fwd)
    return out.shape, out.dtype

N_MICROBATCH = 1
ADAM_LR = 0.001
ADAM_B1 = 0.9
ADAM_B2 = 0.999
ADAM_EPS = 1e-08
ADAM_WD = 0.01
ADAM_STEP = 10
PER_EXAMPLE_BATCH_AXIS = {'x': 0, 'loss_target': 0}
SHARED_INPUTS = []
_WEIGHT_DTYPES = {'norm_mix_g': _jnp.float32, 'w_in': _jnp.float32, 'mu_shift': _jnp.float32, 'rwkv_w0': _jnp.float32, 'rwkv_w2': _jnp.float32, 'rwkv_a0': _jnp.float32, 'rwkv_a2': _jnp.float32, 'rwkv_g2': _jnp.float32, 'rwkv_k_k': _jnp.float32, 'rwkv_k_a': _jnp.float32, 'rwkv_r_k': _jnp.float32, 'rwkv_ln_g': _jnp.float32, 'rwkv_ln_b': _jnp.float32, 'conv_w': _jnp.float32, 'conv_b': _jnp.float32, 'lru_wr': _jnp.float32, 'lru_br': _jnp.float32, 'lru_wi': _jnp.float32, 'lru_bi': _jnp.float32, 'lru_lambda': _jnp.float32, 'lru_norm_g': _jnp.float32, 'w_out': _jnp.float32, 'norm_ffn_g': _jnp.float32, 'ffn_w_gate': _jnp.float32, 'ffn_w_up': _jnp.float32, 'ffn_w_down': _jnp.float32, 'norm_final_g': _jnp.float32}
MOMENT_SCALE = {'norm_mix_g': 4.773614e-02, 'w_in': 2.945898e-02, 'mu_shift': 3.508751e-02, 'rwkv_w0': 7.774518e-03, 'rwkv_w2': 1.007480e-03, 'rwkv_a0': 7.863483e-03, 'rwkv_a2': 7.225716e-03, 'rwkv_g2': 2.085998e-02, 'rwkv_k_k': 2.428332e-02, 'rwkv_k_a': 2.214094e-02, 'rwkv_r_k': 4.874012e-02, 'rwkv_ln_g': 2.043859e-02, 'rwkv_ln_b': 2.213506e-02, 'conv_w': 3.883914e-02, 'conv_b': 4.228705e-01, 'lru_wr': 1.254613e-02, 'lru_br': 1.111871e-02, 'lru_wi': 2.240233e-02, 'lru_bi': 1.362196e-02, 'lru_lambda': 2.088930e-02, 'lru_norm_g': 3.674612e-02, 'w_out': 3.045655e-02, 'norm_ffn_g': 2.696520e-02, 'ffn_w_gate': 1.175447e-02, 'ffn_w_up': 1.138323e-02, 'ffn_w_down': 1.866648e-02, 'norm_final_g': 8.000119e+00}


def _to_microbatches(a, axis):
    t = _jnp.moveaxis(a, axis, 0)
    t = t.reshape((N_MICROBATCH, t.shape[0] // N_MICROBATCH) + t.shape[1:])
    return _jnp.moveaxis(t, 1, axis + 1)


def setup_inputs(seed: int = 0) -> dict:
    inp = _fwd_setup_inputs(seed)
    key = _jax.random.fold_in(_jax.random.key(seed), 7919)
    shape, _ = _output_shape()
    out = dict(inp)
    out["loss_target"] = _jax.random.normal(_jax.random.fold_in(key, 0), shape, _jnp.float32)
    for i, name in enumerate(TWIN_WEIGHTS):
        w = inp[name].astype(_jnp.float32)
        if MOMENT_SCALE is None:
            s = _jnp.sqrt(_jnp.mean(_jnp.square(w)) + 1e-30)
        else:
            s = MOMENT_SCALE[name]
        km, kv = _jax.random.split(_jax.random.fold_in(key, i + 1))
        out[name] = w
        out["m_" + name] = s * _jax.random.normal(km, w.shape, _jnp.float32)
        out["v_" + name] = (s * s) * _jax.random.uniform(kv, w.shape, _jnp.float32, 0.5, 1.5)
    if N_MICROBATCH > 1:
        for name, axis in PER_EXAMPLE_BATCH_AXIS.items():
            out[name] = _to_microbatches(out[name], axis)
    return {'x': out['x'], 'norm_mix_g': out['norm_mix_g'], 'w_in': out['w_in'], 'mu_shift': out['mu_shift'], 'rwkv_w0': out['rwkv_w0'], 'rwkv_w2': out['rwkv_w2'], 'rwkv_a0': out['rwkv_a0'], 'rwkv_a2': out['rwkv_a2'], 'rwkv_g2': out['rwkv_g2'], 'rwkv_k_k': out['rwkv_k_k'], 'rwkv_k_a': out['rwkv_k_a'], 'rwkv_r_k': out['rwkv_r_k'], 'rwkv_ln_g': out['rwkv_ln_g'], 'rwkv_ln_b': out['rwkv_ln_b'], 'conv_w': out['conv_w'], 'conv_b': out['conv_b'], 'lru_wr': out['lru_wr'], 'lru_br': out['lru_br'], 'lru_wi': out['lru_wi'], 'lru_bi': out['lru_bi'], 'lru_lambda': out['lru_lambda'], 'lru_norm_g': out['lru_norm_g'], 'w_out': out['w_out'], 'norm_ffn_g': out['norm_ffn_g'], 'ffn_w_gate': out['ffn_w_gate'], 'ffn_w_up': out['ffn_w_up'], 'ffn_w_down': out['ffn_w_down'], 'norm_final_g': out['norm_final_g'], 'loss_target': out['loss_target'], 'm_norm_mix_g': out['m_norm_mix_g'], 'm_w_in': out['m_w_in'], 'm_mu_shift': out['m_mu_shift'], 'm_rwkv_w0': out['m_rwkv_w0'], 'm_rwkv_w2': out['m_rwkv_w2'], 'm_rwkv_a0': out['m_rwkv_a0'], 'm_rwkv_a2': out['m_rwkv_a2'], 'm_rwkv_g2': out['m_rwkv_g2'], 'm_rwkv_k_k': out['m_rwkv_k_k'], 'm_rwkv_k_a': out['m_rwkv_k_a'], 'm_rwkv_r_k': out['m_rwkv_r_k'], 'm_rwkv_ln_g': out['m_rwkv_ln_g'], 'm_rwkv_ln_b': out['m_rwkv_ln_b'], 'm_conv_w': out['m_conv_w'], 'm_conv_b': out['m_conv_b'], 'm_lru_wr': out['m_lru_wr'], 'm_lru_br': out['m_lru_br'], 'm_lru_wi': out['m_lru_wi'], 'm_lru_bi': out['m_lru_bi'], 'm_lru_lambda': out['m_lru_lambda'], 'm_lru_norm_g': out['m_lru_norm_g'], 'm_w_out': out['m_w_out'], 'm_norm_ffn_g': out['m_norm_ffn_g'], 'm_ffn_w_gate': out['m_ffn_w_gate'], 'm_ffn_w_up': out['m_ffn_w_up'], 'm_ffn_w_down': out['m_ffn_w_down'], 'm_norm_final_g': out['m_norm_final_g'], 'v_norm_mix_g': out['v_norm_mix_g'], 'v_w_in': out['v_w_in'], 'v_mu_shift': out['v_mu_shift'], 'v_rwkv_w0': out['v_rwkv_w0'], 'v_rwkv_w2': out['v_rwkv_w2'], 'v_rwkv_a0': out['v_rwkv_a0'], 'v_rwkv_a2': out['v_rwkv_a2'], 'v_rwkv_g2': out['v_rwkv_g2'], 'v_rwkv_k_k': out['v_rwkv_k_k'], 'v_rwkv_k_a': out['v_rwkv_k_a'], 'v_rwkv_r_k': out['v_rwkv_r_k'], 'v_rwkv_ln_g': out['v_rwkv_ln_g'], 'v_rwkv_ln_b': out['v_rwkv_ln_b'], 'v_conv_w': out['v_conv_w'], 'v_conv_b': out['v_conv_b'], 'v_lru_wr': out['v_lru_wr'], 'v_lru_br': out['v_lru_br'], 'v_lru_wi': out['v_lru_wi'], 'v_lru_bi': out['v_lru_bi'], 'v_lru_lambda': out['v_lru_lambda'], 'v_lru_norm_g': out['v_lru_norm_g'], 'v_w_out': out['v_w_out'], 'v_norm_ffn_g': out['v_norm_ffn_g'], 'v_ffn_w_gate': out['v_ffn_w_gate'], 'v_ffn_w_up': out['v_ffn_w_up'], 'v_ffn_w_down': out['v_ffn_w_down'], 'v_norm_final_g': out['v_norm_final_g']}


def _loss(weights, diff, rest, loss_target):
    with _jax.named_scope("forward"):
        args = {**rest, TWIN_DIFF_INPUT: diff, **{k: w.astype(_WEIGHT_DTYPES[k]) for k, w in weights.items()}}
        y = _forward(args)
    with _jax.named_scope("loss_head"):
        err = _jnp.square(y.astype(_jnp.float32) - loss_target)
        return 0.5 * _jnp.sum(_jnp.mean(err, axis=-1)) if err.ndim else 0.5 * err


def _adamw(w, g, m, v):
    m = ADAM_B1 * m + (1.0 - ADAM_B1) * g
    v = ADAM_B2 * v + (1.0 - ADAM_B2) * _jnp.square(g)
    m_hat = m / (1.0 - ADAM_B1 ** ADAM_STEP)
    v_hat = v / (1.0 - ADAM_B2 ** ADAM_STEP)
    delta = -ADAM_LR * (m_hat / (_jnp.sqrt(v_hat) + ADAM_EPS) + ADAM_WD * w)
    return delta, m, v


def reference(x, norm_mix_g, w_in, mu_shift, rwkv_w0, rwkv_w2, rwkv_a0, rwkv_a2, rwkv_g2, rwkv_k_k, rwkv_k_a, rwkv_r_k, rwkv_ln_g, rwkv_ln_b, conv_w, conv_b, lru_wr, lru_br, lru_wi, lru_bi, lru_lambda, lru_norm_g, w_out, norm_ffn_g, ffn_w_gate, ffn_w_up, ffn_w_down, norm_final_g, loss_target, m_norm_mix_g, m_w_in, m_mu_shift, m_rwkv_w0, m_rwkv_w2, m_rwkv_a0, m_rwkv_a2, m_rwkv_g2, m_rwkv_k_k, m_rwkv_k_a, m_rwkv_r_k, m_rwkv_ln_g, m_rwkv_ln_b, m_conv_w, m_conv_b, m_lru_wr, m_lru_br, m_lru_wi, m_lru_bi, m_lru_lambda, m_lru_norm_g, m_w_out, m_norm_ffn_g, m_ffn_w_gate, m_ffn_w_up, m_ffn_w_down, m_norm_final_g, v_norm_mix_g, v_w_in, v_mu_shift, v_rwkv_w0, v_rwkv_w2, v_rwkv_a0, v_rwkv_a2, v_rwkv_g2, v_rwkv_k_k, v_rwkv_k_a, v_rwkv_r_k, v_rwkv_ln_g, v_rwkv_ln_b, v_conv_w, v_conv_b, v_lru_wr, v_lru_br, v_lru_wi, v_lru_bi, v_lru_lambda, v_lru_norm_g, v_w_out, v_norm_ffn_g, v_ffn_w_gate, v_ffn_w_up, v_ffn_w_down, v_norm_final_g):
    given = dict(x=x, norm_mix_g=norm_mix_g, w_in=w_in, mu_shift=mu_shift, rwkv_w0=rwkv_w0, rwkv_w2=rwkv_w2, rwkv_a0=rwkv_a0, rwkv_a2=rwkv_a2, rwkv_g2=rwkv_g2, rwkv_k_k=rwkv_k_k, rwkv_k_a=rwkv_k_a, rwkv_r_k=rwkv_r_k, rwkv_ln_g=rwkv_ln_g, rwkv_ln_b=rwkv_ln_b, conv_w=conv_w, conv_b=conv_b, lru_wr=lru_wr, lru_br=lru_br, lru_wi=lru_wi, lru_bi=lru_bi, lru_lambda=lru_lambda, lru_norm_g=lru_norm_g, w_out=w_out, norm_ffn_g=norm_ffn_g, ffn_w_gate=ffn_w_gate, ffn_w_up=ffn_w_up, ffn_w_down=ffn_w_down, norm_final_g=norm_final_g, loss_target=loss_target, m_norm_mix_g=m_norm_mix_g, m_w_in=m_w_in, m_mu_shift=m_mu_shift, m_rwkv_w0=m_rwkv_w0, m_rwkv_w2=m_rwkv_w2, m_rwkv_a0=m_rwkv_a0, m_rwkv_a2=m_rwkv_a2, m_rwkv_g2=m_rwkv_g2, m_rwkv_k_k=m_rwkv_k_k, m_rwkv_k_a=m_rwkv_k_a, m_rwkv_r_k=m_rwkv_r_k, m_rwkv_ln_g=m_rwkv_ln_g, m_rwkv_ln_b=m_rwkv_ln_b, m_conv_w=m_conv_w, m_conv_b=m_conv_b, m_lru_wr=m_lru_wr, m_lru_br=m_lru_br, m_lru_wi=m_lru_wi, m_lru_bi=m_lru_bi, m_lru_lambda=m_lru_lambda, m_lru_norm_g=m_lru_norm_g, m_w_out=m_w_out, m_norm_ffn_g=m_norm_ffn_g, m_ffn_w_gate=m_ffn_w_gate, m_ffn_w_up=m_ffn_w_up, m_ffn_w_down=m_ffn_w_down, m_norm_final_g=m_norm_final_g, v_norm_mix_g=v_norm_mix_g, v_w_in=v_w_in, v_mu_shift=v_mu_shift, v_rwkv_w0=v_rwkv_w0, v_rwkv_w2=v_rwkv_w2, v_rwkv_a0=v_rwkv_a0, v_rwkv_a2=v_rwkv_a2, v_rwkv_g2=v_rwkv_g2, v_rwkv_k_k=v_rwkv_k_k, v_rwkv_k_a=v_rwkv_k_a, v_rwkv_r_k=v_rwkv_r_k, v_rwkv_ln_g=v_rwkv_ln_g, v_rwkv_ln_b=v_rwkv_ln_b, v_conv_w=v_conv_w, v_conv_b=v_conv_b, v_lru_wr=v_lru_wr, v_lru_br=v_lru_br, v_lru_wi=v_lru_wi, v_lru_bi=v_lru_bi, v_lru_lambda=v_lru_lambda, v_lru_norm_g=v_lru_norm_g, v_w_out=v_w_out, v_norm_ffn_g=v_norm_ffn_g, v_ffn_w_gate=v_ffn_w_gate, v_ffn_w_up=v_ffn_w_up, v_ffn_w_down=v_ffn_w_down, v_norm_final_g=v_norm_final_g)
    weights = {n: given[n] for n in TWIN_WEIGHTS}
    shared = {n: given[n] for n in SHARED_INPUTS}
    per_example = {n: given[n] for n in ['x']}
    grad_fn = _jax.value_and_grad(_loss, argnums=(0, 1))

    def one_microbatch(ex, loss_target):
        ex = dict(ex)
        diff = ex.pop(TWIN_DIFF_INPUT)
        return grad_fn(weights, diff, {**shared, **ex}, loss_target)

    if N_MICROBATCH == 1:
        loss, (grad_w, grad_x) = one_microbatch(per_example, given["loss_target"])
    else:
        def body(carry, xs):
            loss_sum, grad_sum = carry
            l_k, (gw_k, gx_k) = one_microbatch(xs[0], xs[1])
            with _jax.named_scope("update"):
                return (loss_sum + l_k, _jax.tree.map(_jnp.add, grad_sum, gw_k)), gx_k

        init = (_jnp.zeros((), _jnp.float32), _jax.tree.map(_jnp.zeros_like, weights))
        (loss, grad_w), grad_x = _jax.lax.scan(body, init, (per_example, given["loss_target"]))
    with _jax.named_scope("update"):
        delta_w, new_m, new_v = {}, {}, {}
        for n in TWIN_WEIGHTS:
            delta_w[n], new_m[n], new_v[n] = _adamw(weights[n], grad_w[n], given["m_" + n], given["v_" + n])
    return (loss, grad_x, *[grad_w[n] for n in TWIN_WEIGHTS], *[delta_w[n] for n in TWIN_WEIGHTS],
            *[new_m[n] for n in TWIN_WEIGHTS], *[new_v[n] for n in TWIN_WEIGHTS])
```

```python
import jax
import jax.numpy as jnp
from jax import lax
from jax.experimental import pallas as pl
from jax.experimental.pallas import tpu as pltpu

F32 = jnp.float32
BF16 = jnp.bfloat16
MESH = pl.DeviceIdType.MESH
_call = pl.pallas_call

V7X_VMEM_LIMIT = 56 * 1024 * 1024
LANES = 128
SUBLANES = 8

RWKV_HEAD = 64
LRU_BLOCK_W = 128
CONV_WIDTH = 4
LRU_C = 8.0
NORM_EPS = 1e-6
GN_EPS = 64e-5
KK_EPS = 1e-24
SCAN_CHUNK = 16

ADAM_LR = 0.001
ADAM_B1 = 0.9
ADAM_B2 = 0.999
ADAM_EPS = 1e-08
ADAM_WD = 0.01
ADAM_STEP = 10
_BC1 = 1.0 - ADAM_B1 ** ADAM_STEP
_BC2 = 1.0 - ADAM_B2 ** ADAM_STEP

_HI = lax.Precision.HIGHEST


def _cp(*sem):
    return pltpu.CompilerParams(dimension_semantics=tuple(sem), vmem_limit_bytes=V7X_VMEM_LIMIT)


def _tile(n, cap, unit=LANES):
    if n <= cap:
        return n
    best = None
    d = unit
    while d <= cap:
        if n % d == 0:
            best = d
        d += unit
    return n if best is None else best


def _ceil_to(n, m):
    return -(-n // m) * m


def _sig(x):
    return 1.0 / (1.0 + jnp.exp(-x))


def _log1p(x):
    return jnp.where(x < 0.01, x * (1.0 - x * (0.5 - x * (1.0 / 3.0))), jnp.log(1.0 + x))


def _softplus(x):
    return jnp.maximum(x, 0.0) + _log1p(jnp.exp(-jnp.abs(x)))


def _neg_expm1(x):
    small = -x * (1.0 + x * (0.5 + x * (1.0 / 6.0)))
    return jnp.where(x > -0.01, small, 1.0 - jnp.exp(x))


_GELU_K = 0.7978845608028654
_GELU_C = 0.044715


def _gelu_parts(x):
    th = jnp.tanh(_GELU_K * (x + _GELU_C * x * x * x))
    return 0.5 * x * (1.0 + th), th


def _gelu_grad(x, th):
    return 0.5 * (1.0 + th) + 0.5 * x * (1.0 - th * th) * _GELU_K * (1.0 + 3.0 * _GELU_C * x * x)


def _shift_down(x, prev8, j):
    tb = x.shape[0]
    xr = pltpu.roll(x, j, 0)
    pr = pltpu.roll(prev8, j, 0)
    row = lax.broadcasted_iota(jnp.int32, prev8.shape, 0)
    first = jnp.where(row < j, pr, xr[0:SUBLANES])
    if tb == SUBLANES:
        return first
    return jnp.concatenate([first, xr[SUBLANES:]], axis=0)


def _shift_up(x, next8, j):
    tb = x.shape[0]
    xr = pltpu.roll(x, tb - j, 0)
    nr = pltpu.roll(next8, SUBLANES - j, 0)
    row = lax.broadcasted_iota(jnp.int32, next8.shape, 0)
    last = jnp.where(row >= SUBLANES - j, nr, xr[tb - SUBLANES:])
    if tb == SUBLANES:
        return last
    return jnp.concatenate([xr[:tb - SUBLANES], last], axis=0)


def _head_mats(width, heads_pad):
    e = (lax.broadcasted_iota(jnp.int32, (width, heads_pad), 0) // RWKV_HEAD
         == lax.broadcasted_iota(jnp.int32, (width, heads_pad), 1)).astype(F32)
    et = (lax.broadcasted_iota(jnp.int32, (heads_pad, width), 1) // RWKV_HEAD
          == lax.broadcasted_iota(jnp.int32, (heads_pad, width), 0)).astype(F32)
    return e, et


def _headsum(x, e, et):
    s = jnp.dot(x, e, preferred_element_type=F32, precision=_HI)
    return jnp.dot(s, et, preferred_element_type=F32, precision=_HI)


def _dot(a, b):
    return jnp.dot(a.astype(BF16), b.astype(BF16), preferred_element_type=F32)


def _dot_tn(a, b):
    return lax.dot_general(a.astype(BF16), b.astype(BF16), (((0,), (0,)), ((), ())), preferred_element_type=F32)


def _dot_nt(a, b):
    return lax.dot_general(a.astype(BF16), b.astype(BF16), (((1,), (1,)), ((), ())), preferred_element_type=F32)


def _mm(a, b, *, name, ta=False, tb=False, out_dtype=F32, res=None, n_outer=False, caps=(1024, 512, 4096)):
    m = a.shape[1] if ta else a.shape[0]
    kd = a.shape[0] if ta else a.shape[1]
    n = b.shape[0] if tb else b.shape[1]
    assert kd == (b.shape[1] if tb else b.shape[0])
    tm, tn, tk = _tile(m, caps[0]), _tile(n, caps[1]), _tile(kd, caps[2])
    gm, gn, gk = m // tm, n // tn, kd // tk
    dims = (((0 if ta else 1,), (1 if tb else 0,)), ((), ()))

    def ij(g0, g1):
        return (g1, g0) if n_outer else (g0, g1)

    def a_map(g0, g1, k):
        i, _ = ij(g0, g1)
        return (k, i) if ta else (i, k)

    def b_map(g0, g1, k):
        _, j = ij(g0, g1)
        return (j, k) if tb else (k, j)

    def o_map(g0, g1, k):
        return ij(g0, g1)

    has_res = res is not None

    def body(*refs):
        a_ref, b_ref = refs[0], refs[1]
        res_ref = refs[2] if has_res else None
        o_ref = refs[3] if has_res else refs[2]
        prod = lax.dot_general(a_ref[...], b_ref[...], dims, preferred_element_type=F32)

        def finish(acc):
            if has_res:
                acc = acc + res_ref[...]
            o_ref[...] = acc.astype(out_dtype)

        if gk == 1:
            finish(prod)
        else:
            acc_ref = refs[-1]
            k = pl.program_id(2)

            @pl.when(k == 0)
            def _():
                acc_ref[...] = prod

            @pl.when(k > 0)
            def _():
                acc_ref[...] += prod

            @pl.when(k == gk - 1)
            def _():
                finish(acc_ref[...])

    in_specs = [pl.BlockSpec((tk, tm) if ta else (tm, tk), a_map),
                pl.BlockSpec((tn, tk) if tb else (tk, tn), b_map)]
    args = [a, b]
    if has_res:
        in_specs.append(pl.BlockSpec((tm, tn), o_map))
        args.append(res)
    grid = (gn, gm, gk) if n_outer else (gm, gn, gk)
    return _call(
        body, name=name, grid=grid, in_specs=in_specs,
        out_specs=pl.BlockSpec((tm, tn), o_map),
        out_shape=jax.ShapeDtypeStruct((m, n), out_dtype),
        scratch_shapes=[pltpu.VMEM((tm, tn), F32)] if gk > 1 else [],
        compiler_params=_cp("parallel", "parallel", "arbitrary"),
    )(*args)


def _rmsnorm_fwd(x, g, name):
    n, d = x.shape
    tb = _tile(n, 256, SUBLANES)

    def body(x_ref, g_ref, u_ref):
        xv = x_ref[...]
        rstd = lax.rsqrt(jnp.mean(xv * xv, axis=-1, keepdims=True) + NORM_EPS)
        u_ref[...] = (xv * rstd * g_ref[...]).astype(BF16)

    row = pl.BlockSpec((tb, d), lambda i: (i, 0))
    vec = pl.BlockSpec((1, d), lambda i: (0, 0))
    return _call(body, name=name, grid=(n // tb,), in_specs=[row, vec], out_specs=row,
                 out_shape=jax.ShapeDtypeStruct((n, d), BF16), compiler_params=_cp("parallel"))(x, g)


def _rmsnorm_bwd(du, x, g, dres, name):
    n, d = x.shape
    tb = _tile(n, 256, SUBLANES)

    def body(du_ref, x_ref, g_ref, dres_ref, dx_ref, dxb_ref, dg_ref):
        xv = x_ref[...]
        rstd = lax.rsqrt(jnp.mean(xv * xv, axis=-1, keepdims=True) + NORM_EPS)
        xh = xv * rstd
        duv = du_ref[...]
        t = duv * g_ref[...]
        dx = dres_ref[...] + rstd * (t - xh * jnp.mean(t * xh, axis=-1, keepdims=True))
        dx_ref[...] = dx
        dxb_ref[...] = dx.astype(BF16)

        @pl.when(pl.program_id(0) == 0)
        def _():
            dg_ref[...] = jnp.zeros_like(dg_ref)

        dg_ref[...] += jnp.sum(duv * xh, axis=0, keepdims=True)

    row = pl.BlockSpec((tb, d), lambda i: (i, 0))
    vec = pl.BlockSpec((1, d), lambda i: (0, 0))
    return _call(body, name=name, grid=(n // tb,), in_specs=[row, row, vec, row], out_specs=[row, row, vec],
                 out_shape=[jax.ShapeDtypeStruct((n, d), F32), jax.ShapeDtypeStruct((n, d), BF16),
                            jax.ShapeDtypeStruct((1, d), F32)],
                 compiler_params=_cp("arbitrary"))(du, x, g, dres)


def _loss_head(h, g, target, name):
    n, d = h.shape
    tb = _tile(n, 256, SUBLANES)

    def body(h_ref, g_ref, t_ref, dh_ref, dhb_ref, dg_ref, loss_ref):
        hv = h_ref[...]
        gv = g_ref[...]
        rstd = lax.rsqrt(jnp.mean(hv * hv, axis=-1, keepdims=True) + NORM_EPS)
        hh = hv * rstd
        err = hh * gv - t_ref[...]
        dy = err * (1.0 / d)
        dhh = dy * gv
        dh = rstd * (dhh - hh * jnp.mean(dhh * hh, axis=-1, keepdims=True))
        dh_ref[...] = dh
        dhb_ref[...] = dh.astype(BF16)

        @pl.when(pl.program_id(0) == 0)
        def _():
            dg_ref[...] = jnp.zeros_like(dg_ref)
            loss_ref[...] = jnp.zeros_like(loss_ref)

        dg_ref[...] += jnp.sum(dy * hh, axis=0, keepdims=True)
        loss_ref[...] += jnp.sum(err * err) * (0.5 / d)

    row = pl.BlockSpec((tb, d), lambda i: (i, 0))
    vec = pl.BlockSpec((1, d), lambda i: (0, 0))
    lvec = pl.BlockSpec((1, LANES), lambda i: (0, 0))
    return _call(body, name=name, grid=(n // tb,), in_specs=[row, vec, row], out_specs=[row, row, vec, lvec],
                 out_shape=[jax.ShapeDtypeStruct((n, d), F32), jax.ShapeDtypeStruct((n, d), BF16),
                            jax.ShapeDtypeStruct((1, d), F32), jax.ShapeDtypeStruct((1, LANES), F32)],
                 compiler_params=_cp("arbitrary"))(h, g, target)


def _swiglu_fwd(gate, up, name):
    n, f = gate.shape
    tb, tc = _tile(n, 1024, SUBLANES), _tile(f, 256)

    def body(g_ref, u_ref, o_ref):
        gv = g_ref[...]
        o_ref[...] = (gv * _sig(gv) * u_ref[...]).astype(BF16)

    blk = pl.BlockSpec((tb, tc), lambda i, j: (i, j))
    return _call(body, name=name, grid=(n // tb, f // tc), in_specs=[blk, blk], out_specs=blk,
                 out_shape=jax.ShapeDtypeStruct((n, f), BF16), compiler_params=_cp("parallel", "parallel"))(gate, up)


def _swiglu_bwd(gate, up, dact, name):
    n, f = gate.shape
    tb, tc = _tile(n, 1024, SUBLANES), _tile(f, 256)

    def body(g_ref, u_ref, d_ref, dg_ref, du_ref):
        gv = g_ref[...]
        s = _sig(gv)
        dv = d_ref[...]
        dg_ref[...] = (dv * u_ref[...] * s * (1.0 + gv * (1.0 - s))).astype(BF16)
        du_ref[...] = (dv * gv * s).astype(BF16)

    blk = pl.BlockSpec((tb, tc), lambda i, j: (i, j))
    return _call(body, name=name, grid=(n // tb, f // tc), in_specs=[blk, blk, blk], out_specs=[blk, blk],
                 out_shape=[jax.ShapeDtypeStruct((n, f), BF16)] * 2,
                 compiler_params=_cp("parallel", "parallel"))(gate, up, dact)


def _prep_common(prkv_ref, prkvp_ref, plo_ref, plop_ref, mur_ref, mul_ref, w0_ref, a0_ref, kk_ref, ka_ref,
                 w2_ref, a2_ref, g2_ref, seq_start, w, dlp, alp):
    z8r = jnp.zeros_like(prkvp_ref[...])
    z8l = jnp.zeros_like(plop_ref[...])
    prev_r = jnp.where(seq_start, z8r, prkvp_ref[...])
    prev_l = jnp.where(seq_start, z8l, plop_ref[...])
    p_r = prkv_ref[...]
    p_l = plo_ref[...]
    dif_r = _shift_down(p_r, prev_r, 1) - p_r
    dif_l = _shift_down(p_l, prev_l, 1) - p_l
    q_r = p_r + dif_r * mur_ref[...]
    q_l = p_l + dif_l * mul_ref[...]
    r, k, v = q_r[:, 0:w], q_r[:, w:2 * w], q_r[:, 2 * w:3 * w]
    wd, ad, gd = q_l[:, 0:dlp], q_l[:, dlp:dlp + alp], q_l[:, dlp + alp:]
    tw = jnp.tanh(wd)
    zw = w0_ref[...] + _dot(tw, w2_ref[...])
    wlog = -_softplus(-zw) - 0.5
    ew = jnp.exp(wlog)
    dec = jnp.exp(-ew)
    za = a0_ref[...] + _dot(ad, a2_ref[...])
    av = _sig(za)
    sg = _sig(gd)
    g = _dot(sg, g2_ref[...])
    return dict(dif_r=dif_r, dif_l=dif_l, r=r, k=k, v=v, ad=ad, tw=tw, zw=zw, ew=ew, dec=dec, av=av, sg=sg, g=g)


def _rwkv_prep_specs(n, tb, w, lp, t_len):
    nb8 = tb // SUBLANES
    row3 = pl.BlockSpec((tb, 3 * w), lambda i: (i, 0))
    prev3 = pl.BlockSpec((SUBLANES, 3 * w), lambda i: (jnp.maximum(i * nb8 - 1, 0), 0))
    rowl = pl.BlockSpec((tb, lp), lambda i: (i, 0))
    prevl = pl.BlockSpec((SUBLANES, lp), lambda i: (jnp.maximum(i * nb8 - 1, 0), 0))
    return row3, prev3, rowl, prevl


def _rwkv_prep_fwd(p_rkv, p_lora, prm, t_len, name):
    n, w3 = p_rkv.shape
    w = w3 // 3
    lp = p_lora.shape[1]
    dlp, alp = prm["w2"].shape[0], prm["a2"].shape[0]
    glp = lp - dlp - alp
    hp = max(w // RWKV_HEAD, LANES)
    tb = _tile(min(n, t_len), 128, SUBLANES)
    bps = t_len // tb

    def body(prkv_ref, prkvp_ref, plo_ref, plop_ref, mur_ref, mul_ref, w0_ref, a0_ref, kk_ref, ka_ref,
             w2_ref, a2_ref, g2_ref, r_o, dec_o, k_o, v_o, na_o, nb_o, g_o):
        seq_start = (pl.program_id(0) % bps) == 0
        f = _prep_common(prkv_ref, prkvp_ref, plo_ref, plop_ref, mur_ref, mul_ref, w0_ref, a0_ref, kk_ref, ka_ref,
                         w2_ref, a2_ref, g2_ref, seq_start, w, dlp, alp)
        e, et = _head_mats(w, hp)
        kk0 = f["k"] * kk_ref[...]
        inv = lax.rsqrt(jnp.maximum(_headsum(kk0 * kk0, e, et), KK_EPS))
        kk = kk0 * inv
        r_o[...] = f["r"]
        dec_o[...] = f["dec"]
        k_o[...] = f["k"] * (1.0 + (f["av"] - 1.0) * ka_ref[...])
        v_o[...] = f["v"]
        na_o[...] = -kk
        nb_o[...] = kk * f["av"]
        g_o[...] = f["g"]

    row3, prev3, rowl, prevl = _rwkv_prep_specs(n, tb, w, lp, t_len)
    c0 = lambda i: (0, 0)
    vec3 = pl.BlockSpec((1, 3 * w), c0)
    vecl = pl.BlockSpec((1, lp), c0)
    vec = pl.BlockSpec((1, w), c0)
    out = pl.BlockSpec((tb, w), lambda i: (i, 0))
    return _call(
        body, name=name, grid=(n // tb,),
        in_specs=[row3, prev3, rowl, prevl, vec3, vecl, vec, vec, vec, vec,
                  pl.BlockSpec((dlp, w), c0), pl.BlockSpec((alp, w), c0), pl.BlockSpec((glp, w), c0)],
        out_specs=[out] * 7, out_shape=[jax.ShapeDtypeStruct((n, w), F32)] * 7,
        compiler_params=_cp("parallel"),
    )(p_rkv, p_rkv, p_lora, p_lora, prm["mu_rkv"], prm["mu_lora"], prm["w0"], prm["a0"], prm["k_k"], prm["k_a"],
      prm["w2"], prm["a2"], prm["g2"])


def _rwkv_prep_bwd(p_rkv, p_lora, prm, grads, t_len, name):
    n, w3 = p_rkv.shape
    w = w3 // 3
    lp = p_lora.shape[1]
    dlp, alp = prm["w2"].shape[0], prm["a2"].shape[0]
    glp = lp - dlp - alp
    hp = max(w // RWKV_HEAD, LANES)
    tb = _tile(min(n, t_len), 64, SUBLANES)
    bps = t_len // tb

    def body(prkv_ref, prkvp_ref, plo_ref, plop_ref, mur_ref, mul_ref, w0_ref, a0_ref, kk_ref, ka_ref,
             w2_ref, a2_ref, g2_ref,
             drs_ref, drp_ref, ddec_ref, dks_ref, dkp_ref, dvs_ref, dvp_ref, dna_ref, dnb_ref, dg_ref,
             dqr_o, dql_o, dmur_o, dmul_o, dw0_o, da0_o, dkk_o, dka_o, dw2_o, da2_o, dg2_o):
        seq_start = (pl.program_id(0) % bps) == 0
        f = _prep_common(prkv_ref, prkvp_ref, plo_ref, plop_ref, mur_ref, mul_ref, w0_ref, a0_ref, kk_ref, ka_ref,
                         w2_ref, a2_ref, g2_ref, seq_start, w, dlp, alp)
        e, et = _head_mats(w, hp)
        k, av = f["k"], f["av"]
        k_k, k_a = kk_ref[...], ka_ref[...]
        kk0 = k * k_k
        n2 = _headsum(kk0 * kk0, e, et)
        inv = lax.rsqrt(jnp.maximum(n2, KK_EPS))
        kk = kk0 * inv
        dk2 = dks_ref[...] + dkp_ref[...]
        dnb = dnb_ref[...]
        dkk = dnb * av - dna_ref[...]
        dav = dnb * kk + dk2 * k * k_a
        dk = dk2 * (1.0 + (av - 1.0) * k_a)
        dka = dk2 * k * (av - 1.0)
        proj = jnp.where(n2 > KK_EPS, _headsum(dkk * kk, e, et), 0.0)
        dkk0 = inv * (dkk - kk * proj)
        dk = dk + dkk0 * k_k
        dkkp = dkk0 * k
        dgv = dg_ref[...]
        sg = f["sg"]
        dgd = _dot_nt(dgv, g2_ref[...]) * sg * (1.0 - sg)
        dza = dav * av * (1.0 - av)
        dad = _dot_nt(dza, a2_ref[...])
        dzw = ddec_ref[...] * f["dec"] * (-f["ew"]) * _sig(-f["zw"])
        tw = f["tw"]
        dwd = _dot_nt(dzw, w2_ref[...]) * (1.0 - tw * tw)
        dq_r = jnp.concatenate([drs_ref[...] + drp_ref[...], dk, dvs_ref[...] + dvp_ref[...]], axis=1)
        dq_l = jnp.concatenate([dwd, dad, dgd], axis=1)
        dqr_o[...] = dq_r
        dql_o[...] = dq_l

        @pl.when(pl.program_id(0) == 0)
        def _():
            for o in (dmur_o, dmul_o, dw0_o, da0_o, dkk_o, dka_o, dw2_o, da2_o, dg2_o):
                o[...] = jnp.zeros_like(o)

        def rsum(x):
            return jnp.sum(x, axis=0, keepdims=True)

        dmur_o[...] += rsum(dq_r * f["dif_r"])
        dmul_o[...] += rsum(dq_l * f["dif_l"])
        dw0_o[...] += rsum(dzw)
        da0_o[...] += rsum(dza)
        dkk_o[...] += rsum(dkkp)
        dka_o[...] += rsum(dka)
        dw2_o[...] += _dot_tn(tw, dzw)
        da2_o[...] += _dot_tn(f["ad"], dza)
        dg2_o[...] += _dot_tn(sg, dgv)

    row3, prev3, rowl, prevl = _rwkv_prep_specs(n, tb, w, lp, t_len)
    c0 = lambda i: (0, 0)
    vec3 = pl.BlockSpec((1, 3 * w), c0)
    vecl = pl.BlockSpec((1, lp), c0)
    vec = pl.BlockSpec((1, w), c0)
    blk = pl.BlockSpec((tb, w), lambda i: (i, 0))
    m2, ma, mg = pl.BlockSpec((dlp, w), c0), pl.BlockSpec((alp, w), c0), pl.BlockSpec((glp, w), c0)
    sds = jax.ShapeDtypeStruct
    return _call(
        body, name=name, grid=(n // tb,),
        in_specs=[row3, prev3, rowl, prevl, vec3, vecl, vec, vec, vec, vec, m2, ma, mg] + [blk] * 10,
        out_specs=[row3, rowl, vec3, vecl, vec, vec, vec, vec, m2, ma, mg],
        out_shape=[sds((n, 3 * w), F32), sds((n, lp), F32), sds((1, 3 * w), F32), sds((1, lp), F32),
                   sds((1, w), F32), sds((1, w), F32), sds((1, w), F32), sds((1, w), F32),
                   sds((dlp, w), F32), sds((alp, w), F32), sds((glp, w), F32)],
        compiler_params=_cp("arbitrary"),
    )(p_rkv, p_rkv, p_lora, p_lora, prm["mu_rkv"], prm["mu_lora"], prm["w0"], prm["a0"], prm["k_k"], prm["k_a"],
      prm["w2"], prm["a2"], prm["g2"], *grads)


def _shift_combine(dq, mu, t_len, name):
    n, c = dq.shape
    tb = _tile(min(n, t_len), 256, SUBLANES)
    bps = t_len // tb
    nb8 = tb // SUBLANES
    last8 = n // SUBLANES - 1

    def body(x_ref, nx_ref, mu_ref, o_ref):
        seq_end = (pl.program_id(0) % bps) == bps - 1
        nxt = jnp.where(seq_end, jnp.zeros_like(nx_ref[...]), nx_ref[...])
        x = x_ref[...]
        muv = mu_ref[...]
        o_ref[...] = ((1.0 - muv) * x + muv * _shift_up(x, nxt, 1)).astype(BF16)

    row = pl.BlockSpec((tb, c), lambda i: (i, 0))
    nxt = pl.BlockSpec((SUBLANES, c), lambda i: (jnp.minimum((i + 1) * nb8, last8), 0))
    vec = pl.BlockSpec((1, c), lambda i: (0, 0))
    return _call(body, name=name, grid=(n // tb,), in_specs=[row, nxt, vec], out_specs=row,
                 out_shape=jax.ShapeDtypeStruct((n, c), BF16), compiler_params=_cp("parallel"))(dq, dq, mu)


def _scan_step(s_i, a_t, w_t, b_t, k_t, v_i):
    sa = jnp.sum(s_i * a_t, axis=0, keepdims=True)
    return s_i * w_t + sa * b_t + v_i * k_t, sa


def _rwkv_scan_fwd(r, w, k, a, b, v, name):
    t_len, kd, ln = r.shape
    vh = v.shape[1]
    tc = SCAN_CHUNK
    nc = t_len // tc

    def body(r_ref, w_ref, k_ref, a_ref, b_ref, v_ref, y_ref, ck_ref, s_ref):
        @pl.when(pl.program_id(0) == 0)
        def _():
            s_ref[...] = jnp.zeros_like(s_ref)

        ck_ref[0] = s_ref[...]

        def step(t, carry):
            a_t, w_t, b_t, k_t, r_t = a_ref[t], w_ref[t], b_ref[t], k_ref[t], r_ref[t]
            for i in range(vh):
                s_new, _ = _scan_step(s_ref[i], a_t, w_t, b_t, k_t, v_ref[t, pl.ds(i, 1), :])
                s_ref[i] = s_new
                y_ref[t, pl.ds(i, 1), :] = jnp.sum(s_new * r_t, axis=0, keepdims=True)
            return carry

        lax.fori_loop(0, tc, step, 0)

    kblk = pl.BlockSpec((tc, kd, ln), lambda c: (c, 0, 0))
    vblk = pl.BlockSpec((tc, vh, ln), lambda c: (c, 0, 0))
    return _call(
        body, name=name, grid=(nc,), in_specs=[kblk] * 5 + [vblk],
        out_specs=[vblk, pl.BlockSpec((1, vh, kd, ln), lambda c: (c, 0, 0, 0))],
        out_shape=[jax.ShapeDtypeStruct((t_len, vh, ln), F32), jax.ShapeDtypeStruct((nc, vh, kd, ln), F32)],
        scratch_shapes=[pltpu.VMEM((vh, kd, ln), F32)],
        compiler_params=_cp("arbitrary"),
    )(r, w, k, a, b, v)


def _rwkv_scan_bwd(r, w, k, a, b, v, dy, ckpt, name):
    t_len, kd, ln = r.shape
    vh = v.shape[1]
    tc = SCAN_CHUNK
    nc = t_len // tc
    half = ln // 2

    def body(r_ref, w_ref, k_ref, a_ref, b_ref, v_ref, dy_ref, ck_ref,
             dr_o, dw_o, dk_o, da_o, db_o, dv_o, sbuf, ds_ref, sa_buf):
        @pl.when(pl.program_id(0) == 0)
        def _():
            ds_ref[...] = jnp.zeros_like(ds_ref)

        sbuf[0] = ck_ref[0]

        def fwd(t, carry):
            a_t, w_t, b_t, k_t = a_ref[t], w_ref[t], b_ref[t], k_ref[t]
            for i in range(vh):
                s_new, sa = _scan_step(sbuf[t, i], a_t, w_t, b_t, k_t, v_ref[t, pl.ds(i, 1), :])
                sbuf[t + 1, i] = s_new
                sa_buf[t, pl.ds(i, 1), :] = sa
            return carry

        lax.fori_loop(0, tc, fwd, 0)

        def bwd(tt, carry):
            t = tc - 1 - tt
            a_t, w_t, b_t, k_t, r_t = a_ref[t], w_ref[t], b_ref[t], k_ref[t], r_ref[t]
            z = jnp.zeros((kd, ln), F32)
            dr, dw, dk, da, db = z, z, z, z, z
            for i in range(vh):
                dy_i = dy_ref[t, pl.ds(i, 1), :]
                s_t = sbuf[t + 1, i]
                s_p = sbuf[t, i]
                d = ds_ref[i] + dy_i * r_t
                dr = dr + s_t * dy_i
                dv_o[t, pl.ds(i, 1), :] = jnp.sum(d * k_t, axis=0, keepdims=True)
                dk = dk + d * v_ref[t, pl.ds(i, 1), :]
                dsa = jnp.sum(d * b_t, axis=0, keepdims=True)
                db = db + d * sa_buf[t, pl.ds(i, 1), :]
                dw = dw + d * s_p
                da = da + s_p * dsa
                ds_ref[i] = d * w_t + dsa * a_t

            def both(x):
                return x + pltpu.roll(x, half, 1)

            dr_o[t] = both(dr)
            dw_o[t] = both(dw)
            dk_o[t] = both(dk)
            da_o[t] = both(da)
            db_o[t] = both(db)
            return carry

        lax.fori_loop(0, tc, bwd, 0)

    kblk = pl.BlockSpec((tc, kd, ln), lambda c: (nc - 1 - c, 0, 0))
    vblk = pl.BlockSpec((tc, vh, ln), lambda c: (nc - 1 - c, 0, 0))
    ksd = jax.ShapeDtypeStruct((t_len, kd, ln), F32)
    return _call(
        body, name=name, grid=(nc,),
        in_specs=[kblk] * 5 + [vblk, vblk, pl.BlockSpec((1, vh, kd, ln), lambda c: (nc - 1 - c, 0, 0, 0))],
        out_specs=[kblk] * 5 + [vblk],
        out_shape=[ksd] * 5 + [jax.ShapeDtypeStruct((t_len, vh, ln), F32)],
        scratch_shapes=[pltpu.VMEM((tc + 1, vh, kd, ln), F32), pltpu.VMEM((vh, kd, ln), F32),
                        pltpu.VMEM((tc, vh, ln), F32)],
        compiler_params=_cp("arbitrary"),
    )(r, w, k, a, b, v, dy, ckpt)


def _post_common(y_ref, r_ref, k_ref, v_ref, lng_ref, lnb_ref, rk_ref, e, et):
    y = y_ref[...]
    inv_n = 1.0 / RWKV_HEAD
    mean = _headsum(y, e, et) * inv_n
    yc = y - mean
    var = _headsum(yc * yc, e, et) * inv_n
    rstd = lax.rsqrt(var + GN_EPS)
    yh = yc * rstd
    yn = yh * lng_ref[...] + lnb_ref[...]
    bonus = _headsum(r_ref[...] * k_ref[...] * rk_ref[...], e, et)
    return yh, rstd, yn, bonus


def _rwkv_post_fwd(y, r, k, v, g, ln_g, ln_b, r_k, name):
    n, w = y.shape
    hp = max(w // RWKV_HEAD, LANES)
    tb = _tile(n, 256, SUBLANES)

    def body(y_ref, r_ref, k_ref, v_ref, g_ref, lng_ref, lnb_ref, rk_ref, o_ref):
        e, et = _head_mats(w, hp)
        _, _, yn, bonus = _post_common(y_ref, r_ref, k_ref, v_ref, lng_ref, lnb_ref, rk_ref, e, et)
        o_ref[...] = ((yn + bonus * v_ref[...]) * g_ref[...]).astype(BF16)

    blk = pl.BlockSpec((tb, w), lambda i: (i, 0))
    vec = pl.BlockSpec((1, w), lambda i: (0, 0))
    return _call(body, name=name, grid=(n // tb,), in_specs=[blk] * 5 + [vec] * 3, out_specs=blk,
                 out_shape=jax.ShapeDtypeStruct((n, w), BF16),
                 compiler_params=_cp("parallel"))(y, r, k, v, g, ln_g, ln_b, r_k)


def _rwkv_post_bwd(y, r, k, v, g, ln_g, ln_b, r_k, do_cat, name):
    n, w = y.shape
    hp = max(w // RWKV_HEAD, LANES)
    tb = _tile(n, 128, SUBLANES)

    def body(y_ref, r_ref, k_ref, v_ref, g_ref, lng_ref, lnb_ref, rk_ref, do_ref,
             dy_o, dr_o, dk_o, dv_o, dg_o, dlng_o, dlnb_o, drk_o):
        e, et = _head_mats(w, hp)
        yh, rstd, yn, bonus = _post_common(y_ref, r_ref, k_ref, v_ref, lng_ref, lnb_ref, rk_ref, e, et)
        do = do_ref[...]
        vv, rv, kv, rk = v_ref[...], r_ref[...], k_ref[...], rk_ref[...]
        dg_o[...] = do * (yn + bonus * vv)
        dz = do * g_ref[...]
        dbonus = _headsum(dz * vv, e, et)
        dv_o[...] = dz * bonus
        dr_o[...] = dbonus * kv * rk
        dk_o[...] = dbonus * rv * rk
        dyh = dz * lng_ref[...]
        inv_n = 1.0 / RWKV_HEAD
        dy_o[...] = rstd * (dyh - _headsum(dyh, e, et) * inv_n - yh * (_headsum(dyh * yh, e, et) * inv_n))

        @pl.when(pl.program_id(0) == 0)
        def _():
            for o in (dlng_o, dlnb_o, drk_o):
                o[...] = jnp.zeros_like(o)

        dlng_o[...] += jnp.sum(dz * yh, axis=0, keepdims=True)
        dlnb_o[...] += jnp.sum(dz, axis=0, keepdims=True)
        drk_o[...] += jnp.sum(dbonus * rv * kv, axis=0, keepdims=True)

    blk = pl.BlockSpec((tb, w), lambda i: (i, 0))
    vec = pl.BlockSpec((1, w), lambda i: (0, 0))
    sds = jax.ShapeDtypeStruct
    return _call(body, name=name, grid=(n // tb,), in_specs=[blk] * 5 + [vec] * 3 + [blk],
                 out_specs=[blk] * 5 + [vec] * 3,
                 out_shape=[sds((n, w), F32)] * 5 + [sds((1, w), F32)] * 3,
                 compiler_params=_cp("arbitrary"))(y, r, k, v, g, ln_g, ln_b, r_k, do_cat)


def _lru_gates(xb, prev8, gate, cw_ref, cb_ref, wr_ref, br_ref, wi_ref, bi_ref, lam_ref, is_t0):
    c = xb.shape[1]
    nblk = c // LRU_BLOCK_W
    xs = [xb] + [_shift_down(xb, prev8, j) for j in range(1, CONV_WIDTH)]
    xc = cb_ref[...]
    for j in range(CONV_WIDTH):
        xc = xc + xs[CONV_WIDTH - 1 - j] * cw_ref[pl.ds(j, 1), :]
    xcb = xc.astype(BF16)

    def blockmm(w_ref):
        return jnp.concatenate(
            [jnp.dot(xcb[:, h * LRU_BLOCK_W:(h + 1) * LRU_BLOCK_W], w_ref[h], preferred_element_type=F32)
             for h in range(nblk)], axis=1)

    rg = _sig(blockmm(wr_ref) + br_ref[...])
    ig = _sig(blockmm(wi_ref) + bi_ref[...])
    sp = _softplus(-lam_ref[...])
    la = -LRU_C * rg * sp
    av = jnp.exp(la)
    mult = jnp.where(is_t0, 1.0, jnp.sqrt(_neg_expm1(2.0 * la)))
    ge, th = _gelu_parts(gate)
    return dict(xs=xs, xc=xc, xcb=xcb, rg=rg, ig=ig, sp=sp, a=av, mult=mult, ge=ge, th=th)


def _lru_specs(tb, c, nb, rev):
    nb8 = tb // SUBLANES

    def blk_i(i):
        return nb - 1 - i if rev else i

    xb = pl.BlockSpec((tb, c), lambda b, i: (b * nb + blk_i(i), 0))
    gate = pl.BlockSpec((tb, c), lambda b, i: (b * nb + blk_i(i), 1))
    prev = pl.BlockSpec((SUBLANES, c), lambda b, i: (jnp.maximum((b * nb + blk_i(i)) * nb8 - 1, 0), 0))
    return xb, gate, prev


def _lru_fwd(p_lru, prm, t_len, name):
    n, c2 = p_lru.shape
    c = c2 // 2
    nblk = c // LRU_BLOCK_W
    tb = _tile(t_len, 256, SUBLANES)
    nb = t_len // tb
    bsz = n // t_len

    def body(xb_ref, gate_ref, prev_ref, cw_ref, cb_ref, wr_ref, br_ref, wi_ref, bi_ref, lam_ref, ng_ref,
             y_o, h_o, carry):
        i = pl.program_id(1)
        prev8 = jnp.where(i == 0, jnp.zeros_like(prev_ref[...]), prev_ref[...])
        row = lax.broadcasted_iota(jnp.int32, (tb, c), 0)
        f = _lru_gates(xb_ref[...], prev8, gate_ref[...], cw_ref, cb_ref, wr_ref, br_ref, wi_ref, bi_ref, lam_ref,
                       jnp.logical_and(i == 0, row == 0))
        acc_a = f["a"]
        acc_b = f["mult"] * f["ig"] * f["xc"]
        s = 1
        while s < tb:
            keep = row >= s
            a_sh = jnp.where(keep, pltpu.roll(acc_a, s, 0), 1.0)
            b_sh = jnp.where(keep, pltpu.roll(acc_b, s, 0), 0.0)
            acc_b = acc_a * b_sh + acc_b
            acc_a = acc_a * a_sh
            s *= 2

        @pl.when(i == 0)
        def _():
            carry[...] = jnp.zeros_like(carry)

        h = acc_b + acc_a * carry[0:1, :]
        carry[0:1, :] = h[tb - 1:tb, :]
        h_o[...] = h
        y = h * f["ge"]
        rstd = lax.rsqrt(jnp.mean(y * y, axis=-1, keepdims=True) + NORM_EPS)
        y_o[...] = (y * rstd * ng_ref[...]).astype(BF16)

    xb_s, gate_s, prev_s = _lru_specs(tb, c, nb, False)
    c0 = lambda b, i: (0, 0)
    vec = pl.BlockSpec((1, c), c0)
    wsp = pl.BlockSpec((nblk, LRU_BLOCK_W, LRU_BLOCK_W), lambda b, i: (0, 0, 0))
    out = pl.BlockSpec((tb, c), lambda b, i: (b * nb + i, 0))
    return _call(
        body, name=name, grid=(bsz, nb),
        in_specs=[xb_s, gate_s, prev_s, pl.BlockSpec((CONV_WIDTH, c), c0), vec, wsp, vec, wsp, vec, vec, vec],
        out_specs=[out, out],
        out_shape=[jax.ShapeDtypeStruct((n, c), BF16), jax.ShapeDtypeStruct((n, c), F32)],
        scratch_shapes=[pltpu.VMEM((SUBLANES, c), F32)],
        compiler_params=_cp("arbitrary", "arbitrary"),
    )(p_lru, p_lru, p_lru, prm["conv_w"], prm["conv_b"], prm["wr"], prm["br"], prm["wi"], prm["bi"],
      prm["lam"], prm["norm_g"])


def _lru_bwd(p_lru, h, do_cat, prm, t_len, name):
    n, c2 = p_lru.shape
    c = c2 // 2
    nblk = c // LRU_BLOCK_W
    tb = _tile(t_len, 128, SUBLANES)
    nb = t_len // tb
    bsz = n // t_len

    def body(xb_ref, gate_ref, prev_ref, h_ref, hprev_ref, do_ref,
             cw_ref, cb_ref, wr_ref, br_ref, wi_ref, bi_ref, lam_ref, ng_ref,
             dp_o, dcw_o, dcb_o, dwr_o, dbr_o, dwi_o, dbi_o, dlam_o, dng_o,
             a_next, g_next, dxc_next):
        b = pl.program_id(0)
        i = pl.program_id(1)
        blk = nb - 1 - i
        first = blk == 0
        prev8 = jnp.where(first, jnp.zeros_like(prev_ref[...]), prev_ref[...])
        hprev8 = jnp.where(first, jnp.zeros_like(hprev_ref[...]), hprev_ref[...])
        row = lax.broadcasted_iota(jnp.int32, (tb, c), 0)
        is_t0 = jnp.logical_and(first, row == 0)
        gate = gate_ref[...]
        f = _lru_gates(xb_ref[...], prev8, gate, cw_ref, cb_ref, wr_ref, br_ref, wi_ref, bi_ref, lam_ref, is_t0)

        @pl.when(i == 0)
        def _():
            a_next[...] = jnp.zeros_like(a_next)
            g_next[...] = jnp.zeros_like(g_next)
            dxc_next[...] = jnp.zeros_like(dxc_next)

        @pl.when(jnp.logical_and(b == 0, i == 0))
        def _():
            for o in (dcw_o, dcb_o, dwr_o, dbr_o, dwi_o, dbi_o, dlam_o, dng_o):
                o[...] = jnp.zeros_like(o)

        def rsum(x):
            return jnp.sum(x, axis=0, keepdims=True)

        hv = h_ref[...]
        hprev = _shift_down(hv, hprev8, 1)
        ge = f["ge"]
        y = hv * ge
        rstd = lax.rsqrt(jnp.mean(y * y, axis=-1, keepdims=True) + NORM_EPS)
        yh = y * rstd
        dyn = do_ref[...]
        t = dyn * ng_ref[...]
        dy = rstd * (t - yh * jnp.mean(t * yh, axis=-1, keepdims=True))
        dng_o[...] += rsum(dyn * yh)
        dgate = dy * hv * _gelu_grad(gate, f["th"])

        av = f["a"]
        acc_c = _shift_up(av, a_next[...], 1)
        acc_g = dy * ge
        s = 1
        while s < tb:
            keep = row < tb - s
            c_sh = jnp.where(keep, pltpu.roll(acc_c, tb - s, 0), 1.0)
            g_sh = jnp.where(keep, pltpu.roll(acc_g, tb - s, 0), 0.0)
            acc_g = acc_g + acc_c * g_sh
            acc_c = acc_c * c_sh
            s *= 2
        gtot = acc_g + acc_c * g_next[0:1, :]
        a_next[0:1, :] = av[0:1, :]
        g_next[0:1, :] = gtot[0:1, :]

        xc, ig, rg, mult = f["xc"], f["ig"], f["rg"], f["mult"]
        da = gtot * hprev
        dmult = gtot * ig * xc
        dig = gtot * mult * xc
        dxc = gtot * mult * ig
        da = da + jnp.where(is_t0, 0.0, -dmult * av / mult)
        dla = da * av
        drg = dla * (-LRU_C) * f["sp"]
        dlam_o[...] += rsum(dla * rg) * LRU_C * _sig(-lam_ref[...])
        dzr = drg * rg * (1.0 - rg)
        dzi = dig * ig * (1.0 - ig)
        dbr_o[...] += rsum(dzr)
        dbi_o[...] += rsum(dzi)
        dzrb, dzib = dzr.astype(BF16), dzi.astype(BF16)
        xcb = f["xcb"]
        back = []
        for hh in range(nblk):
            sl = slice(hh * LRU_BLOCK_W, (hh + 1) * LRU_BLOCK_W)
            dwr_o[hh] += _dot_tn(xcb[:, sl], dzrb[:, sl])
            dwi_o[hh] += _dot_tn(xcb[:, sl], dzib[:, sl])
            back.append(_dot_nt(dzrb[:, sl], wr_ref[hh]) + _dot_nt(dzib[:, sl], wi_ref[hh]))
        dxc = dxc + jnp.concatenate(back, axis=1)
        dcb_o[...] += rsum(dxc)
        xs = f["xs"]
        dcw_o[...] += jnp.concatenate([rsum(dxc * xs[CONV_WIDTH - 1 - j]) for j in range(CONV_WIDTH)], axis=0)
        nxt = dxc_next[...]
        dxb = dxc * cw_ref[pl.ds(CONV_WIDTH - 1, 1), :]
        for j in range(1, CONV_WIDTH):
            dxb = dxb + _shift_up(dxc, nxt, j) * cw_ref[pl.ds(CONV_WIDTH - 1 - j, 1), :]
        dxc_next[...] = dxc[0:SUBLANES, :]
        dp_o[:, 0:c] = dxb.astype(BF16)
        dp_o[:, c:2 * c] = dgate.astype(BF16)

    xb_s, gate_s, prev_s = _lru_specs(tb, c, nb, True)
    c0 = lambda b, i: (0, 0)
    vec = pl.BlockSpec((1, c), c0)
    wsp = pl.BlockSpec((nblk, LRU_BLOCK_W, LRU_BLOCK_W), lambda b, i: (0, 0, 0))
    cwsp = pl.BlockSpec((CONV_WIDTH, c), c0)
    sds = jax.ShapeDtypeStruct
    return _call(
        body, name=name, grid=(bsz, nb),
        in_specs=[xb_s, gate_s, prev_s, xb_s, prev_s, gate_s, cwsp, vec, wsp, vec, wsp, vec, vec, vec],
        out_specs=[pl.BlockSpec((tb, 2 * c), lambda b, i: (b * nb + nb - 1 - i, 0)),
                   cwsp, vec, wsp, vec, wsp, vec, vec, vec],
        out_shape=[sds((n, 2 * c), BF16), sds((CONV_WIDTH, c), F32), sds((1, c), F32),
                   sds((nblk, LRU_BLOCK_W, LRU_BLOCK_W), F32), sds((1, c), F32),
                   sds((nblk, LRU_BLOCK_W, LRU_BLOCK_W), F32), sds((1, c), F32), sds((1, c), F32), sds((1, c), F32)],
        scratch_shapes=[pltpu.VMEM((SUBLANES, c), F32)] * 3,
        compiler_params=_cp("arbitrary", "arbitrary"),
    )(p_lru, p_lru, p_lru, h, h, do_cat, prm["conv_w"], prm["conv_b"], prm["wr"], prm["br"], prm["wi"], prm["bi"],
      prm["lam"], prm["norm_g"])


def _adamw(g, w, m, v, name):
    rows, cols = g.shape
    tb = _tile(rows, 128, SUBLANES)

    def body(g_ref, w_ref, m_ref, v_ref, d_o, m_o, v_o):
        gv = g_ref[...]
        mn = ADAM_B1 * m_ref[...] + (1.0 - ADAM_B1) * gv
        vn = ADAM_B2 * v_ref[...] + (1.0 - ADAM_B2) * (gv * gv)
        m_o[...] = mn
        v_o[...] = vn
        d_o[...] = -ADAM_LR * ((mn / _BC1) / (jnp.sqrt(vn / _BC2) + ADAM_EPS) + ADAM_WD * w_ref[...])

    blk = pl.BlockSpec((tb, cols), lambda i: (i, 0))
    return _call(body, name=name, grid=(rows // tb,), in_specs=[blk] * 4, out_specs=[blk] * 3,
                 out_shape=[jax.ShapeDtypeStruct((rows, cols), F32)] * 3, compiler_params=_cp("parallel"))(g, w, m, v)


def _pair_sum(x4, recv, name):
    _, _, a, b = x4.shape
    ta = _tile(a, 256, SUBLANES)

    def body(x_ref, r_ref, o_ref):
        mine = x_ref[lax.axis_index("c")]
        o_ref[...] = (mine.astype(F32) + r_ref[...].astype(F32)).astype(BF16)

    return _call(
        body, name=name, grid=(4, a // ta),
        in_specs=[pl.BlockSpec((None, 2, ta, b), lambda j, i: (j, 0, i, 0)),
                  pl.BlockSpec((None, ta, b), lambda j, i: (j, i, 0))],
        out_specs=pl.BlockSpec((None, ta, b), lambda j, i: (j, i, 0)),
        out_shape=jax.ShapeDtypeStruct((4, a, b), BF16), compiler_params=_cp("parallel", "parallel"))(x4, recv)


def _chip_sum(x4, name, out_dtype=F32):
    _, a, b = x4.shape
    ta = _tile(a, 256, SUBLANES)

    def body(x_ref, o_ref):
        acc = x_ref[0].astype(F32) + x_ref[1].astype(F32)
        acc = acc + x_ref[2].astype(F32)
        acc = acc + x_ref[3].astype(F32)
        o_ref[...] = acc.astype(out_dtype)

    return _call(
        body, name=name, grid=(a // ta,),
        in_specs=[pl.BlockSpec((4, ta, b), lambda i: (0, i, 0))],
        out_specs=pl.BlockSpec((ta, b), lambda i: (i, 0)),
        out_shape=jax.ShapeDtypeStruct((a, b), out_dtype), compiler_params=_cp("parallel"))(x4)


def _add2(x, y, name):
    rows, cols = x.shape
    tb = _tile(rows, 512, SUBLANES)

    def body(x_ref, y_ref, o_ref):
        o_ref[...] = x_ref[...] + y_ref[...]

    blk = pl.BlockSpec((tb, cols), lambda i: (i, 0))
    return _call(body, name=name, grid=(rows // tb,), in_specs=[blk, blk], out_specs=blk,
                 out_shape=jax.ShapeDtypeStruct((rows, cols), x.dtype), compiler_params=_cp("parallel"))(x, y)


_HBM = pl.BlockSpec(memory_space=pltpu.HBM)


def _place():
    x, y, c = lax.axis_index("x"), lax.axis_index("y"), lax.axis_index("c")
    chips = [(1 - x, y), (x, 1 - y), (1 - x, 1 - y)]
    return x, y, c, chips


def _comm_call(body, name, xs, out_shapes, n_sems):
    return _call(
        body, name=name, in_specs=[_HBM] * len(xs), out_specs=[_HBM] * len(out_shapes), out_shape=out_shapes,
        scratch_shapes=[pltpu.SemaphoreType.DMA((n_sems,)), pltpu.SemaphoreType.DMA((n_sems,)),
                        pltpu.SemaphoreType.DMA((len(xs),))],
    )(*xs)


def _all_gather_chips(xs, name):
    n = len(xs)

    def body(*refs):
        ins, outs = refs[:n], refs[n:2 * n]
        ssem, rsem, lsem = refs[2 * n:]
        x, y, c, chips = _place()
        me = 2 * x + y
        sib = (x, y, 1 - c)
        local, sends = [], []
        for i in range(n):
            cp = pltpu.make_async_copy(ins[i], outs[i].at[me], lsem.at[i])
            cp.start()
            local.append(cp)
            for j, (px, py) in enumerate(chips):
                cp = pltpu.make_async_remote_copy(
                    src_ref=ins[i].at[c], dst_ref=outs[i].at[me, c], send_sem=ssem.at[6 * i + j],
                    recv_sem=rsem.at[6 * i + j], device_id=(px, py, c), device_id_type=MESH)
                cp.start()
                sends.append(cp)
        for i in range(n):
            for j, (px, py) in enumerate(chips):
                slot = outs[i].at[2 * px + py, c]
                pltpu.make_async_remote_copy(
                    src_ref=slot, dst_ref=slot, send_sem=ssem.at[6 * i + j], recv_sem=rsem.at[6 * i + j],
                    device_id=(px, py, c), device_id_type=MESH).wait_recv()
                cp = pltpu.make_async_remote_copy(
                    src_ref=slot, dst_ref=slot, send_sem=ssem.at[6 * i + 3 + j], recv_sem=rsem.at[6 * i + 3 + j],
                    device_id=sib, device_id_type=MESH)
                cp.start()
                sends.append(cp)
        for i in range(n):
            for j, (px, py) in enumerate(chips):
                slot = outs[i].at[2 * px + py, 1 - c]
                pltpu.make_async_remote_copy(
                    src_ref=slot, dst_ref=slot, send_sem=ssem.at[6 * i + 3 + j], recv_sem=rsem.at[6 * i + 3 + j],
                    device_id=sib, device_id_type=MESH).wait_recv()
        for cp in sends:
            cp.wait_send()
        for cp in local:
            cp.wait()

    outs = [jax.ShapeDtypeStruct((4,) + v.shape, v.dtype) for v in xs]
    return _comm_call(body, name, xs, outs, 6 * n)


def _sibling_swap(xs, pick_half, name):
    n = len(xs)

    def body(*refs):
        ins, outs = refs[:n], refs[n:2 * n]
        ssem, rsem, _ = refs[2 * n:]
        x, y, c, _ = _place()
        cps = []
        for i in range(n):
            src = ins[i].at[:, 1 - c] if pick_half else ins[i]
            cp = pltpu.make_async_remote_copy(src_ref=src, dst_ref=outs[i], send_sem=ssem.at[i], recv_sem=rsem.at[i],
                                              device_id=(x, y, 1 - c), device_id_type=MESH)
            cp.start()
            cps.append(cp)
        for cp in cps:
            cp.wait()

    outs = [jax.ShapeDtypeStruct((v.shape[0],) + v.shape[2:] if pick_half else v.shape, v.dtype) for v in xs]
    return _comm_call(body, name, xs, outs, n)


def _chip_exchange(xs, bcast, name):
    n = len(xs)

    def body(*refs):
        ins, outs = refs[:n], refs[n:2 * n]
        ssem, rsem, lsem = refs[2 * n:]
        x, y, c, chips = _place()
        me = 2 * x + y
        cps = []
        for i in range(n):
            cp = pltpu.make_async_copy(ins[i] if bcast else ins[i].at[me], outs[i].at[me], lsem.at[i])
            cp.start()
            cps.append(cp)
            for j, (px, py) in enumerate(chips):
                src = ins[i] if bcast else ins[i].at[2 * px + py]
                cp = pltpu.make_async_remote_copy(
                    src_ref=src, dst_ref=outs[i].at[me], send_sem=ssem.at[3 * i + j], recv_sem=rsem.at[3 * i + j],
                    device_id=(px, py, c), device_id_type=MESH)
                cp.start()
                cps.append(cp)
        for i in range(n):
            for j, (px, py) in enumerate(chips):
                slot = outs[i].at[2 * px + py]
                pltpu.make_async_remote_copy(
                    src_ref=slot, dst_ref=slot, send_sem=ssem.at[3 * i + j], recv_sem=rsem.at[3 * i + j],
                    device_id=(px, py, c), device_id_type=MESH).wait_recv()
        for i in range(n):
            cps[4 * i].wait()
            for j in range(3):
                cps[4 * i + 1 + j].wait_send()

    outs = [jax.ShapeDtypeStruct(((4,) + v.shape) if bcast else v.shape, v.dtype) for v in xs]
    return _comm_call(body, name, xs, outs, 3 * n)


def _sibling_share(xs, name):
    n = len(xs)

    def body(*refs):
        ins, outs = refs[:n], refs[n:2 * n]
        ssem, rsem, lsem = refs[2 * n:]
        x, y, c, _ = _place()
        cps = []
        for i in range(n):
            loc = pltpu.make_async_copy(ins[i], outs[i].at[c], lsem.at[i])
            loc.start()
            cp = pltpu.make_async_remote_copy(src_ref=ins[i], dst_ref=outs[i].at[c], send_sem=ssem.at[i],
                                              recv_sem=rsem.at[i], device_id=(x, y, 1 - c), device_id_type=MESH)
            cp.start()
            cps.append((loc, cp))
        for i, (loc, cp) in enumerate(cps):
            slot = outs[i].at[1 - c]
            pltpu.make_async_remote_copy(src_ref=slot, dst_ref=slot, send_sem=ssem.at[i], recv_sem=rsem.at[i],
                                         device_id=(x, y, 1 - c), device_id_type=MESH).wait_recv()
            cp.wait_send()
            loc.wait()

    outs = [jax.ShapeDtypeStruct((2,) + v.shape, v.dtype) for v in xs]
    return _comm_call(body, name, xs, outs, n)


def _to_scan_k(x, bsz, t_len):
    h = x.shape[1] // RWKV_HEAD
    y = x.reshape(bsz, t_len, h, RWKV_HEAD).transpose(1, 3, 0, 2).reshape(t_len, RWKV_HEAD, bsz * h)
    return jnp.concatenate([y, y], axis=-1)


def _to_scan_v(x, bsz, t_len):
    h = x.shape[1] // RWKV_HEAD
    y = x.reshape(bsz, t_len, h, 2, RWKV_HEAD // 2).transpose(1, 4, 3, 0, 2)
    return y.reshape(t_len, RWKV_HEAD // 2, 2 * bsz * h)


def _from_scan_k(x, bsz, t_len):
    h = x.shape[2] // (2 * bsz)
    y = x[:, :, :bsz * h].reshape(t_len, RWKV_HEAD, bsz, h).transpose(2, 0, 3, 1)
    return y.reshape(bsz * t_len, h * RWKV_HEAD)


def _from_scan_v(x, bsz, t_len):
    h = x.shape[2] // (2 * bsz)
    y = x.reshape(t_len, RWKV_HEAD // 2, 2, bsz, h).transpose(3, 0, 4, 2, 1)
    return y.reshape(bsz * t_len, h * RWKV_HEAD)


def _pad_rows(x, rows):
    return jnp.pad(x, ((0, rows - x.shape[0]), (0, 0)))


def _pad_cols(x, cols):
    return jnp.pad(x, ((0, 0), (0, cols - x.shape[1])))


def _cols_from_shards(g4):
    _, r, cs = g4.shape
    return g4.transpose(1, 0, 2).reshape(r, 4 * cs)


def _cols_to_shards(g):
    r, cols = g.shape
    return g.reshape(r, 4, cols // 4).transpose(1, 0, 2)


def kernel(x, norm_mix_g, w_in, mu_shift, rwkv_w0, rwkv_w2, rwkv_a0, rwkv_a2, rwkv_g2, rwkv_k_k, rwkv_k_a, rwkv_r_k, rwkv_ln_g, rwkv_ln_b, conv_w, conv_b, lru_wr, lru_br, lru_wi, lru_bi, lru_lambda, lru_norm_g, w_out, norm_ffn_g, ffn_w_gate, ffn_w_up, ffn_w_down, norm_final_g, loss_target, m_norm_mix_g, m_w_in, m_mu_shift, m_rwkv_w0, m_rwkv_w2, m_rwkv_a0, m_rwkv_a2, m_rwkv_g2, m_rwkv_k_k, m_rwkv_k_a, m_rwkv_r_k, m_rwkv_ln_g, m_rwkv_ln_b, m_conv_w, m_conv_b, m_lru_wr, m_lru_br, m_lru_wi, m_lru_bi, m_lru_lambda, m_lru_norm_g, m_w_out, m_norm_ffn_g, m_ffn_w_gate, m_ffn_w_up, m_ffn_w_down, m_norm_final_g, v_norm_mix_g, v_w_in, v_mu_shift, v_rwkv_w0, v_rwkv_w2, v_rwkv_a0, v_rwkv_a2, v_rwkv_g2, v_rwkv_k_k, v_rwkv_k_a, v_rwkv_r_k, v_rwkv_ln_g, v_rwkv_ln_b, v_conv_w, v_conv_b, v_lru_wr, v_lru_br, v_lru_wi, v_lru_bi, v_lru_lambda, v_lru_norm_g, v_w_out, v_norm_ffn_g, v_ffn_w_gate, v_ffn_w_up, v_ffn_w_down, v_norm_final_g):
    names = ['norm_mix_g', 'w_in', 'mu_shift', 'rwkv_w0', 'rwkv_w2', 'rwkv_a0', 'rwkv_a2', 'rwkv_g2', 'rwkv_k_k',
             'rwkv_k_a', 'rwkv_r_k', 'rwkv_ln_g', 'rwkv_ln_b', 'conv_w', 'conv_b', 'lru_wr', 'lru_br', 'lru_wi',
             'lru_bi', 'lru_lambda', 'lru_norm_g', 'w_out', 'norm_ffn_g', 'ffn_w_gate', 'ffn_w_up', 'ffn_w_down',
             'norm_final_g']
    env = locals()
    wts = {k: env[k] for k in names}
    mom_m = {k: env["m_" + k] for k in names}
    mom_v = {k: env["v_" + k] for k in names}

    bsz, t_len, d = x.shape
    n = bsz * t_len
    w = rwkv_w0.shape[1]
    lw = lru_br.shape[1]
    dl, al, gl = rwkv_w2.shape[1], rwkv_a2.shape[1], rwkv_g2.shape[1]
    dlp, alp, glp = _ceil_to(dl, LANES), _ceil_to(al, LANES), _ceil_to(gl, LANES)
    lp = dlp + alp + glp
    rc = 3 * w + dl + al + gl
    chip = 2 * lax.axis_index("x") + lax.axis_index("y")

    big = ['w_in', 'w_out', 'ffn_w_gate', 'ffn_w_up', 'ffn_w_down']
    small_sh = ['rwkv_w2', 'rwkv_a2', 'rwkv_g2', 'conv_w']

    def halves(a2d):
        return a2d.reshape(2, a2d.shape[0] // 2, a2d.shape[1])

    send = [halves(wts[k][0].astype(BF16)) for k in big] + [halves(wts[k][0]) for k in small_sh]
    got = _all_gather_chips(send, "gather_weights")
    full = {}
    for k, g in zip(big + small_sh, got):
        g = g.reshape(4, g.shape[1] * g.shape[2], g.shape[3])
        if k in ('w_out', 'ffn_w_down'):
            full[k] = g.reshape(4 * g.shape[1], g.shape[2])
        else:
            full[k] = _cols_from_shards(g)
    wi = full['w_in']
    w_rkv = wi[:, :3 * w]
    w_lru = wi[:, rc:]
    o = 3 * w
    w_lora = jnp.concatenate([_pad_cols(wi[:, o:o + dl], dlp), _pad_cols(wi[:, o + dl:o + dl + al], alp),
                              _pad_cols(wi[:, o + dl + al:rc], glp)], axis=1)
    mu = mu_shift
    prm_r = dict(
        mu_rkv=mu[:, :3 * w],
        mu_lora=jnp.concatenate([_pad_cols(mu[:, o:o + dl], dlp), _pad_cols(mu[:, o + dl:o + dl + al], alp),
                                 _pad_cols(mu[:, o + dl + al:rc], glp)], axis=1),
        w0=rwkv_w0, a0=rwkv_a0, k_k=rwkv_k_k, k_a=rwkv_k_a,
        w2=_pad_rows(full['rwkv_w2'], dlp).astype(BF16), a2=_pad_rows(full['rwkv_a2'], alp).astype(BF16),
        g2=_pad_rows(full['rwkv_g2'], glp).astype(BF16))
    ln_g, ln_b, r_k = rwkv_ln_g, rwkv_ln_b, rwkv_r_k.reshape(1, w)
    prm_l = dict(conv_w=full['conv_w'], conv_b=conv_b, wr=lru_wr[0].astype(BF16), br=lru_br,
                 wi=lru_wi[0].astype(BF16), bi=lru_bi, lam=lru_lambda, norm_g=lru_norm_g)
    g_final = norm_final_g.reshape(1, d)

    x2 = x.reshape(n, d)
    u1 = _rmsnorm_fwd(x2, norm_mix_g, "norm_mix")
    p_rkv = _mm(u1, w_rkv, name="in_rkv")
    p_lru = _mm(u1, w_lru, name="in_lru")
    p_lora = _mm(u1, w_lora, name="in_lora")
    r_t, dec_t, k_t, v_t, na_t, nb_t, g_t = _rwkv_prep_fwd(p_rkv, p_lora, prm_r, t_len, "rwkv_prep")
    sk = [_to_scan_k(a, bsz, t_len) for a in (r_t, dec_t, k_t, na_t, nb_t)]
    sv = _to_scan_v(v_t, bsz, t_len)
    y_s, ckpt = _rwkv_scan_fwd(*sk, sv, name="rwkv_scan")
    y_t = _from_scan_v(y_s, bsz, t_len)
    y_a = _rwkv_post_fwd(y_t, r_t, k_t, v_t, g_t, ln_g, ln_b, r_k, "rwkv_post")
    y_b, h_lru = _lru_fwd(p_lru, prm_l, t_len, "lru_fwd")
    wo = full['w_out']
    h1 = _mm(y_a, wo[:w], name="out_a", res=x2)
    h1 = _mm(y_b, wo[w:], name="out_b", res=h1)
    u2 = _rmsnorm_fwd(h1, norm_ffn_g, "norm_ffn")
    wg, wu, wd = full['ffn_w_gate'], full['ffn_w_up'], full['ffn_w_down']
    ffc = (1024, 256, 4096)
    gate = _mm(u2, wg, name="ffn_gate", caps=ffc)
    up = _mm(u2, wu, name="ffn_up", caps=ffc)
    act = _swiglu_fwd(gate, up, "swiglu")
    h2 = _mm(act, wd, name="ffn_down", res=h1, caps=(512, 512, 5504))

    dh2, dh2b, g_norm_final, loss_vec = _loss_head(h2, g_final, loss_target.reshape(n, d), "loss_head")
    loss = lax.psum(loss_vec[0, 0], ("x", "y", "c"))
    dact = _mm(dh2b, wd, name="d_act", tb=True, caps=(1024, 256, 4096))
    dgate, dup = _swiglu_bwd(gate, up, dact, "swiglu_bwd")
    gw_down = _mm(act, dh2b, name="dw_down", ta=True, out_dtype=BF16, n_outer=True, caps=(256, 1024, 4096))
    gw_gate = _mm(u2, dgate, name="dw_gate", ta=True, out_dtype=BF16, caps=(1024, 256, 4096))
    gw_up = _mm(u2, dup, name="dw_up", ta=True, out_dtype=BF16, caps=(1024, 256, 4096))
    du2 = _mm(dgate, wg, name="du2_gate", tb=True, caps=(512, 512, 5504))
    du2 = _mm(dup, wu, name="du2_up", tb=True, res=du2, caps=(512, 512, 5504))
    dh1, dh1b, g_norm_ffn = _rmsnorm_bwd(du2, h1, norm_ffn_g, dh2, "norm_ffn_bwd")
    dcat = _mm(dh1b, wo, name="d_cat", tb=True)
    gw_out = jnp.concatenate([_mm(y_a, dh1b, name="dw_out_a", ta=True, out_dtype=BF16),
                              _mm(y_b, dh1b, name="dw_out_b", ta=True, out_dtype=BF16)], axis=0)
    (dp_lru, g_conv_w, g_conv_b, g_wr, g_br, g_wi, g_bi, g_lam, g_lng) = _lru_bwd(
        p_lru, h_lru, dcat, prm_l, t_len, "lru_bwd")
    dy_t, dr_p, dk_p, dv_p, dg_t, g_ln_g, g_ln_b, g_r_k = _rwkv_post_bwd(
        y_t, r_t, k_t, v_t, g_t, ln_g, ln_b, r_k, dcat, "rwkv_post_bwd")
    dr_s, dw_s, dk_s, da_s, db_s, dv_s = _rwkv_scan_bwd(*sk, sv, _to_scan_v(dy_t, bsz, t_len), ckpt,
                                                        name="rwkv_scan_bwd")
    grads = [_from_scan_k(dr_s, bsz, t_len), dr_p, _from_scan_k(dw_s, bsz, t_len), _from_scan_k(dk_s, bsz, t_len),
             dk_p, _from_scan_v(dv_s, bsz, t_len), dv_p, _from_scan_k(da_s, bsz, t_len),
             _from_scan_k(db_s, bsz, t_len), dg_t]
    (dq_r, dq_l, g_mu_r, g_mu_l, g_w0, g_a0, g_kk, g_ka, g_w2, g_a2, g_g2) = _rwkv_prep_bwd(
        p_rkv, p_lora, prm_r, grads, t_len, "rwkv_prep_bwd")
    dp_rkv = _shift_combine(dq_r, prm_r["mu_rkv"], t_len, "shift_bwd_rkv")
    dp_lora = _shift_combine(dq_l, prm_r["mu_lora"], t_len, "shift_bwd_lora")
    du1 = _mm(dp_rkv, w_rkv, name="du1_rkv", tb=True)
    du1 = _mm(dp_lru, w_lru, name="du1_lru", tb=True, res=du1)
    du1 = _mm(dp_lora, w_lora, name="du1_lora", tb=True, res=du1)
    gx, _, g_norm_mix = _rmsnorm_bwd(du1, x2, norm_mix_g, dh1, "norm_mix_bwd")
    gi_rkv = _mm(u1, dp_rkv, name="dw_in_rkv", ta=True, out_dtype=BF16)
    gi_lru = _mm(u1, dp_lru, name="dw_in_lru", ta=True, out_dtype=BF16)
    gi_lora = _mm(u1, dp_lora, name="dw_in_lora", ta=True, out_dtype=BF16)
    gw_in = jnp.concatenate([gi_rkv, gi_lora[:, :dl], gi_lora[:, dlp:dlp + al], gi_lora[:, dlp + alp:dlp + alp + gl],
                             gi_lru], axis=1)

    def shards_rows(g):
        return g.reshape(4, 2, g.shape[0] // 8, g.shape[1])

    def shards_cols(g):
        s = _cols_to_shards(g)
        return s.reshape(4, 2, s.shape[1] // 2, s.shape[2])

    g4 = [shards_cols(gw_in), shards_rows(gw_out), shards_cols(gw_gate), shards_cols(gw_up), shards_rows(gw_down)]
    sib = _sibling_swap(g4, True, "grad_sibling")
    pair = [_pair_sum(a4, s, "grad_pair_sum_%d" % i) for i, (a4, s) in enumerate(zip(g4, sib))]
    parts = _chip_exchange(pair, False, "grad_chips")
    mine = [_chip_sum(p4, "grad_chip_sum_%d" % i) for i, p4 in enumerate(parts)]
    shared = _sibling_share(mine, "grad_share")
    big_grads = {k: s.reshape(1, 2 * s.shape[1], s.shape[2]) for k, s in zip(big, shared)}

    g_mu = jnp.concatenate([g_mu_r, g_mu_l[:, :dl], g_mu_l[:, dlp:dlp + al], g_mu_l[:, dlp + alp:dlp + alp + gl]],
                           axis=1)
    small = dict(norm_mix_g=g_norm_mix, mu_shift=g_mu, rwkv_w0=g_w0, rwkv_w2=g_w2[:dl], rwkv_a0=g_a0,
                 rwkv_a2=g_a2[:al], rwkv_g2=g_g2[:gl], rwkv_k_k=g_kk, rwkv_k_a=g_ka, rwkv_r_k=g_r_k,
                 rwkv_ln_g=g_ln_g, rwkv_ln_b=g_ln_b, conv_w=g_conv_w, conv_b=g_conv_b, lru_wr=g_wr, lru_br=g_br,
                 lru_wi=g_wi, lru_bi=g_bi, lru_lambda=g_lam, lru_norm_g=g_lng, norm_ffn_g=g_norm_ffn,
                 norm_final_g=g_norm_final)
    small_names = list(small)
    sizes = [small[k].size for k in small_names]
    total = sum(sizes)
    padded = _ceil_to(total, SUBLANES * LANES)

    def pack(arrs):
        flat = jnp.concatenate([a.reshape(-1) for a in arrs] + [jnp.zeros((padded - sum(a.size for a in arrs),), F32)])
        return flat.reshape(padded // LANES, LANES)

    packed = pack([small[k] for k in small_names])
    other = _sibling_swap([packed], False, "small_sibling")[0]
    chip_sum = _add2(packed, other, "small_pair_sum")
    all4 = _chip_exchange([chip_sum], True, "small_chips")[0]
    red = _chip_sum(all4, "small_chip_sum").reshape(-1)
    small_g = {}
    off = 0
    for k, sz in zip(small_names, sizes):
        full_g = red[off:off + sz].reshape(small[k].shape)
        off += sz
        if k in small_sh:
            cs = full_g.shape[1] // 4
            full_g = lax.dynamic_slice_in_dim(full_g, chip * cs, cs, axis=1)
        small_g[k] = full_g.reshape(wts[k].shape)

    grad_w, delta_w, new_m, new_v = {}, {}, {}, {}
    for k in big:
        g = big_grads[k]
        grad_w[k] = g
        dlt, mn, vn = _adamw(g[0], wts[k][0], mom_m[k][0], mom_v[k][0], "adamw_" + k)
        delta_w[k], new_m[k], new_v[k] = dlt[None], mn[None], vn[None]
    lsizes = [small_g[k].size for k in small_names]
    lpad = _ceil_to(sum(lsizes), SUBLANES * LANES)

    def lpack(tree):
        arrs = [tree[k].reshape(-1) for k in small_names]
        flat = jnp.concatenate(arrs + [jnp.zeros((lpad - sum(lsizes),), F32)])
        return flat.reshape(lpad // LANES, LANES)

    dlt, mn, vn = _adamw(lpack(small_g), lpack(wts), lpack(mom_m), lpack(mom_v), "adamw_small")
    off = 0
    for k, sz in zip(small_names, lsizes):
        shp = wts[k].shape
        grad_w[k] = small_g[k]
        delta_w[k] = dlt.reshape(-1)[off:off + sz].reshape(shp)
        new_m[k] = mn.reshape(-1)[off:off + sz].reshape(shp)
        new_v[k] = vn.reshape(-1)[off:off + sz].reshape(shp)
        off += sz

    return (loss, gx.reshape(bsz, t_len, d), *[grad_w[k] for k in names], *[delta_w[k] for k in names],
            *[new_m[k] for k in names], *[new_v[k] for k in names])
```

```python
import jax
import jax.numpy as jnp
from jax import lax
from jax.experimental import pallas as pl
from jax.experimental.pallas import tpu as pltpu

F32 = jnp.float32
BF16 = jnp.bfloat16
MESH = pl.DeviceIdType.MESH
_call = pl.pallas_call

V7X_VMEM_LIMIT = 56 * 1024 * 1024
LANES = 128
SUBLANES = 8

RWKV_HEAD = 64
LRU_BLOCK_W = 128
CONV_WIDTH = 4
LRU_C = 8.0
NORM_EPS = 1e-6
GN_EPS = 64e-5
KK_EPS = 1e-24
SCAN_CHUNK = 16

ADAM_LR = 0.001
ADAM_B1 = 0.9
ADAM_B2 = 0.999
ADAM_EPS = 1e-08
ADAM_WD = 0.01
ADAM_STEP = 10
_BC1 = 1.0 - ADAM_B1 ** ADAM_STEP
_BC2 = 1.0 - ADAM_B2 ** ADAM_STEP

_HI = lax.Precision.HIGHEST


def _cp(*sem):
    return pltpu.CompilerParams(dimension_semantics=tuple(sem), vmem_limit_bytes=V7X_VMEM_LIMIT)


def _tile(n, cap, unit=LANES):
    if n <= cap:
        return n
    best = None
    d = unit
    while d <= cap:
        if n % d == 0:
            best = d
        d += unit
    return n if best is None else best


def _ceil_to(n, m):
    return -(-n // m) * m


def _sig(x):
    return 1.0 / (1.0 + jnp.exp(-x))


def _log1p(x):
    return jnp.where(x < 0.01, x * (1.0 - x * (0.5 - x * (1.0 / 3.0))), jnp.log(1.0 + x))


def _softplus(x):
    return jnp.maximum(x, 0.0) + _log1p(jnp.exp(-jnp.abs(x)))


def _neg_expm1(x):
    small = -x * (1.0 + x * (0.5 + x * (1.0 / 6.0)))
    return jnp.where(x > -0.01, small, 1.0 - jnp.exp(x))


_GELU_K = 0.7978845608028654
_GELU_C = 0.044715


def _gelu_parts(x):
    th = jnp.tanh(_GELU_K * (x + _GELU_C * x * x * x))
    return 0.5 * x * (1.0 + th), th


def _gelu_grad(x, th):
    return 0.5 * (1.0 + th) + 0.5 * x * (1.0 - th * th) * _GELU_K * (1.0 + 3.0 * _GELU_C * x * x)


def _shift_down(x, prev8, j):
    tb = x.shape[0]
    xr = pltpu.roll(x, j, 0)
    pr = pltpu.roll(prev8, j, 0)
    row = lax.broadcasted_iota(jnp.int32, prev8.shape, 0)
    first = jnp.where(row < j, pr, xr[0:SUBLANES])
    if tb == SUBLANES:
        return first
    return jnp.concatenate([first, xr[SUBLANES:]], axis=0)


def _shift_up(x, next8, j):
    tb = x.shape[0]
    xr = pltpu.roll(x, tb - j, 0)
    nr = pltpu.roll(next8, SUBLANES - j, 0)
    row = lax.broadcasted_iota(jnp.int32, next8.shape, 0)
    last = jnp.where(row >= SUBLANES - j, nr, xr[tb - SUBLANES:])
    if tb == SUBLANES:
        return last
    return jnp.concatenate([xr[:tb - SUBLANES], last], axis=0)


def _head_mats(width, heads_pad):
    e = (lax.broadcasted_iota(jnp.int32, (width, heads_pad), 0) // RWKV_HEAD
         == lax.broadcasted_iota(jnp.int32, (width, heads_pad), 1)).astype(F32)
    et = (lax.broadcasted_iota(jnp.int32, (heads_pad, width), 1) // RWKV_HEAD
          == lax.broadcasted_iota(jnp.int32, (heads_pad, width), 0)).astype(F32)
    return e, et


def _headsum(x, e, et):
    s = jnp.dot(x, e, preferred_element_type=F32, precision=_HI)
    return jnp.dot(s, et, preferred_element_type=F32, precision=_HI)


def _dot(a, b):
    return jnp.dot(a.astype(BF16), b.astype(BF16), preferred_element_type=F32)


def _dot_tn(a, b):
    return lax.dot_general(a.astype(BF16), b.astype(BF16), (((0,), (0,)), ((), ())), preferred_element_type=F32)


def _dot_nt(a, b):
    return lax.dot_general(a.astype(BF16), b.astype(BF16), (((1,), (1,)), ((), ())), preferred_element_type=F32)


def _mm(a, b, *, name, ta=False, tb=False, out_dtype=F32, res=None, n_outer=False, caps=(1024, 512, 4096)):
    m = a.shape[1] if ta else a.shape[0]
    kd = a.shape[0] if ta else a.shape[1]
    n = b.shape[0] if tb else b.shape[1]
    assert kd == (b.shape[1] if tb else b.shape[0])
    tm, tn, tk = _tile(m, caps[0]), _tile(n, caps[1]), _tile(kd, caps[2])
    gm, gn, gk = m // tm, n // tn, kd // tk
    dims = (((0 if ta else 1,), (1 if tb else 0,)), ((), ()))

    def ij(g0, g1):
        return (g1, g0) if n_outer else (g0, g1)

    def a_map(g0, g1, k):
        i, _ = ij(g0, g1)
        return (k, i) if ta else (i, k)

    def b_map(g0, g1, k):
        _, j = ij(g0, g1)
        return (j, k) if tb else (k, j)

    def o_map(g0, g1, k):
        return ij(g0, g1)

    has_res = res is not None

    def body(*refs):
        a_ref, b_ref = refs[0], refs[1]
        res_ref = refs[2] if has_res else None
        o_ref = refs[3] if has_res else refs[2]
        prod = lax.dot_general(a_ref[...], b_ref[...], dims, preferred_element_type=F32)

        def finish(acc):
            if has_res:
                acc = acc + res_ref[...]
            o_ref[...] = acc.astype(out_dtype)

        if gk == 1:
            finish(prod)
        else:
            acc_ref = refs[-1]
            k = pl.program_id(2)

            @pl.when(k == 0)
            def _():
                acc_ref[...] = prod

            @pl.when(k > 0)
            def _():
                acc_ref[...] += prod

            @pl.when(k == gk - 1)
            def _():
                finish(acc_ref[...])

    in_specs = [pl.BlockSpec((tk, tm) if ta else (tm, tk), a_map),
                pl.BlockSpec((tn, tk) if tb else (tk, tn), b_map)]
    args = [a, b]
    if has_res:
        in_specs.append(pl.BlockSpec((tm, tn), o_map))
        args.append(res)
    grid = (gn, gm, gk) if n_outer else (gm, gn, gk)
    return _call(
        body, name=name, grid=grid, in_specs=in_specs,
        out_specs=pl.BlockSpec((tm, tn), o_map),
        out_shape=jax.ShapeDtypeStruct((m, n), out_dtype),
        scratch_shapes=[pltpu.VMEM((tm, tn), F32)] if gk > 1 else [],
        compiler_params=_cp("parallel", "parallel", "arbitrary"),
    )(*args)


def _rmsnorm_fwd(x, g, name):
    n, d = x.shape
    tb = _tile(n, 256, SUBLANES)

    def body(x_ref, g_ref, u_ref):
        xv = x_ref[...]
        rstd = lax.rsqrt(jnp.mean(xv * xv, axis=-1, keepdims=True) + NORM_EPS)
        u_ref[...] = (xv * rstd * g_ref[...]).astype(BF16)

    row = pl.BlockSpec((tb, d), lambda i: (i, 0))
    vec = pl.BlockSpec((1, d), lambda i: (0, 0))
    return _call(body, name=name, grid=(n // tb,), in_specs=[row, vec], out_specs=row,
                 out_shape=jax.ShapeDtypeStruct((n, d), BF16), compiler_params=_cp("parallel"))(x, g)


def _rmsnorm_bwd(du, x, g, dres, name):
    n, d = x.shape
    tb = _tile(n, 256, SUBLANES)

    def body(du_ref, x_ref, g_ref, dres_ref, dx_ref, dxb_ref, dg_ref):
        xv = x_ref[...]
        rstd = lax.rsqrt(jnp.mean(xv * xv, axis=-1, keepdims=True) + NORM_EPS)
        xh = xv * rstd
        duv = du_ref[...]
        t = duv * g_ref[...]
        dx = dres_ref[...] + rstd * (t - xh * jnp.mean(t * xh, axis=-1, keepdims=True))
        dx_ref[...] = dx
        dxb_ref[...] = dx.astype(BF16)

        @pl.when(pl.program_id(0) == 0)
        def _():
            dg_ref[...] = jnp.zeros_like(dg_ref)

        dg_ref[...] += jnp.sum(duv * xh, axis=0, keepdims=True)

    row = pl.BlockSpec((tb, d), lambda i: (i, 0))
    vec = pl.BlockSpec((1, d), lambda i: (0, 0))
    return _call(body, name=name, grid=(n // tb,), in_specs=[row, row, vec, row], out_specs=[row, row, vec],
                 out_shape=[jax.ShapeDtypeStruct((n, d), F32), jax.ShapeDtypeStruct((n, d), BF16),
                            jax.ShapeDtypeStruct((1, d), F32)],
                 compiler_params=_cp("arbitrary"))(du, x, g, dres)


def _loss_head(h, g, target, name):
    n, d = h.shape
    tb = _tile(n, 256, SUBLANES)

    def body(h_ref, g_ref, t_ref, dh_ref, dhb_ref, dg_ref, loss_ref):
        hv = h_ref[...]
        gv = g_ref[...]
        rstd = lax.rsqrt(jnp.mean(hv * hv, axis=-1, keepdims=True) + NORM_EPS)
        hh = hv * rstd
        err = hh * gv - t_ref[...]
        dy = err * (1.0 / d)
        dhh = dy * gv
        dh = rstd * (dhh - hh * jnp.mean(dhh * hh, axis=-1, keepdims=True))
        dh_ref[...] = dh
        dhb_ref[...] = dh.astype(BF16)

        @pl.when(pl.program_id(0) == 0)
        def _():
            dg_ref[...] = jnp.zeros_like(dg_ref)
            loss_ref[...] = jnp.zeros_like(loss_ref)

        dg_ref[...] += jnp.sum(dy * hh, axis=0, keepdims=True)
        loss_ref[...] += jnp.sum(err * err) * (0.5 / d)

    row = pl.BlockSpec((tb, d), lambda i: (i, 0))
    vec = pl.BlockSpec((1, d), lambda i: (0, 0))
    lvec = pl.BlockSpec((1, LANES), lambda i: (0, 0))
    return _call(body, name=name, grid=(n // tb,), in_specs=[row, vec, row], out_specs=[row, row, vec, lvec],
                 out_shape=[jax.ShapeDtypeStruct((n, d), F32), jax.ShapeDtypeStruct((n, d), BF16),
                            jax.ShapeDtypeStruct((1, d), F32), jax.ShapeDtypeStruct((1, LANES), F32)],
                 compiler_params=_cp("arbitrary"))(h, g, target)


def _swiglu_fwd(gate, up, name):
    n, f = gate.shape
    tb, tc = _tile(n, 1024, SUBLANES), _tile(f, 256)

    def body(g_ref, u_ref, o_ref):
        gv = g_ref[...]
        o_ref[...] = (gv * _sig(gv) * u_ref[...]).astype(BF16)

    blk = pl.BlockSpec((tb, tc), lambda i, j: (i, j))
    return _call(body, name=name, grid=(n // tb, f // tc), in_specs=[blk, blk], out_specs=blk,
                 out_shape=jax.ShapeDtypeStruct((n, f), BF16), compiler_params=_cp("parallel", "parallel"))(gate, up)


def _swiglu_bwd(gate, up, dact, name):
    n, f = gate.shape
    tb, tc = _tile(n, 1024, SUBLANES), _tile(f, 256)

    def body(g_ref, u_ref, d_ref, dg_ref, du_ref):
        gv = g_ref[...]
        s = _sig(gv)
        dv = d_ref[...]
        dg_ref[...] = (dv * u_ref[...] * s * (1.0 + gv * (1.0 - s))).astype(BF16)
        du_ref[...] = (dv * gv * s).astype(BF16)

    blk = pl.BlockSpec((tb, tc), lambda i, j: (i, j))
    return _call(body, name=name, grid=(n // tb, f // tc), in_specs=[blk, blk, blk], out_specs=[blk, blk],
                 out_shape=[jax.ShapeDtypeStruct((n, f), BF16)] * 2,
                 compiler_params=_cp("parallel", "parallel"))(gate, up, dact)


def _prep_common(prkv_ref, prkvp_ref, plo_ref, plop_ref, mur_ref, mul_ref, w0_ref, a0_ref, kk_ref, ka_ref,
                 w2_ref, a2_ref, g2_ref, seq_start, w, dlp, alp):
    z8r = jnp.zeros_like(prkvp_ref[...])
    z8l = jnp.zeros_like(plop_ref[...])
    prev_r = jnp.where(seq_start, z8r, prkvp_ref[...])
    prev_l = jnp.where(seq_start, z8l, plop_ref[...])
    p_r = prkv_ref[...]
    p_l = plo_ref[...]
    dif_r = _shift_down(p_r, prev_r, 1) - p_r
    dif_l = _shift_down(p_l, prev_l, 1) - p_l
    q_r = p_r + dif_r * mur_ref[...]
    q_l = p_l + dif_l * mul_ref[...]
    r, k, v = q_r[:, 0:w], q_r[:, w:2 * w], q_r[:, 2 * w:3 * w]
    wd, ad, gd = q_l[:, 0:dlp], q_l[:, dlp:dlp + alp], q_l[:, dlp + alp:]
    tw = jnp.tanh(wd)
    zw = w0_ref[...] + _dot(tw, w2_ref[...])
    wlog = -_softplus(-zw) - 0.5
    ew = jnp.exp(wlog)
    dec = jnp.exp(-ew)
    za = a0_ref[...] + _dot(ad, a2_ref[...])
    av = _sig(za)
    sg = _sig(gd)
    g = _dot(sg, g2_ref[...])
    return dict(dif_r=dif_r, dif_l=dif_l, r=r, k=k, v=v, ad=ad, tw=tw, zw=zw, ew=ew, dec=dec, av=av, sg=sg, g=g)


def _rwkv_prep_specs(n, tb, w, lp, t_len):
    nb8 = tb // SUBLANES
    row3 = pl.BlockSpec((tb, 3 * w), lambda i: (i, 0))
    prev3 = pl.BlockSpec((SUBLANES, 3 * w), lambda i: (jnp.maximum(i * nb8 - 1, 0), 0))
    rowl = pl.BlockSpec((tb, lp), lambda i: (i, 0))
    prevl = pl.BlockSpec((SUBLANES, lp), lambda i: (jnp.maximum(i * nb8 - 1, 0), 0))
    return row3, prev3, rowl, prevl


def _rwkv_prep_fwd(p_rkv, p_lora, prm, t_len, name):
    n, w3 = p_rkv.shape
    w = w3 // 3
    lp = p_lora.shape[1]
    dlp, alp = prm["w2"].shape[0], prm["a2"].shape[0]
    glp = lp - dlp - alp
    hp = max(w // RWKV_HEAD, LANES)
    tb = _tile(min(n, t_len), 128, SUBLANES)
    bps = t_len // tb

    def body(prkv_ref, prkvp_ref, plo_ref, plop_ref, mur_ref, mul_ref, w0_ref, a0_ref, kk_ref, ka_ref,
             w2_ref, a2_ref, g2_ref, r_o, dec_o, k_o, v_o, na_o, nb_o, g_o):
        seq_start = (pl.program_id(0) % bps) == 0
        f = _prep_common(prkv_ref, prkvp_ref, plo_ref, plop_ref, mur_ref, mul_ref, w0_ref, a0_ref, kk_ref, ka_ref,
                         w2_ref, a2_ref, g2_ref, seq_start, w, dlp, alp)
        e, et = _head_mats(w, hp)
        kk0 = f["k"] * kk_ref[...]
        inv = lax.rsqrt(jnp.maximum(_headsum(kk0 * kk0, e, et), KK_EPS))
        kk = kk0 * inv
        r_o[...] = f["r"]
        dec_o[...] = f["dec"]
        k_o[...] = f["k"] * (1.0 + (f["av"] - 1.0) * ka_ref[...])
        v_o[...] = f["v"]
        na_o[...] = -kk
        nb_o[...] = kk * f["av"]
        g_o[...] = f["g"]

    row3, prev3, rowl, prevl = _rwkv_prep_specs(n, tb, w, lp, t_len)
    c0 = lambda i: (0, 0)
    vec3 = pl.BlockSpec((1, 3 * w), c0)
    vecl = pl.BlockSpec((1, lp), c0)
    vec = pl.BlockSpec((1, w), c0)
    out = pl.BlockSpec((tb, w), lambda i: (i, 0))
    return _call(
        body, name=name, grid=(n // tb,),
        in_specs=[row3, prev3, rowl, prevl, vec3, vecl, vec, vec, vec, vec,
                  pl.BlockSpec((dlp, w), c0), pl.BlockSpec((alp, w), c0), pl.BlockSpec((glp, w), c0)],
        out_specs=[out] * 7, out_shape=[jax.ShapeDtypeStruct((n, w), F32)] * 7,
        compiler_params=_cp("parallel"),
    )(p_rkv, p_rkv, p_lora, p_lora, prm["mu_rkv"], prm["mu_lora"], prm["w0"], prm["a0"], prm["k_k"], prm["k_a"],
      prm["w2"], prm["a2"], prm["g2"])


def _rwkv_prep_bwd(p_rkv, p_lora, prm, grads, t_len, name):
    n, w3 = p_rkv.shape
    w = w3 // 3
    lp = p_lora.shape[1]
    dlp, alp = prm["w2"].shape[0], prm["a2"].shape[0]
    glp = lp - dlp - alp
    hp = max(w // RWKV_HEAD, LANES)
    tb = _tile(min(n, t_len), 64, SUBLANES)
    bps = t_len // tb

    def body(prkv_ref, prkvp_ref, plo_ref, plop_ref, mur_ref, mul_ref, w0_ref, a0_ref, kk_ref, ka_ref,
             w2_ref, a2_ref, g2_ref,
             drs_ref, drp_ref, ddec_ref, dks_ref, dkp_ref, dvs_ref, dvp_ref, dna_ref, dnb_ref, dg_ref,
             dqr_o, dql_o, dmur_o, dmul_o, dw0_o, da0_o, dkk_o, dka_o, dw2_o, da2_o, dg2_o):
        seq_start = (pl.program_id(0) % bps) == 0
        f = _prep_common(prkv_ref, prkvp_ref, plo_ref, plop_ref, mur_ref, mul_ref, w0_ref, a0_ref, kk_ref, ka_ref,
                         w2_ref, a2_ref, g2_ref, seq_start, w, dlp, alp)
        e, et = _head_mats(w, hp)
        k, av = f["k"], f["av"]
        k_k, k_a = kk_ref[...], ka_ref[...]
        kk0 = k * k_k
        n2 = _headsum(kk0 * kk0, e, et)
        inv = lax.rsqrt(jnp.maximum(n2, KK_EPS))
        kk = kk0 * inv
        dk2 = dks_ref[...] + dkp_ref[...]
        dnb = dnb_ref[...]
        dkk = dnb * av - dna_ref[...]
        dav = dnb * kk + dk2 * k * k_a
        dk = dk2 * (1.0 + (av - 1.0) * k_a)
        dka = dk2 * k * (av - 1.0)
        proj = jnp.where(n2 > KK_EPS, _headsum(dkk * kk, e, et), 0.0)
        dkk0 = inv * (dkk - kk * proj)
        dk = dk + dkk0 * k_k
        dkkp = dkk0 * k
        dgv = dg_ref[...]
        sg = f["sg"]
        dgd = _dot_nt(dgv, g2_ref[...]) * sg * (1.0 - sg)
        dza = dav * av * (1.0 - av)
        dad = _dot_nt(dza, a2_ref[...])
        dzw = ddec_ref[...] * f["dec"] * (-f["ew"]) * _sig(-f["zw"])
        tw = f["tw"]
        dwd = _dot_nt(dzw, w2_ref[...]) * (1.0 - tw * tw)
        dq_r = jnp.concatenate([drs_ref[...] + drp_ref[...], dk, dvs_ref[...] + dvp_ref[...]], axis=1)
        dq_l = jnp.concatenate([dwd, dad, dgd], axis=1)
        dqr_o[...] = dq_r
        dql_o[...] = dq_l

        @pl.when(pl.program_id(0) == 0)
        def _():
            for o in (dmur_o, dmul_o, dw0_o, da0_o, dkk_o, dka_o, dw2_o, da2_o, dg2_o):
                o[...] = jnp.zeros_like(o)

        def rsum(x):
            return jnp.sum(x, axis=0, keepdims=True)

        dmur_o[...] += rsum(dq_r * f["dif_r"])
        dmul_o[...] += rsum(dq_l * f["dif_l"])
        dw0_o[...] += rsum(dzw)
        da0_o[...] += rsum(dza)
        dkk_o[...] += rsum(dkkp)
        dka_o[...] += rsum(dka)
        dw2_o[...] += _dot_tn(tw, dzw)
        da2_o[...] += _dot_tn(f["ad"], dza)
        dg2_o[...] += _dot_tn(sg, dgv)

    row3, prev3, rowl, prevl = _rwkv_prep_specs(n, tb, w, lp, t_len)
    c0 = lambda i: (0, 0)
    vec3 = pl.BlockSpec((1, 3 * w), c0)
    vecl = pl.BlockSpec((1, lp), c0)
    vec = pl.BlockSpec((1, w), c0)
    blk = pl.BlockSpec((tb, w), lambda i: (i, 0))
    m2, ma, mg = pl.BlockSpec((dlp, w), c0), pl.BlockSpec((alp, w), c0), pl.BlockSpec((glp, w), c0)
    sds = jax.ShapeDtypeStruct
    return _call(
        body, name=name, grid=(n // tb,),
        in_specs=[row3, prev3, rowl, prevl, vec3, vecl, vec, vec, vec, vec, m2, ma, mg] + [blk] * 10,
        out_specs=[row3, rowl, vec3, vecl, vec, vec, vec, vec, m2, ma, mg],
        out_shape=[sds((n, 3 * w), F32), sds((n, lp), F32), sds((1, 3 * w), F32), sds((1, lp), F32),
                   sds((1, w), F32), sds((1, w), F32), sds((1, w), F32), sds((1, w), F32),
                   sds((dlp, w), F32), sds((alp, w), F32), sds((glp, w), F32)],
        compiler_params=_cp("arbitrary"),
    )(p_rkv, p_rkv, p_lora, p_lora, prm["mu_rkv"], prm["mu_lora"], prm["w0"], prm["a0"], prm["k_k"], prm["k_a"],
      prm["w2"], prm["a2"], prm["g2"], *grads)


def _shift_combine(dq, mu, t_len, name):
    n, c = dq.shape
    tb = _tile(min(n, t_len), 256, SUBLANES)
    bps = t_len // tb
    nb8 = tb // SUBLANES
    last8 = n // SUBLANES - 1

    def body(x_ref, nx_ref, mu_ref, o_ref):
        seq_end = (pl.program_id(0) % bps) == bps - 1
        nxt = jnp.where(seq_end, jnp.zeros_like(nx_ref[...]), nx_ref[...])
        x = x_ref[...]
        muv = mu_ref[...]
        o_ref[...] = ((1.0 - muv) * x + muv * _shift_up(x, nxt, 1)).astype(BF16)

    row = pl.BlockSpec((tb, c), lambda i: (i, 0))
    nxt = pl.BlockSpec((SUBLANES, c), lambda i: (jnp.minimum((i + 1) * nb8, last8), 0))
    vec = pl.BlockSpec((1, c), lambda i: (0, 0))
    return _call(body, name=name, grid=(n // tb,), in_specs=[row, nxt, vec], out_specs=row,
                 out_shape=jax.ShapeDtypeStruct((n, c), BF16), compiler_params=_cp("parallel"))(dq, dq, mu)


def _scan_step(s_i, a_t, w_t, b_t, k_t, v_i):
    sa = jnp.sum(s_i * a_t, axis=0, keepdims=True)
    return s_i * w_t + sa * b_t + v_i * k_t, sa


def _rwkv_scan_fwd(r, w, k, a, b, v, name):
    t_len, kd, ln = r.shape
    vh = v.shape[1]
    tc = SCAN_CHUNK
    nc = t_len // tc

    def body(r_ref, w_ref, k_ref, a_ref, b_ref, v_ref, y_ref, ck_ref, s_ref):
        @pl.when(pl.program_id(0) == 0)
        def _():
            s_ref[...] = jnp.zeros_like(s_ref)

        ck_ref[0] = s_ref[...]

        def step(t, carry):
            a_t, w_t, b_t, k_t, r_t = a_ref[t], w_ref[t], b_ref[t], k_ref[t], r_ref[t]
            for i in range(vh):
                s_new, _ = _scan_step(s_ref[i], a_t, w_t, b_t, k_t, v_ref[t, pl.ds(i, 1), :])
                s_ref[i] = s_new
                y_ref[t, pl.ds(i, 1), :] = jnp.sum(s_new * r_t, axis=0, keepdims=True)
            return carry

        lax.fori_loop(0, tc, step, 0)

    kblk = pl.BlockSpec((tc, kd, ln), lambda c: (c, 0, 0))
    vblk = pl.BlockSpec((tc, vh, ln), lambda c: (c, 0, 0))
    return _call(
        body, name=name, grid=(nc,), in_specs=[kblk] * 5 + [vblk],
        out_specs=[vblk, pl.BlockSpec((1, vh, kd, ln), lambda c: (c, 0, 0, 0))],
        out_shape=[jax.ShapeDtypeStruct((t_len, vh, ln), F32), jax.ShapeDtypeStruct((nc, vh, kd, ln), F32)],
        scratch_shapes=[pltpu.VMEM((vh, kd, ln), F32)],
        compiler_params=_cp("arbitrary"),
    )(r, w, k, a, b, v)


def _rwkv_scan_bwd(r, w, k, a, b, v, dy, ckpt, name):
    t_len, kd, ln = r.shape
    vh = v.shape[1]
    tc = SCAN_CHUNK
    nc = t_len // tc
    half = ln // 2

    def body(r_ref, w_ref, k_ref, a_ref, b_ref, v_ref, dy_ref, ck_ref,
             dr_o, dw_o, dk_o, da_o, db_o, dv_o, sbuf, ds_ref, sa_buf):
        @pl.when(pl.program_id(0) == 0)
        def _():
            ds_ref[...] = jnp.zeros_like(ds_ref)

        sbuf[0] = ck_ref[0]

        def fwd(t, carry):
            a_t, w_t, b_t, k_t = a_ref[t], w_ref[t], b_ref[t], k_ref[t]
            for i in range(vh):
                s_new, sa = _scan_step(sbuf[t, i], a_t, w_t, b_t, k_t, v_ref[t, pl.ds(i, 1), :])
                sbuf[t + 1, i] = s_new
                sa_buf[t, pl.ds(i, 1), :] = sa
            return carry

        lax.fori_loop(0, tc, fwd, 0)

        def bwd(tt, carry):
            t = tc - 1 - tt
            a_t, w_t, b_t, k_t, r_t = a_ref[t], w_ref[t], b_ref[t], k_ref[t], r_ref[t]
            z = jnp.zeros((kd, ln), F32)
            dr, dw, dk, da, db = z, z, z, z, z
            for i in range(vh):
                dy_i = dy_ref[t, pl.ds(i, 1), :]
                s_t = sbuf[t + 1, i]
                s_p = sbuf[t, i]
                d = ds_ref[i] + dy_i * r_t
                dr = dr + s_t * dy_i
                dv_o[t, pl.ds(i, 1), :] = jnp.sum(d * k_t, axis=0, keepdims=True)
                dk = dk + d * v_ref[t, pl.ds(i, 1), :]
                dsa = jnp.sum(d * b_t, axis=0, keepdims=True)
                db = db + d * sa_buf[t, pl.ds(i, 1), :]
                dw = dw + d * s_p
                da = da + s_p * dsa
                ds_ref[i] = d * w_t + dsa * a_t

            def both(x):
                return x + pltpu.roll(x, half, 1)

            dr_o[t] = both(dr)
            dw_o[t] = both(dw)
            dk_o[t] = both(dk)
            da_o[t] = both(da)
            db_o[t] = both(db)
            return carry

        lax.fori_loop(0, tc, bwd, 0)

    kblk = pl.BlockSpec((tc, kd, ln), lambda c: (nc - 1 - c, 0, 0))
    vblk = pl.BlockSpec((tc, vh, ln), lambda c: (nc - 1 - c, 0, 0))
    ksd = jax.ShapeDtypeStruct((t_len, kd, ln), F32)
    return _call(
        body, name=name, grid=(nc,),
        in_specs=[kblk] * 5 + [vblk, vblk, pl.BlockSpec((1, vh, kd, ln), lambda c: (nc - 1 - c, 0, 0, 0))],
        out_specs=[kblk] * 5 + [vblk],
        out_shape=[ksd] * 5 + [jax.ShapeDtypeStruct((t_len, vh, ln), F32)],
        scratch_shapes=[pltpu.VMEM((tc + 1, vh, kd, ln), F32), pltpu.VMEM((vh, kd, ln), F32),
                        pltpu.VMEM((tc, vh, ln), F32)],
        compiler_params=_cp("arbitrary"),
    )(r, w, k, a, b, v, dy, ckpt)


def _post_common(y_ref, r_ref, k_ref, v_ref, lng_ref, lnb_ref, rk_ref, e, et):
    y = y_ref[...]
    inv_n = 1.0 / RWKV_HEAD
    mean = _headsum(y, e, et) * inv_n
    yc = y - mean
    var = _headsum(yc * yc, e, et) * inv_n
    rstd = lax.rsqrt(var + GN_EPS)
    yh = yc * rstd
    yn = yh * lng_ref[...] + lnb_ref[...]
    bonus = _headsum(r_ref[...] * k_ref[...] * rk_ref[...], e, et)
    return yh, rstd, yn, bonus


def _rwkv_post_fwd(y, r, k, v, g, ln_g, ln_b, r_k, name):
    n, w = y.shape
    hp = max(w // RWKV_HEAD, LANES)
    tb = _tile(n, 256, SUBLANES)

    def body(y_ref, r_ref, k_ref, v_ref, g_ref, lng_ref, lnb_ref, rk_ref, o_ref):
        e, et = _head_mats(w, hp)
        _, _, yn, bonus = _post_common(y_ref, r_ref, k_ref, v_ref, lng_ref, lnb_ref, rk_ref, e, et)
        o_ref[...] = ((yn + bonus * v_ref[...]) * g_ref[...]).astype(BF16)

    blk = pl.BlockSpec((tb, w), lambda i: (i, 0))
    vec = pl.BlockSpec((1, w), lambda i: (0, 0))
    return _call(body, name=name, grid=(n // tb,), in_specs=[blk] * 5 + [vec] * 3, out_specs=blk,
                 out_shape=jax.ShapeDtypeStruct((n, w), BF16),
                 compiler_params=_cp("parallel"))(y, r, k, v, g, ln_g, ln_b, r_k)


def _rwkv_post_bwd(y, r, k, v, g, ln_g, ln_b, r_k, do_cat, name):
    n, w = y.shape
    hp = max(w // RWKV_HEAD, LANES)
    tb = _tile(n, 128, SUBLANES)

    def body(y_ref, r_ref, k_ref, v_ref, g_ref, lng_ref, lnb_ref, rk_ref, do_ref,
             dy_o, dr_o, dk_o, dv_o, dg_o, dlng_o, dlnb_o, drk_o):
        e, et = _head_mats(w, hp)
        yh, rstd, yn, bonus = _post_common(y_ref, r_ref, k_ref, v_ref, lng_ref, lnb_ref, rk_ref, e, et)
        do = do_ref[...]
        vv, rv, kv, rk = v_ref[...], r_ref[...], k_ref[...], rk_ref[...]
        dg_o[...] = do * (yn + bonus * vv)
        dz = do * g_ref[...]
        dbonus = _headsum(dz * vv, e, et)
        dv_o[...] = dz * bonus
        dr_o[...] = dbonus * kv * rk
        dk_o[...] = dbonus * rv * rk
        dyh = dz * lng_ref[...]
        inv_n = 1.0 / RWKV_HEAD
        dy_o[...] = rstd * (dyh - _headsum(dyh, e, et) * inv_n - yh * (_headsum(dyh * yh, e, et) * inv_n))

        @pl.when(pl.program_id(0) == 0)
        def _():
            for o in (dlng_o, dlnb_o, drk_o):
                o[...] = jnp.zeros_like(o)

        dlng_o[...] += jnp.sum(dz * yh, axis=0, keepdims=True)
        dlnb_o[...] += jnp.sum(dz, axis=0, keepdims=True)
        drk_o[...] += jnp.sum(dbonus * rv * kv, axis=0, keepdims=True)

    blk = pl.BlockSpec((tb, w), lambda i: (i, 0))
    vec = pl.BlockSpec((1, w), lambda i: (0, 0))
    sds = jax.ShapeDtypeStruct
    return _call(body, name=name, grid=(n // tb,), in_specs=[blk] * 5 + [vec] * 3 + [blk],
                 out_specs=[blk] * 5 + [vec] * 3,
                 out_shape=[sds((n, w), F32)] * 5 + [sds((1, w), F32)] * 3,
                 compiler_params=_cp("arbitrary"))(y, r, k, v, g, ln_g, ln_b, r_k, do_cat)


def _lru_gates(xb, prev8, gate, cw_ref, cb_ref, wr_ref, br_ref, wi_ref, bi_ref, lam_ref, is_t0):
    c = xb.shape[1]
    nblk = c // LRU_BLOCK_W
    xs = [xb] + [_shift_down(xb, prev8, j) for j in range(1, CONV_WIDTH)]
    xc = cb_ref[...]
    for j in range(CONV_WIDTH):
        xc = xc + xs[CONV_WIDTH - 1 - j] * cw_ref[pl.ds(j, 1), :]
    xcb = xc.astype(BF16)

    def blockmm(w_ref):
        return jnp.concatenate(
            [jnp.dot(xcb[:, h * LRU_BLOCK_W:(h + 1) * LRU_BLOCK_W], w_ref[h], preferred_element_type=F32)
             for h in range(nblk)], axis=1)

    rg = _sig(blockmm(wr_ref) + br_ref[...])
    ig = _sig(blockmm(wi_ref) + bi_ref[...])
    sp = _softplus(-lam_ref[...])
    la = -LRU_C * rg * sp
    av = jnp.exp(la)
    mult = jnp.where(is_t0, 1.0, jnp.sqrt(_neg_expm1(2.0 * la)))
    ge, th = _gelu_parts(gate)
    return dict(xs=xs, xc=xc, xcb=xcb, rg=rg, ig=ig, sp=sp, a=av, mult=mult, ge=ge, th=th)


def _lru_specs(tb, c, nb, rev):
    nb8 = tb // SUBLANES

    def blk_i(i):
        return nb - 1 - i if rev else i

    xb = pl.BlockSpec((tb, c), lambda b, i: (b * nb + blk_i(i), 0))
    gate = pl.BlockSpec((tb, c), lambda b, i: (b * nb + blk_i(i), 1))
    prev = pl.BlockSpec((SUBLANES, c), lambda b, i: (jnp.maximum((b * nb + blk_i(i)) * nb8 - 1, 0), 0))
    return xb, gate, prev


def _lru_fwd(p_lru, prm, t_len, name):
    n, c2 = p_lru.shape
    c = c2 // 2
    nblk = c // LRU_BLOCK_W
    tb = _tile(t_len, 256, SUBLANES)
    nb = t_len // tb
    bsz = n // t_len

    def body(xb_ref, gate_ref, prev_ref, cw_ref, cb_ref, wr_ref, br_ref, wi_ref, bi_ref, lam_ref, ng_ref,
             y_o, h_o, carry):
        i = pl.program_id(1)
        prev8 = jnp.where(i == 0, jnp.zeros_like(prev_ref[...]), prev_ref[...])
        row = lax.broadcasted_iota(jnp.int32, (tb, c), 0)
        f = _lru_gates(xb_ref[...], prev8, gate_ref[...], cw_ref, cb_ref, wr_ref, br_ref, wi_ref, bi_ref, lam_ref,
                       jnp.logical_and(i == 0, row == 0))
        acc_a = f["a"]
        acc_b = f["mult"] * f["ig"] * f["xc"]
        s = 1
        while s < tb:
            keep = row >= s
            a_sh = jnp.where(keep, pltpu.roll(acc_a, s, 0), 1.0)
            b_sh = jnp.where(keep, pltpu.roll(acc_b, s, 0), 0.0)
            acc_b = acc_a * b_sh + acc_b
            acc_a = acc_a * a_sh
            s *= 2

        @pl.when(i == 0)
        def _():
            carry[...] = jnp.zeros_like(carry)

        h = acc_b + acc_a * carry[0:1, :]
        carry[0:1, :] = h[tb - 1:tb, :]
        h_o[...] = h
        y = h * f["ge"]
        rstd = lax.rsqrt(jnp.mean(y * y, axis=-1, keepdims=True) + NORM_EPS)
        y_o[...] = (y * rstd * ng_ref[...]).astype(BF16)

    xb_s, gate_s, prev_s = _lru_specs(tb, c, nb, False)
    c0 = lambda b, i: (0, 0)
    vec = pl.BlockSpec((1, c), c0)
    wsp = pl.BlockSpec((nblk, LRU_BLOCK_W, LRU_BLOCK_W), lambda b, i: (0, 0, 0))
    out = pl.BlockSpec((tb, c), lambda b, i: (b * nb + i, 0))
    return _call(
        body, name=name, grid=(bsz, nb),
        in_specs=[xb_s, gate_s, prev_s, pl.BlockSpec((CONV_WIDTH, c), c0), vec, wsp, vec, wsp, vec, vec, vec],
        out_specs=[out, out],
        out_shape=[jax.ShapeDtypeStruct((n, c), BF16), jax.ShapeDtypeStruct((n, c), F32)],
        scratch_shapes=[pltpu.VMEM((SUBLANES, c), F32)],
        compiler_params=_cp("arbitrary", "arbitrary"),
    )(p_lru, p_lru, p_lru, prm["conv_w"], prm["conv_b"], prm["wr"], prm["br"], prm["wi"], prm["bi"],
      prm["lam"], prm["norm_g"])


def _lru_bwd(p_lru, h, do_cat, prm, t_len, name):
    n, c2 = p_lru.shape
    c = c2 // 2
    nblk = c // LRU_BLOCK_W
    tb = _tile(t_len, 128, SUBLANES)
    nb = t_len // tb
    bsz = n // t_len

    def body(xb_ref, gate_ref, prev_ref, h_ref, hprev_ref, do_ref,
             cw_ref, cb_ref, wr_ref, br_ref, wi_ref, bi_ref, lam_ref, ng_ref,
             dp_o, dcw_o, dcb_o, dwr_o, dbr_o, dwi_o, dbi_o, dlam_o, dng_o,
             a_next, g_next, dxc_next):
        b = pl.program_id(0)
        i = pl.program_id(1)
        blk = nb - 1 - i
        first = blk == 0
        prev8 = jnp.where(first, jnp.zeros_like(prev_ref[...]), prev_ref[...])
        hprev8 = jnp.where(first, jnp.zeros_like(hprev_ref[...]), hprev_ref[...])
        row = lax.broadcasted_iota(jnp.int32, (tb, c), 0)
        is_t0 = jnp.logical_and(first, row == 0)
        gate = gate_ref[...]
        f = _lru_gates(xb_ref[...], prev8, gate, cw_ref, cb_ref, wr_ref, br_ref, wi_ref, bi_ref, lam_ref, is_t0)

        @pl.when(i == 0)
        def _():
            a_next[...] = jnp.zeros_like(a_next)
            g_next[...] = jnp.zeros_like(g_next)
            dxc_next[...] = jnp.zeros_like(dxc_next)

        @pl.when(jnp.logical_and(b == 0, i == 0))
        def _():
            for o in (dcw_o, dcb_o, dwr_o, dbr_o, dwi_o, dbi_o, dlam_o, dng_o):
                o[...] = jnp.zeros_like(o)

        def rsum(x):
            return jnp.sum(x, axis=0, keepdims=True)

        hv = h_ref[...]
        hprev = _shift_down(hv, hprev8, 1)
        ge = f["ge"]
        y = hv * ge
        rstd = lax.rsqrt(jnp.mean(y * y, axis=-1, keepdims=True) + NORM_EPS)
        yh = y * rstd
        dyn = do_ref[...]
        t = dyn * ng_ref[...]
        dy = rstd * (t - yh * jnp.mean(t * yh, axis=-1, keepdims=True))
        dng_o[...] += rsum(dyn * yh)
        dgate = dy * hv * _gelu_grad(gate, f["th"])

        av = f["a"]
        acc_c = _shift_up(av, a_next[...], 1)
        acc_g = dy * ge
        s = 1
        while s < tb:
            keep = row < tb - s
            c_sh = jnp.where(keep, pltpu.roll(acc_c, tb - s, 0), 1.0)
            g_sh = jnp.where(keep, pltpu.roll(acc_g, tb - s, 0), 0.0)
            acc_g = acc_g + acc_c * g_sh
            acc_c = acc_c * c_sh
            s *= 2
        gtot = acc_g + acc_c * g_next[0:1, :]
        a_next[0:1, :] = av[0:1, :]
        g_next[0:1, :] = gtot[0:1, :]

        xc, ig, rg, mult = f["xc"], f["ig"], f["rg"], f["mult"]
        da = gtot * hprev
        dmult = gtot * ig * xc
        dig = gtot * mult * xc
        dxc = gtot * mult * ig
        da = da + jnp.where(is_t0, 0.0, -dmult * av / mult)
        dla = da * av
        drg = dla * (-LRU_C) * f["sp"]
        dlam_o[...] += rsum(dla * rg) * LRU_C * _sig(-lam_ref[...])
        dzr = drg * rg * (1.0 - rg)
        dzi = dig * ig * (1.0 - ig)
        dbr_o[...] += rsum(dzr)
        dbi_o[...] += rsum(dzi)
        dzrb, dzib = dzr.astype(BF16), dzi.astype(BF16)
        xcb = f["xcb"]
        back = []
        for hh in range(nblk):
            sl = slice(hh * LRU_BLOCK_W, (hh + 1) * LRU_BLOCK_W)
            dwr_o[hh] += _dot_tn(xcb[:, sl], dzrb[:, sl])
            dwi_o[hh] += _dot_tn(xcb[:, sl], dzib[:, sl])
            back.append(_dot_nt(dzrb[:, sl], wr_ref[hh]) + _dot_nt(dzib[:, sl], wi_ref[hh]))
        dxc = dxc + jnp.concatenate(back, axis=1)
        dcb_o[...] += rsum(dxc)
        xs = f["xs"]
        dcw_o[...] += jnp.concatenate([rsum(dxc * xs[CONV_WIDTH - 1 - j]) for j in range(CONV_WIDTH)], axis=0)
        nxt = dxc_next[...]
        dxb = dxc * cw_ref[pl.ds(CONV_WIDTH - 1, 1), :]
        for j in range(1, CONV_WIDTH):
            dxb = dxb + _shift_up(dxc, nxt, j) * cw_ref[pl.ds(CONV_WIDTH - 1 - j, 1), :]
        dxc_next[...] = dxc[0:SUBLANES, :]
        dp_o[:, 0:c] = dxb.astype(BF16)
        dp_o[:, c:2 * c] = dgate.astype(BF16)

    xb_s, gate_s, prev_s = _lru_specs(tb, c, nb, True)
    c0 = lambda b, i: (0, 0)
    vec = pl.BlockSpec((1, c), c0)
    wsp = pl.BlockSpec((nblk, LRU_BLOCK_W, LRU_BLOCK_W), lambda b, i: (0, 0, 0))
    cwsp = pl.BlockSpec((CONV_WIDTH, c), c0)
    sds = jax.ShapeDtypeStruct
    return _call(
        body, name=name, grid=(bsz, nb),
        in_specs=[xb_s, gate_s, prev_s, xb_s, prev_s, gate_s, cwsp, vec, wsp, vec, wsp, vec, vec, vec],
        out_specs=[pl.BlockSpec((tb, 2 * c), lambda b, i: (b * nb + nb - 1 - i, 0)),
                   cwsp, vec, wsp, vec, wsp, vec, vec, vec],
        out_shape=[sds((n, 2 * c), BF16), sds((CONV_WIDTH, c), F32), sds((1, c), F32),
                   sds((nblk, LRU_BLOCK_W, LRU_BLOCK_W), F32), sds((1, c), F32),
                   sds((nblk, LRU_BLOCK_W, LRU_BLOCK_W), F32), sds((1, c), F32), sds((1, c), F32), sds((1, c), F32)],
        scratch_shapes=[pltpu.VMEM((SUBLANES, c), F32)] * 3,
        compiler_params=_cp("arbitrary", "arbitrary"),
    )(p_lru, p_lru, p_lru, h, h, do_cat, prm["conv_w"], prm["conv_b"], prm["wr"], prm["br"], prm["wi"], prm["bi"],
      prm["lam"], prm["norm_g"])


def _adamw(g, w, m, v, name):
    rows, cols = g.shape
    tb = _tile(rows, 128, SUBLANES)

    def body(g_ref, w_ref, m_ref, v_ref, d_o, m_o, v_o):
        gv = g_ref[...]
        mn = ADAM_B1 * m_ref[...] + (1.0 - ADAM_B1) * gv
        vn = ADAM_B2 * v_ref[...] + (1.0 - ADAM_B2) * (gv * gv)
        m_o[...] = mn
        v_o[...] = vn
        d_o[...] = -ADAM_LR * ((mn / _BC1) / (jnp.sqrt(vn / _BC2) + ADAM_EPS) + ADAM_WD * w_ref[...])

    blk = pl.BlockSpec((tb, cols), lambda i: (i, 0))
    return _call(body, name=name, grid=(rows // tb,), in_specs=[blk] * 4, out_specs=[blk] * 3,
                 out_shape=[jax.ShapeDtypeStruct((rows, cols), F32)] * 3, compiler_params=_cp("parallel"))(g, w, m, v)


def _adamw_halves(mine, theirs, w, m, v, name):
    a, b = mine.shape
    ta = _tile(a, 128, SUBLANES)
    w, m, v = (t.reshape(2, a, b) for t in (w, m, v))

    def body(mine_ref, theirs_ref, w_ref, m_ref, v_ref, g_o, d_o, m_o, v_o):
        gv = jnp.where(pl.program_id(0) == lax.axis_index("c"), mine_ref[...], theirs_ref[...])
        mn = ADAM_B1 * m_ref[...] + (1.0 - ADAM_B1) * gv
        vn = ADAM_B2 * v_ref[...] + (1.0 - ADAM_B2) * (gv * gv)
        g_o[...] = gv
        m_o[...] = mn
        v_o[...] = vn
        d_o[...] = -ADAM_LR * ((mn / _BC1) / (jnp.sqrt(vn / _BC2) + ADAM_EPS) + ADAM_WD * w_ref[...])

    half = pl.BlockSpec((ta, b), lambda h, i: (i, 0))
    blk = pl.BlockSpec((None, ta, b), lambda h, i: (h, i, 0))
    return _call(body, name=name, grid=(2, a // ta), in_specs=[half, half, blk, blk, blk], out_specs=[blk] * 4,
                 out_shape=[jax.ShapeDtypeStruct((2, a, b), F32)] * 4,
                 compiler_params=_cp("parallel", "parallel"))(mine, theirs, w, m, v)


def _pair_sum(x4, recv, name):
    _, _, a, b = x4.shape
    ta = _tile(a, 256, SUBLANES)

    def body(x_ref, r_ref, o_ref):
        mine = x_ref[lax.axis_index("c")]
        o_ref[...] = (mine.astype(F32) + r_ref[...].astype(F32)).astype(BF16)

    return _call(
        body, name=name, grid=(4, a // ta),
        in_specs=[pl.BlockSpec((None, 2, ta, b), lambda j, i: (j, 0, i, 0)),
                  pl.BlockSpec((None, ta, b), lambda j, i: (j, i, 0))],
        out_specs=pl.BlockSpec((None, ta, b), lambda j, i: (j, i, 0)),
        out_shape=jax.ShapeDtypeStruct((4, a, b), BF16), compiler_params=_cp("parallel", "parallel"))(x4, recv)


def _chip_sum(x4, name):
    _, a, b = x4.shape
    ta = _tile(a, 256, SUBLANES)

    def body(x_ref, o_ref):
        acc = x_ref[0] + x_ref[1]
        acc = acc + x_ref[2]
        o_ref[...] = acc + x_ref[3]

    return _call(
        body, name=name, grid=(a // ta,),
        in_specs=[pl.BlockSpec((4, ta, b), lambda i: (0, i, 0))],
        out_specs=pl.BlockSpec((ta, b), lambda i: (i, 0)),
        out_shape=jax.ShapeDtypeStruct((a, b), F32), compiler_params=_cp("parallel"))(x4)


def _peer_sum(own4, parts, name):
    _, a, b = parts.shape
    ta = _tile(a, 256, SUBLANES)

    def body(own_ref, p_ref, o_ref):
        me = 2 * lax.axis_index("x") + lax.axis_index("y")
        acc = own_ref[me].astype(F32) + p_ref[0].astype(F32)
        acc = acc + p_ref[1].astype(F32)
        o_ref[...] = acc + p_ref[2].astype(F32)

    return _call(
        body, name=name, grid=(a // ta,),
        in_specs=[pl.BlockSpec((4, ta, b), lambda i: (0, i, 0)), pl.BlockSpec((3, ta, b), lambda i: (0, i, 0))],
        out_specs=pl.BlockSpec((ta, b), lambda i: (i, 0)),
        out_shape=jax.ShapeDtypeStruct((a, b), F32), compiler_params=_cp("parallel"))(own4, parts)


def _add2(x, y, name):
    rows, cols = x.shape
    tb = _tile(rows, 512, SUBLANES)

    def body(x_ref, y_ref, o_ref):
        o_ref[...] = x_ref[...] + y_ref[...]

    blk = pl.BlockSpec((tb, cols), lambda i: (i, 0))
    return _call(body, name=name, grid=(rows // tb,), in_specs=[blk, blk], out_specs=blk,
                 out_shape=jax.ShapeDtypeStruct((rows, cols), x.dtype), compiler_params=_cp("parallel"))(x, y)


_HBM = pl.BlockSpec(memory_space=pltpu.HBM)


def _place():
    x, y, c = lax.axis_index("x"), lax.axis_index("y"), lax.axis_index("c")
    chips = [(1 - x, y), (x, 1 - y), (1 - x, 1 - y)]
    return x, y, c, chips


def _comm_call(body, name, xs, out_shapes, n_sems):
    return _call(
        body, name=name, in_specs=[_HBM] * len(xs), out_specs=[_HBM] * len(out_shapes), out_shape=out_shapes,
        scratch_shapes=[pltpu.SemaphoreType.DMA((n_sems,)), pltpu.SemaphoreType.DMA((n_sems,)),
                        pltpu.SemaphoreType.DMA((len(xs),))],
    )(*xs)


def _all_gather_chips(xs, name):
    n = len(xs)
    per = 7

    def body(*refs):
        ins, outs = refs[:n], refs[n:2 * n]
        ssem, rsem, _ = refs[2 * n:]
        x, y, c, chips = _place()
        me = 2 * x + y
        sib = (x, y, 1 - c)
        sends = []
        for i in range(n):
            for j, (px, py) in enumerate(chips):
                cp = pltpu.make_async_remote_copy(
                    src_ref=ins[i].at[c], dst_ref=outs[i].at[me, c], send_sem=ssem.at[per * i + j],
                    recv_sem=rsem.at[per * i + j], device_id=(px, py, c), device_id_type=MESH)
                cp.start()
                sends.append(cp)
        for i in range(n):
            cp = pltpu.make_async_remote_copy(
                src_ref=ins[i], dst_ref=outs[i].at[me], send_sem=ssem.at[per * i + 6], recv_sem=rsem.at[per * i + 6],
                device_id=sib, device_id_type=MESH)
            cp.start()
            sends.append(cp)
        for i in range(n):
            for j, (px, py) in enumerate(chips):
                slot = outs[i].at[2 * px + py, c]
                pltpu.make_async_remote_copy(
                    src_ref=slot, dst_ref=slot, send_sem=ssem.at[per * i + j], recv_sem=rsem.at[per * i + j],
                    device_id=(px, py, c), device_id_type=MESH).wait_recv()
                cp = pltpu.make_async_remote_copy(
                    src_ref=slot, dst_ref=slot, send_sem=ssem.at[per * i + 3 + j], recv_sem=rsem.at[per * i + 3 + j],
                    device_id=sib, device_id_type=MESH)
                cp.start()
                sends.append(cp)
        for i in range(n):
            own = outs[i].at[me]
            pltpu.make_async_remote_copy(
                src_ref=own, dst_ref=own, send_sem=ssem.at[per * i + 6], recv_sem=rsem.at[per * i + 6],
                device_id=sib, device_id_type=MESH).wait_recv()
            for j, (px, py) in enumerate(chips):
                slot = outs[i].at[2 * px + py, 1 - c]
                pltpu.make_async_remote_copy(
                    src_ref=slot, dst_ref=slot, send_sem=ssem.at[per * i + 3 + j], recv_sem=rsem.at[per * i + 3 + j],
                    device_id=sib, device_id_type=MESH).wait_recv()
        for cp in sends:
            cp.wait_send()

    outs = [jax.ShapeDtypeStruct((4,) + v.shape, v.dtype) for v in xs]
    return _comm_call(body, name, xs, outs, per * n)


def _sibling_swap(xs, pick_half, name):
    n = len(xs)

    def body(*refs):
        ins, outs = refs[:n], refs[n:2 * n]
        ssem, rsem, _ = refs[2 * n:]
        x, y, c, _ = _place()
        cps = []
        for i in range(n):
            src = ins[i].at[:, 1 - c] if pick_half else ins[i]
            cp = pltpu.make_async_remote_copy(src_ref=src, dst_ref=outs[i], send_sem=ssem.at[i], recv_sem=rsem.at[i],
                                              device_id=(x, y, 1 - c), device_id_type=MESH)
            cp.start()
            cps.append(cp)
        for cp in cps:
            cp.wait()

    outs = [jax.ShapeDtypeStruct((v.shape[0],) + v.shape[2:] if pick_half else v.shape, v.dtype) for v in xs]
    return _comm_call(body, name, xs, outs, n)


def _chip_broadcast(xs, name):
    n = len(xs)

    def body(*refs):
        ins, outs = refs[:n], refs[n:2 * n]
        ssem, rsem, lsem = refs[2 * n:]
        x, y, c, chips = _place()
        me = 2 * x + y
        cps = []
        for i in range(n):
            cp = pltpu.make_async_copy(ins[i], outs[i].at[me], lsem.at[i])
            cp.start()
            cps.append(cp)
            for j, (px, py) in enumerate(chips):
                cp = pltpu.make_async_remote_copy(
                    src_ref=ins[i], dst_ref=outs[i].at[me], send_sem=ssem.at[3 * i + j], recv_sem=rsem.at[3 * i + j],
                    device_id=(px, py, c), device_id_type=MESH)
                cp.start()
                cps.append(cp)
        for i in range(n):
            for j, (px, py) in enumerate(chips):
                slot = outs[i].at[2 * px + py]
                pltpu.make_async_remote_copy(
                    src_ref=slot, dst_ref=slot, send_sem=ssem.at[3 * i + j], recv_sem=rsem.at[3 * i + j],
                    device_id=(px, py, c), device_id_type=MESH).wait_recv()
        for i in range(n):
            cps[4 * i].wait()
            for j in range(3):
                cps[4 * i + 1 + j].wait_send()

    outs = [jax.ShapeDtypeStruct((4,) + v.shape, v.dtype) for v in xs]
    return _comm_call(body, name, xs, outs, 3 * n)


def _peer_exchange(xs, name):
    n = len(xs)

    def body(*refs):
        ins, outs = refs[:n], refs[n:2 * n]
        ssem, rsem, _ = refs[2 * n:]
        x, y, c, chips = _place()
        cps = []
        for i in range(n):
            for j, (px, py) in enumerate(chips):
                cp = pltpu.make_async_remote_copy(
                    src_ref=ins[i].at[2 * px + py], dst_ref=outs[i].at[j], send_sem=ssem.at[3 * i + j],
                    recv_sem=rsem.at[3 * i + j], device_id=(px, py, c), device_id_type=MESH)
                cp.start()
                cps.append(cp)
        for cp in cps:
            cp.wait()

    outs = [jax.ShapeDtypeStruct((3,) + v.shape[1:], v.dtype) for v in xs]
    return _comm_call(body, name, xs, outs, 3 * n)


def _to_scan_k(x, bsz, t_len):
    h = x.shape[1] // RWKV_HEAD
    y = x.reshape(bsz, t_len, h, RWKV_HEAD).transpose(1, 3, 0, 2).reshape(t_len, RWKV_HEAD, bsz * h)
    return jnp.concatenate([y, y], axis=-1)


def _to_scan_v(x, bsz, t_len):
    h = x.shape[1] // RWKV_HEAD
    y = x.reshape(bsz, t_len, h, 2, RWKV_HEAD // 2).transpose(1, 4, 3, 0, 2)
    return y.reshape(t_len, RWKV_HEAD // 2, 2 * bsz * h)


def _from_scan_k(x, bsz, t_len):
    h = x.shape[2] // (2 * bsz)
    y = x[:, :, :bsz * h].reshape(t_len, RWKV_HEAD, bsz, h).transpose(2, 0, 3, 1)
    return y.reshape(bsz * t_len, h * RWKV_HEAD)


def _from_scan_v(x, bsz, t_len):
    h = x.shape[2] // (2 * bsz)
    y = x.reshape(t_len, RWKV_HEAD // 2, 2, bsz, h).transpose(3, 0, 4, 2, 1)
    return y.reshape(bsz * t_len, h * RWKV_HEAD)


def _pad_rows(x, rows):
    return jnp.pad(x, ((0, rows - x.shape[0]), (0, 0)))


def _pad_cols(x, cols):
    return jnp.pad(x, ((0, 0), (0, cols - x.shape[1])))


def _cols_from_shards(g4):
    _, r, cs = g4.shape
    return g4.transpose(1, 0, 2).reshape(r, 4 * cs)


def _cols_to_shards(g):
    r, cols = g.shape
    return g.reshape(r, 4, cols // 4).transpose(1, 0, 2)


def kernel(x, norm_mix_g, w_in, mu_shift, rwkv_w0, rwkv_w2, rwkv_a0, rwkv_a2, rwkv_g2, rwkv_k_k, rwkv_k_a, rwkv_r_k, rwkv_ln_g, rwkv_ln_b, conv_w, conv_b, lru_wr, lru_br, lru_wi, lru_bi, lru_lambda, lru_norm_g, w_out, norm_ffn_g, ffn_w_gate, ffn_w_up, ffn_w_down, norm_final_g, loss_target, m_norm_mix_g, m_w_in, m_mu_shift, m_rwkv_w0, m_rwkv_w2, m_rwkv_a0, m_rwkv_a2, m_rwkv_g2, m_rwkv_k_k, m_rwkv_k_a, m_rwkv_r_k, m_rwkv_ln_g, m_rwkv_ln_b, m_conv_w, m_conv_b, m_lru_wr, m_lru_br, m_lru_wi, m_lru_bi, m_lru_lambda, m_lru_norm_g, m_w_out, m_norm_ffn_g, m_ffn_w_gate, m_ffn_w_up, m_ffn_w_down, m_norm_final_g, v_norm_mix_g, v_w_in, v_mu_shift, v_rwkv_w0, v_rwkv_w2, v_rwkv_a0, v_rwkv_a2, v_rwkv_g2, v_rwkv_k_k, v_rwkv_k_a, v_rwkv_r_k, v_rwkv_ln_g, v_rwkv_ln_b, v_conv_w, v_conv_b, v_lru_wr, v_lru_br, v_lru_wi, v_lru_bi, v_lru_lambda, v_lru_norm_g, v_w_out, v_norm_ffn_g, v_ffn_w_gate, v_ffn_w_up, v_ffn_w_down, v_norm_final_g):
    names = ['norm_mix_g', 'w_in', 'mu_shift', 'rwkv_w0', 'rwkv_w2', 'rwkv_a0', 'rwkv_a2', 'rwkv_g2', 'rwkv_k_k',
             'rwkv_k_a', 'rwkv_r_k', 'rwkv_ln_g', 'rwkv_ln_b', 'conv_w', 'conv_b', 'lru_wr', 'lru_br', 'lru_wi',
             'lru_bi', 'lru_lambda', 'lru_norm_g', 'w_out', 'norm_ffn_g', 'ffn_w_gate', 'ffn_w_up', 'ffn_w_down',
             'norm_final_g']
    env = locals()
    wts = {k: env[k] for k in names}
    mom_m = {k: env["m_" + k] for k in names}
    mom_v = {k: env["v_" + k] for k in names}

    bsz, t_len, d = x.shape
    n = bsz * t_len
    w = rwkv_w0.shape[1]
    lw = lru_br.shape[1]
    dl, al, gl = rwkv_w2.shape[1], rwkv_a2.shape[1], rwkv_g2.shape[1]
    dlp, alp, glp = _ceil_to(dl, LANES), _ceil_to(al, LANES), _ceil_to(gl, LANES)
    lp = dlp + alp + glp
    rc = 3 * w + dl + al + gl
    chip = 2 * lax.axis_index("x") + lax.axis_index("y")

    big = ['w_in', 'w_out', 'ffn_w_gate', 'ffn_w_up', 'ffn_w_down']
    small_sh = ['rwkv_w2', 'rwkv_a2', 'rwkv_g2', 'conv_w']

    def halves(a2d):
        return a2d.reshape(2, a2d.shape[0] // 2, a2d.shape[1])

    send = [halves(wts[k][0].astype(BF16)) for k in big] + [halves(wts[k][0]) for k in small_sh]
    got = _all_gather_chips(send, "gather_weights")
    full = {}
    for k, g in zip(big + small_sh, got):
        g = g.reshape(4, g.shape[1] * g.shape[2], g.shape[3])
        if k in ('w_out', 'ffn_w_down'):
            full[k] = g.reshape(4 * g.shape[1], g.shape[2])
        else:
            full[k] = _cols_from_shards(g)
    wi = full['w_in']
    w_rkv = wi[:, :3 * w]
    w_lru = wi[:, rc:]
    o = 3 * w
    w_lora = jnp.concatenate([_pad_cols(wi[:, o:o + dl], dlp), _pad_cols(wi[:, o + dl:o + dl + al], alp),
                              _pad_cols(wi[:, o + dl + al:rc], glp)], axis=1)
    mu = mu_shift
    prm_r = dict(
        mu_rkv=mu[:, :3 * w],
        mu_lora=jnp.concatenate([_pad_cols(mu[:, o:o + dl], dlp), _pad_cols(mu[:, o + dl:o + dl + al], alp),
                                 _pad_cols(mu[:, o + dl + al:rc], glp)], axis=1),
        w0=rwkv_w0, a0=rwkv_a0, k_k=rwkv_k_k, k_a=rwkv_k_a,
        w2=_pad_rows(full['rwkv_w2'], dlp).astype(BF16), a2=_pad_rows(full['rwkv_a2'], alp).astype(BF16),
        g2=_pad_rows(full['rwkv_g2'], glp).astype(BF16))
    ln_g, ln_b, r_k = rwkv_ln_g, rwkv_ln_b, rwkv_r_k.reshape(1, w)
    prm_l = dict(conv_w=full['conv_w'], conv_b=conv_b, wr=lru_wr[0].astype(BF16), br=lru_br,
                 wi=lru_wi[0].astype(BF16), bi=lru_bi, lam=lru_lambda, norm_g=lru_norm_g)
    g_final = norm_final_g.reshape(1, d)

    x2 = x.reshape(n, d)
    u1 = _rmsnorm_fwd(x2, norm_mix_g, "norm_mix")
    p_rkv = _mm(u1, w_rkv, name="in_rkv")
    p_lru = _mm(u1, w_lru, name="in_lru")
    p_lora = _mm(u1, w_lora, name="in_lora")
    r_t, dec_t, k_t, v_t, na_t, nb_t, g_t = _rwkv_prep_fwd(p_rkv, p_lora, prm_r, t_len, "rwkv_prep")
    sk = [_to_scan_k(a, bsz, t_len) for a in (r_t, dec_t, k_t, na_t, nb_t)]
    sv = _to_scan_v(v_t, bsz, t_len)
    y_s, ckpt = _rwkv_scan_fwd(*sk, sv, name="rwkv_scan")
    y_t = _from_scan_v(y_s, bsz, t_len)
    y_a = _rwkv_post_fwd(y_t, r_t, k_t, v_t, g_t, ln_g, ln_b, r_k, "rwkv_post")
    y_b, h_lru = _lru_fwd(p_lru, prm_l, t_len, "lru_fwd")
    wo = full['w_out']
    h1 = _mm(y_a, wo[:w], name="out_a", res=x2)
    h1 = _mm(y_b, wo[w:], name="out_b", res=h1)
    u2 = _rmsnorm_fwd(h1, norm_ffn_g, "norm_ffn")
    wg, wu, wd = full['ffn_w_gate'], full['ffn_w_up'], full['ffn_w_down']
    ffc = (1024, 256, 4096)
    gate = _mm(u2, wg, name="ffn_gate", caps=ffc)
    up = _mm(u2, wu, name="ffn_up", caps=ffc)
    act = _swiglu_fwd(gate, up, "swiglu")
    h2 = _mm(act, wd, name="ffn_down", res=h1, caps=(512, 256, 11008))

    dh2, dh2b, g_norm_final, loss_vec = _loss_head(h2, g_final, loss_target.reshape(n, d), "loss_head")
    loss = lax.psum(loss_vec[0, 0], ("x", "y", "c"))
    dact = _mm(dh2b, wd, name="d_act", tb=True, caps=(1024, 256, 4096))
    dgate, dup = _swiglu_bwd(gate, up, dact, "swiglu_bwd")
    gw_down = _mm(act, dh2b, name="dw_down", ta=True, out_dtype=BF16, n_outer=True, caps=(256, 1024, 4096))
    gw_gate = _mm(u2, dgate, name="dw_gate", ta=True, out_dtype=BF16, caps=(1024, 256, 4096))
    gw_up = _mm(u2, dup, name="dw_up", ta=True, out_dtype=BF16, caps=(1024, 256, 4096))
    du2 = _mm(dgate, wg, name="du2_gate", tb=True, caps=(512, 256, 11008))
    du2 = _mm(dup, wu, name="du2_up", tb=True, res=du2, caps=(512, 256, 11008))
    dh1, dh1b, g_norm_ffn = _rmsnorm_bwd(du2, h1, norm_ffn_g, dh2, "norm_ffn_bwd")
    dcat = _mm(dh1b, wo, name="d_cat", tb=True)
    gw_out = jnp.concatenate([_mm(y_a, dh1b, name="dw_out_a", ta=True, out_dtype=BF16),
                              _mm(y_b, dh1b, name="dw_out_b", ta=True, out_dtype=BF16)], axis=0)
    (dp_lru, g_conv_w, g_conv_b, g_wr, g_br, g_wi, g_bi, g_lam, g_lng) = _lru_bwd(
        p_lru, h_lru, dcat, prm_l, t_len, "lru_bwd")
    dy_t, dr_p, dk_p, dv_p, dg_t, g_ln_g, g_ln_b, g_r_k = _rwkv_post_bwd(
        y_t, r_t, k_t, v_t, g_t, ln_g, ln_b, r_k, dcat, "rwkv_post_bwd")
    dr_s, dw_s, dk_s, da_s, db_s, dv_s = _rwkv_scan_bwd(*sk, sv, _to_scan_v(dy_t, bsz, t_len), ckpt,
                                                        name="rwkv_scan_bwd")
    grads = [_from_scan_k(dr_s, bsz, t_len), dr_p, _from_scan_k(dw_s, bsz, t_len), _from_scan_k(dk_s, bsz, t_len),
             dk_p, _from_scan_v(dv_s, bsz, t_len), dv_p, _from_scan_k(da_s, bsz, t_len),
             _from_scan_k(db_s, bsz, t_len), dg_t]
    (dq_r, dq_l, g_mu_r, g_mu_l, g_w0, g_a0, g_kk, g_ka, g_w2, g_a2, g_g2) = _rwkv_prep_bwd(
        p_rkv, p_lora, prm_r, grads, t_len, "rwkv_prep_bwd")
    dp_rkv = _shift_combine(dq_r, prm_r["mu_rkv"], t_len, "shift_bwd_rkv")
    dp_lora = _shift_combine(dq_l, prm_r["mu_lora"], t_len, "shift_bwd_lora")
    du1 = _mm(dp_rkv, w_rkv, name="du1_rkv", tb=True)
    du1 = _mm(dp_lru, w_lru, name="du1_lru", tb=True, res=du1)
    du1 = _mm(dp_lora, w_lora, name="du1_lora", tb=True, res=du1)
    gx, _, g_norm_mix = _rmsnorm_bwd(du1, x2, norm_mix_g, dh1, "norm_mix_bwd")
    gi_rkv = _mm(u1, dp_rkv, name="dw_in_rkv", ta=True, out_dtype=BF16)
    gi_lru = _mm(u1, dp_lru, name="dw_in_lru", ta=True, out_dtype=BF16)
    gi_lora = _mm(u1, dp_lora, name="dw_in_lora", ta=True, out_dtype=BF16)
    gw_in = jnp.concatenate([gi_rkv, gi_lora[:, :dl], gi_lora[:, dlp:dlp + al], gi_lora[:, dlp + alp:dlp + alp + gl],
                             gi_lru], axis=1)

    def shards_rows(g):
        return g.reshape(4, 2, g.shape[0] // 8, g.shape[1])

    def shards_cols(g):
        s = _cols_to_shards(g)
        return s.reshape(4, 2, s.shape[1] // 2, s.shape[2])

    g4 = [shards_cols(gw_in), shards_rows(gw_out), shards_cols(gw_gate), shards_cols(gw_up), shards_rows(gw_down)]
    sib = _sibling_swap(g4, True, "grad_sibling")
    pair = [_pair_sum(a4, s, "grad_pair_sum_%d" % i) for i, (a4, s) in enumerate(zip(g4, sib))]
    parts = _peer_exchange(pair, "grad_chips")
    mine = [_peer_sum(own4, p3, "grad_chip_sum_%d" % i) for i, (own4, p3) in enumerate(zip(pair, parts))]
    theirs = _sibling_swap(mine, False, "grad_share")

    g_mu = jnp.concatenate([g_mu_r, g_mu_l[:, :dl], g_mu_l[:, dlp:dlp + al], g_mu_l[:, dlp + alp:dlp + alp + gl]],
                           axis=1)
    small = dict(norm_mix_g=g_norm_mix, mu_shift=g_mu, rwkv_w0=g_w0, rwkv_w2=g_w2[:dl], rwkv_a0=g_a0,
                 rwkv_a2=g_a2[:al], rwkv_g2=g_g2[:gl], rwkv_k_k=g_kk, rwkv_k_a=g_ka, rwkv_r_k=g_r_k,
                 rwkv_ln_g=g_ln_g, rwkv_ln_b=g_ln_b, conv_w=g_conv_w, conv_b=g_conv_b, lru_wr=g_wr, lru_br=g_br,
                 lru_wi=g_wi, lru_bi=g_bi, lru_lambda=g_lam, lru_norm_g=g_lng, norm_ffn_g=g_norm_ffn,
                 norm_final_g=g_norm_final)
    small_names = list(small)
    sizes = [small[k].size for k in small_names]
    total = sum(sizes)
    padded = _ceil_to(total, 512 * LANES)

    def pack(arrs):
        flat = jnp.concatenate([a.reshape(-1) for a in arrs] + [jnp.zeros((padded - sum(a.size for a in arrs),), F32)])
        return flat.reshape(padded // LANES, LANES)

    packed = pack([small[k] for k in small_names])
    other = _sibling_swap([packed], False, "small_sibling")[0]
    chip_sum = _add2(packed, other, "small_pair_sum")
    all4 = _chip_broadcast([chip_sum], "small_chips")[0]
    red = _chip_sum(all4, "small_chip_sum").reshape(-1)
    small_g = {}
    off = 0
    for k, sz in zip(small_names, sizes):
        full_g = red[off:off + sz].reshape(small[k].shape)
        off += sz
        if k in small_sh:
            cs = full_g.shape[1] // 4
            full_g = lax.dynamic_slice_in_dim(full_g, chip * cs, cs, axis=1)
        small_g[k] = full_g.reshape(wts[k].shape)

    grad_w, delta_w, new_m, new_v = {}, {}, {}, {}
    for k, g_mine, g_theirs in zip(big, mine, theirs):
        res = _adamw_halves(g_mine, g_theirs, wts[k][0], mom_m[k][0], mom_v[k][0], "adamw_" + k)
        grad_w[k], delta_w[k], new_m[k], new_v[k] = (t.reshape(wts[k].shape) for t in res)
    lsizes = [small_g[k].size for k in small_names]
    lpad = _ceil_to(sum(lsizes), 128 * LANES)

    def lpack(tree):
        arrs = [tree[k].reshape(-1) for k in small_names]
        flat = jnp.concatenate(arrs + [jnp.zeros((lpad - sum(lsizes),), F32)])
        return flat.reshape(lpad // LANES, LANES)

    dlt, mn, vn = _adamw(lpack(small_g), lpack(wts), lpack(mom_m), lpack(mom_v), "adamw_small")
    off = 0
    for k, sz in zip(small_names, lsizes):
        shp = wts[k].shape
        grad_w[k] = small_g[k]
        delta_w[k] = dlt.reshape(-1)[off:off + sz].reshape(shp)
        new_m[k] = mn.reshape(-1)[off:off + sz].reshape(shp)
        new_v[k] = vn.reshape(-1)[off:off + sz].reshape(shp)
        off += sz

    return (loss, gx.reshape(bsz, t_len, d), *[grad_w[k] for k in names], *[delta_w[k] for k in names],
            *[new_m[k] for k in names], *[new_v[k] for k in names])
```

```python
import jax
import jax.numpy as jnp
from jax import lax
from jax.experimental import pallas as pl
from jax.experimental.pallas import tpu as pltpu

F32 = jnp.float32
BF16 = jnp.bfloat16
MESH = pl.DeviceIdType.MESH
_call = pl.pallas_call

V7X_VMEM_LIMIT = 56 * 1024 * 1024
LANES = 128
SUBLANES = 8

RWKV_HEAD = 64
LRU_BLOCK_W = 128
CONV_WIDTH = 4
LRU_C = 8.0
NORM_EPS = 1e-6
GN_EPS = 64e-5
KK_EPS = 1e-24
SCAN_CHUNK = 16

ADAM_LR = 0.001
ADAM_B1 = 0.9
ADAM_B2 = 0.999
ADAM_EPS = 1e-08
ADAM_WD = 0.01
ADAM_STEP = 10
_BC1 = 1.0 - ADAM_B1 ** ADAM_STEP
_BC2 = 1.0 - ADAM_B2 ** ADAM_STEP

_HI = lax.Precision.HIGHEST


def _cp(*sem):
    return pltpu.CompilerParams(dimension_semantics=tuple(sem), vmem_limit_bytes=V7X_VMEM_LIMIT)


def _tile(n, cap, unit=LANES):
    if n <= cap:
        return n
    best = None
    d = unit
    while d <= cap:
        if n % d == 0:
            best = d
        d += unit
    return n if best is None else best


def _ceil_to(n, m):
    return -(-n // m) * m


def _sig(x):
    return 1.0 / (1.0 + jnp.exp(-x))


def _log1p(x):
    return jnp.where(x < 0.01, x * (1.0 - x * (0.5 - x * (1.0 / 3.0))), jnp.log(1.0 + x))


def _softplus(x):
    return jnp.maximum(x, 0.0) + _log1p(jnp.exp(-jnp.abs(x)))


def _neg_expm1(x):
    small = -x * (1.0 + x * (0.5 + x * (1.0 / 6.0)))
    return jnp.where(x > -0.01, small, 1.0 - jnp.exp(x))


_GELU_K = 0.7978845608028654
_GELU_C = 0.044715


def _gelu_parts(x):
    th = jnp.tanh(_GELU_K * (x + _GELU_C * x * x * x))
    return 0.5 * x * (1.0 + th), th


def _gelu_grad(x, th):
    return 0.5 * (1.0 + th) + 0.5 * x * (1.0 - th * th) * _GELU_K * (1.0 + 3.0 * _GELU_C * x * x)


def _shift_down(x, prev8, j):
    tb = x.shape[0]
    xr = pltpu.roll(x, j, 0)
    pr = pltpu.roll(prev8, j, 0)
    row = lax.broadcasted_iota(jnp.int32, prev8.shape, 0)
    first = jnp.where(row < j, pr, xr[0:SUBLANES])
    if tb == SUBLANES:
        return first
    return jnp.concatenate([first, xr[SUBLANES:]], axis=0)


def _shift_up(x, next8, j):
    tb = x.shape[0]
    xr = pltpu.roll(x, tb - j, 0)
    nr = pltpu.roll(next8, SUBLANES - j, 0)
    row = lax.broadcasted_iota(jnp.int32, next8.shape, 0)
    last = jnp.where(row >= SUBLANES - j, nr, xr[tb - SUBLANES:])
    if tb == SUBLANES:
        return last
    return jnp.concatenate([xr[:tb - SUBLANES], last], axis=0)


def _head_mats(width, heads_pad):
    e = (lax.broadcasted_iota(jnp.int32, (width, heads_pad), 0) // RWKV_HEAD
         == lax.broadcasted_iota(jnp.int32, (width, heads_pad), 1)).astype(F32)
    et = (lax.broadcasted_iota(jnp.int32, (heads_pad, width), 1) // RWKV_HEAD
          == lax.broadcasted_iota(jnp.int32, (heads_pad, width), 0)).astype(F32)
    return e, et


def _headsum(x, e, et):
    s = jnp.dot(x, e, preferred_element_type=F32, precision=_HI)
    return jnp.dot(s, et, preferred_element_type=F32, precision=_HI)


def _dot(a, b):
    return jnp.dot(a.astype(BF16), b.astype(BF16), preferred_element_type=F32)


def _dot_tn(a, b):
    return lax.dot_general(a.astype(BF16), b.astype(BF16), (((0,), (0,)), ((), ())), preferred_element_type=F32)


def _dot_nt(a, b):
    return lax.dot_general(a.astype(BF16), b.astype(BF16), (((1,), (1,)), ((), ())), preferred_element_type=F32)


def _mm(a, b, *, name, ta=False, tb=False, out_dtype=F32, res=None, n_outer=False, caps=(1024, 512, 4096)):
    m = a.shape[1] if ta else a.shape[0]
    kd = a.shape[0] if ta else a.shape[1]
    n = b.shape[0] if tb else b.shape[1]
    assert kd == (b.shape[1] if tb else b.shape[0])
    tm, tn, tk = _tile(m, caps[0]), _tile(n, caps[1]), _tile(kd, caps[2])
    gm, gn, gk = m // tm, n // tn, kd // tk
    dims = (((0 if ta else 1,), (1 if tb else 0,)), ((), ()))

    def ij(g0, g1):
        return (g1, g0) if n_outer else (g0, g1)

    def a_map(g0, g1, k):
        i, _ = ij(g0, g1)
        return (k, i) if ta else (i, k)

    def b_map(g0, g1, k):
        _, j = ij(g0, g1)
        return (j, k) if tb else (k, j)

    def o_map(g0, g1, k):
        return ij(g0, g1)

    has_res = res is not None

    def body(*refs):
        a_ref, b_ref = refs[0], refs[1]
        res_ref = refs[2] if has_res else None
        o_ref = refs[3] if has_res else refs[2]
        prod = lax.dot_general(a_ref[...], b_ref[...], dims, preferred_element_type=F32)

        def finish(acc):
            if has_res:
                acc = acc + res_ref[...]
            o_ref[...] = acc.astype(out_dtype)

        if gk == 1:
            finish(prod)
        else:
            acc_ref = refs[-1]
            k = pl.program_id(2)

            @pl.when(k == 0)
            def _():
                acc_ref[...] = prod

            @pl.when(k > 0)
            def _():
                acc_ref[...] += prod

            @pl.when(k == gk - 1)
            def _():
                finish(acc_ref[...])

    in_specs = [pl.BlockSpec((tk, tm) if ta else (tm, tk), a_map),
                pl.BlockSpec((tn, tk) if tb else (tk, tn), b_map)]
    args = [a, b]
    if has_res:
        in_specs.append(pl.BlockSpec((tm, tn), o_map))
        args.append(res)
    grid = (gn, gm, gk) if n_outer else (gm, gn, gk)
    return _call(
        body, name=name, grid=grid, in_specs=in_specs,
        out_specs=pl.BlockSpec((tm, tn), o_map),
        out_shape=jax.ShapeDtypeStruct((m, n), out_dtype),
        scratch_shapes=[pltpu.VMEM((tm, tn), F32)] if gk > 1 else [],
        compiler_params=_cp("parallel", "parallel", "arbitrary"),
    )(*args)


def _rmsnorm_fwd(x, g, name):
    n, d = x.shape
    tb = _tile(n, 256, SUBLANES)

    def body(x_ref, g_ref, u_ref):
        xv = x_ref[...]
        rstd = lax.rsqrt(jnp.mean(xv * xv, axis=-1, keepdims=True) + NORM_EPS)
        u_ref[...] = (xv * rstd * g_ref[...]).astype(BF16)

    row = pl.BlockSpec((tb, d), lambda i: (i, 0))
    vec = pl.BlockSpec((1, d), lambda i: (0, 0))
    return _call(body, name=name, grid=(n // tb,), in_specs=[row, vec], out_specs=row,
                 out_shape=jax.ShapeDtypeStruct((n, d), BF16), compiler_params=_cp("parallel"))(x, g)


def _rmsnorm_bwd(du, x, g, dres, name):
    n, d = x.shape
    tb = _tile(n, 256, SUBLANES)

    def body(du_ref, x_ref, g_ref, dres_ref, dx_ref, dxb_ref, dg_ref):
        xv = x_ref[...]
        rstd = lax.rsqrt(jnp.mean(xv * xv, axis=-1, keepdims=True) + NORM_EPS)
        xh = xv * rstd
        duv = du_ref[...]
        t = duv * g_ref[...]
        dx = dres_ref[...] + rstd * (t - xh * jnp.mean(t * xh, axis=-1, keepdims=True))
        dx_ref[...] = dx
        dxb_ref[...] = dx.astype(BF16)

        @pl.when(pl.program_id(0) == 0)
        def _():
            dg_ref[...] = jnp.zeros_like(dg_ref)

        dg_ref[...] += jnp.sum(duv * xh, axis=0, keepdims=True)

    row = pl.BlockSpec((tb, d), lambda i: (i, 0))
    vec = pl.BlockSpec((1, d), lambda i: (0, 0))
    return _call(body, name=name, grid=(n // tb,), in_specs=[row, row, vec, row], out_specs=[row, row, vec],
                 out_shape=[jax.ShapeDtypeStruct((n, d), F32), jax.ShapeDtypeStruct((n, d), BF16),
                            jax.ShapeDtypeStruct((1, d), F32)],
                 compiler_params=_cp("arbitrary"))(du, x, g, dres)


def _loss_head(h, g, target, name):
    n, d = h.shape
    tb = _tile(n, 256, SUBLANES)

    def body(h_ref, g_ref, t_ref, dh_ref, dhb_ref, dg_ref, loss_ref):
        hv = h_ref[...]
        gv = g_ref[...]
        rstd = lax.rsqrt(jnp.mean(hv * hv, axis=-1, keepdims=True) + NORM_EPS)
        hh = hv * rstd
        err = hh * gv - t_ref[...]
        dy = err * (1.0 / d)
        dhh = dy * gv
        dh = rstd * (dhh - hh * jnp.mean(dhh * hh, axis=-1, keepdims=True))
        dh_ref[...] = dh
        dhb_ref[...] = dh.astype(BF16)

        @pl.when(pl.program_id(0) == 0)
        def _():
            dg_ref[...] = jnp.zeros_like(dg_ref)
            loss_ref[...] = jnp.zeros_like(loss_ref)

        dg_ref[...] += jnp.sum(dy * hh, axis=0, keepdims=True)
        loss_ref[...] += jnp.sum(err * err) * (0.5 / d)

    row = pl.BlockSpec((tb, d), lambda i: (i, 0))
    vec = pl.BlockSpec((1, d), lambda i: (0, 0))
    lvec = pl.BlockSpec((1, LANES), lambda i: (0, 0))
    return _call(body, name=name, grid=(n // tb,), in_specs=[row, vec, row], out_specs=[row, row, vec, lvec],
                 out_shape=[jax.ShapeDtypeStruct((n, d), F32), jax.ShapeDtypeStruct((n, d), BF16),
                            jax.ShapeDtypeStruct((1, d), F32), jax.ShapeDtypeStruct((1, LANES), F32)],
                 compiler_params=_cp("arbitrary"))(h, g, target)


def _swiglu_fwd(gate, up, name):
    n, f = gate.shape
    tb, tc = _tile(n, 1024, SUBLANES), _tile(f, 256)

    def body(g_ref, u_ref, o_ref):
        gv = g_ref[...]
        o_ref[...] = (gv * _sig(gv) * u_ref[...]).astype(BF16)

    blk = pl.BlockSpec((tb, tc), lambda i, j: (i, j))
    return _call(body, name=name, grid=(n // tb, f // tc), in_specs=[blk, blk], out_specs=blk,
                 out_shape=jax.ShapeDtypeStruct((n, f), BF16), compiler_params=_cp("parallel", "parallel"))(gate, up)


def _swiglu_bwd(gate, up, dact, name):
    n, f = gate.shape
    tb, tc = _tile(n, 1024, SUBLANES), _tile(f, 256)

    def body(g_ref, u_ref, d_ref, dg_ref, du_ref):
        gv = g_ref[...]
        s = _sig(gv)
        dv = d_ref[...]
        dg_ref[...] = (dv * u_ref[...] * s * (1.0 + gv * (1.0 - s))).astype(BF16)
        du_ref[...] = (dv * gv * s).astype(BF16)

    blk = pl.BlockSpec((tb, tc), lambda i, j: (i, j))
    return _call(body, name=name, grid=(n // tb, f // tc), in_specs=[blk, blk, blk], out_specs=[blk, blk],
                 out_shape=[jax.ShapeDtypeStruct((n, f), BF16)] * 2,
                 compiler_params=_cp("parallel", "parallel"))(gate, up, dact)


def _prep_common(prkv_ref, prkvp_ref, plo_ref, plop_ref, mur_ref, mul_ref, w0_ref, a0_ref, kk_ref, ka_ref,
                 w2_ref, a2_ref, g2_ref, seq_start, w, dlp, alp):
    z8r = jnp.zeros_like(prkvp_ref[...])
    z8l = jnp.zeros_like(plop_ref[...])
    prev_r = jnp.where(seq_start, z8r, prkvp_ref[...])
    prev_l = jnp.where(seq_start, z8l, plop_ref[...])
    p_r = prkv_ref[...]
    p_l = plo_ref[...]
    dif_r = _shift_down(p_r, prev_r, 1) - p_r
    dif_l = _shift_down(p_l, prev_l, 1) - p_l
    q_r = p_r + dif_r * mur_ref[...]
    q_l = p_l + dif_l * mul_ref[...]
    r, k, v = q_r[:, 0:w], q_r[:, w:2 * w], q_r[:, 2 * w:3 * w]
    wd, ad, gd = q_l[:, 0:dlp], q_l[:, dlp:dlp + alp], q_l[:, dlp + alp:]
    tw = jnp.tanh(wd)
    zw = w0_ref[...] + _dot(tw, w2_ref[...])
    wlog = -_softplus(-zw) - 0.5
    ew = jnp.exp(wlog)
    dec = jnp.exp(-ew)
    za = a0_ref[...] + _dot(ad, a2_ref[...])
    av = _sig(za)
    sg = _sig(gd)
    g = _dot(sg, g2_ref[...])
    return dict(dif_r=dif_r, dif_l=dif_l, r=r, k=k, v=v, ad=ad, tw=tw, zw=zw, ew=ew, dec=dec, av=av, sg=sg, g=g)


def _rwkv_prep_specs(n, tb, w, lp, t_len):
    nb8 = tb // SUBLANES
    row3 = pl.BlockSpec((tb, 3 * w), lambda i: (i, 0))
    prev3 = pl.BlockSpec((SUBLANES, 3 * w), lambda i: (jnp.maximum(i * nb8 - 1, 0), 0))
    rowl = pl.BlockSpec((tb, lp), lambda i: (i, 0))
    prevl = pl.BlockSpec((SUBLANES, lp), lambda i: (jnp.maximum(i * nb8 - 1, 0), 0))
    return row3, prev3, rowl, prevl


def _rwkv_prep_fwd(p_rkv, p_lora, prm, t_len, name):
    n, w3 = p_rkv.shape
    w = w3 // 3
    lp = p_lora.shape[1]
    dlp, alp = prm["w2"].shape[0], prm["a2"].shape[0]
    glp = lp - dlp - alp
    hp = max(w // RWKV_HEAD, LANES)
    tb = _tile(min(n, t_len), 128, SUBLANES)
    bps = t_len // tb

    def body(prkv_ref, prkvp_ref, plo_ref, plop_ref, mur_ref, mul_ref, w0_ref, a0_ref, kk_ref, ka_ref,
             w2_ref, a2_ref, g2_ref, r_o, dec_o, k_o, v_o, na_o, nb_o, g_o):
        seq_start = (pl.program_id(0) % bps) == 0
        f = _prep_common(prkv_ref, prkvp_ref, plo_ref, plop_ref, mur_ref, mul_ref, w0_ref, a0_ref, kk_ref, ka_ref,
                         w2_ref, a2_ref, g2_ref, seq_start, w, dlp, alp)
        e, et = _head_mats(w, hp)
        kk0 = f["k"] * kk_ref[...]
        inv = lax.rsqrt(jnp.maximum(_headsum(kk0 * kk0, e, et), KK_EPS))
        kk = kk0 * inv
        r_o[...] = f["r"]
        dec_o[...] = f["dec"]
        k_o[...] = f["k"] * (1.0 + (f["av"] - 1.0) * ka_ref[...])
        v_o[...] = f["v"]
        na_o[...] = -kk
        nb_o[...] = kk * f["av"]
        g_o[...] = f["g"]

    row3, prev3, rowl, prevl = _rwkv_prep_specs(n, tb, w, lp, t_len)
    c0 = lambda i: (0, 0)
    vec3 = pl.BlockSpec((1, 3 * w), c0)
    vecl = pl.BlockSpec((1, lp), c0)
    vec = pl.BlockSpec((1, w), c0)
    out = pl.BlockSpec((tb, w), lambda i: (i, 0))
    return _call(
        body, name=name, grid=(n // tb,),
        in_specs=[row3, prev3, rowl, prevl, vec3, vecl, vec, vec, vec, vec,
                  pl.BlockSpec((dlp, w), c0), pl.BlockSpec((alp, w), c0), pl.BlockSpec((glp, w), c0)],
        out_specs=[out] * 7, out_shape=[jax.ShapeDtypeStruct((n, w), F32)] * 7,
        compiler_params=_cp("parallel"),
    )(p_rkv, p_rkv, p_lora, p_lora, prm["mu_rkv"], prm["mu_lora"], prm["w0"], prm["a0"], prm["k_k"], prm["k_a"],
      prm["w2"], prm["a2"], prm["g2"])


def _rwkv_prep_bwd(p_rkv, p_lora, prm, grads, t_len, name):
    n, w3 = p_rkv.shape
    w = w3 // 3
    lp = p_lora.shape[1]
    dlp, alp = prm["w2"].shape[0], prm["a2"].shape[0]
    glp = lp - dlp - alp
    hp = max(w // RWKV_HEAD, LANES)
    tb = _tile(min(n, t_len), 64, SUBLANES)
    bps = t_len // tb

    def body(prkv_ref, prkvp_ref, plo_ref, plop_ref, mur_ref, mul_ref, w0_ref, a0_ref, kk_ref, ka_ref,
             w2_ref, a2_ref, g2_ref,
             drs_ref, drp_ref, ddec_ref, dks_ref, dkp_ref, dvs_ref, dvp_ref, dna_ref, dnb_ref, dg_ref,
             dqr_o, dql_o, dmur_o, dmul_o, dw0_o, da0_o, dkk_o, dka_o, dw2_o, da2_o, dg2_o):
        seq_start = (pl.program_id(0) % bps) == 0
        f = _prep_common(prkv_ref, prkvp_ref, plo_ref, plop_ref, mur_ref, mul_ref, w0_ref, a0_ref, kk_ref, ka_ref,
                         w2_ref, a2_ref, g2_ref, seq_start, w, dlp, alp)
        e, et = _head_mats(w, hp)
        k, av = f["k"], f["av"]
        k_k, k_a = kk_ref[...], ka_ref[...]
        kk0 = k * k_k
        n2 = _headsum(kk0 * kk0, e, et)
        inv = lax.rsqrt(jnp.maximum(n2, KK_EPS))
        kk = kk0 * inv
        dk2 = dks_ref[...] + dkp_ref[...]
        dnb = dnb_ref[...]
        dkk = dnb * av - dna_ref[...]
        dav = dnb * kk + dk2 * k * k_a
        dk = dk2 * (1.0 + (av - 1.0) * k_a)
        dka = dk2 * k * (av - 1.0)
        proj = jnp.where(n2 > KK_EPS, _headsum(dkk * kk, e, et), 0.0)
        dkk0 = inv * (dkk - kk * proj)
        dk = dk + dkk0 * k_k
        dkkp = dkk0 * k
        dgv = dg_ref[...]
        sg = f["sg"]
        dgd = _dot_nt(dgv, g2_ref[...]) * sg * (1.0 - sg)
        dza = dav * av * (1.0 - av)
        dad = _dot_nt(dza, a2_ref[...])
        dzw = ddec_ref[...] * f["dec"] * (-f["ew"]) * _sig(-f["zw"])
        tw = f["tw"]
        dwd = _dot_nt(dzw, w2_ref[...]) * (1.0 - tw * tw)
        dq_r = jnp.concatenate([drs_ref[...] + drp_ref[...], dk, dvs_ref[...] + dvp_ref[...]], axis=1)
        dq_l = jnp.concatenate([dwd, dad, dgd], axis=1)
        dqr_o[...] = dq_r
        dql_o[...] = dq_l

        @pl.when(pl.program_id(0) == 0)
        def _():
            for o in (dmur_o, dmul_o, dw0_o, da0_o, dkk_o, dka_o, dw2_o, da2_o, dg2_o):
                o[...] = jnp.zeros_like(o)

        def rsum(x):
            return jnp.sum(x, axis=0, keepdims=True)

        dmur_o[...] += rsum(dq_r * f["dif_r"])
        dmul_o[...] += rsum(dq_l * f["dif_l"])
        dw0_o[...] += rsum(dzw)
        da0_o[...] += rsum(dza)
        dkk_o[...] += rsum(dkkp)
        dka_o[...] += rsum(dka)
        dw2_o[...] += _dot_tn(tw, dzw)
        da2_o[...] += _dot_tn(f["ad"], dza)
        dg2_o[...] += _dot_tn(sg, dgv)

    row3, prev3, rowl, prevl = _rwkv_prep_specs(n, tb, w, lp, t_len)
    c0 = lambda i: (0, 0)
    vec3 = pl.BlockSpec((1, 3 * w), c0)
    vecl = pl.BlockSpec((1, lp), c0)
    vec = pl.BlockSpec((1, w), c0)
    blk = pl.BlockSpec((tb, w), lambda i: (i, 0))
    m2, ma, mg = pl.BlockSpec((dlp, w), c0), pl.BlockSpec((alp, w), c0), pl.BlockSpec((glp, w), c0)
    sds = jax.ShapeDtypeStruct
    return _call(
        body, name=name, grid=(n // tb,),
        in_specs=[row3, prev3, rowl, prevl, vec3, vecl, vec, vec, vec, vec, m2, ma, mg] + [blk] * 10,
        out_specs=[row3, rowl, vec3, vecl, vec, vec, vec, vec, m2, ma, mg],
        out_shape=[sds((n, 3 * w), F32), sds((n, lp), F32), sds((1, 3 * w), F32), sds((1, lp), F32),
                   sds((1, w), F32), sds((1, w), F32), sds((1, w), F32), sds((1, w), F32),
                   sds((dlp, w), F32), sds((alp, w), F32), sds((glp, w), F32)],
        compiler_params=_cp("arbitrary"),
    )(p_rkv, p_rkv, p_lora, p_lora, prm["mu_rkv"], prm["mu_lora"], prm["w0"], prm["a0"], prm["k_k"], prm["k_a"],
      prm["w2"], prm["a2"], prm["g2"], *grads)


def _shift_combine(dq, mu, t_len, name):
    n, c = dq.shape
    tb = _tile(min(n, t_len), 256, SUBLANES)
    bps = t_len // tb
    nb8 = tb // SUBLANES
    last8 = n // SUBLANES - 1

    def body(x_ref, nx_ref, mu_ref, o_ref):
        seq_end = (pl.program_id(0) % bps) == bps - 1
        nxt = jnp.where(seq_end, jnp.zeros_like(nx_ref[...]), nx_ref[...])
        x = x_ref[...]
        muv = mu_ref[...]
        o_ref[...] = ((1.0 - muv) * x + muv * _shift_up(x, nxt, 1)).astype(BF16)

    row = pl.BlockSpec((tb, c), lambda i: (i, 0))
    nxt = pl.BlockSpec((SUBLANES, c), lambda i: (jnp.minimum((i + 1) * nb8, last8), 0))
    vec = pl.BlockSpec((1, c), lambda i: (0, 0))
    return _call(body, name=name, grid=(n // tb,), in_specs=[row, nxt, vec], out_specs=row,
                 out_shape=jax.ShapeDtypeStruct((n, c), BF16), compiler_params=_cp("parallel"))(dq, dq, mu)


def _scan_step(s_i, a_t, w_t, b_t, k_t, v_i):
    sa = jnp.sum(s_i * a_t, axis=0, keepdims=True)
    return s_i * w_t + sa * b_t + v_i * k_t, sa


def _carried(comm):
    if comm is None:
        return [], [], [], [], None, None
    outs, n_sems, start, finish = _comm_plan(*comm)
    xs = list(comm[1])
    sems = [pltpu.SemaphoreType.DMA((n_sems,)), pltpu.SemaphoreType.DMA((n_sems,))]
    return xs, [_HBM] * len(xs), outs, sems, start, finish


def _rwkv_scan_fwd(r, w, k, a, b, v, name, comm=None):
    t_len, kd, ln = r.shape
    vh = v.shape[1]
    tc = SCAN_CHUNK
    nc = t_len // tc
    cx, cx_specs, c_outs, c_sems, c_start, c_finish = _carried(comm)
    nx = len(cx)

    def body(r_ref, w_ref, k_ref, a_ref, b_ref, v_ref, *rest):
        c_ins, (y_ref, ck_ref), c_out_refs = rest[:nx], rest[nx:nx + 2], rest[nx + 2:nx + 2 + len(c_outs)]
        s_ref = rest[nx + 2 + len(c_outs)]

        @pl.when(pl.program_id(0) == 0)
        def _():
            s_ref[...] = jnp.zeros_like(s_ref)
            if nx:
                c_start(c_ins, c_out_refs, rest[-2], rest[-1])

        ck_ref[0] = s_ref[...]

        def step(t, carry):
            a_t, w_t, b_t, k_t, r_t = a_ref[t], w_ref[t], b_ref[t], k_ref[t], r_ref[t]
            for i in range(vh):
                s_new, _ = _scan_step(s_ref[i], a_t, w_t, b_t, k_t, v_ref[t, pl.ds(i, 1), :])
                s_ref[i] = s_new
                y_ref[t, pl.ds(i, 1), :] = jnp.sum(s_new * r_t, axis=0, keepdims=True)
            return carry

        lax.fori_loop(0, tc, step, 0)

        if nx:
            @pl.when(pl.program_id(0) == nc - 1)
            def _():
                c_finish(c_ins, c_out_refs, rest[-2], rest[-1])

    kblk = pl.BlockSpec((tc, kd, ln), lambda c: (c, 0, 0))
    vblk = pl.BlockSpec((tc, vh, ln), lambda c: (c, 0, 0))
    return _call(
        body, name=name, grid=(nc,), in_specs=[kblk] * 5 + [vblk] + cx_specs,
        out_specs=[vblk, pl.BlockSpec((1, vh, kd, ln), lambda c: (c, 0, 0, 0))] + [_HBM] * len(c_outs),
        out_shape=[jax.ShapeDtypeStruct((t_len, vh, ln), F32), jax.ShapeDtypeStruct((nc, vh, kd, ln), F32)] + c_outs,
        scratch_shapes=[pltpu.VMEM((vh, kd, ln), F32)] + c_sems,
        compiler_params=_cp("arbitrary"),
    )(r, w, k, a, b, v, *cx)


def _rwkv_scan_bwd(r, w, k, a, b, v, dy, ckpt, name, comm=None):
    t_len, kd, ln = r.shape
    vh = v.shape[1]
    tc = SCAN_CHUNK
    nc = t_len // tc
    half = ln // 2
    cx, cx_specs, c_outs, c_sems, c_start, c_finish = _carried(comm)
    nx = len(cx)

    def body(r_ref, w_ref, k_ref, a_ref, b_ref, v_ref, dy_ref, ck_ref, *rest):
        c_ins = rest[:nx]
        dr_o, dw_o, dk_o, da_o, db_o, dv_o = rest[nx:nx + 6]
        c_out_refs = rest[nx + 6:nx + 6 + len(c_outs)]
        sbuf, ds_ref, sa_buf = rest[nx + 6 + len(c_outs):nx + 9 + len(c_outs)]

        @pl.when(pl.program_id(0) == 0)
        def _():
            ds_ref[...] = jnp.zeros_like(ds_ref)
            if nx:
                c_start(c_ins, c_out_refs, rest[-2], rest[-1])

        sbuf[0] = ck_ref[0]

        def fwd(t, carry):
            a_t, w_t, b_t, k_t = a_ref[t], w_ref[t], b_ref[t], k_ref[t]
            for i in range(vh):
                s_new, sa = _scan_step(sbuf[t, i], a_t, w_t, b_t, k_t, v_ref[t, pl.ds(i, 1), :])
                sbuf[t + 1, i] = s_new
                sa_buf[t, pl.ds(i, 1), :] = sa
            return carry

        lax.fori_loop(0, tc, fwd, 0)

        def bwd(tt, carry):
            t = tc - 1 - tt
            a_t, w_t, b_t, k_t, r_t = a_ref[t], w_ref[t], b_ref[t], k_ref[t], r_ref[t]
            z = jnp.zeros((kd, ln), F32)
            dr, dw, dk, da, db = z, z, z, z, z
            for i in range(vh):
                dy_i = dy_ref[t, pl.ds(i, 1), :]
                s_t = sbuf[t + 1, i]
                s_p = sbuf[t, i]
                d = ds_ref[i] + dy_i * r_t
                dr = dr + s_t * dy_i
                dv_o[t, pl.ds(i, 1), :] = jnp.sum(d * k_t, axis=0, keepdims=True)
                dk = dk + d * v_ref[t, pl.ds(i, 1), :]
                dsa = jnp.sum(d * b_t, axis=0, keepdims=True)
                db = db + d * sa_buf[t, pl.ds(i, 1), :]
                dw = dw + d * s_p
                da = da + s_p * dsa
                ds_ref[i] = d * w_t + dsa * a_t

            def both(x):
                return x + pltpu.roll(x, half, 1)

            dr_o[t] = both(dr)
            dw_o[t] = both(dw)
            dk_o[t] = both(dk)
            da_o[t] = both(da)
            db_o[t] = both(db)
            return carry

        lax.fori_loop(0, tc, bwd, 0)

        if nx:
            @pl.when(pl.program_id(0) == nc - 1)
            def _():
                c_finish(c_ins, c_out_refs, rest[-2], rest[-1])

    kblk = pl.BlockSpec((tc, kd, ln), lambda c: (nc - 1 - c, 0, 0))
    vblk = pl.BlockSpec((tc, vh, ln), lambda c: (nc - 1 - c, 0, 0))
    ksd = jax.ShapeDtypeStruct((t_len, kd, ln), F32)
    return _call(
        body, name=name, grid=(nc,),
        in_specs=[kblk] * 5 + [vblk, vblk, pl.BlockSpec((1, vh, kd, ln), lambda c: (nc - 1 - c, 0, 0, 0))] + cx_specs,
        out_specs=[kblk] * 5 + [vblk] + [_HBM] * len(c_outs),
        out_shape=[ksd] * 5 + [jax.ShapeDtypeStruct((t_len, vh, ln), F32)] + c_outs,
        scratch_shapes=[pltpu.VMEM((tc + 1, vh, kd, ln), F32), pltpu.VMEM((vh, kd, ln), F32),
                        pltpu.VMEM((tc, vh, ln), F32)] + c_sems,
        compiler_params=_cp("arbitrary"),
    )(r, w, k, a, b, v, dy, ckpt, *cx)


def _post_common(y_ref, r_ref, k_ref, v_ref, lng_ref, lnb_ref, rk_ref, e, et):
    y = y_ref[...]
    inv_n = 1.0 / RWKV_HEAD
    mean = _headsum(y, e, et) * inv_n
    yc = y - mean
    var = _headsum(yc * yc, e, et) * inv_n
    rstd = lax.rsqrt(var + GN_EPS)
    yh = yc * rstd
    yn = yh * lng_ref[...] + lnb_ref[...]
    bonus = _headsum(r_ref[...] * k_ref[...] * rk_ref[...], e, et)
    return yh, rstd, yn, bonus


def _rwkv_post_fwd(y, r, k, v, g, ln_g, ln_b, r_k, name):
    n, w = y.shape
    hp = max(w // RWKV_HEAD, LANES)
    tb = _tile(n, 256, SUBLANES)

    def body(y_ref, r_ref, k_ref, v_ref, g_ref, lng_ref, lnb_ref, rk_ref, o_ref):
        e, et = _head_mats(w, hp)
        _, _, yn, bonus = _post_common(y_ref, r_ref, k_ref, v_ref, lng_ref, lnb_ref, rk_ref, e, et)
        o_ref[...] = ((yn + bonus * v_ref[...]) * g_ref[...]).astype(BF16)

    blk = pl.BlockSpec((tb, w), lambda i: (i, 0))
    vec = pl.BlockSpec((1, w), lambda i: (0, 0))
    return _call(body, name=name, grid=(n // tb,), in_specs=[blk] * 5 + [vec] * 3, out_specs=blk,
                 out_shape=jax.ShapeDtypeStruct((n, w), BF16),
                 compiler_params=_cp("parallel"))(y, r, k, v, g, ln_g, ln_b, r_k)


def _rwkv_post_bwd(y, r, k, v, g, ln_g, ln_b, r_k, do_cat, name):
    n, w = y.shape
    hp = max(w // RWKV_HEAD, LANES)
    tb = _tile(n, 128, SUBLANES)

    def body(y_ref, r_ref, k_ref, v_ref, g_ref, lng_ref, lnb_ref, rk_ref, do_ref,
             dy_o, dr_o, dk_o, dv_o, dg_o, dlng_o, dlnb_o, drk_o):
        e, et = _head_mats(w, hp)
        yh, rstd, yn, bonus = _post_common(y_ref, r_ref, k_ref, v_ref, lng_ref, lnb_ref, rk_ref, e, et)
        do = do_ref[...]
        vv, rv, kv, rk = v_ref[...], r_ref[...], k_ref[...], rk_ref[...]
        dg_o[...] = do * (yn + bonus * vv)
        dz = do * g_ref[...]
        dbonus = _headsum(dz * vv, e, et)
        dv_o[...] = dz * bonus
        dr_o[...] = dbonus * kv * rk
        dk_o[...] = dbonus * rv * rk
        dyh = dz * lng_ref[...]
        inv_n = 1.0 / RWKV_HEAD
        dy_o[...] = rstd * (dyh - _headsum(dyh, e, et) * inv_n - yh * (_headsum(dyh * yh, e, et) * inv_n))

        @pl.when(pl.program_id(0) == 0)
        def _():
            for o in (dlng_o, dlnb_o, drk_o):
                o[...] = jnp.zeros_like(o)

        dlng_o[...] += jnp.sum(dz * yh, axis=0, keepdims=True)
        dlnb_o[...] += jnp.sum(dz, axis=0, keepdims=True)
        drk_o[...] += jnp.sum(dbonus * rv * kv, axis=0, keepdims=True)

    blk = pl.BlockSpec((tb, w), lambda i: (i, 0))
    vec = pl.BlockSpec((1, w), lambda i: (0, 0))
    sds = jax.ShapeDtypeStruct
    return _call(body, name=name, grid=(n // tb,), in_specs=[blk] * 5 + [vec] * 3 + [blk],
                 out_specs=[blk] * 5 + [vec] * 3,
                 out_shape=[sds((n, w), F32)] * 5 + [sds((1, w), F32)] * 3,
                 compiler_params=_cp("arbitrary"))(y, r, k, v, g, ln_g, ln_b, r_k, do_cat)


def _lru_gates(xb, prev8, gate, cw_ref, cb_ref, wr_ref, br_ref, wi_ref, bi_ref, lam_ref, is_t0):
    c = xb.shape[1]
    nblk = c // LRU_BLOCK_W
    xs = [xb] + [_shift_down(xb, prev8, j) for j in range(1, CONV_WIDTH)]
    xc = cb_ref[...]
    for j in range(CONV_WIDTH):
        xc = xc + xs[CONV_WIDTH - 1 - j] * cw_ref[pl.ds(j, 1), :]
    xcb = xc.astype(BF16)

    def blockmm(w_ref):
        return jnp.concatenate(
            [jnp.dot(xcb[:, h * LRU_BLOCK_W:(h + 1) * LRU_BLOCK_W], w_ref[h], preferred_element_type=F32)
             for h in range(nblk)], axis=1)

    rg = _sig(blockmm(wr_ref) + br_ref[...])
    ig = _sig(blockmm(wi_ref) + bi_ref[...])
    sp = _softplus(-lam_ref[...])
    la = -LRU_C * rg * sp
    av = jnp.exp(la)
    mult = jnp.where(is_t0, 1.0, jnp.sqrt(_neg_expm1(2.0 * la)))
    ge, th = _gelu_parts(gate)
    return dict(xs=xs, xc=xc, xcb=xcb, rg=rg, ig=ig, sp=sp, a=av, mult=mult, ge=ge, th=th)


def _lru_specs(tb, c, nb, rev):
    nb8 = tb // SUBLANES

    def blk_i(i):
        return nb - 1 - i if rev else i

    xb = pl.BlockSpec((tb, c), lambda b, i: (b * nb + blk_i(i), 0))
    gate = pl.BlockSpec((tb, c), lambda b, i: (b * nb + blk_i(i), 1))
    prev = pl.BlockSpec((SUBLANES, c), lambda b, i: (jnp.maximum((b * nb + blk_i(i)) * nb8 - 1, 0), 0))
    return xb, gate, prev


def _lru_fwd(p_lru, prm, t_len, name):
    n, c2 = p_lru.shape
    c = c2 // 2
    nblk = c // LRU_BLOCK_W
    tb = _tile(t_len, 256, SUBLANES)
    nb = t_len // tb
    bsz = n // t_len

    def body(xb_ref, gate_ref, prev_ref, cw_ref, cb_ref, wr_ref, br_ref, wi_ref, bi_ref, lam_ref, ng_ref,
             y_o, h_o, carry):
        i = pl.program_id(1)
        prev8 = jnp.where(i == 0, jnp.zeros_like(prev_ref[...]), prev_ref[...])
        row = lax.broadcasted_iota(jnp.int32, (tb, c), 0)
        f = _lru_gates(xb_ref[...], prev8, gate_ref[...], cw_ref, cb_ref, wr_ref, br_ref, wi_ref, bi_ref, lam_ref,
                       jnp.logical_and(i == 0, row == 0))
        acc_a = f["a"]
        acc_b = f["mult"] * f["ig"] * f["xc"]
        s = 1
        while s < tb:
            keep = row >= s
            a_sh = jnp.where(keep, pltpu.roll(acc_a, s, 0), 1.0)
            b_sh = jnp.where(keep, pltpu.roll(acc_b, s, 0), 0.0)
            acc_b = acc_a * b_sh + acc_b
            acc_a = acc_a * a_sh
            s *= 2

        @pl.when(i == 0)
        def _():
            carry[...] = jnp.zeros_like(carry)

        h = acc_b + acc_a * carry[0:1, :]
        carry[0:1, :] = h[tb - 1:tb, :]
        h_o[...] = h
        y = h * f["ge"]
        rstd = lax.rsqrt(jnp.mean(y * y, axis=-1, keepdims=True) + NORM_EPS)
        y_o[...] = (y * rstd * ng_ref[...]).astype(BF16)

    xb_s, gate_s, prev_s = _lru_specs(tb, c, nb, False)
    c0 = lambda b, i: (0, 0)
    vec = pl.BlockSpec((1, c), c0)
    wsp = pl.BlockSpec((nblk, LRU_BLOCK_W, LRU_BLOCK_W), lambda b, i: (0, 0, 0))
    out = pl.BlockSpec((tb, c), lambda b, i: (b * nb + i, 0))
    return _call(
        body, name=name, grid=(bsz, nb),
        in_specs=[xb_s, gate_s, prev_s, pl.BlockSpec((CONV_WIDTH, c), c0), vec, wsp, vec, wsp, vec, vec, vec],
        out_specs=[out, out],
        out_shape=[jax.ShapeDtypeStruct((n, c), BF16), jax.ShapeDtypeStruct((n, c), F32)],
        scratch_shapes=[pltpu.VMEM((SUBLANES, c), F32)],
        compiler_params=_cp("arbitrary", "arbitrary"),
    )(p_lru, p_lru, p_lru, prm["conv_w"], prm["conv_b"], prm["wr"], prm["br"], prm["wi"], prm["bi"],
      prm["lam"], prm["norm_g"])


def _lru_bwd(p_lru, h, do_cat, prm, t_len, name):
    n, c2 = p_lru.shape
    c = c2 // 2
    nblk = c // LRU_BLOCK_W
    tb = _tile(t_len, 128, SUBLANES)
    nb = t_len // tb
    bsz = n // t_len

    def body(xb_ref, gate_ref, prev_ref, h_ref, hprev_ref, do_ref,
             cw_ref, cb_ref, wr_ref, br_ref, wi_ref, bi_ref, lam_ref, ng_ref,
             dp_o, dcw_o, dcb_o, dwr_o, dbr_o, dwi_o, dbi_o, dlam_o, dng_o,
             a_next, g_next, dxc_next):
        b = pl.program_id(0)
        i = pl.program_id(1)
        blk = nb - 1 - i
        first = blk == 0
        prev8 = jnp.where(first, jnp.zeros_like(prev_ref[...]), prev_ref[...])
        hprev8 = jnp.where(first, jnp.zeros_like(hprev_ref[...]), hprev_ref[...])
        row = lax.broadcasted_iota(jnp.int32, (tb, c), 0)
        is_t0 = jnp.logical_and(first, row == 0)
        gate = gate_ref[...]
        f = _lru_gates(xb_ref[...], prev8, gate, cw_ref, cb_ref, wr_ref, br_ref, wi_ref, bi_ref, lam_ref, is_t0)

        @pl.when(i == 0)
        def _():
            a_next[...] = jnp.zeros_like(a_next)
            g_next[...] = jnp.zeros_like(g_next)
            dxc_next[...] = jnp.zeros_like(dxc_next)

        @pl.when(jnp.logical_and(b == 0, i == 0))
        def _():
            for o in (dcw_o, dcb_o, dwr_o, dbr_o, dwi_o, dbi_o, dlam_o, dng_o):
                o[...] = jnp.zeros_like(o)

        def rsum(x):
            return jnp.sum(x, axis=0, keepdims=True)

        hv = h_ref[...]
        hprev = _shift_down(hv, hprev8, 1)
        ge = f["ge"]
        y = hv * ge
        rstd = lax.rsqrt(jnp.mean(y * y, axis=-1, keepdims=True) + NORM_EPS)
        yh = y * rstd
        dyn = do_ref[...]
        t = dyn * ng_ref[...]
        dy = rstd * (t - yh * jnp.mean(t * yh, axis=-1, keepdims=True))
        dng_o[...] += rsum(dyn * yh)
        dgate = dy * hv * _gelu_grad(gate, f["th"])

        av = f["a"]
        acc_c = _shift_up(av, a_next[...], 1)
        acc_g = dy * ge
        s = 1
        while s < tb:
            keep = row < tb - s
            c_sh = jnp.where(keep, pltpu.roll(acc_c, tb - s, 0), 1.0)
            g_sh = jnp.where(keep, pltpu.roll(acc_g, tb - s, 0), 0.0)
            acc_g = acc_g + acc_c * g_sh
            acc_c = acc_c * c_sh
            s *= 2
        gtot = acc_g + acc_c * g_next[0:1, :]
        a_next[0:1, :] = av[0:1, :]
        g_next[0:1, :] = gtot[0:1, :]

        xc, ig, rg, mult = f["xc"], f["ig"], f["rg"], f["mult"]
        da = gtot * hprev
        dmult = gtot * ig * xc
        dig = gtot * mult * xc
        dxc = gtot * mult * ig
        da = da + jnp.where(is_t0, 0.0, -dmult * av / mult)
        dla = da * av
        drg = dla * (-LRU_C) * f["sp"]
        dlam_o[...] += rsum(dla * rg) * LRU_C * _sig(-lam_ref[...])
        dzr = drg * rg * (1.0 - rg)
        dzi = dig * ig * (1.0 - ig)
        dbr_o[...] += rsum(dzr)
        dbi_o[...] += rsum(dzi)
        dzrb, dzib = dzr.astype(BF16), dzi.astype(BF16)
        xcb = f["xcb"]
        back = []
        for hh in range(nblk):
            sl = slice(hh * LRU_BLOCK_W, (hh + 1) * LRU_BLOCK_W)
            dwr_o[hh] += _dot_tn(xcb[:, sl], dzrb[:, sl])
            dwi_o[hh] += _dot_tn(xcb[:, sl], dzib[:, sl])
            back.append(_dot_nt(dzrb[:, sl], wr_ref[hh]) + _dot_nt(dzib[:, sl], wi_ref[hh]))
        dxc = dxc + jnp.concatenate(back, axis=1)
        dcb_o[...] += rsum(dxc)
        xs = f["xs"]
        dcw_o[...] += jnp.concatenate([rsum(dxc * xs[CONV_WIDTH - 1 - j]) for j in range(CONV_WIDTH)], axis=0)
        nxt = dxc_next[...]
        dxb = dxc * cw_ref[pl.ds(CONV_WIDTH - 1, 1), :]
        for j in range(1, CONV_WIDTH):
            dxb = dxb + _shift_up(dxc, nxt, j) * cw_ref[pl.ds(CONV_WIDTH - 1 - j, 1), :]
        dxc_next[...] = dxc[0:SUBLANES, :]
        dp_o[:, 0:c] = dxb.astype(BF16)
        dp_o[:, c:2 * c] = dgate.astype(BF16)

    xb_s, gate_s, prev_s = _lru_specs(tb, c, nb, True)
    c0 = lambda b, i: (0, 0)
    vec = pl.BlockSpec((1, c), c0)
    wsp = pl.BlockSpec((nblk, LRU_BLOCK_W, LRU_BLOCK_W), lambda b, i: (0, 0, 0))
    cwsp = pl.BlockSpec((CONV_WIDTH, c), c0)
    sds = jax.ShapeDtypeStruct
    return _call(
        body, name=name, grid=(bsz, nb),
        in_specs=[xb_s, gate_s, prev_s, xb_s, prev_s, gate_s, cwsp, vec, wsp, vec, wsp, vec, vec, vec],
        out_specs=[pl.BlockSpec((tb, 2 * c), lambda b, i: (b * nb + nb - 1 - i, 0)),
                   cwsp, vec, wsp, vec, wsp, vec, vec, vec],
        out_shape=[sds((n, 2 * c), BF16), sds((CONV_WIDTH, c), F32), sds((1, c), F32),
                   sds((nblk, LRU_BLOCK_W, LRU_BLOCK_W), F32), sds((1, c), F32),
                   sds((nblk, LRU_BLOCK_W, LRU_BLOCK_W), F32), sds((1, c), F32), sds((1, c), F32), sds((1, c), F32)],
        scratch_shapes=[pltpu.VMEM((SUBLANES, c), F32)] * 3,
        compiler_params=_cp("arbitrary", "arbitrary"),
    )(p_lru, p_lru, p_lru, h, h, do_cat, prm["conv_w"], prm["conv_b"], prm["wr"], prm["br"], prm["wi"], prm["bi"],
      prm["lam"], prm["norm_g"])


def _adamw(g, w, m, v, name):
    rows, cols = g.shape
    tb = _tile(rows, 128, SUBLANES)

    def body(g_ref, w_ref, m_ref, v_ref, d_o, m_o, v_o):
        gv = g_ref[...]
        mn = ADAM_B1 * m_ref[...] + (1.0 - ADAM_B1) * gv
        vn = ADAM_B2 * v_ref[...] + (1.0 - ADAM_B2) * (gv * gv)
        m_o[...] = mn
        v_o[...] = vn
        d_o[...] = -ADAM_LR * ((mn / _BC1) / (jnp.sqrt(vn / _BC2) + ADAM_EPS) + ADAM_WD * w_ref[...])

    blk = pl.BlockSpec((tb, cols), lambda i: (i, 0))
    return _call(body, name=name, grid=(rows // tb,), in_specs=[blk] * 4, out_specs=[blk] * 3,
                 out_shape=[jax.ShapeDtypeStruct((rows, cols), F32)] * 3, compiler_params=_cp("parallel"))(g, w, m, v)


def _adamw_halves(mine, theirs, w, m, v, name):
    a, b = mine.shape
    ta = _tile(a, 128, SUBLANES)
    w, m, v = (t.reshape(2, a, b) for t in (w, m, v))

    def body(mine_ref, theirs_ref, w_ref, m_ref, v_ref, g_o, d_o, m_o, v_o):
        gv = jnp.where(pl.program_id(0) == lax.axis_index("c"), mine_ref[...], theirs_ref[...])
        mn = ADAM_B1 * m_ref[...] + (1.0 - ADAM_B1) * gv
        vn = ADAM_B2 * v_ref[...] + (1.0 - ADAM_B2) * (gv * gv)
        g_o[...] = gv
        m_o[...] = mn
        v_o[...] = vn
        d_o[...] = -ADAM_LR * ((mn / _BC1) / (jnp.sqrt(vn / _BC2) + ADAM_EPS) + ADAM_WD * w_ref[...])

    half = pl.BlockSpec((ta, b), lambda h, i: (i, 0))
    blk = pl.BlockSpec((None, ta, b), lambda h, i: (h, i, 0))
    return _call(body, name=name, grid=(2, a // ta), in_specs=[half, half, blk, blk, blk], out_specs=[blk] * 4,
                 out_shape=[jax.ShapeDtypeStruct((2, a, b), F32)] * 4,
                 compiler_params=_cp("parallel", "parallel"))(mine, theirs, w, m, v)


def _pair_sum(x4, recv, name):
    _, _, a, b = x4.shape
    ta = _tile(a, 256, SUBLANES)

    def body(x_ref, r_ref, o_ref):
        mine = x_ref[lax.axis_index("c")]
        o_ref[...] = (mine.astype(F32) + r_ref[...].astype(F32)).astype(BF16)

    return _call(
        body, name=name, grid=(4, a // ta),
        in_specs=[pl.BlockSpec((None, 2, ta, b), lambda j, i: (j, 0, i, 0)),
                  pl.BlockSpec((None, ta, b), lambda j, i: (j, i, 0))],
        out_specs=pl.BlockSpec((None, ta, b), lambda j, i: (j, i, 0)),
        out_shape=jax.ShapeDtypeStruct((4, a, b), BF16), compiler_params=_cp("parallel", "parallel"))(x4, recv)


def _chip_sum(x4, name):
    _, a, b = x4.shape
    ta = _tile(a, 256, SUBLANES)

    def body(x_ref, o_ref):
        acc = x_ref[0] + x_ref[1]
        acc = acc + x_ref[2]
        o_ref[...] = acc + x_ref[3]

    return _call(
        body, name=name, grid=(a // ta,),
        in_specs=[pl.BlockSpec((4, ta, b), lambda i: (0, i, 0))],
        out_specs=pl.BlockSpec((ta, b), lambda i: (i, 0)),
        out_shape=jax.ShapeDtypeStruct((a, b), F32), compiler_params=_cp("parallel"))(x4)


def _peer_sum(own4, parts, name):
    _, a, b = parts.shape
    ta = _tile(a, 256, SUBLANES)

    def body(own_ref, p_ref, o_ref):
        me = 2 * lax.axis_index("x") + lax.axis_index("y")
        acc = own_ref[me].astype(F32) + p_ref[0].astype(F32)
        acc = acc + p_ref[1].astype(F32)
        o_ref[...] = acc + p_ref[2].astype(F32)

    return _call(
        body, name=name, grid=(a // ta,),
        in_specs=[pl.BlockSpec((4, ta, b), lambda i: (0, i, 0)), pl.BlockSpec((3, ta, b), lambda i: (0, i, 0))],
        out_specs=pl.BlockSpec((ta, b), lambda i: (i, 0)),
        out_shape=jax.ShapeDtypeStruct((a, b), F32), compiler_params=_cp("parallel"))(own4, parts)


def _add2(x, y, name):
    rows, cols = x.shape
    tb = _tile(rows, 512, SUBLANES)

    def body(x_ref, y_ref, o_ref):
        o_ref[...] = x_ref[...] + y_ref[...]

    blk = pl.BlockSpec((tb, cols), lambda i: (i, 0))
    return _call(body, name=name, grid=(rows // tb,), in_specs=[blk, blk], out_specs=blk,
                 out_shape=jax.ShapeDtypeStruct((rows, cols), x.dtype), compiler_params=_cp("parallel"))(x, y)


_HBM = pl.BlockSpec(memory_space=pltpu.HBM)


def _place():
    x, y, c = lax.axis_index("x"), lax.axis_index("y"), lax.axis_index("c")
    chips = [(1 - x, y), (x, 1 - y), (1 - x, 1 - y)]
    return x, y, c, chips


def _comm_call(body, name, xs, out_shapes, n_sems):
    return _call(
        body, name=name, in_specs=[_HBM] * len(xs), out_specs=[_HBM] * len(out_shapes), out_shape=out_shapes,
        scratch_shapes=[pltpu.SemaphoreType.DMA((n_sems,)), pltpu.SemaphoreType.DMA((n_sems,)),
                        pltpu.SemaphoreType.DMA((len(xs),))],
    )(*xs)


def _all_gather_chips(xs, name):
    n = len(xs)

    def body(*refs):
        ins, outs = refs[:n], refs[n:2 * n]
        ssem, rsem, _ = refs[2 * n:]
        _gather_start(ins, outs, ssem, rsem)
        _gather_finish(ins, outs, ssem, rsem)

    outs = [jax.ShapeDtypeStruct((4,) + v.shape, v.dtype) for v in xs]
    return _comm_call(body, name, xs, outs, GATHER_SEMS * n)


GATHER_SEMS = 7
PEER_SEMS = 3


def _remote(src, dst, ssem, rsem, k, dev):
    return pltpu.make_async_remote_copy(src_ref=src, dst_ref=dst, send_sem=ssem.at[k], recv_sem=rsem.at[k],
                                        device_id=dev, device_id_type=MESH)


def _gather_start(ins, outs, ssem, rsem):
    x, y, c, chips = _place()
    me = 2 * x + y
    for i in range(len(ins)):
        for j, (px, py) in enumerate(chips):
            _remote(ins[i].at[c], outs[i].at[me, c], ssem, rsem, GATHER_SEMS * i + j, (px, py, c)).start()
        _remote(ins[i], outs[i].at[me], ssem, rsem, GATHER_SEMS * i + 6, (x, y, 1 - c)).start()


def _gather_finish(ins, outs, ssem, rsem):
    x, y, c, chips = _place()
    me = 2 * x + y
    sib = (x, y, 1 - c)
    n = len(ins)
    for i in range(n):
        for j, (px, py) in enumerate(chips):
            slot = outs[i].at[2 * px + py, c]
            _remote(slot, slot, ssem, rsem, GATHER_SEMS * i + j, (px, py, c)).wait_recv()
            _remote(slot, slot, ssem, rsem, GATHER_SEMS * i + 3 + j, sib).start()
    for i in range(n):
        own = outs[i].at[me]
        _remote(own, own, ssem, rsem, GATHER_SEMS * i + 6, sib).wait_recv()
        for j, (px, py) in enumerate(chips):
            slot = outs[i].at[2 * px + py, 1 - c]
            _remote(slot, slot, ssem, rsem, GATHER_SEMS * i + 3 + j, sib).wait_recv()
    for i in range(n):
        for j, (px, py) in enumerate(chips):
            slot = outs[i].at[2 * px + py, c]
            _remote(ins[i].at[c], outs[i].at[me, c], ssem, rsem, GATHER_SEMS * i + j, (px, py, c)).wait_send()
            _remote(slot, slot, ssem, rsem, GATHER_SEMS * i + 3 + j, sib).wait_send()
        _remote(ins[i], outs[i].at[me], ssem, rsem, GATHER_SEMS * i + 6, sib).wait_send()


def _peer_copies(ins, outs, ssem, rsem):
    x, y, c, chips = _place()
    return [_remote(ins[i].at[2 * px + py], outs[i].at[j], ssem, rsem, PEER_SEMS * i + j, (px, py, c))
            for i in range(len(ins)) for j, (px, py) in enumerate(chips)]


def _comm_plan(kind, xs):
    if kind == "gather":
        outs = [jax.ShapeDtypeStruct((4,) + v.shape, v.dtype) for v in xs]
        return outs, GATHER_SEMS * len(xs), _gather_start, _gather_finish

    def start(ins, outs, ssem, rsem):
        for cp in _peer_copies(ins, outs, ssem, rsem):
            cp.start()

    def finish(ins, outs, ssem, rsem):
        for cp in _peer_copies(ins, outs, ssem, rsem):
            cp.wait()

    outs = [jax.ShapeDtypeStruct((3,) + v.shape[1:], v.dtype) for v in xs]
    return outs, PEER_SEMS * len(xs), start, finish


def _sibling_swap(xs, pick_half, name):
    n = len(xs)

    def body(*refs):
        ins, outs = refs[:n], refs[n:2 * n]
        ssem, rsem, _ = refs[2 * n:]
        x, y, c, _ = _place()
        cps = []
        for i in range(n):
            src = ins[i].at[:, 1 - c] if pick_half else ins[i]
            cp = pltpu.make_async_remote_copy(src_ref=src, dst_ref=outs[i], send_sem=ssem.at[i], recv_sem=rsem.at[i],
                                              device_id=(x, y, 1 - c), device_id_type=MESH)
            cp.start()
            cps.append(cp)
        for cp in cps:
            cp.wait()

    outs = [jax.ShapeDtypeStruct((v.shape[0],) + v.shape[2:] if pick_half else v.shape, v.dtype) for v in xs]
    return _comm_call(body, name, xs, outs, n)


def _chip_broadcast(xs, name):
    n = len(xs)

    def body(*refs):
        ins, outs = refs[:n], refs[n:2 * n]
        ssem, rsem, lsem = refs[2 * n:]
        x, y, c, chips = _place()
        me = 2 * x + y
        cps = []
        for i in range(n):
            cp = pltpu.make_async_copy(ins[i], outs[i].at[me], lsem.at[i])
            cp.start()
            cps.append(cp)
            for j, (px, py) in enumerate(chips):
                cp = pltpu.make_async_remote_copy(
                    src_ref=ins[i], dst_ref=outs[i].at[me], send_sem=ssem.at[3 * i + j], recv_sem=rsem.at[3 * i + j],
                    device_id=(px, py, c), device_id_type=MESH)
                cp.start()
                cps.append(cp)
        for i in range(n):
            for j, (px, py) in enumerate(chips):
                slot = outs[i].at[2 * px + py]
                pltpu.make_async_remote_copy(
                    src_ref=slot, dst_ref=slot, send_sem=ssem.at[3 * i + j], recv_sem=rsem.at[3 * i + j],
                    device_id=(px, py, c), device_id_type=MESH).wait_recv()
        for i in range(n):
            cps[4 * i].wait()
            for j in range(3):
                cps[4 * i + 1 + j].wait_send()

    outs = [jax.ShapeDtypeStruct((4,) + v.shape, v.dtype) for v in xs]
    return _comm_call(body, name, xs, outs, 3 * n)


def _peer_exchange(xs, name):
    n = len(xs)

    def body(*refs):
        ins, outs = refs[:n], refs[n:2 * n]
        ssem, rsem, _ = refs[2 * n:]
        cps = _peer_copies(ins, outs, ssem, rsem)
        for cp in cps:
            cp.start()
        for cp in cps:
            cp.wait()

    outs = [jax.ShapeDtypeStruct((3,) + v.shape[1:], v.dtype) for v in xs]
    return _comm_call(body, name, xs, outs, PEER_SEMS * n)


def _to_scan_k(x, bsz, t_len):
    h = x.shape[1] // RWKV_HEAD
    y = x.reshape(bsz, t_len, h, RWKV_HEAD).transpose(1, 3, 0, 2).reshape(t_len, RWKV_HEAD, bsz * h)
    return jnp.concatenate([y, y], axis=-1)


def _to_scan_v(x, bsz, t_len):
    h = x.shape[1] // RWKV_HEAD
    y = x.reshape(bsz, t_len, h, 2, RWKV_HEAD // 2).transpose(1, 4, 3, 0, 2)
    return y.reshape(t_len, RWKV_HEAD // 2, 2 * bsz * h)


def _from_scan_k(x, bsz, t_len):
    h = x.shape[2] // (2 * bsz)
    y = x[:, :, :bsz * h].reshape(t_len, RWKV_HEAD, bsz, h).transpose(2, 0, 3, 1)
    return y.reshape(bsz * t_len, h * RWKV_HEAD)


def _from_scan_v(x, bsz, t_len):
    h = x.shape[2] // (2 * bsz)
    y = x.reshape(t_len, RWKV_HEAD // 2, 2, bsz, h).transpose(3, 0, 4, 2, 1)
    return y.reshape(bsz * t_len, h * RWKV_HEAD)


def _pad_rows(x, rows):
    return jnp.pad(x, ((0, rows - x.shape[0]), (0, 0)))


def _pad_cols(x, cols):
    return jnp.pad(x, ((0, 0), (0, cols - x.shape[1])))


def _cols_from_shards(g4):
    _, r, cs = g4.shape
    return g4.transpose(1, 0, 2).reshape(r, 4 * cs)


def _cols_to_shards(g):
    r, cols = g.shape
    return g.reshape(r, 4, cols // 4).transpose(1, 0, 2)


def kernel(x, norm_mix_g, w_in, mu_shift, rwkv_w0, rwkv_w2, rwkv_a0, rwkv_a2, rwkv_g2, rwkv_k_k, rwkv_k_a, rwkv_r_k, rwkv_ln_g, rwkv_ln_b, conv_w, conv_b, lru_wr, lru_br, lru_wi, lru_bi, lru_lambda, lru_norm_g, w_out, norm_ffn_g, ffn_w_gate, ffn_w_up, ffn_w_down, norm_final_g, loss_target, m_norm_mix_g, m_w_in, m_mu_shift, m_rwkv_w0, m_rwkv_w2, m_rwkv_a0, m_rwkv_a2, m_rwkv_g2, m_rwkv_k_k, m_rwkv_k_a, m_rwkv_r_k, m_rwkv_ln_g, m_rwkv_ln_b, m_conv_w, m_conv_b, m_lru_wr, m_lru_br, m_lru_wi, m_lru_bi, m_lru_lambda, m_lru_norm_g, m_w_out, m_norm_ffn_g, m_ffn_w_gate, m_ffn_w_up, m_ffn_w_down, m_norm_final_g, v_norm_mix_g, v_w_in, v_mu_shift, v_rwkv_w0, v_rwkv_w2, v_rwkv_a0, v_rwkv_a2, v_rwkv_g2, v_rwkv_k_k, v_rwkv_k_a, v_rwkv_r_k, v_rwkv_ln_g, v_rwkv_ln_b, v_conv_w, v_conv_b, v_lru_wr, v_lru_br, v_lru_wi, v_lru_bi, v_lru_lambda, v_lru_norm_g, v_w_out, v_norm_ffn_g, v_ffn_w_gate, v_ffn_w_up, v_ffn_w_down, v_norm_final_g):
    names = ['norm_mix_g', 'w_in', 'mu_shift', 'rwkv_w0', 'rwkv_w2', 'rwkv_a0', 'rwkv_a2', 'rwkv_g2', 'rwkv_k_k',
             'rwkv_k_a', 'rwkv_r_k', 'rwkv_ln_g', 'rwkv_ln_b', 'conv_w', 'conv_b', 'lru_wr', 'lru_br', 'lru_wi',
             'lru_bi', 'lru_lambda', 'lru_norm_g', 'w_out', 'norm_ffn_g', 'ffn_w_gate', 'ffn_w_up', 'ffn_w_down',
             'norm_final_g']
    env = locals()
    wts = {k: env[k] for k in names}
    mom_m = {k: env["m_" + k] for k in names}
    mom_v = {k: env["v_" + k] for k in names}

    bsz, t_len, d = x.shape
    n = bsz * t_len
    w = rwkv_w0.shape[1]
    lw = lru_br.shape[1]
    dl, al, gl = rwkv_w2.shape[1], rwkv_a2.shape[1], rwkv_g2.shape[1]
    dlp, alp, glp = _ceil_to(dl, LANES), _ceil_to(al, LANES), _ceil_to(gl, LANES)
    lp = dlp + alp + glp
    rc = 3 * w + dl + al + gl
    chip = 2 * lax.axis_index("x") + lax.axis_index("y")

    big = ['w_in', 'w_out', 'ffn_w_gate', 'ffn_w_up', 'ffn_w_down']
    small_sh = ['rwkv_w2', 'rwkv_a2', 'rwkv_g2', 'conv_w']

    def halves(a2d):
        return a2d.reshape(2, a2d.shape[0] // 2, a2d.shape[1])

    col_sharded = ('w_in', 'ffn_w_gate', 'ffn_w_up')

    def work(k, t):
        return jnp.swapaxes(t[0], 0, 1) if k in col_sharded else t[0]

    def unwork(k, t2):
        return (jnp.swapaxes(t2, 0, 1) if k in col_sharded else t2)[None]

    def rows_of(g):
        return g.reshape(g.shape[0] * g.shape[1] * g.shape[2], g.shape[3])

    send = [halves(work('w_in', w_in).astype(BF16))] + [halves(wts[k][0]) for k in small_sh]
    got = _all_gather_chips(send, "gather_w_in")
    later = ['w_out', 'ffn_w_gate', 'ffn_w_up', 'ffn_w_down']
    send_later = [halves(work(k, wts[k]).astype(BF16)) for k in later]
    full = {}
    for k, g in zip(small_sh, got[1:]):
        full[k] = _cols_from_shards(g.reshape(4, g.shape[1] * g.shape[2], g.shape[3]))
    wi_t = rows_of(got[0])
    w_rkv = wi_t[:3 * w]
    w_lru = wi_t[rc:]
    o = 3 * w
    w_lora = jnp.concatenate([_pad_rows(wi_t[o:o + dl], dlp), _pad_rows(wi_t[o + dl:o + dl + al], alp),
                              _pad_rows(wi_t[o + dl + al:rc], glp)], axis=0)
    mu = mu_shift
    prm_r = dict(
        mu_rkv=mu[:, :3 * w],
        mu_lora=jnp.concatenate([_pad_cols(mu[:, o:o + dl], dlp), _pad_cols(mu[:, o + dl:o + dl + al], alp),
                                 _pad_cols(mu[:, o + dl + al:rc], glp)], axis=1),
        w0=rwkv_w0, a0=rwkv_a0, k_k=rwkv_k_k, k_a=rwkv_k_a,
        w2=_pad_rows(full['rwkv_w2'], dlp).astype(BF16), a2=_pad_rows(full['rwkv_a2'], alp).astype(BF16),
        g2=_pad_rows(full['rwkv_g2'], glp).astype(BF16))
    ln_g, ln_b, r_k = rwkv_ln_g, rwkv_ln_b, rwkv_r_k.reshape(1, w)
    prm_l = dict(conv_w=full['conv_w'], conv_b=conv_b, wr=lru_wr[0].astype(BF16), br=lru_br,
                 wi=lru_wi[0].astype(BF16), bi=lru_bi, lam=lru_lambda, norm_g=lru_norm_g)
    g_final = norm_final_g.reshape(1, d)

    x2 = x.reshape(n, d)
    u1 = _rmsnorm_fwd(x2, norm_mix_g, "norm_mix")
    p_rkv = _mm(u1, w_rkv, name="in_rkv", tb=True)
    p_lru = _mm(u1, w_lru, name="in_lru", tb=True)
    p_lora = _mm(u1, w_lora, name="in_lora", tb=True)
    r_t, dec_t, k_t, v_t, na_t, nb_t, g_t = _rwkv_prep_fwd(p_rkv, p_lora, prm_r, t_len, "rwkv_prep")
    sk = [_to_scan_k(a, bsz, t_len) for a in (r_t, dec_t, k_t, na_t, nb_t)]
    sv = _to_scan_v(v_t, bsz, t_len)
    y_s, ckpt, *got_later = _rwkv_scan_fwd(*sk, sv, name="rwkv_scan", comm=("gather", send_later))
    wo, wg, wu, wd = (rows_of(g) for g in got_later)
    y_t = _from_scan_v(y_s, bsz, t_len)
    y_a = _rwkv_post_fwd(y_t, r_t, k_t, v_t, g_t, ln_g, ln_b, r_k, "rwkv_post")
    y_b, h_lru = _lru_fwd(p_lru, prm_l, t_len, "lru_fwd")
    h1 = _mm(y_a, wo[:w], name="out_a", res=x2)
    h1 = _mm(y_b, wo[w:], name="out_b", res=h1)
    u2 = _rmsnorm_fwd(h1, norm_ffn_g, "norm_ffn")
    ffc = (1024, 256, 4096)
    gate = _mm(u2, wg, name="ffn_gate", tb=True, caps=ffc)
    up = _mm(u2, wu, name="ffn_up", tb=True, caps=ffc)
    act = _swiglu_fwd(gate, up, "swiglu")
    h2 = _mm(act, wd, name="ffn_down", res=h1, caps=(512, 256, 11008))

    dh2, dh2b, g_norm_final, loss_vec = _loss_head(h2, g_final, loss_target.reshape(n, d), "loss_head")
    loss = lax.psum(loss_vec[0, 0], ("x", "y", "c"))
    dact = _mm(dh2b, wd, name="d_act", tb=True, caps=(1024, 256, 4096))
    dgate, dup = _swiglu_bwd(gate, up, dact, "swiglu_bwd")
    shards = lambda g: g.reshape(4, 2, g.shape[0] // 8, g.shape[1])
    dwc = dict(ta=True, out_dtype=BF16, n_outer=True, caps=(256, 1024, 4096))
    gw_down = _mm(act, dh2b, name="dw_down", **dwc)
    gw_gate = _mm(dgate, u2, name="dw_gate", **dwc)
    gw_up = _mm(dup, u2, name="dw_up", **dwc)
    du2 = _mm(dgate, wg, name="du2_gate", caps=(512, 256, 11008))
    du2 = _mm(dup, wu, name="du2_up", res=du2, caps=(512, 256, 11008))
    dh1, dh1b, g_norm_ffn = _rmsnorm_bwd(du2, h1, norm_ffn_g, dh2, "norm_ffn_bwd")
    dcat = _mm(dh1b, wo, name="d_cat", tb=True)
    gw_out = jnp.concatenate([_mm(y_a, dh1b, name="dw_out_a", ta=True, out_dtype=BF16),
                              _mm(y_b, dh1b, name="dw_out_b", ta=True, out_dtype=BF16)], axis=0)
    g4a = [shards(g) for g in (gw_out, gw_gate, gw_up, gw_down)]
    sib_a = _sibling_swap(g4a, True, "grad_sibling")
    pair_a = [_pair_sum(a4, s, "grad_pair_sum_%d" % i) for i, (a4, s) in enumerate(zip(g4a, sib_a))]
    (dp_lru, g_conv_w, g_conv_b, g_wr, g_br, g_wi, g_bi, g_lam, g_lng) = _lru_bwd(
        p_lru, h_lru, dcat, prm_l, t_len, "lru_bwd")
    dy_t, dr_p, dk_p, dv_p, dg_t, g_ln_g, g_ln_b, g_r_k = _rwkv_post_bwd(
        y_t, r_t, k_t, v_t, g_t, ln_g, ln_b, r_k, dcat, "rwkv_post_bwd")
    dr_s, dw_s, dk_s, da_s, db_s, dv_s, *parts_a = _rwkv_scan_bwd(
        *sk, sv, _to_scan_v(dy_t, bsz, t_len), ckpt, name="rwkv_scan_bwd", comm=("peer", pair_a))
    grads = [_from_scan_k(dr_s, bsz, t_len), dr_p, _from_scan_k(dw_s, bsz, t_len), _from_scan_k(dk_s, bsz, t_len),
             dk_p, _from_scan_v(dv_s, bsz, t_len), dv_p, _from_scan_k(da_s, bsz, t_len),
             _from_scan_k(db_s, bsz, t_len), dg_t]
    (dq_r, dq_l, g_mu_r, g_mu_l, g_w0, g_a0, g_kk, g_ka, g_w2, g_a2, g_g2) = _rwkv_prep_bwd(
        p_rkv, p_lora, prm_r, grads, t_len, "rwkv_prep_bwd")
    dp_rkv = _shift_combine(dq_r, prm_r["mu_rkv"], t_len, "shift_bwd_rkv")
    dp_lora = _shift_combine(dq_l, prm_r["mu_lora"], t_len, "shift_bwd_lora")
    du1 = _mm(dp_rkv, w_rkv, name="du1_rkv")
    du1 = _mm(dp_lru, w_lru, name="du1_lru", res=du1)
    du1 = _mm(dp_lora, w_lora, name="du1_lora", res=du1)
    gx, _, g_norm_mix = _rmsnorm_bwd(du1, x2, norm_mix_g, dh1, "norm_mix_bwd")
    gi_rkv = _mm(dp_rkv, u1, name="dw_in_rkv", ta=True, out_dtype=BF16)
    gi_lru = _mm(dp_lru, u1, name="dw_in_lru", ta=True, out_dtype=BF16)
    gi_lora = _mm(dp_lora, u1, name="dw_in_lora", ta=True, out_dtype=BF16)
    gw_in = jnp.concatenate([gi_rkv, gi_lora[:dl], gi_lora[dlp:dlp + al], gi_lora[dlp + alp:dlp + alp + gl], gi_lru],
                            axis=0)

    g4b = [shards(gw_in)]
    sib_b = _sibling_swap(g4b, True, "grad_sibling_in")
    pair_b = [_pair_sum(g4b[0], sib_b[0], "grad_pair_sum_in")]
    parts_b = _peer_exchange(pair_b, "grad_chips_in")
    pair, parts = pair_b + pair_a, list(parts_b) + list(parts_a)
    mine = [_peer_sum(own4, p3, "grad_chip_sum_%d" % i) for i, (own4, p3) in enumerate(zip(pair, parts))]
    theirs = _sibling_swap(mine, False, "grad_share")

    g_mu = jnp.concatenate([g_mu_r, g_mu_l[:, :dl], g_mu_l[:, dlp:dlp + al], g_mu_l[:, dlp + alp:dlp + alp + gl]],
                           axis=1)
    small = dict(norm_mix_g=g_norm_mix, mu_shift=g_mu, rwkv_w0=g_w0, rwkv_w2=g_w2[:dl], rwkv_a0=g_a0,
                 rwkv_a2=g_a2[:al], rwkv_g2=g_g2[:gl], rwkv_k_k=g_kk, rwkv_k_a=g_ka, rwkv_r_k=g_r_k,
                 rwkv_ln_g=g_ln_g, rwkv_ln_b=g_ln_b, conv_w=g_conv_w, conv_b=g_conv_b, lru_wr=g_wr, lru_br=g_br,
                 lru_wi=g_wi, lru_bi=g_bi, lru_lambda=g_lam, lru_norm_g=g_lng, norm_ffn_g=g_norm_ffn,
                 norm_final_g=g_norm_final)
    small_names = list(small)
    sizes = [small[k].size for k in small_names]
    total = sum(sizes)
    padded = _ceil_to(total, 512 * LANES)

    def pack(arrs):
        flat = jnp.concatenate([a.reshape(-1) for a in arrs] + [jnp.zeros((padded - sum(a.size for a in arrs),), F32)])
        return flat.reshape(padded // LANES, LANES)

    packed = pack([small[k] for k in small_names])
    other = _sibling_swap([packed], False, "small_sibling")[0]
    chip_sum = _add2(packed, other, "small_pair_sum")
    all4 = _chip_broadcast([chip_sum], "small_chips")[0]
    red = _chip_sum(all4, "small_chip_sum").reshape(-1)
    small_g = {}
    off = 0
    for k, sz in zip(small_names, sizes):
        full_g = red[off:off + sz].reshape(small[k].shape)
        off += sz
        if k in small_sh:
            cs = full_g.shape[1] // 4
            full_g = lax.dynamic_slice_in_dim(full_g, chip * cs, cs, axis=1)
        small_g[k] = full_g.reshape(wts[k].shape)

    grad_w, delta_w, new_m, new_v = {}, {}, {}, {}
    for k, g_mine, g_theirs in zip(big, mine, theirs):
        res = _adamw_halves(g_mine, g_theirs, work(k, wts[k]), work(k, mom_m[k]), work(k, mom_v[k]), "adamw_" + k)
        grad_w[k], delta_w[k], new_m[k], new_v[k] = (unwork(k, t.reshape(2 * t.shape[1], t.shape[2])) for t in res)
    lsizes = [small_g[k].size for k in small_names]
    lpad = _ceil_to(sum(lsizes), 128 * LANES)

    def lpack(tree):
        arrs = [tree[k].reshape(-1) for k in small_names]
        flat = jnp.concatenate(arrs + [jnp.zeros((lpad - sum(lsizes),), F32)])
        return flat.reshape(lpad // LANES, LANES)

    dlt, mn, vn = _adamw(lpack(small_g), lpack(wts), lpack(mom_m), lpack(mom_v), "adamw_small")
    off = 0
    for k, sz in zip(small_names, lsizes):
        shp = wts[k].shape
        grad_w[k] = small_g[k]
        delta_w[k] = dlt.reshape(-1)[off:off + sz].reshape(shp)
        new_m[k] = mn.reshape(-1)[off:off + sz].reshape(shp)
        new_v[k] = vn.reshape(-1)[off:off + sz].reshape(shp)
        off += sz

    return (loss, gx.reshape(bsz, t_len, d), *[grad_w[k] for k in names], *[delta_w[k] for k in names],
            *[new_m[k] for k in names], *[new_v[k] for k in names])
```

```python
import jax
import jax.numpy as jnp
from jax import lax
from jax.experimental import pallas as pl
from jax.experimental.pallas import tpu as pltpu

F32 = jnp.float32
BF16 = jnp.bfloat16
MESH = pl.DeviceIdType.MESH
_call = pl.pallas_call

V7X_VMEM_LIMIT = 56 * 1024 * 1024
LANES = 128
SUBLANES = 8

RWKV_HEAD = 64
LRU_BLOCK_W = 128
CONV_WIDTH = 4
LRU_C = 8.0
NORM_EPS = 1e-6
GN_EPS = 64e-5
KK_EPS = 1e-24
SCAN_CHUNK = 16

ADAM_LR = 0.001
ADAM_B1 = 0.9
ADAM_B2 = 0.999
ADAM_EPS = 1e-08
ADAM_WD = 0.01
ADAM_STEP = 10
_BC1 = 1.0 - ADAM_B1 ** ADAM_STEP
_BC2 = 1.0 - ADAM_B2 ** ADAM_STEP

_HI = lax.Precision.HIGHEST


def _cp(*sem):
    return pltpu.CompilerParams(dimension_semantics=tuple(sem), vmem_limit_bytes=V7X_VMEM_LIMIT)


def _tile(n, cap, unit=LANES):
    if n <= cap:
        return n
    best = None
    d = unit
    while d <= cap:
        if n % d == 0:
            best = d
        d += unit
    return n if best is None else best


def _ceil_to(n, m):
    return -(-n // m) * m


ELEMENTWISE_BLOCK_BYTES = 3 * 512 * 1024


def _col_tile(rows, cols):
    cap = max(LANES, ELEMENTWISE_BLOCK_BYTES // (4 * rows) // LANES * LANES)
    return _tile(cols, cap)


def _sig(x):
    return 1.0 / (1.0 + jnp.exp(-x))


def _log1p(x):
    return jnp.where(x < 0.01, x * (1.0 - x * (0.5 - x * (1.0 / 3.0))), jnp.log(1.0 + x))


def _softplus(x):
    return jnp.maximum(x, 0.0) + _log1p(jnp.exp(-jnp.abs(x)))


def _neg_expm1(x):
    small = -x * (1.0 + x * (0.5 + x * (1.0 / 6.0)))
    return jnp.where(x > -0.01, small, 1.0 - jnp.exp(x))


_GELU_K = 0.7978845608028654
_GELU_C = 0.044715


def _gelu_parts(x):
    th = jnp.tanh(_GELU_K * (x + _GELU_C * x * x * x))
    return 0.5 * x * (1.0 + th), th


def _gelu_grad(x, th):
    return 0.5 * (1.0 + th) + 0.5 * x * (1.0 - th * th) * _GELU_K * (1.0 + 3.0 * _GELU_C * x * x)


def _shift_down(x, prev8, j):
    tb = x.shape[0]
    xr = pltpu.roll(x, j, 0)
    pr = pltpu.roll(prev8, j, 0)
    row = lax.broadcasted_iota(jnp.int32, prev8.shape, 0)
    first = jnp.where(row < j, pr, xr[0:SUBLANES])
    if tb == SUBLANES:
        return first
    return jnp.concatenate([first, xr[SUBLANES:]], axis=0)


def _shift_up(x, next8, j):
    tb = x.shape[0]
    xr = pltpu.roll(x, tb - j, 0)
    nr = pltpu.roll(next8, SUBLANES - j, 0)
    row = lax.broadcasted_iota(jnp.int32, next8.shape, 0)
    last = jnp.where(row >= SUBLANES - j, nr, xr[tb - SUBLANES:])
    if tb == SUBLANES:
        return last
    return jnp.concatenate([xr[:tb - SUBLANES], last], axis=0)


def _head_mats(width, heads_pad):
    e = (lax.broadcasted_iota(jnp.int32, (width, heads_pad), 0) // RWKV_HEAD
         == lax.broadcasted_iota(jnp.int32, (width, heads_pad), 1)).astype(F32)
    et = (lax.broadcasted_iota(jnp.int32, (heads_pad, width), 1) // RWKV_HEAD
          == lax.broadcasted_iota(jnp.int32, (heads_pad, width), 0)).astype(F32)
    return e, et


def _headsum(x, e, et):
    s = jnp.dot(x, e, preferred_element_type=F32, precision=_HI)
    return jnp.dot(s, et, preferred_element_type=F32, precision=_HI)


def _dot(a, b):
    return jnp.dot(a.astype(BF16), b.astype(BF16), preferred_element_type=F32)


def _dot_tn(a, b):
    return lax.dot_general(a.astype(BF16), b.astype(BF16), (((0,), (0,)), ((), ())), preferred_element_type=F32)


def _dot_nt(a, b):
    return lax.dot_general(a.astype(BF16), b.astype(BF16), (((1,), (1,)), ((), ())), preferred_element_type=F32)


def _mm(a, b, *, name, ta=False, tb=False, out_dtype=F32, res=None, n_outer=False, caps=(1024, 512, 4096),
        comm=None):
    m = a.shape[1] if ta else a.shape[0]
    kd = a.shape[0] if ta else a.shape[1]
    n = b.shape[0] if tb else b.shape[1]
    assert kd == (b.shape[1] if tb else b.shape[0])
    tm, tn, tk = _tile(m, caps[0]), _tile(n, caps[1]), _tile(kd, caps[2])
    gm, gn, gk = m // tm, n // tn, kd // tk
    dims = (((0 if ta else 1,), (1 if tb else 0,)), ((), ()))
    grid = (gn, gm, gk) if n_outer else (gm, gn, gk)
    cx, cx_specs, c_outs, c_sems, c_start, c_finish = _carried(comm)
    nx, n_in = len(cx), 3 if res is not None else 2

    def ij(g0, g1):
        return (g1, g0) if n_outer else (g0, g1)

    def a_map(g0, g1, k):
        i, _ = ij(g0, g1)
        return (k, i) if ta else (i, k)

    def b_map(g0, g1, k):
        _, j = ij(g0, g1)
        return (j, k) if tb else (k, j)

    def o_map(g0, g1, k):
        return ij(g0, g1)

    has_res = res is not None

    def body(*refs):
        a_ref, b_ref = refs[0], refs[1]
        res_ref = refs[2] if has_res else None
        c_ins = refs[n_in:n_in + nx]
        o_ref = refs[n_in + nx]
        c_out_refs = refs[n_in + nx + 1:n_in + nx + 1 + len(c_outs)]
        acc_ref = refs[n_in + nx + 1 + len(c_outs)] if gk > 1 else None
        steps = [pl.program_id(ax) for ax in range(3)]
        if nx:
            @pl.when(jnp.logical_and(jnp.logical_and(steps[0] == 0, steps[1] == 0), steps[2] == 0))
            def _():
                c_start(c_ins, c_out_refs, refs[-2], refs[-1])

        prod = lax.dot_general(a_ref[...], b_ref[...], dims, preferred_element_type=F32)

        def finish(acc):
            if has_res:
                acc = acc + res_ref[...]
            o_ref[...] = acc.astype(out_dtype)

        if gk == 1:
            finish(prod)
        else:
            k = steps[2]

            @pl.when(k == 0)
            def _():
                acc_ref[...] = prod

            @pl.when(k > 0)
            def _():
                acc_ref[...] += prod

            @pl.when(k == gk - 1)
            def _():
                finish(acc_ref[...])

        if nx:
            @pl.when(jnp.logical_and(jnp.logical_and(steps[0] == grid[0] - 1, steps[1] == grid[1] - 1),
                                     steps[2] == grid[2] - 1))
            def _():
                c_finish(c_ins, c_out_refs, refs[-2], refs[-1])

    in_specs = [pl.BlockSpec((tk, tm) if ta else (tm, tk), a_map),
                pl.BlockSpec((tn, tk) if tb else (tk, tn), b_map)]
    args = [a, b]
    if has_res:
        in_specs.append(pl.BlockSpec((tm, tn), o_map))
        args.append(res)
    out = _call(
        body, name=name, grid=grid, in_specs=in_specs + cx_specs,
        out_specs=[pl.BlockSpec((tm, tn), o_map)] + [_HBM] * len(c_outs),
        out_shape=[jax.ShapeDtypeStruct((m, n), out_dtype)] + c_outs,
        scratch_shapes=([pltpu.VMEM((tm, tn), F32)] if gk > 1 else []) + c_sems,
        compiler_params=_cp(*(("arbitrary",) * 3 if nx else ("parallel", "parallel", "arbitrary"))),
    )(*args, *cx)
    return out if nx else out[0]


def _rmsnorm_fwd(x, g, name):
    n, d = x.shape
    tb = _tile(n, 256, SUBLANES)

    def body(x_ref, g_ref, u_ref):
        xv = x_ref[...]
        rstd = lax.rsqrt(jnp.mean(xv * xv, axis=-1, keepdims=True) + NORM_EPS)
        u_ref[...] = (xv * rstd * g_ref[...]).astype(BF16)

    row = pl.BlockSpec((tb, d), lambda i: (i, 0))
    vec = pl.BlockSpec((1, d), lambda i: (0, 0))
    return _call(body, name=name, grid=(n // tb,), in_specs=[row, vec], out_specs=row,
                 out_shape=jax.ShapeDtypeStruct((n, d), BF16), compiler_params=_cp("parallel"))(x, g)


def _rmsnorm_bwd(du, x, g, dres, name):
    n, d = x.shape
    tb = _tile(n, 256, SUBLANES)

    def body(du_ref, x_ref, g_ref, dres_ref, dx_ref, dxb_ref, dg_ref):
        xv = x_ref[...]
        rstd = lax.rsqrt(jnp.mean(xv * xv, axis=-1, keepdims=True) + NORM_EPS)
        xh = xv * rstd
        duv = du_ref[...]
        t = duv * g_ref[...]
        dx = dres_ref[...] + rstd * (t - xh * jnp.mean(t * xh, axis=-1, keepdims=True))
        dx_ref[...] = dx
        dxb_ref[...] = dx.astype(BF16)

        @pl.when(pl.program_id(0) == 0)
        def _():
            dg_ref[...] = jnp.zeros_like(dg_ref)

        dg_ref[...] += jnp.sum(duv * xh, axis=0, keepdims=True)

    row = pl.BlockSpec((tb, d), lambda i: (i, 0))
    vec = pl.BlockSpec((1, d), lambda i: (0, 0))
    return _call(body, name=name, grid=(n // tb,), in_specs=[row, row, vec, row], out_specs=[row, row, vec],
                 out_shape=[jax.ShapeDtypeStruct((n, d), F32), jax.ShapeDtypeStruct((n, d), BF16),
                            jax.ShapeDtypeStruct((1, d), F32)],
                 compiler_params=_cp("arbitrary"))(du, x, g, dres)


def _loss_head(h, g, target, name):
    n, d = h.shape
    tb = _tile(n, 256, SUBLANES)

    def body(h_ref, g_ref, t_ref, dh_ref, dhb_ref, dg_ref, loss_ref):
        hv = h_ref[...]
        gv = g_ref[...]
        rstd = lax.rsqrt(jnp.mean(hv * hv, axis=-1, keepdims=True) + NORM_EPS)
        hh = hv * rstd
        err = hh * gv - t_ref[...]
        dy = err * (1.0 / d)
        dhh = dy * gv
        dh = rstd * (dhh - hh * jnp.mean(dhh * hh, axis=-1, keepdims=True))
        dh_ref[...] = dh
        dhb_ref[...] = dh.astype(BF16)

        @pl.when(pl.program_id(0) == 0)
        def _():
            dg_ref[...] = jnp.zeros_like(dg_ref)
            loss_ref[...] = jnp.zeros_like(loss_ref)

        dg_ref[...] += jnp.sum(dy * hh, axis=0, keepdims=True)
        loss_ref[...] += jnp.sum(err * err) * (0.5 / d)

    row = pl.BlockSpec((tb, d), lambda i: (i, 0))
    vec = pl.BlockSpec((1, d), lambda i: (0, 0))
    lvec = pl.BlockSpec((1, LANES), lambda i: (0, 0))
    return _call(body, name=name, grid=(n // tb,), in_specs=[row, vec, row], out_specs=[row, row, vec, lvec],
                 out_shape=[jax.ShapeDtypeStruct((n, d), F32), jax.ShapeDtypeStruct((n, d), BF16),
                            jax.ShapeDtypeStruct((1, d), F32), jax.ShapeDtypeStruct((1, LANES), F32)],
                 compiler_params=_cp("arbitrary"))(h, g, target)


def _swiglu_fwd(gate, up, name):
    n, f = gate.shape
    tb, tc = _tile(n, 1024, SUBLANES), _tile(f, 256)

    def body(g_ref, u_ref, o_ref):
        gv = g_ref[...]
        o_ref[...] = (gv * _sig(gv) * u_ref[...]).astype(BF16)

    blk = pl.BlockSpec((tb, tc), lambda i, j: (i, j))
    return _call(body, name=name, grid=(n // tb, f // tc), in_specs=[blk, blk], out_specs=blk,
                 out_shape=jax.ShapeDtypeStruct((n, f), BF16), compiler_params=_cp("parallel", "parallel"))(gate, up)


def _swiglu_bwd(gate, up, dact, name):
    n, f = gate.shape
    tb, tc = _tile(n, 1024, SUBLANES), _tile(f, 256)

    def body(g_ref, u_ref, d_ref, dg_ref, du_ref):
        gv = g_ref[...]
        s = _sig(gv)
        dv = d_ref[...]
        dg_ref[...] = (dv * u_ref[...] * s * (1.0 + gv * (1.0 - s))).astype(BF16)
        du_ref[...] = (dv * gv * s).astype(BF16)

    blk = pl.BlockSpec((tb, tc), lambda i, j: (i, j))
    return _call(body, name=name, grid=(n // tb, f // tc), in_specs=[blk, blk, blk], out_specs=[blk, blk],
                 out_shape=[jax.ShapeDtypeStruct((n, f), BF16)] * 2,
                 compiler_params=_cp("parallel", "parallel"))(gate, up, dact)


def _prep_common(prkv_ref, prkvp_ref, plo_ref, plop_ref, mur_ref, mul_ref, w0_ref, a0_ref, kk_ref, ka_ref,
                 w2_ref, a2_ref, g2_ref, seq_start, w, dlp, alp):
    z8r = jnp.zeros_like(prkvp_ref[...])
    z8l = jnp.zeros_like(plop_ref[...])
    prev_r = jnp.where(seq_start, z8r, prkvp_ref[...])
    prev_l = jnp.where(seq_start, z8l, plop_ref[...])
    p_r = prkv_ref[...]
    p_l = plo_ref[...]
    dif_r = _shift_down(p_r, prev_r, 1) - p_r
    dif_l = _shift_down(p_l, prev_l, 1) - p_l
    q_r = p_r + dif_r * mur_ref[...]
    q_l = p_l + dif_l * mul_ref[...]
    r, k, v = q_r[:, 0:w], q_r[:, w:2 * w], q_r[:, 2 * w:3 * w]
    wd, ad, gd = q_l[:, 0:dlp], q_l[:, dlp:dlp + alp], q_l[:, dlp + alp:]
    tw = jnp.tanh(wd)
    zw = w0_ref[...] + _dot(tw, w2_ref[...])
    wlog = -_softplus(-zw) - 0.5
    ew = jnp.exp(wlog)
    dec = jnp.exp(-ew)
    za = a0_ref[...] + _dot(ad, a2_ref[...])
    av = _sig(za)
    sg = _sig(gd)
    g = _dot(sg, g2_ref[...])
    return dict(dif_r=dif_r, dif_l=dif_l, r=r, k=k, v=v, ad=ad, tw=tw, zw=zw, ew=ew, dec=dec, av=av, sg=sg, g=g)


def _rwkv_prep_specs(n, tb, w, lp, t_len):
    nb8 = tb // SUBLANES
    row3 = pl.BlockSpec((tb, 3 * w), lambda i: (i, 0))
    prev3 = pl.BlockSpec((SUBLANES, 3 * w), lambda i: (jnp.maximum(i * nb8 - 1, 0), 0))
    rowl = pl.BlockSpec((tb, lp), lambda i: (i, 0))
    prevl = pl.BlockSpec((SUBLANES, lp), lambda i: (jnp.maximum(i * nb8 - 1, 0), 0))
    return row3, prev3, rowl, prevl


def _rwkv_prep_fwd(p_rkv, p_lora, prm, t_len, name):
    n, w3 = p_rkv.shape
    w = w3 // 3
    lp = p_lora.shape[1]
    dlp, alp = prm["w2"].shape[0], prm["a2"].shape[0]
    glp = lp - dlp - alp
    hp = max(w // RWKV_HEAD, LANES)
    tb = _tile(min(n, t_len), 128, SUBLANES)
    bps = t_len // tb

    def body(prkv_ref, prkvp_ref, plo_ref, plop_ref, mur_ref, mul_ref, w0_ref, a0_ref, kk_ref, ka_ref,
             w2_ref, a2_ref, g2_ref, r_o, dec_o, k_o, v_o, na_o, nb_o, g_o):
        seq_start = (pl.program_id(0) % bps) == 0
        f = _prep_common(prkv_ref, prkvp_ref, plo_ref, plop_ref, mur_ref, mul_ref, w0_ref, a0_ref, kk_ref, ka_ref,
                         w2_ref, a2_ref, g2_ref, seq_start, w, dlp, alp)
        e, et = _head_mats(w, hp)
        kk0 = f["k"] * kk_ref[...]
        inv = lax.rsqrt(jnp.maximum(_headsum(kk0 * kk0, e, et), KK_EPS))
        kk = kk0 * inv
        r_o[...] = f["r"]
        dec_o[...] = f["dec"]
        k_o[...] = f["k"] * (1.0 + (f["av"] - 1.0) * ka_ref[...])
        v_o[...] = f["v"]
        na_o[...] = -kk
        nb_o[...] = kk * f["av"]
        g_o[...] = f["g"]

    row3, prev3, rowl, prevl = _rwkv_prep_specs(n, tb, w, lp, t_len)
    c0 = lambda i: (0, 0)
    vec3 = pl.BlockSpec((1, 3 * w), c0)
    vecl = pl.BlockSpec((1, lp), c0)
    vec = pl.BlockSpec((1, w), c0)
    out = pl.BlockSpec((tb, w), lambda i: (i, 0))
    return _call(
        body, name=name, grid=(n // tb,),
        in_specs=[row3, prev3, rowl, prevl, vec3, vecl, vec, vec, vec, vec,
                  pl.BlockSpec((dlp, w), c0), pl.BlockSpec((alp, w), c0), pl.BlockSpec((glp, w), c0)],
        out_specs=[out] * 7, out_shape=[jax.ShapeDtypeStruct((n, w), F32)] * 7,
        compiler_params=_cp("parallel"),
    )(p_rkv, p_rkv, p_lora, p_lora, prm["mu_rkv"], prm["mu_lora"], prm["w0"], prm["a0"], prm["k_k"], prm["k_a"],
      prm["w2"], prm["a2"], prm["g2"])


def _rwkv_prep_bwd(p_rkv, p_lora, prm, grads, t_len, name):
    n, w3 = p_rkv.shape
    w = w3 // 3
    lp = p_lora.shape[1]
    dlp, alp = prm["w2"].shape[0], prm["a2"].shape[0]
    glp = lp - dlp - alp
    hp = max(w // RWKV_HEAD, LANES)
    tb = _tile(min(n, t_len), 64, SUBLANES)
    bps = t_len // tb

    def body(prkv_ref, prkvp_ref, plo_ref, plop_ref, mur_ref, mul_ref, w0_ref, a0_ref, kk_ref, ka_ref,
             w2_ref, a2_ref, g2_ref,
             drs_ref, drp_ref, ddec_ref, dks_ref, dkp_ref, dvs_ref, dvp_ref, dna_ref, dnb_ref, dg_ref,
             dqr_o, dql_o, dmur_o, dmul_o, dw0_o, da0_o, dkk_o, dka_o, dw2_o, da2_o, dg2_o):
        seq_start = (pl.program_id(0) % bps) == 0
        f = _prep_common(prkv_ref, prkvp_ref, plo_ref, plop_ref, mur_ref, mul_ref, w0_ref, a0_ref, kk_ref, ka_ref,
                         w2_ref, a2_ref, g2_ref, seq_start, w, dlp, alp)
        e, et = _head_mats(w, hp)
        k, av = f["k"], f["av"]
        k_k, k_a = kk_ref[...], ka_ref[...]
        kk0 = k * k_k
        n2 = _headsum(kk0 * kk0, e, et)
        inv = lax.rsqrt(jnp.maximum(n2, KK_EPS))
        kk = kk0 * inv
        dk2 = dks_ref[...] + dkp_ref[...]
        dnb = dnb_ref[...]
        dkk = dnb * av - dna_ref[...]
        dav = dnb * kk + dk2 * k * k_a
        dk = dk2 * (1.0 + (av - 1.0) * k_a)
        dka = dk2 * k * (av - 1.0)
        proj = jnp.where(n2 > KK_EPS, _headsum(dkk * kk, e, et), 0.0)
        dkk0 = inv * (dkk - kk * proj)
        dk = dk + dkk0 * k_k
        dkkp = dkk0 * k
        dgv = dg_ref[...]
        sg = f["sg"]
        dgd = _dot_nt(dgv, g2_ref[...]) * sg * (1.0 - sg)
        dza = dav * av * (1.0 - av)
        dad = _dot_nt(dza, a2_ref[...])
        dzw = ddec_ref[...] * f["dec"] * (-f["ew"]) * _sig(-f["zw"])
        tw = f["tw"]
        dwd = _dot_nt(dzw, w2_ref[...]) * (1.0 - tw * tw)
        dq_r = jnp.concatenate([drs_ref[...] + drp_ref[...], dk, dvs_ref[...] + dvp_ref[...]], axis=1)
        dq_l = jnp.concatenate([dwd, dad, dgd], axis=1)
        dqr_o[...] = dq_r
        dql_o[...] = dq_l

        @pl.when(pl.program_id(0) == 0)
        def _():
            for o in (dmur_o, dmul_o, dw0_o, da0_o, dkk_o, dka_o, dw2_o, da2_o, dg2_o):
                o[...] = jnp.zeros_like(o)

        def rsum(x):
            return jnp.sum(x, axis=0, keepdims=True)

        dmur_o[...] += rsum(dq_r * f["dif_r"])
        dmul_o[...] += rsum(dq_l * f["dif_l"])
        dw0_o[...] += rsum(dzw)
        da0_o[...] += rsum(dza)
        dkk_o[...] += rsum(dkkp)
        dka_o[...] += rsum(dka)
        dw2_o[...] += _dot_tn(tw, dzw)
        da2_o[...] += _dot_tn(f["ad"], dza)
        dg2_o[...] += _dot_tn(sg, dgv)

    row3, prev3, rowl, prevl = _rwkv_prep_specs(n, tb, w, lp, t_len)
    c0 = lambda i: (0, 0)
    vec3 = pl.BlockSpec((1, 3 * w), c0)
    vecl = pl.BlockSpec((1, lp), c0)
    vec = pl.BlockSpec((1, w), c0)
    blk = pl.BlockSpec((tb, w), lambda i: (i, 0))
    m2, ma, mg = pl.BlockSpec((dlp, w), c0), pl.BlockSpec((alp, w), c0), pl.BlockSpec((glp, w), c0)
    sds = jax.ShapeDtypeStruct
    return _call(
        body, name=name, grid=(n // tb,),
        in_specs=[row3, prev3, rowl, prevl, vec3, vecl, vec, vec, vec, vec, m2, ma, mg] + [blk] * 10,
        out_specs=[row3, rowl, vec3, vecl, vec, vec, vec, vec, m2, ma, mg],
        out_shape=[sds((n, 3 * w), F32), sds((n, lp), F32), sds((1, 3 * w), F32), sds((1, lp), F32),
                   sds((1, w), F32), sds((1, w), F32), sds((1, w), F32), sds((1, w), F32),
                   sds((dlp, w), F32), sds((alp, w), F32), sds((glp, w), F32)],
        compiler_params=_cp("arbitrary"),
    )(p_rkv, p_rkv, p_lora, p_lora, prm["mu_rkv"], prm["mu_lora"], prm["w0"], prm["a0"], prm["k_k"], prm["k_a"],
      prm["w2"], prm["a2"], prm["g2"], *grads)


def _shift_combine(dq, mu, t_len, name):
    n, c = dq.shape
    tb = _tile(min(n, t_len), 256, SUBLANES)
    bps = t_len // tb
    nb8 = tb // SUBLANES
    last8 = n // SUBLANES - 1

    def body(x_ref, nx_ref, mu_ref, o_ref):
        seq_end = (pl.program_id(0) % bps) == bps - 1
        nxt = jnp.where(seq_end, jnp.zeros_like(nx_ref[...]), nx_ref[...])
        x = x_ref[...]
        muv = mu_ref[...]
        o_ref[...] = ((1.0 - muv) * x + muv * _shift_up(x, nxt, 1)).astype(BF16)

    row = pl.BlockSpec((tb, c), lambda i: (i, 0))
    nxt = pl.BlockSpec((SUBLANES, c), lambda i: (jnp.minimum((i + 1) * nb8, last8), 0))
    vec = pl.BlockSpec((1, c), lambda i: (0, 0))
    return _call(body, name=name, grid=(n // tb,), in_specs=[row, nxt, vec], out_specs=row,
                 out_shape=jax.ShapeDtypeStruct((n, c), BF16), compiler_params=_cp("parallel"))(dq, dq, mu)


def _scan_step(s_i, a_t, w_t, b_t, k_t, v_i):
    sa = jnp.sum(s_i * a_t, axis=0, keepdims=True)
    return s_i * w_t + sa * b_t + v_i * k_t, sa


def _carried(comm):
    if comm is None:
        return [], [], [], [], None, None
    outs, n_sems, start, finish = _comm_plan(*comm)
    xs = list(comm[1])
    sems = [pltpu.SemaphoreType.DMA((n_sems,)), pltpu.SemaphoreType.DMA((n_sems,))]
    return xs, [_HBM] * len(xs), outs, sems, start, finish


def _rwkv_scan_fwd(r, w, k, a, b, v, name, comm=None):
    t_len, kd, ln = r.shape
    vh = v.shape[1]
    tc = SCAN_CHUNK
    nc = t_len // tc
    cx, cx_specs, c_outs, c_sems, c_start, c_finish = _carried(comm)
    nx = len(cx)

    def body(r_ref, w_ref, k_ref, a_ref, b_ref, v_ref, *rest):
        c_ins, (y_ref, ck_ref), c_out_refs = rest[:nx], rest[nx:nx + 2], rest[nx + 2:nx + 2 + len(c_outs)]
        s_ref = rest[nx + 2 + len(c_outs)]

        @pl.when(pl.program_id(0) == 0)
        def _():
            s_ref[...] = jnp.zeros_like(s_ref)
            if nx:
                c_start(c_ins, c_out_refs, rest[-2], rest[-1])

        ck_ref[0] = s_ref[...]

        def step(t, carry):
            a_t, w_t, b_t, k_t, r_t = a_ref[t], w_ref[t], b_ref[t], k_ref[t], r_ref[t]
            for i in range(vh):
                s_new, _ = _scan_step(s_ref[i], a_t, w_t, b_t, k_t, v_ref[t, pl.ds(i, 1), :])
                s_ref[i] = s_new
                y_ref[t, pl.ds(i, 1), :] = jnp.sum(s_new * r_t, axis=0, keepdims=True)
            return carry

        lax.fori_loop(0, tc, step, 0)

        if nx:
            @pl.when(pl.program_id(0) == nc - 1)
            def _():
                c_finish(c_ins, c_out_refs, rest[-2], rest[-1])

    kblk = pl.BlockSpec((tc, kd, ln), lambda c: (c, 0, 0))
    vblk = pl.BlockSpec((tc, vh, ln), lambda c: (c, 0, 0))
    return _call(
        body, name=name, grid=(nc,), in_specs=[kblk] * 5 + [vblk] + cx_specs,
        out_specs=[vblk, pl.BlockSpec((1, vh, kd, ln), lambda c: (c, 0, 0, 0))] + [_HBM] * len(c_outs),
        out_shape=[jax.ShapeDtypeStruct((t_len, vh, ln), F32), jax.ShapeDtypeStruct((nc, vh, kd, ln), F32)] + c_outs,
        scratch_shapes=[pltpu.VMEM((vh, kd, ln), F32)] + c_sems,
        compiler_params=_cp("arbitrary"),
    )(r, w, k, a, b, v, *cx)


def _rwkv_scan_bwd(r, w, k, a, b, v, dy, ckpt, name, comm=None):
    t_len, kd, ln = r.shape
    vh = v.shape[1]
    tc = SCAN_CHUNK
    nc = t_len // tc
    half = ln // 2
    cx, cx_specs, c_outs, c_sems, c_start, c_finish = _carried(comm)
    nx = len(cx)

    def body(r_ref, w_ref, k_ref, a_ref, b_ref, v_ref, dy_ref, ck_ref, *rest):
        c_ins = rest[:nx]
        dr_o, dw_o, dk_o, da_o, db_o, dv_o = rest[nx:nx + 6]
        c_out_refs = rest[nx + 6:nx + 6 + len(c_outs)]
        sbuf, ds_ref, sa_buf = rest[nx + 6 + len(c_outs):nx + 9 + len(c_outs)]

        @pl.when(pl.program_id(0) == 0)
        def _():
            ds_ref[...] = jnp.zeros_like(ds_ref)
            if nx:
                c_start(c_ins, c_out_refs, rest[-2], rest[-1])

        sbuf[0] = ck_ref[0]

        def fwd(t, carry):
            a_t, w_t, b_t, k_t = a_ref[t], w_ref[t], b_ref[t], k_ref[t]
            for i in range(vh):
                s_new, sa = _scan_step(sbuf[t, i], a_t, w_t, b_t, k_t, v_ref[t, pl.ds(i, 1), :])
                sbuf[t + 1, i] = s_new
                sa_buf[t, pl.ds(i, 1), :] = sa
            return carry

        lax.fori_loop(0, tc, fwd, 0)

        def bwd(tt, carry):
            t = tc - 1 - tt
            a_t, w_t, b_t, k_t, r_t = a_ref[t], w_ref[t], b_ref[t], k_ref[t], r_ref[t]
            z = jnp.zeros((kd, ln), F32)
            dr, dw, dk, da, db = z, z, z, z, z
            for i in range(vh):
                dy_i = dy_ref[t, pl.ds(i, 1), :]
                s_t = sbuf[t + 1, i]
                s_p = sbuf[t, i]
                d = ds_ref[i] + dy_i * r_t
                dr = dr + s_t * dy_i
                dv_o[t, pl.ds(i, 1), :] = jnp.sum(d * k_t, axis=0, keepdims=True)
                dk = dk + d * v_ref[t, pl.ds(i, 1), :]
                dsa = jnp.sum(d * b_t, axis=0, keepdims=True)
                db = db + d * sa_buf[t, pl.ds(i, 1), :]
                dw = dw + d * s_p
                da = da + s_p * dsa
                ds_ref[i] = d * w_t + dsa * a_t

            def both(x):
                return x + pltpu.roll(x, half, 1)

            dr_o[t] = both(dr)
            dw_o[t] = both(dw)
            dk_o[t] = both(dk)
            da_o[t] = both(da)
            db_o[t] = both(db)
            return carry

        lax.fori_loop(0, tc, bwd, 0)

        if nx:
            @pl.when(pl.program_id(0) == nc - 1)
            def _():
                c_finish(c_ins, c_out_refs, rest[-2], rest[-1])

    kblk = pl.BlockSpec((tc, kd, ln), lambda c: (nc - 1 - c, 0, 0))
    vblk = pl.BlockSpec((tc, vh, ln), lambda c: (nc - 1 - c, 0, 0))
    ksd = jax.ShapeDtypeStruct((t_len, kd, ln), F32)
    return _call(
        body, name=name, grid=(nc,),
        in_specs=[kblk] * 5 + [vblk, vblk, pl.BlockSpec((1, vh, kd, ln), lambda c: (nc - 1 - c, 0, 0, 0))] + cx_specs,
        out_specs=[kblk] * 5 + [vblk] + [_HBM] * len(c_outs),
        out_shape=[ksd] * 5 + [jax.ShapeDtypeStruct((t_len, vh, ln), F32)] + c_outs,
        scratch_shapes=[pltpu.VMEM((tc + 1, vh, kd, ln), F32), pltpu.VMEM((vh, kd, ln), F32),
                        pltpu.VMEM((tc, vh, ln), F32)] + c_sems,
        compiler_params=_cp("arbitrary"),
    )(r, w, k, a, b, v, dy, ckpt, *cx)


def _post_common(y_ref, r_ref, k_ref, v_ref, lng_ref, lnb_ref, rk_ref, e, et):
    y = y_ref[...]
    inv_n = 1.0 / RWKV_HEAD
    mean = _headsum(y, e, et) * inv_n
    yc = y - mean
    var = _headsum(yc * yc, e, et) * inv_n
    rstd = lax.rsqrt(var + GN_EPS)
    yh = yc * rstd
    yn = yh * lng_ref[...] + lnb_ref[...]
    bonus = _headsum(r_ref[...] * k_ref[...] * rk_ref[...], e, et)
    return yh, rstd, yn, bonus


def _rwkv_post_fwd(y, r, k, v, g, ln_g, ln_b, r_k, name):
    n, w = y.shape
    hp = max(w // RWKV_HEAD, LANES)
    tb = _tile(n, 256, SUBLANES)

    def body(y_ref, r_ref, k_ref, v_ref, g_ref, lng_ref, lnb_ref, rk_ref, o_ref):
        e, et = _head_mats(w, hp)
        _, _, yn, bonus = _post_common(y_ref, r_ref, k_ref, v_ref, lng_ref, lnb_ref, rk_ref, e, et)
        o_ref[...] = ((yn + bonus * v_ref[...]) * g_ref[...]).astype(BF16)

    blk = pl.BlockSpec((tb, w), lambda i: (i, 0))
    vec = pl.BlockSpec((1, w), lambda i: (0, 0))
    return _call(body, name=name, grid=(n // tb,), in_specs=[blk] * 5 + [vec] * 3, out_specs=blk,
                 out_shape=jax.ShapeDtypeStruct((n, w), BF16),
                 compiler_params=_cp("parallel"))(y, r, k, v, g, ln_g, ln_b, r_k)


def _rwkv_post_bwd(y, r, k, v, g, ln_g, ln_b, r_k, do_cat, name):
    n, w = y.shape
    hp = max(w // RWKV_HEAD, LANES)
    tb = _tile(n, 128, SUBLANES)

    def body(y_ref, r_ref, k_ref, v_ref, g_ref, lng_ref, lnb_ref, rk_ref, do_ref,
             dy_o, dr_o, dk_o, dv_o, dg_o, dlng_o, dlnb_o, drk_o):
        e, et = _head_mats(w, hp)
        yh, rstd, yn, bonus = _post_common(y_ref, r_ref, k_ref, v_ref, lng_ref, lnb_ref, rk_ref, e, et)
        do = do_ref[...]
        vv, rv, kv, rk = v_ref[...], r_ref[...], k_ref[...], rk_ref[...]
        dg_o[...] = do * (yn + bonus * vv)
        dz = do * g_ref[...]
        dbonus = _headsum(dz * vv, e, et)
        dv_o[...] = dz * bonus
        dr_o[...] = dbonus * kv * rk
        dk_o[...] = dbonus * rv * rk
        dyh = dz * lng_ref[...]
        inv_n = 1.0 / RWKV_HEAD
        dy_o[...] = rstd * (dyh - _headsum(dyh, e, et) * inv_n - yh * (_headsum(dyh * yh, e, et) * inv_n))

        @pl.when(pl.program_id(0) == 0)
        def _():
            for o in (dlng_o, dlnb_o, drk_o):
                o[...] = jnp.zeros_like(o)

        dlng_o[...] += jnp.sum(dz * yh, axis=0, keepdims=True)
        dlnb_o[...] += jnp.sum(dz, axis=0, keepdims=True)
        drk_o[...] += jnp.sum(dbonus * rv * kv, axis=0, keepdims=True)

    blk = pl.BlockSpec((tb, w), lambda i: (i, 0))
    vec = pl.BlockSpec((1, w), lambda i: (0, 0))
    sds = jax.ShapeDtypeStruct
    return _call(body, name=name, grid=(n // tb,), in_specs=[blk] * 5 + [vec] * 3 + [blk],
                 out_specs=[blk] * 5 + [vec] * 3,
                 out_shape=[sds((n, w), F32)] * 5 + [sds((1, w), F32)] * 3,
                 compiler_params=_cp("arbitrary"))(y, r, k, v, g, ln_g, ln_b, r_k, do_cat)


def _lru_gates(xb, prev8, gate, cw_ref, cb_ref, wr_ref, br_ref, wi_ref, bi_ref, lam_ref, is_t0):
    c = xb.shape[1]
    nblk = c // LRU_BLOCK_W
    xs = [xb] + [_shift_down(xb, prev8, j) for j in range(1, CONV_WIDTH)]
    xc = cb_ref[...]
    for j in range(CONV_WIDTH):
        xc = xc + xs[CONV_WIDTH - 1 - j] * cw_ref[pl.ds(j, 1), :]
    xcb = xc.astype(BF16)

    def blockmm(w_ref):
        return jnp.concatenate(
            [jnp.dot(xcb[:, h * LRU_BLOCK_W:(h + 1) * LRU_BLOCK_W], w_ref[h], preferred_element_type=F32)
             for h in range(nblk)], axis=1)

    rg = _sig(blockmm(wr_ref) + br_ref[...])
    ig = _sig(blockmm(wi_ref) + bi_ref[...])
    sp = _softplus(-lam_ref[...])
    la = -LRU_C * rg * sp
    av = jnp.exp(la)
    mult = jnp.where(is_t0, 1.0, jnp.sqrt(_neg_expm1(2.0 * la)))
    ge, th = _gelu_parts(gate)
    return dict(xs=xs, xc=xc, xcb=xcb, rg=rg, ig=ig, sp=sp, a=av, mult=mult, ge=ge, th=th)


def _lru_specs(tb, c, nb, rev):
    nb8 = tb // SUBLANES

    def blk_i(i):
        return nb - 1 - i if rev else i

    xb = pl.BlockSpec((tb, c), lambda b, i: (b * nb + blk_i(i), 0))
    gate = pl.BlockSpec((tb, c), lambda b, i: (b * nb + blk_i(i), 1))
    prev = pl.BlockSpec((SUBLANES, c), lambda b, i: (jnp.maximum((b * nb + blk_i(i)) * nb8 - 1, 0), 0))
    return xb, gate, prev


def _lru_fwd(p_lru, prm, t_len, name):
    n, c2 = p_lru.shape
    c = c2 // 2
    nblk = c // LRU_BLOCK_W
    tb = _tile(t_len, 256, SUBLANES)
    nb = t_len // tb
    bsz = n // t_len

    def body(xb_ref, gate_ref, prev_ref, cw_ref, cb_ref, wr_ref, br_ref, wi_ref, bi_ref, lam_ref, ng_ref,
             y_o, h_o, carry):
        i = pl.program_id(1)
        prev8 = jnp.where(i == 0, jnp.zeros_like(prev_ref[...]), prev_ref[...])
        row = lax.broadcasted_iota(jnp.int32, (tb, c), 0)
        f = _lru_gates(xb_ref[...], prev8, gate_ref[...], cw_ref, cb_ref, wr_ref, br_ref, wi_ref, bi_ref, lam_ref,
                       jnp.logical_and(i == 0, row == 0))
        acc_a = f["a"]
        acc_b = f["mult"] * f["ig"] * f["xc"]
        s = 1
        while s < tb:
            keep = row >= s
            a_sh = jnp.where(keep, pltpu.roll(acc_a, s, 0), 1.0)
            b_sh = jnp.where(keep, pltpu.roll(acc_b, s, 0), 0.0)
            acc_b = acc_a * b_sh + acc_b
            acc_a = acc_a * a_sh
            s *= 2

        @pl.when(i == 0)
        def _():
            carry[...] = jnp.zeros_like(carry)

        h = acc_b + acc_a * carry[0:1, :]
        carry[0:1, :] = h[tb - 1:tb, :]
        h_o[...] = h
        y = h * f["ge"]
        rstd = lax.rsqrt(jnp.mean(y * y, axis=-1, keepdims=True) + NORM_EPS)
        y_o[...] = (y * rstd * ng_ref[...]).astype(BF16)

    xb_s, gate_s, prev_s = _lru_specs(tb, c, nb, False)
    c0 = lambda b, i: (0, 0)
    vec = pl.BlockSpec((1, c), c0)
    wsp = pl.BlockSpec((nblk, LRU_BLOCK_W, LRU_BLOCK_W), lambda b, i: (0, 0, 0))
    out = pl.BlockSpec((tb, c), lambda b, i: (b * nb + i, 0))
    return _call(
        body, name=name, grid=(bsz, nb),
        in_specs=[xb_s, gate_s, prev_s, pl.BlockSpec((CONV_WIDTH, c), c0), vec, wsp, vec, wsp, vec, vec, vec],
        out_specs=[out, out],
        out_shape=[jax.ShapeDtypeStruct((n, c), BF16), jax.ShapeDtypeStruct((n, c), F32)],
        scratch_shapes=[pltpu.VMEM((SUBLANES, c), F32)],
        compiler_params=_cp("arbitrary", "arbitrary"),
    )(p_lru, p_lru, p_lru, prm["conv_w"], prm["conv_b"], prm["wr"], prm["br"], prm["wi"], prm["bi"],
      prm["lam"], prm["norm_g"])


def _lru_bwd(p_lru, h, do_cat, prm, t_len, name):
    n, c2 = p_lru.shape
    c = c2 // 2
    nblk = c // LRU_BLOCK_W
    tb = _tile(t_len, 128, SUBLANES)
    nb = t_len // tb
    bsz = n // t_len

    def body(xb_ref, gate_ref, prev_ref, h_ref, hprev_ref, do_ref,
             cw_ref, cb_ref, wr_ref, br_ref, wi_ref, bi_ref, lam_ref, ng_ref,
             dp_o, dcw_o, dcb_o, dwr_o, dbr_o, dwi_o, dbi_o, dlam_o, dng_o,
             a_next, g_next, dxc_next):
        b = pl.program_id(0)
        i = pl.program_id(1)
        blk = nb - 1 - i
        first = blk == 0
        prev8 = jnp.where(first, jnp.zeros_like(prev_ref[...]), prev_ref[...])
        hprev8 = jnp.where(first, jnp.zeros_like(hprev_ref[...]), hprev_ref[...])
        row = lax.broadcasted_iota(jnp.int32, (tb, c), 0)
        is_t0 = jnp.logical_and(first, row == 0)
        gate = gate_ref[...]
        f = _lru_gates(xb_ref[...], prev8, gate, cw_ref, cb_ref, wr_ref, br_ref, wi_ref, bi_ref, lam_ref, is_t0)

        @pl.when(i == 0)
        def _():
            a_next[...] = jnp.zeros_like(a_next)
            g_next[...] = jnp.zeros_like(g_next)
            dxc_next[...] = jnp.zeros_like(dxc_next)

        @pl.when(jnp.logical_and(b == 0, i == 0))
        def _():
            for o in (dcw_o, dcb_o, dwr_o, dbr_o, dwi_o, dbi_o, dlam_o, dng_o):
                o[...] = jnp.zeros_like(o)

        def rsum(x):
            return jnp.sum(x, axis=0, keepdims=True)

        hv = h_ref[...]
        hprev = _shift_down(hv, hprev8, 1)
        ge = f["ge"]
        y = hv * ge
        rstd = lax.rsqrt(jnp.mean(y * y, axis=-1, keepdims=True) + NORM_EPS)
        yh = y * rstd
        dyn = do_ref[...]
        t = dyn * ng_ref[...]
        dy = rstd * (t - yh * jnp.mean(t * yh, axis=-1, keepdims=True))
        dng_o[...] += rsum(dyn * yh)
        dgate = dy * hv * _gelu_grad(gate, f["th"])

        av = f["a"]
        acc_c = _shift_up(av, a_next[...], 1)
        acc_g = dy * ge
        s = 1
        while s < tb:
            keep = row < tb - s
            c_sh = jnp.where(keep, pltpu.roll(acc_c, tb - s, 0), 1.0)
            g_sh = jnp.where(keep, pltpu.roll(acc_g, tb - s, 0), 0.0)
            acc_g = acc_g + acc_c * g_sh
            acc_c = acc_c * c_sh
            s *= 2
        gtot = acc_g + acc_c * g_next[0:1, :]
        a_next[0:1, :] = av[0:1, :]
        g_next[0:1, :] = gtot[0:1, :]

        xc, ig, rg, mult = f["xc"], f["ig"], f["rg"], f["mult"]
        da = gtot * hprev
        dmult = gtot * ig * xc
        dig = gtot * mult * xc
        dxc = gtot * mult * ig
        da = da + jnp.where(is_t0, 0.0, -dmult * av / mult)
        dla = da * av
        drg = dla * (-LRU_C) * f["sp"]
        dlam_o[...] += rsum(dla * rg) * LRU_C * _sig(-lam_ref[...])
        dzr = drg * rg * (1.0 - rg)
        dzi = dig * ig * (1.0 - ig)
        dbr_o[...] += rsum(dzr)
        dbi_o[...] += rsum(dzi)
        dzrb, dzib = dzr.astype(BF16), dzi.astype(BF16)
        xcb = f["xcb"]
        back = []
        for hh in range(nblk):
            sl = slice(hh * LRU_BLOCK_W, (hh + 1) * LRU_BLOCK_W)
            dwr_o[hh] += _dot_tn(xcb[:, sl], dzrb[:, sl])
            dwi_o[hh] += _dot_tn(xcb[:, sl], dzib[:, sl])
            back.append(_dot_nt(dzrb[:, sl], wr_ref[hh]) + _dot_nt(dzib[:, sl], wi_ref[hh]))
        dxc = dxc + jnp.concatenate(back, axis=1)
        dcb_o[...] += rsum(dxc)
        xs = f["xs"]
        dcw_o[...] += jnp.concatenate([rsum(dxc * xs[CONV_WIDTH - 1 - j]) for j in range(CONV_WIDTH)], axis=0)
        nxt = dxc_next[...]
        dxb = dxc * cw_ref[pl.ds(CONV_WIDTH - 1, 1), :]
        for j in range(1, CONV_WIDTH):
            dxb = dxb + _shift_up(dxc, nxt, j) * cw_ref[pl.ds(CONV_WIDTH - 1 - j, 1), :]
        dxc_next[...] = dxc[0:SUBLANES, :]
        dp_o[:, 0:c] = dxb.astype(BF16)
        dp_o[:, c:2 * c] = dgate.astype(BF16)

    xb_s, gate_s, prev_s = _lru_specs(tb, c, nb, True)
    c0 = lambda b, i: (0, 0)
    vec = pl.BlockSpec((1, c), c0)
    wsp = pl.BlockSpec((nblk, LRU_BLOCK_W, LRU_BLOCK_W), lambda b, i: (0, 0, 0))
    cwsp = pl.BlockSpec((CONV_WIDTH, c), c0)
    sds = jax.ShapeDtypeStruct
    return _call(
        body, name=name, grid=(bsz, nb),
        in_specs=[xb_s, gate_s, prev_s, xb_s, prev_s, gate_s, cwsp, vec, wsp, vec, wsp, vec, vec, vec],
        out_specs=[pl.BlockSpec((tb, 2 * c), lambda b, i: (b * nb + nb - 1 - i, 0)),
                   cwsp, vec, wsp, vec, wsp, vec, vec, vec],
        out_shape=[sds((n, 2 * c), BF16), sds((CONV_WIDTH, c), F32), sds((1, c), F32),
                   sds((nblk, LRU_BLOCK_W, LRU_BLOCK_W), F32), sds((1, c), F32),
                   sds((nblk, LRU_BLOCK_W, LRU_BLOCK_W), F32), sds((1, c), F32), sds((1, c), F32), sds((1, c), F32)],
        scratch_shapes=[pltpu.VMEM((SUBLANES, c), F32)] * 3,
        compiler_params=_cp("arbitrary", "arbitrary"),
    )(p_lru, p_lru, p_lru, h, h, do_cat, prm["conv_w"], prm["conv_b"], prm["wr"], prm["br"], prm["wi"], prm["bi"],
      prm["lam"], prm["norm_g"])


def _adamw(g, w, m, v, name):
    rows, cols = g.shape
    tb = _tile(rows, 128, SUBLANES)

    def body(g_ref, w_ref, m_ref, v_ref, d_o, m_o, v_o):
        gv = g_ref[...]
        mn = ADAM_B1 * m_ref[...] + (1.0 - ADAM_B1) * gv
        vn = ADAM_B2 * v_ref[...] + (1.0 - ADAM_B2) * (gv * gv)
        m_o[...] = mn
        v_o[...] = vn
        d_o[...] = -ADAM_LR * ((mn / _BC1) / (jnp.sqrt(vn / _BC2) + ADAM_EPS) + ADAM_WD * w_ref[...])

    blk = pl.BlockSpec((tb, cols), lambda i: (i, 0))
    return _call(body, name=name, grid=(rows // tb,), in_specs=[blk] * 4, out_specs=[blk] * 3,
                 out_shape=[jax.ShapeDtypeStruct((rows, cols), F32)] * 3, compiler_params=_cp("parallel"))(g, w, m, v)


def _adamw_halves(mine, theirs, w, m, v, name):
    a, b = mine.shape
    tc = _col_tile(a, b)
    w, m, v = (t.reshape(2, a, b) for t in (w, m, v))

    def body(mine_ref, theirs_ref, w_ref, m_ref, v_ref, g_o, d_o, m_o, v_o):
        gv = jnp.where(pl.program_id(0) == lax.axis_index("c"), mine_ref[...], theirs_ref[...])
        mn = ADAM_B1 * m_ref[...] + (1.0 - ADAM_B1) * gv
        vn = ADAM_B2 * v_ref[...] + (1.0 - ADAM_B2) * (gv * gv)
        g_o[...] = gv
        m_o[...] = mn
        v_o[...] = vn
        d_o[...] = -ADAM_LR * ((mn / _BC1) / (jnp.sqrt(vn / _BC2) + ADAM_EPS) + ADAM_WD * w_ref[...])

    half = pl.BlockSpec((a, tc), lambda h, j: (0, j))
    blk = pl.BlockSpec((None, a, tc), lambda h, j: (h, 0, j))
    return _call(body, name=name, grid=(2, b // tc), in_specs=[half, half, blk, blk, blk], out_specs=[blk] * 4,
                 out_shape=[jax.ShapeDtypeStruct((2, a, b), F32)] * 4,
                 compiler_params=_cp("parallel", "parallel"))(mine, theirs, w, m, v)


def _pair_sum(x4, recv, name):
    _, _, a, b = x4.shape
    tc = _col_tile(a, b)

    def body(x_ref, r_ref, o_ref):
        mine = x_ref[lax.axis_index("c")]
        o_ref[...] = (mine.astype(F32) + r_ref[...].astype(F32)).astype(BF16)

    return _call(
        body, name=name, grid=(4, b // tc),
        in_specs=[pl.BlockSpec((None, 2, a, tc), lambda j, i: (j, 0, 0, i)),
                  pl.BlockSpec((None, a, tc), lambda j, i: (j, 0, i))],
        out_specs=pl.BlockSpec((None, a, tc), lambda j, i: (j, 0, i)),
        out_shape=jax.ShapeDtypeStruct((4, a, b), BF16), compiler_params=_cp("parallel", "parallel"))(x4, recv)


def _chip_sum(x4, name):
    _, a, b = x4.shape
    ta = _tile(a, 256, SUBLANES)

    def body(x_ref, o_ref):
        acc = x_ref[0] + x_ref[1]
        acc = acc + x_ref[2]
        o_ref[...] = acc + x_ref[3]

    return _call(
        body, name=name, grid=(a // ta,),
        in_specs=[pl.BlockSpec((4, ta, b), lambda i: (0, i, 0))],
        out_specs=pl.BlockSpec((ta, b), lambda i: (i, 0)),
        out_shape=jax.ShapeDtypeStruct((a, b), F32), compiler_params=_cp("parallel"))(x4)


def _peer_sum(own4, parts, name):
    _, a, b = parts.shape
    tc = _col_tile(a, b)

    def body(own_ref, p_ref, o_ref):
        me = 2 * lax.axis_index("x") + lax.axis_index("y")
        acc = own_ref[me].astype(F32) + p_ref[0].astype(F32)
        acc = acc + p_ref[1].astype(F32)
        o_ref[...] = acc + p_ref[2].astype(F32)

    return _call(
        body, name=name, grid=(b // tc,),
        in_specs=[pl.BlockSpec((4, a, tc), lambda i: (0, 0, i)), pl.BlockSpec((3, a, tc), lambda i: (0, 0, i))],
        out_specs=pl.BlockSpec((a, tc), lambda i: (0, i)),
        out_shape=jax.ShapeDtypeStruct((a, b), F32), compiler_params=_cp("parallel"))(own4, parts)


def _add2(x, y, name):
    rows, cols = x.shape
    tb = _tile(rows, 512, SUBLANES)

    def body(x_ref, y_ref, o_ref):
        o_ref[...] = x_ref[...] + y_ref[...]

    blk = pl.BlockSpec((tb, cols), lambda i: (i, 0))
    return _call(body, name=name, grid=(rows // tb,), in_specs=[blk, blk], out_specs=blk,
                 out_shape=jax.ShapeDtypeStruct((rows, cols), x.dtype), compiler_params=_cp("parallel"))(x, y)


_HBM = pl.BlockSpec(memory_space=pltpu.HBM)


def _place():
    x, y, c = lax.axis_index("x"), lax.axis_index("y"), lax.axis_index("c")
    chips = [(1 - x, y), (x, 1 - y), (1 - x, 1 - y)]
    return x, y, c, chips


def _comm_call(body, name, xs, out_shapes, n_sems):
    return _call(
        body, name=name, in_specs=[_HBM] * len(xs), out_specs=[_HBM] * len(out_shapes), out_shape=out_shapes,
        scratch_shapes=[pltpu.SemaphoreType.DMA((n_sems,)), pltpu.SemaphoreType.DMA((n_sems,)),
                        pltpu.SemaphoreType.DMA((len(xs),))],
    )(*xs)


def _all_gather_chips(xs, name):
    n = len(xs)

    def body(*refs):
        ins, outs = refs[:n], refs[n:2 * n]
        ssem, rsem, _ = refs[2 * n:]
        _gather_start(ins, outs, ssem, rsem)
        _gather_finish(ins, outs, ssem, rsem)

    outs = [jax.ShapeDtypeStruct((4,) + v.shape, v.dtype) for v in xs]
    return _comm_call(body, name, xs, outs, GATHER_SEMS * n)


GATHER_SEMS = 7
PEER_SEMS = 3


def _remote(src, dst, ssem, rsem, k, dev):
    return pltpu.make_async_remote_copy(src_ref=src, dst_ref=dst, send_sem=ssem.at[k], recv_sem=rsem.at[k],
                                        device_id=dev, device_id_type=MESH)


def _gather_start(ins, outs, ssem, rsem):
    x, y, c, chips = _place()
    me = 2 * x + y
    for i in range(len(ins)):
        for j, (px, py) in enumerate(chips):
            _remote(ins[i].at[c], outs[i].at[me, c], ssem, rsem, GATHER_SEMS * i + j, (px, py, c)).start()
        _remote(ins[i], outs[i].at[me], ssem, rsem, GATHER_SEMS * i + 6, (x, y, 1 - c)).start()


def _gather_finish(ins, outs, ssem, rsem):
    x, y, c, chips = _place()
    me = 2 * x + y
    sib = (x, y, 1 - c)
    n = len(ins)
    for i in range(n):
        for j, (px, py) in enumerate(chips):
            slot = outs[i].at[2 * px + py, c]
            _remote(slot, slot, ssem, rsem, GATHER_SEMS * i + j, (px, py, c)).wait_recv()
            _remote(slot, slot, ssem, rsem, GATHER_SEMS * i + 3 + j, sib).start()
    for i in range(n):
        own = outs[i].at[me]
        _remote(own, own, ssem, rsem, GATHER_SEMS * i + 6, sib).wait_recv()
        for j, (px, py) in enumerate(chips):
            slot = outs[i].at[2 * px + py, 1 - c]
            _remote(slot, slot, ssem, rsem, GATHER_SEMS * i + 3 + j, sib).wait_recv()
    for i in range(n):
        for j, (px, py) in enumerate(chips):
            slot = outs[i].at[2 * px + py, c]
            _remote(ins[i].at[c], outs[i].at[me, c], ssem, rsem, GATHER_SEMS * i + j, (px, py, c)).wait_send()
            _remote(slot, slot, ssem, rsem, GATHER_SEMS * i + 3 + j, sib).wait_send()
        _remote(ins[i], outs[i].at[me], ssem, rsem, GATHER_SEMS * i + 6, sib).wait_send()


def _peer_copies(ins, outs, ssem, rsem):
    x, y, c, chips = _place()
    return [_remote(ins[i].at[2 * px + py], outs[i].at[j], ssem, rsem, PEER_SEMS * i + j, (px, py, c))
            for i in range(len(ins)) for j, (px, py) in enumerate(chips)]


def _comm_plan(kind, xs):
    if kind == "gather":
        outs = [jax.ShapeDtypeStruct((4,) + v.shape, v.dtype) for v in xs]
        return outs, GATHER_SEMS * len(xs), _gather_start, _gather_finish
    if kind == "swap_half":
        def swaps(ins, outs, ssem, rsem):
            x, y, c, _ = _place()
            return [_remote(ins[i].at[:, 1 - c], outs[i], ssem, rsem, i, (x, y, 1 - c)) for i in range(len(ins))]

        def start_swaps(ins, outs, ssem, rsem):
            for cp in swaps(ins, outs, ssem, rsem):
                cp.start()

        def finish_swaps(ins, outs, ssem, rsem):
            for cp in swaps(ins, outs, ssem, rsem):
                cp.wait()

        outs = [jax.ShapeDtypeStruct((v.shape[0],) + v.shape[2:], v.dtype) for v in xs]
        return outs, len(xs), start_swaps, finish_swaps

    def start(ins, outs, ssem, rsem):
        for cp in _peer_copies(ins, outs, ssem, rsem):
            cp.start()

    def finish(ins, outs, ssem, rsem):
        for cp in _peer_copies(ins, outs, ssem, rsem):
            cp.wait()

    outs = [jax.ShapeDtypeStruct((3,) + v.shape[1:], v.dtype) for v in xs]
    return outs, PEER_SEMS * len(xs), start, finish


def _sibling_swap(xs, pick_half, name):
    n = len(xs)

    def body(*refs):
        ins, outs = refs[:n], refs[n:2 * n]
        ssem, rsem, _ = refs[2 * n:]
        x, y, c, _ = _place()
        cps = []
        for i in range(n):
            src = ins[i].at[:, 1 - c] if pick_half else ins[i]
            cp = pltpu.make_async_remote_copy(src_ref=src, dst_ref=outs[i], send_sem=ssem.at[i], recv_sem=rsem.at[i],
                                              device_id=(x, y, 1 - c), device_id_type=MESH)
            cp.start()
            cps.append(cp)
        for cp in cps:
            cp.wait()

    outs = [jax.ShapeDtypeStruct((v.shape[0],) + v.shape[2:] if pick_half else v.shape, v.dtype) for v in xs]
    return _comm_call(body, name, xs, outs, n)


def _chip_broadcast(xs, name):
    n = len(xs)

    def body(*refs):
        ins, outs = refs[:n], refs[n:2 * n]
        ssem, rsem, lsem = refs[2 * n:]
        x, y, c, chips = _place()
        me = 2 * x + y
        cps = []
        for i in range(n):
            cp = pltpu.make_async_copy(ins[i], outs[i].at[me], lsem.at[i])
            cp.start()
            cps.append(cp)
            for j, (px, py) in enumerate(chips):
                cp = pltpu.make_async_remote_copy(
                    src_ref=ins[i], dst_ref=outs[i].at[me], send_sem=ssem.at[3 * i + j], recv_sem=rsem.at[3 * i + j],
                    device_id=(px, py, c), device_id_type=MESH)
                cp.start()
                cps.append(cp)
        for i in range(n):
            for j, (px, py) in enumerate(chips):
                slot = outs[i].at[2 * px + py]
                pltpu.make_async_remote_copy(
                    src_ref=slot, dst_ref=slot, send_sem=ssem.at[3 * i + j], recv_sem=rsem.at[3 * i + j],
                    device_id=(px, py, c), device_id_type=MESH).wait_recv()
        for i in range(n):
            cps[4 * i].wait()
            for j in range(3):
                cps[4 * i + 1 + j].wait_send()

    outs = [jax.ShapeDtypeStruct((4,) + v.shape, v.dtype) for v in xs]
    return _comm_call(body, name, xs, outs, 3 * n)


def _to_scan_k(x, bsz, t_len):
    h = x.shape[1] // RWKV_HEAD
    y = x.reshape(bsz, t_len, h, RWKV_HEAD).transpose(1, 3, 0, 2).reshape(t_len, RWKV_HEAD, bsz * h)
    return jnp.concatenate([y, y], axis=-1)


def _to_scan_v(x, bsz, t_len):
    h = x.shape[1] // RWKV_HEAD
    y = x.reshape(bsz, t_len, h, 2, RWKV_HEAD // 2).transpose(1, 4, 3, 0, 2)
    return y.reshape(t_len, RWKV_HEAD // 2, 2 * bsz * h)


def _from_scan_k(x, bsz, t_len):
    h = x.shape[2] // (2 * bsz)
    y = x[:, :, :bsz * h].reshape(t_len, RWKV_HEAD, bsz, h).transpose(2, 0, 3, 1)
    return y.reshape(bsz * t_len, h * RWKV_HEAD)


def _from_scan_v(x, bsz, t_len):
    h = x.shape[2] // (2 * bsz)
    y = x.reshape(t_len, RWKV_HEAD // 2, 2, bsz, h).transpose(3, 0, 4, 2, 1)
    return y.reshape(bsz * t_len, h * RWKV_HEAD)


def _pad_rows(x, rows):
    return jnp.pad(x, ((0, rows - x.shape[0]), (0, 0)))


def _pad_cols(x, cols):
    return jnp.pad(x, ((0, 0), (0, cols - x.shape[1])))


def _cols_from_shards(g4):
    _, r, cs = g4.shape
    return g4.transpose(1, 0, 2).reshape(r, 4 * cs)


def _cols_to_shards(g):
    r, cols = g.shape
    return g.reshape(r, 4, cols // 4).transpose(1, 0, 2)


def kernel(x, norm_mix_g, w_in, mu_shift, rwkv_w0, rwkv_w2, rwkv_a0, rwkv_a2, rwkv_g2, rwkv_k_k, rwkv_k_a, rwkv_r_k, rwkv_ln_g, rwkv_ln_b, conv_w, conv_b, lru_wr, lru_br, lru_wi, lru_bi, lru_lambda, lru_norm_g, w_out, norm_ffn_g, ffn_w_gate, ffn_w_up, ffn_w_down, norm_final_g, loss_target, m_norm_mix_g, m_w_in, m_mu_shift, m_rwkv_w0, m_rwkv_w2, m_rwkv_a0, m_rwkv_a2, m_rwkv_g2, m_rwkv_k_k, m_rwkv_k_a, m_rwkv_r_k, m_rwkv_ln_g, m_rwkv_ln_b, m_conv_w, m_conv_b, m_lru_wr, m_lru_br, m_lru_wi, m_lru_bi, m_lru_lambda, m_lru_norm_g, m_w_out, m_norm_ffn_g, m_ffn_w_gate, m_ffn_w_up, m_ffn_w_down, m_norm_final_g, v_norm_mix_g, v_w_in, v_mu_shift, v_rwkv_w0, v_rwkv_w2, v_rwkv_a0, v_rwkv_a2, v_rwkv_g2, v_rwkv_k_k, v_rwkv_k_a, v_rwkv_r_k, v_rwkv_ln_g, v_rwkv_ln_b, v_conv_w, v_conv_b, v_lru_wr, v_lru_br, v_lru_wi, v_lru_bi, v_lru_lambda, v_lru_norm_g, v_w_out, v_norm_ffn_g, v_ffn_w_gate, v_ffn_w_up, v_ffn_w_down, v_norm_final_g):
    names = ['norm_mix_g', 'w_in', 'mu_shift', 'rwkv_w0', 'rwkv_w2', 'rwkv_a0', 'rwkv_a2', 'rwkv_g2', 'rwkv_k_k',
             'rwkv_k_a', 'rwkv_r_k', 'rwkv_ln_g', 'rwkv_ln_b', 'conv_w', 'conv_b', 'lru_wr', 'lru_br', 'lru_wi',
             'lru_bi', 'lru_lambda', 'lru_norm_g', 'w_out', 'norm_ffn_g', 'ffn_w_gate', 'ffn_w_up', 'ffn_w_down',
             'norm_final_g']
    env = locals()
    wts = {k: env[k] for k in names}
    mom_m = {k: env["m_" + k] for k in names}
    mom_v = {k: env["v_" + k] for k in names}

    bsz, t_len, d = x.shape
    n = bsz * t_len
    w = rwkv_w0.shape[1]
    lw = lru_br.shape[1]
    dl, al, gl = rwkv_w2.shape[1], rwkv_a2.shape[1], rwkv_g2.shape[1]
    dlp, alp, glp = _ceil_to(dl, LANES), _ceil_to(al, LANES), _ceil_to(gl, LANES)
    lp = dlp + alp + glp
    rc = 3 * w + dl + al + gl
    chip = 2 * lax.axis_index("x") + lax.axis_index("y")

    big = ['w_in', 'w_out', 'ffn_w_gate', 'ffn_w_up', 'ffn_w_down']
    small_sh = ['rwkv_w2', 'rwkv_a2', 'rwkv_g2', 'conv_w']

    def halves(a2d):
        return a2d.reshape(2, a2d.shape[0] // 2, a2d.shape[1])

    col_sharded = ('w_in', 'ffn_w_gate', 'ffn_w_up')

    def work(k, t):
        return jnp.swapaxes(t[0], 0, 1) if k in col_sharded else t[0]

    def unwork(k, t2):
        return (jnp.swapaxes(t2, 0, 1) if k in col_sharded else t2)[None]

    def rows_of(g):
        return g.reshape(g.shape[0] * g.shape[1] * g.shape[2], g.shape[3])

    send = [halves(work('w_in', w_in).astype(BF16))] + [halves(wts[k][0]) for k in small_sh]
    got = _all_gather_chips(send, "gather_w_in")
    later = ['w_out', 'ffn_w_gate', 'ffn_w_up', 'ffn_w_down']
    send_later = [halves(work(k, wts[k]).astype(BF16)) for k in later]
    full = {}
    for k, g in zip(small_sh, got[1:]):
        full[k] = _cols_from_shards(g.reshape(4, g.shape[1] * g.shape[2], g.shape[3]))
    wi_t = rows_of(got[0])
    w_rkv = wi_t[:3 * w]
    w_lru = wi_t[rc:]
    o = 3 * w
    w_lora = jnp.concatenate([_pad_rows(wi_t[o:o + dl], dlp), _pad_rows(wi_t[o + dl:o + dl + al], alp),
                              _pad_rows(wi_t[o + dl + al:rc], glp)], axis=0)
    mu = mu_shift
    prm_r = dict(
        mu_rkv=mu[:, :3 * w],
        mu_lora=jnp.concatenate([_pad_cols(mu[:, o:o + dl], dlp), _pad_cols(mu[:, o + dl:o + dl + al], alp),
                                 _pad_cols(mu[:, o + dl + al:rc], glp)], axis=1),
        w0=rwkv_w0, a0=rwkv_a0, k_k=rwkv_k_k, k_a=rwkv_k_a,
        w2=_pad_rows(full['rwkv_w2'], dlp).astype(BF16), a2=_pad_rows(full['rwkv_a2'], alp).astype(BF16),
        g2=_pad_rows(full['rwkv_g2'], glp).astype(BF16))
    ln_g, ln_b, r_k = rwkv_ln_g, rwkv_ln_b, rwkv_r_k.reshape(1, w)
    prm_l = dict(conv_w=full['conv_w'], conv_b=conv_b, wr=lru_wr[0].astype(BF16), br=lru_br,
                 wi=lru_wi[0].astype(BF16), bi=lru_bi, lam=lru_lambda, norm_g=lru_norm_g)
    g_final = norm_final_g.reshape(1, d)

    x2 = x.reshape(n, d)
    u1 = _rmsnorm_fwd(x2, norm_mix_g, "norm_mix")
    p_rkv = _mm(u1, w_rkv, name="in_rkv", tb=True)
    p_lru = _mm(u1, w_lru, name="in_lru", tb=True)
    p_lora = _mm(u1, w_lora, name="in_lora", tb=True)
    r_t, dec_t, k_t, v_t, na_t, nb_t, g_t = _rwkv_prep_fwd(p_rkv, p_lora, prm_r, t_len, "rwkv_prep")
    sk = [_to_scan_k(a, bsz, t_len) for a in (r_t, dec_t, k_t, na_t, nb_t)]
    sv = _to_scan_v(v_t, bsz, t_len)
    y_s, ckpt, got_wo, got_wg = _rwkv_scan_fwd(*sk, sv, name="rwkv_scan", comm=("gather", send_later[:2]))
    wo, wg = rows_of(got_wo), rows_of(got_wg)
    y_t = _from_scan_v(y_s, bsz, t_len)
    y_a = _rwkv_post_fwd(y_t, r_t, k_t, v_t, g_t, ln_g, ln_b, r_k, "rwkv_post")
    y_b, h_lru = _lru_fwd(p_lru, prm_l, t_len, "lru_fwd")
    h1 = _mm(y_a, wo[:w], name="out_a", res=x2)
    h1 = _mm(y_b, wo[w:], name="out_b", res=h1)
    u2 = _rmsnorm_fwd(h1, norm_ffn_g, "norm_ffn")
    ffc = (1024, 256, 4096)
    gate, got_wu = _mm(u2, wg, name="ffn_gate", tb=True, caps=ffc, comm=("gather", send_later[2:3]))
    wu = rows_of(got_wu)
    up, got_wd = _mm(u2, wu, name="ffn_up", tb=True, caps=ffc, comm=("gather", send_later[3:4]))
    wd = rows_of(got_wd)
    act = _swiglu_fwd(gate, up, "swiglu")
    h2 = _mm(act, wd, name="ffn_down", res=h1, caps=(512, 256, 11008))

    dh2, dh2b, g_norm_final, loss_vec = _loss_head(h2, g_final, loss_target.reshape(n, d), "loss_head")
    loss = lax.psum(loss_vec[0, 0], ("x", "y", "c"))
    dact = _mm(dh2b, wd, name="d_act", tb=True, caps=(1024, 256, 4096))
    dgate, dup = _swiglu_bwd(gate, up, dact, "swiglu_bwd")
    shards = lambda g: g.reshape(4, 2, g.shape[0] // 8, g.shape[1])
    dwc = dict(ta=True, out_dtype=BF16, n_outer=True, caps=(256, 1024, 4096))
    gw_down = _mm(act, dh2b, name="dw_down", **dwc)
    gw_gate = _mm(dgate, u2, name="dw_gate", **dwc)
    gw_up = _mm(dup, u2, name="dw_up", **dwc)
    g4f = [shards(g) for g in (gw_gate, gw_up, gw_down)]
    du2, *sib_f = _mm(dgate, wg, name="du2_gate", caps=(512, 256, 11008), comm=("swap_half", g4f))
    du2 = _mm(dup, wu, name="du2_up", res=du2, caps=(512, 256, 11008))
    dh1, dh1b, g_norm_ffn = _rmsnorm_bwd(du2, h1, norm_ffn_g, dh2, "norm_ffn_bwd")
    dcat = _mm(dh1b, wo, name="d_cat", tb=True)
    gw_out = jnp.concatenate([_mm(y_a, dh1b, name="dw_out_a", ta=True, out_dtype=BF16),
                              _mm(y_b, dh1b, name="dw_out_b", ta=True, out_dtype=BF16)], axis=0)
    g4o = [shards(gw_out)]
    sib_o = _sibling_swap(g4o, True, "grad_sibling_out")
    pair_a = [_pair_sum(a4, s, "grad_pair_sum_%d" % i) for i, (a4, s) in enumerate(zip(g4o + g4f, list(sib_o) + list(sib_f)))]
    (dp_lru, g_conv_w, g_conv_b, g_wr, g_br, g_wi, g_bi, g_lam, g_lng) = _lru_bwd(
        p_lru, h_lru, dcat, prm_l, t_len, "lru_bwd")
    dy_t, dr_p, dk_p, dv_p, dg_t, g_ln_g, g_ln_b, g_r_k = _rwkv_post_bwd(
        y_t, r_t, k_t, v_t, g_t, ln_g, ln_b, r_k, dcat, "rwkv_post_bwd")
    dr_s, dw_s, dk_s, da_s, db_s, dv_s, *parts_a = _rwkv_scan_bwd(
        *sk, sv, _to_scan_v(dy_t, bsz, t_len), ckpt, name="rwkv_scan_bwd", comm=("peer", pair_a))
    grads = [_from_scan_k(dr_s, bsz, t_len), dr_p, _from_scan_k(dw_s, bsz, t_len), _from_scan_k(dk_s, bsz, t_len),
             dk_p, _from_scan_v(dv_s, bsz, t_len), dv_p, _from_scan_k(da_s, bsz, t_len),
             _from_scan_k(db_s, bsz, t_len), dg_t]
    (dq_r, dq_l, g_mu_r, g_mu_l, g_w0, g_a0, g_kk, g_ka, g_w2, g_a2, g_g2) = _rwkv_prep_bwd(
        p_rkv, p_lora, prm_r, grads, t_len, "rwkv_prep_bwd")
    dp_rkv = _shift_combine(dq_r, prm_r["mu_rkv"], t_len, "shift_bwd_rkv")
    dp_lora = _shift_combine(dq_l, prm_r["mu_lora"], t_len, "shift_bwd_lora")
    gi_rkv = _mm(dp_rkv, u1, name="dw_in_rkv", ta=True, out_dtype=BF16)
    gi_lru = _mm(dp_lru, u1, name="dw_in_lru", ta=True, out_dtype=BF16)
    gi_lora = _mm(dp_lora, u1, name="dw_in_lora", ta=True, out_dtype=BF16)
    gw_in = jnp.concatenate([gi_rkv, gi_lora[:dl], gi_lora[dlp:dlp + al], gi_lora[dlp + alp:dlp + alp + gl], gi_lru],
                            axis=0)
    g4b = [shards(gw_in)]
    sib_b = _sibling_swap(g4b, True, "grad_sibling_in")
    pair_b = [_pair_sum(g4b[0], sib_b[0], "grad_pair_sum_in")]
    du1, *parts_b = _mm(dp_rkv, w_rkv, name="du1_rkv", comm=("peer", pair_b))
    du1 = _mm(dp_lru, w_lru, name="du1_lru", res=du1)
    du1 = _mm(dp_lora, w_lora, name="du1_lora", res=du1)
    gx, _, g_norm_mix = _rmsnorm_bwd(du1, x2, norm_mix_g, dh1, "norm_mix_bwd")

    pair, parts = pair_b + pair_a, list(parts_b) + list(parts_a)
    mine = [_peer_sum(own4, p3, "grad_chip_sum_%d" % i) for i, (own4, p3) in enumerate(zip(pair, parts))]
    theirs = _sibling_swap(mine, False, "grad_share")

    g_mu = jnp.concatenate([g_mu_r, g_mu_l[:, :dl], g_mu_l[:, dlp:dlp + al], g_mu_l[:, dlp + alp:dlp + alp + gl]],
                           axis=1)
    small = dict(norm_mix_g=g_norm_mix, mu_shift=g_mu, rwkv_w0=g_w0, rwkv_w2=g_w2[:dl], rwkv_a0=g_a0,
                 rwkv_a2=g_a2[:al], rwkv_g2=g_g2[:gl], rwkv_k_k=g_kk, rwkv_k_a=g_ka, rwkv_r_k=g_r_k,
                 rwkv_ln_g=g_ln_g, rwkv_ln_b=g_ln_b, conv_w=g_conv_w, conv_b=g_conv_b, lru_wr=g_wr, lru_br=g_br,
                 lru_wi=g_wi, lru_bi=g_bi, lru_lambda=g_lam, lru_norm_g=g_lng, norm_ffn_g=g_norm_ffn,
                 norm_final_g=g_norm_final)
    small_names = list(small)
    sizes = [small[k].size for k in small_names]
    total = sum(sizes)
    padded = _ceil_to(total, 512 * LANES)

    def pack(arrs):
        flat = jnp.concatenate([a.reshape(-1) for a in arrs] + [jnp.zeros((padded - sum(a.size for a in arrs),), F32)])
        return flat.reshape(padded // LANES, LANES)

    packed = pack([small[k] for k in small_names])
    other = _sibling_swap([packed], False, "small_sibling")[0]
    chip_sum = _add2(packed, other, "small_pair_sum")
    all4 = _chip_broadcast([chip_sum], "small_chips")[0]
    red = _chip_sum(all4, "small_chip_sum").reshape(-1)
    small_g = {}
    off = 0
    for k, sz in zip(small_names, sizes):
        full_g = red[off:off + sz].reshape(small[k].shape)
        off += sz
        if k in small_sh:
            cs = full_g.shape[1] // 4
            full_g = lax.dynamic_slice_in_dim(full_g, chip * cs, cs, axis=1)
        small_g[k] = full_g.reshape(wts[k].shape)

    grad_w, delta_w, new_m, new_v = {}, {}, {}, {}
    for k, g_mine, g_theirs in zip(big, mine, theirs):
        res = _adamw_halves(g_mine, g_theirs, work(k, wts[k]), work(k, mom_m[k]), work(k, mom_v[k]), "adamw_" + k)
        grad_w[k], delta_w[k], new_m[k], new_v[k] = (unwork(k, t.reshape(2 * t.shape[1], t.shape[2])) for t in res)
    lsizes = [small_g[k].size for k in small_names]
    lpad = _ceil_to(sum(lsizes), 128 * LANES)

    def lpack(tree):
        arrs = [tree[k].reshape(-1) for k in small_names]
        flat = jnp.concatenate(arrs + [jnp.zeros((lpad - sum(lsizes),), F32)])
        return flat.reshape(lpad // LANES, LANES)

    dlt, mn, vn = _adamw(lpack(small_g), lpack(wts), lpack(mom_m), lpack(mom_v), "adamw_small")
    off = 0
    for k, sz in zip(small_names, lsizes):
        shp = wts[k].shape
        grad_w[k] = small_g[k]
        delta_w[k] = dlt.reshape(-1)[off:off + sz].reshape(shp)
        new_m[k] = mn.reshape(-1)[off:off + sz].reshape(shp)
        new_v[k] = vn.reshape(-1)[off:off + sz].reshape(shp)
        off += sz

    return (loss, gx.reshape(bsz, t_len, d), *[grad_w[k] for k in names], *[delta_w[k] for k in names],
            *[new_m[k] for k in names], *[new_v[k] for k in names])
```

```python
import jax
import jax.numpy as jnp
from jax import lax
from jax.experimental import pallas as pl
from jax.experimental.pallas import tpu as pltpu

F32 = jnp.float32
BF16 = jnp.bfloat16
MESH = pl.DeviceIdType.MESH
_call = pl.pallas_call

V7X_VMEM_LIMIT = 56 * 1024 * 1024
LANES = 128
SUBLANES = 8

RWKV_HEAD = 64
LRU_BLOCK_W = 128
CONV_WIDTH = 4
LRU_C = 8.0
NORM_EPS = 1e-6
GN_EPS = 64e-5
KK_EPS = 1e-24
SCAN_CHUNK = 16

ADAM_LR = 0.001
ADAM_B1 = 0.9
ADAM_B2 = 0.999
ADAM_EPS = 1e-08
ADAM_WD = 0.01
ADAM_STEP = 10
_BC1 = 1.0 - ADAM_B1 ** ADAM_STEP
_BC2 = 1.0 - ADAM_B2 ** ADAM_STEP

_HI = lax.Precision.HIGHEST


def _cp(*sem):
    return pltpu.CompilerParams(dimension_semantics=tuple(sem), vmem_limit_bytes=V7X_VMEM_LIMIT)


def _tile(n, cap, unit=LANES):
    if n <= cap:
        return n
    best = None
    d = unit
    while d <= cap:
        if n % d == 0:
            best = d
        d += unit
    return n if best is None else best


def _ceil_to(n, m):
    return -(-n // m) * m


ELEMENTWISE_BLOCK_BYTES = 3 * 512 * 1024


def _col_tile(rows, cols):
    cap = max(LANES, ELEMENTWISE_BLOCK_BYTES // (4 * rows) // LANES * LANES)
    return _tile(cols, cap)


def _sig(x):
    return 1.0 / (1.0 + jnp.exp(-x))


def _log1p(x):
    return jnp.where(x < 0.01, x * (1.0 - x * (0.5 - x * (1.0 / 3.0))), jnp.log(1.0 + x))


def _softplus(x):
    return jnp.maximum(x, 0.0) + _log1p(jnp.exp(-jnp.abs(x)))


def _neg_expm1(x):
    small = -x * (1.0 + x * (0.5 + x * (1.0 / 6.0)))
    return jnp.where(x > -0.01, small, 1.0 - jnp.exp(x))


_GELU_K = 0.7978845608028654
_GELU_C = 0.044715


def _gelu_parts(x):
    th = jnp.tanh(_GELU_K * (x + _GELU_C * x * x * x))
    return 0.5 * x * (1.0 + th), th


def _gelu_grad(x, th):
    return 0.5 * (1.0 + th) + 0.5 * x * (1.0 - th * th) * _GELU_K * (1.0 + 3.0 * _GELU_C * x * x)


def _shift_down(x, prev8, j):
    tb = x.shape[0]
    xr = pltpu.roll(x, j, 0)
    pr = pltpu.roll(prev8, j, 0)
    row = lax.broadcasted_iota(jnp.int32, prev8.shape, 0)
    first = jnp.where(row < j, pr, xr[0:SUBLANES])
    if tb == SUBLANES:
        return first
    return jnp.concatenate([first, xr[SUBLANES:]], axis=0)


def _shift_up(x, next8, j):
    tb = x.shape[0]
    xr = pltpu.roll(x, tb - j, 0)
    nr = pltpu.roll(next8, SUBLANES - j, 0)
    row = lax.broadcasted_iota(jnp.int32, next8.shape, 0)
    last = jnp.where(row >= SUBLANES - j, nr, xr[tb - SUBLANES:])
    if tb == SUBLANES:
        return last
    return jnp.concatenate([xr[:tb - SUBLANES], last], axis=0)


def _head_mats(width, heads_pad):
    e = (lax.broadcasted_iota(jnp.int32, (width, heads_pad), 0) // RWKV_HEAD
         == lax.broadcasted_iota(jnp.int32, (width, heads_pad), 1)).astype(F32)
    et = (lax.broadcasted_iota(jnp.int32, (heads_pad, width), 1) // RWKV_HEAD
          == lax.broadcasted_iota(jnp.int32, (heads_pad, width), 0)).astype(F32)
    return e, et


def _headsum(x, e, et):
    s = jnp.dot(x, e, preferred_element_type=F32, precision=_HI)
    return jnp.dot(s, et, preferred_element_type=F32, precision=_HI)


def _dot(a, b):
    return jnp.dot(a.astype(BF16), b.astype(BF16), preferred_element_type=F32)


def _dot_tn(a, b):
    return lax.dot_general(a.astype(BF16), b.astype(BF16), (((0,), (0,)), ((), ())), preferred_element_type=F32)


def _dot_nt(a, b):
    return lax.dot_general(a.astype(BF16), b.astype(BF16), (((1,), (1,)), ((), ())), preferred_element_type=F32)


def _mm(a, b, *, name, ta=False, tb=False, out_dtype=F32, res=None, n_outer=False, caps=(1024, 512, 4096),
        comm=None):
    m = a.shape[1] if ta else a.shape[0]
    kd = a.shape[0] if ta else a.shape[1]
    n = b.shape[0] if tb else b.shape[1]
    assert kd == (b.shape[1] if tb else b.shape[0])
    tm, tn, tk = _tile(m, caps[0]), _tile(n, caps[1]), _tile(kd, caps[2])
    gm, gn, gk = m // tm, n // tn, kd // tk
    dims = (((0 if ta else 1,), (1 if tb else 0,)), ((), ()))
    grid = (gn, gm, gk) if n_outer else (gm, gn, gk)
    cx, cx_specs, c_outs, c_sems, c_start, c_finish = _carried(comm)
    nx, n_in = len(cx), 3 if res is not None else 2

    def ij(g0, g1):
        return (g1, g0) if n_outer else (g0, g1)

    def a_map(g0, g1, k):
        i, _ = ij(g0, g1)
        return (k, i) if ta else (i, k)

    def b_map(g0, g1, k):
        _, j = ij(g0, g1)
        return (j, k) if tb else (k, j)

    def o_map(g0, g1, k):
        return ij(g0, g1)

    has_res = res is not None

    def body(*refs):
        a_ref, b_ref = refs[0], refs[1]
        res_ref = refs[2] if has_res else None
        c_ins = refs[n_in:n_in + nx]
        o_ref = refs[n_in + nx]
        c_out_refs = refs[n_in + nx + 1:n_in + nx + 1 + len(c_outs)]
        acc_ref = refs[n_in + nx + 1 + len(c_outs)] if gk > 1 else None
        steps = [pl.program_id(ax) for ax in range(3)]
        if nx:
            @pl.when(jnp.logical_and(jnp.logical_and(steps[0] == 0, steps[1] == 0), steps[2] == 0))
            def _():
                c_start(c_ins, c_out_refs, refs[-2], refs[-1])

        prod = lax.dot_general(a_ref[...], b_ref[...], dims, preferred_element_type=F32)

        def finish(acc):
            if has_res:
                acc = acc + res_ref[...]
            o_ref[...] = acc.astype(out_dtype)

        if gk == 1:
            finish(prod)
        else:
            k = steps[2]

            @pl.when(k == 0)
            def _():
                acc_ref[...] = prod

            @pl.when(k > 0)
            def _():
                acc_ref[...] += prod

            @pl.when(k == gk - 1)
            def _():
                finish(acc_ref[...])

        if nx:
            @pl.when(jnp.logical_and(jnp.logical_and(steps[0] == grid[0] - 1, steps[1] == grid[1] - 1),
                                     steps[2] == grid[2] - 1))
            def _():
                c_finish(c_ins, c_out_refs, refs[-2], refs[-1])

    in_specs = [pl.BlockSpec((tk, tm) if ta else (tm, tk), a_map),
                pl.BlockSpec((tn, tk) if tb else (tk, tn), b_map)]
    args = [a, b]
    if has_res:
        in_specs.append(pl.BlockSpec((tm, tn), o_map))
        args.append(res)
    out = _call(
        body, name=name, grid=grid, in_specs=in_specs + cx_specs,
        out_specs=[pl.BlockSpec((tm, tn), o_map)] + [_HBM] * len(c_outs),
        out_shape=[jax.ShapeDtypeStruct((m, n), out_dtype)] + c_outs,
        scratch_shapes=([pltpu.VMEM((tm, tn), F32)] if gk > 1 else []) + c_sems,
        compiler_params=_cp(*(("arbitrary",) * 3 if nx else ("parallel", "parallel", "arbitrary"))),
    )(*args, *cx)
    return out if nx else out[0]


def _rmsnorm_fwd(x, g, name):
    n, d = x.shape
    tb = _tile(n, 256, SUBLANES)

    def body(x_ref, g_ref, u_ref):
        xv = x_ref[...]
        rstd = lax.rsqrt(jnp.mean(xv * xv, axis=-1, keepdims=True) + NORM_EPS)
        u_ref[...] = (xv * rstd * g_ref[...]).astype(BF16)

    row = pl.BlockSpec((tb, d), lambda i: (i, 0))
    vec = pl.BlockSpec((1, d), lambda i: (0, 0))
    return _call(body, name=name, grid=(n // tb,), in_specs=[row, vec], out_specs=row,
                 out_shape=jax.ShapeDtypeStruct((n, d), BF16), compiler_params=_cp("parallel"))(x, g)


def _rmsnorm_bwd(du, x, g, dres, name):
    n, d = x.shape
    tb = _tile(n, 256, SUBLANES)

    def body(du_ref, x_ref, g_ref, dres_ref, dx_ref, dxb_ref, dg_ref):
        xv = x_ref[...]
        rstd = lax.rsqrt(jnp.mean(xv * xv, axis=-1, keepdims=True) + NORM_EPS)
        xh = xv * rstd
        duv = du_ref[...]
        t = duv * g_ref[...]
        dx = dres_ref[...] + rstd * (t - xh * jnp.mean(t * xh, axis=-1, keepdims=True))
        dx_ref[...] = dx
        dxb_ref[...] = dx.astype(BF16)

        @pl.when(pl.program_id(0) == 0)
        def _():
            dg_ref[...] = jnp.zeros_like(dg_ref)

        dg_ref[...] += jnp.sum(duv * xh, axis=0, keepdims=True)

    row = pl.BlockSpec((tb, d), lambda i: (i, 0))
    vec = pl.BlockSpec((1, d), lambda i: (0, 0))
    return _call(body, name=name, grid=(n // tb,), in_specs=[row, row, vec, row], out_specs=[row, row, vec],
                 out_shape=[jax.ShapeDtypeStruct((n, d), F32), jax.ShapeDtypeStruct((n, d), BF16),
                            jax.ShapeDtypeStruct((1, d), F32)],
                 compiler_params=_cp("arbitrary"))(du, x, g, dres)


def _loss_head(h, g, target, name):
    n, d = h.shape
    tb = _tile(n, 256, SUBLANES)

    def body(h_ref, g_ref, t_ref, dh_ref, dhb_ref, dg_ref, loss_ref):
        hv = h_ref[...]
        gv = g_ref[...]
        rstd = lax.rsqrt(jnp.mean(hv * hv, axis=-1, keepdims=True) + NORM_EPS)
        hh = hv * rstd
        err = hh * gv - t_ref[...]
        dy = err * (1.0 / d)
        dhh = dy * gv
        dh = rstd * (dhh - hh * jnp.mean(dhh * hh, axis=-1, keepdims=True))
        dh_ref[...] = dh
        dhb_ref[...] = dh.astype(BF16)

        @pl.when(pl.program_id(0) == 0)
        def _():
            dg_ref[...] = jnp.zeros_like(dg_ref)
            loss_ref[...] = jnp.zeros_like(loss_ref)

        dg_ref[...] += jnp.sum(dy * hh, axis=0, keepdims=True)
        loss_ref[...] += jnp.sum(err * err) * (0.5 / d)

    row = pl.BlockSpec((tb, d), lambda i: (i, 0))
    vec = pl.BlockSpec((1, d), lambda i: (0, 0))
    lvec = pl.BlockSpec((1, LANES), lambda i: (0, 0))
    return _call(body, name=name, grid=(n // tb,), in_specs=[row, vec, row], out_specs=[row, row, vec, lvec],
                 out_shape=[jax.ShapeDtypeStruct((n, d), F32), jax.ShapeDtypeStruct((n, d), BF16),
                            jax.ShapeDtypeStruct((1, d), F32), jax.ShapeDtypeStruct((1, LANES), F32)],
                 compiler_params=_cp("arbitrary"))(h, g, target)


def _swiglu_fwd(gate, up, name):
    n, f = gate.shape
    tb, tc = _tile(n, 1024, SUBLANES), _tile(f, 256)

    def body(g_ref, u_ref, o_ref):
        gv = g_ref[...]
        o_ref[...] = (gv * _sig(gv) * u_ref[...]).astype(BF16)

    blk = pl.BlockSpec((tb, tc), lambda i, j: (i, j))
    return _call(body, name=name, grid=(n // tb, f // tc), in_specs=[blk, blk], out_specs=blk,
                 out_shape=jax.ShapeDtypeStruct((n, f), BF16), compiler_params=_cp("parallel", "parallel"))(gate, up)


def _swiglu_bwd(gate, up, dact, name):
    n, f = gate.shape
    tb, tc = _tile(n, 1024, SUBLANES), _tile(f, 256)

    def body(g_ref, u_ref, d_ref, dg_ref, du_ref):
        gv = g_ref[...]
        s = _sig(gv)
        dv = d_ref[...]
        dg_ref[...] = (dv * u_ref[...] * s * (1.0 + gv * (1.0 - s))).astype(BF16)
        du_ref[...] = (dv * gv * s).astype(BF16)

    blk = pl.BlockSpec((tb, tc), lambda i, j: (i, j))
    return _call(body, name=name, grid=(n // tb, f // tc), in_specs=[blk, blk, blk], out_specs=[blk, blk],
                 out_shape=[jax.ShapeDtypeStruct((n, f), BF16)] * 2,
                 compiler_params=_cp("parallel", "parallel"))(gate, up, dact)


def _prep_common(prkv_ref, prkvp_ref, plo_ref, plop_ref, mur_ref, mul_ref, w0_ref, a0_ref, kk_ref, ka_ref,
                 w2_ref, a2_ref, g2_ref, seq_start, w, dlp, alp):
    z8r = jnp.zeros_like(prkvp_ref[...])
    z8l = jnp.zeros_like(plop_ref[...])
    prev_r = jnp.where(seq_start, z8r, prkvp_ref[...])
    prev_l = jnp.where(seq_start, z8l, plop_ref[...])
    p_r = prkv_ref[...]
    p_l = plo_ref[...]
    dif_r = _shift_down(p_r, prev_r, 1) - p_r
    dif_l = _shift_down(p_l, prev_l, 1) - p_l
    q_r = p_r + dif_r * mur_ref[...]
    q_l = p_l + dif_l * mul_ref[...]
    r, k, v = q_r[:, 0:w], q_r[:, w:2 * w], q_r[:, 2 * w:3 * w]
    wd, ad, gd = q_l[:, 0:dlp], q_l[:, dlp:dlp + alp], q_l[:, dlp + alp:]
    tw = jnp.tanh(wd)
    zw = w0_ref[...] + _dot(tw, w2_ref[...])
    wlog = -_softplus(-zw) - 0.5
    ew = jnp.exp(wlog)
    dec = jnp.exp(-ew)
    za = a0_ref[...] + _dot(ad, a2_ref[...])
    av = _sig(za)
    sg = _sig(gd)
    g = _dot(sg, g2_ref[...])
    return dict(dif_r=dif_r, dif_l=dif_l, r=r, k=k, v=v, ad=ad, tw=tw, zw=zw, ew=ew, dec=dec, av=av, sg=sg, g=g)


def _rwkv_prep_specs(n, tb, w, lp, t_len):
    nb8 = tb // SUBLANES
    row3 = pl.BlockSpec((tb, 3 * w), lambda i: (i, 0))
    prev3 = pl.BlockSpec((SUBLANES, 3 * w), lambda i: (jnp.maximum(i * nb8 - 1, 0), 0))
    rowl = pl.BlockSpec((tb, lp), lambda i: (i, 0))
    prevl = pl.BlockSpec((SUBLANES, lp), lambda i: (jnp.maximum(i * nb8 - 1, 0), 0))
    return row3, prev3, rowl, prevl


def _rwkv_prep_fwd(p_rkv, p_lora, prm, t_len, name):
    n, w3 = p_rkv.shape
    w = w3 // 3
    lp = p_lora.shape[1]
    dlp, alp = prm["w2"].shape[0], prm["a2"].shape[0]
    glp = lp - dlp - alp
    hp = max(w // RWKV_HEAD, LANES)
    tb = _tile(min(n, t_len), 128, SUBLANES)
    bps = t_len // tb

    def body(prkv_ref, prkvp_ref, plo_ref, plop_ref, mur_ref, mul_ref, w0_ref, a0_ref, kk_ref, ka_ref,
             w2_ref, a2_ref, g2_ref, r_o, dec_o, k_o, v_o, na_o, nb_o, g_o):
        seq_start = (pl.program_id(0) % bps) == 0
        f = _prep_common(prkv_ref, prkvp_ref, plo_ref, plop_ref, mur_ref, mul_ref, w0_ref, a0_ref, kk_ref, ka_ref,
                         w2_ref, a2_ref, g2_ref, seq_start, w, dlp, alp)
        e, et = _head_mats(w, hp)
        kk0 = f["k"] * kk_ref[...]
        inv = lax.rsqrt(jnp.maximum(_headsum(kk0 * kk0, e, et), KK_EPS))
        kk = kk0 * inv
        r_o[...] = f["r"]
        dec_o[...] = f["dec"]
        k_o[...] = f["k"] * (1.0 + (f["av"] - 1.0) * ka_ref[...])
        v_o[...] = f["v"]
        na_o[...] = -kk
        nb_o[...] = kk * f["av"]
        g_o[...] = f["g"]

    row3, prev3, rowl, prevl = _rwkv_prep_specs(n, tb, w, lp, t_len)
    c0 = lambda i: (0, 0)
    vec3 = pl.BlockSpec((1, 3 * w), c0)
    vecl = pl.BlockSpec((1, lp), c0)
    vec = pl.BlockSpec((1, w), c0)
    out = pl.BlockSpec((tb, w), lambda i: (i, 0))
    return _call(
        body, name=name, grid=(n // tb,),
        in_specs=[row3, prev3, rowl, prevl, vec3, vecl, vec, vec, vec, vec,
                  pl.BlockSpec((dlp, w), c0), pl.BlockSpec((alp, w), c0), pl.BlockSpec((glp, w), c0)],
        out_specs=[out] * 7, out_shape=[jax.ShapeDtypeStruct((n, w), F32)] * 7,
        compiler_params=_cp("parallel"),
    )(p_rkv, p_rkv, p_lora, p_lora, prm["mu_rkv"], prm["mu_lora"], prm["w0"], prm["a0"], prm["k_k"], prm["k_a"],
      prm["w2"], prm["a2"], prm["g2"])


def _rwkv_prep_bwd(p_rkv, p_lora, prm, grads, t_len, name):
    n, w3 = p_rkv.shape
    w = w3 // 3
    lp = p_lora.shape[1]
    dlp, alp = prm["w2"].shape[0], prm["a2"].shape[0]
    glp = lp - dlp - alp
    hp = max(w // RWKV_HEAD, LANES)
    tb = _tile(min(n, t_len), 64, SUBLANES)
    bps = t_len // tb

    def body(prkv_ref, prkvp_ref, plo_ref, plop_ref, mur_ref, mul_ref, w0_ref, a0_ref, kk_ref, ka_ref,
             w2_ref, a2_ref, g2_ref,
             drs_ref, drp_ref, ddec_ref, dks_ref, dkp_ref, dvs_ref, dvp_ref, dna_ref, dnb_ref, dg_ref,
             dqr_o, dql_o, dmur_o, dmul_o, dw0_o, da0_o, dkk_o, dka_o, dw2_o, da2_o, dg2_o):
        seq_start = (pl.program_id(0) % bps) == 0
        f = _prep_common(prkv_ref, prkvp_ref, plo_ref, plop_ref, mur_ref, mul_ref, w0_ref, a0_ref, kk_ref, ka_ref,
                         w2_ref, a2_ref, g2_ref, seq_start, w, dlp, alp)
        e, et = _head_mats(w, hp)
        k, av = f["k"], f["av"]
        k_k, k_a = kk_ref[...], ka_ref[...]
        kk0 = k * k_k
        n2 = _headsum(kk0 * kk0, e, et)
        inv = lax.rsqrt(jnp.maximum(n2, KK_EPS))
        kk = kk0 * inv
        dk2 = dks_ref[...] + dkp_ref[...]
        dnb = dnb_ref[...]
        dkk = dnb * av - dna_ref[...]
        dav = dnb * kk + dk2 * k * k_a
        dk = dk2 * (1.0 + (av - 1.0) * k_a)
        dka = dk2 * k * (av - 1.0)
        proj = jnp.where(n2 > KK_EPS, _headsum(dkk * kk, e, et), 0.0)
        dkk0 = inv * (dkk - kk * proj)
        dk = dk + dkk0 * k_k
        dkkp = dkk0 * k
        dgv = dg_ref[...]
        sg = f["sg"]
        dgd = _dot_nt(dgv, g2_ref[...]) * sg * (1.0 - sg)
        dza = dav * av * (1.0 - av)
        dad = _dot_nt(dza, a2_ref[...])
        dzw = ddec_ref[...] * f["dec"] * (-f["ew"]) * _sig(-f["zw"])
        tw = f["tw"]
        dwd = _dot_nt(dzw, w2_ref[...]) * (1.0 - tw * tw)
        dq_r = jnp.concatenate([drs_ref[...] + drp_ref[...], dk, dvs_ref[...] + dvp_ref[...]], axis=1)
        dq_l = jnp.concatenate([dwd, dad, dgd], axis=1)
        dqr_o[...] = dq_r
        dql_o[...] = dq_l

        @pl.when(pl.program_id(0) == 0)
        def _():
            for o in (dmur_o, dmul_o, dw0_o, da0_o, dkk_o, dka_o, dw2_o, da2_o, dg2_o):
                o[...] = jnp.zeros_like(o)

        def rsum(x):
            return jnp.sum(x, axis=0, keepdims=True)

        dmur_o[...] += rsum(dq_r * f["dif_r"])
        dmul_o[...] += rsum(dq_l * f["dif_l"])
        dw0_o[...] += rsum(dzw)
        da0_o[...] += rsum(dza)
        dkk_o[...] += rsum(dkkp)
        dka_o[...] += rsum(dka)
        dw2_o[...] += _dot_tn(tw, dzw)
        da2_o[...] += _dot_tn(f["ad"], dza)
        dg2_o[...] += _dot_tn(sg, dgv)

    row3, prev3, rowl, prevl = _rwkv_prep_specs(n, tb, w, lp, t_len)
    c0 = lambda i: (0, 0)
    vec3 = pl.BlockSpec((1, 3 * w), c0)
    vecl = pl.BlockSpec((1, lp), c0)
    vec = pl.BlockSpec((1, w), c0)
    blk = pl.BlockSpec((tb, w), lambda i: (i, 0))
    m2, ma, mg = pl.BlockSpec((dlp, w), c0), pl.BlockSpec((alp, w), c0), pl.BlockSpec((glp, w), c0)
    sds = jax.ShapeDtypeStruct
    return _call(
        body, name=name, grid=(n // tb,),
        in_specs=[row3, prev3, rowl, prevl, vec3, vecl, vec, vec, vec, vec, m2, ma, mg] + [blk] * 10,
        out_specs=[row3, rowl, vec3, vecl, vec, vec, vec, vec, m2, ma, mg],
        out_shape=[sds((n, 3 * w), F32), sds((n, lp), F32), sds((1, 3 * w), F32), sds((1, lp), F32),
                   sds((1, w), F32), sds((1, w), F32), sds((1, w), F32), sds((1, w), F32),
                   sds((dlp, w), F32), sds((alp, w), F32), sds((glp, w), F32)],
        compiler_params=_cp("arbitrary"),
    )(p_rkv, p_rkv, p_lora, p_lora, prm["mu_rkv"], prm["mu_lora"], prm["w0"], prm["a0"], prm["k_k"], prm["k_a"],
      prm["w2"], prm["a2"], prm["g2"], *grads)


def _shift_combine(dq, mu, t_len, name):
    n, c = dq.shape
    tb = _tile(min(n, t_len), 256, SUBLANES)
    bps = t_len // tb
    nb8 = tb // SUBLANES
    last8 = n // SUBLANES - 1

    def body(x_ref, nx_ref, mu_ref, o_ref):
        seq_end = (pl.program_id(0) % bps) == bps - 1
        nxt = jnp.where(seq_end, jnp.zeros_like(nx_ref[...]), nx_ref[...])
        x = x_ref[...]
        muv = mu_ref[...]
        o_ref[...] = ((1.0 - muv) * x + muv * _shift_up(x, nxt, 1)).astype(BF16)

    row = pl.BlockSpec((tb, c), lambda i: (i, 0))
    nxt = pl.BlockSpec((SUBLANES, c), lambda i: (jnp.minimum((i + 1) * nb8, last8), 0))
    vec = pl.BlockSpec((1, c), lambda i: (0, 0))
    return _call(body, name=name, grid=(n // tb,), in_specs=[row, nxt, vec], out_specs=row,
                 out_shape=jax.ShapeDtypeStruct((n, c), BF16), compiler_params=_cp("parallel"))(dq, dq, mu)


def _scan_step(s_i, a_t, w_t, b_t, k_t, v_i):
    sa = jnp.sum(s_i * a_t, axis=0, keepdims=True)
    return s_i * w_t + sa * b_t + v_i * k_t, sa


def _carried(comm):
    if comm is None:
        return [], [], [], [], None, None
    outs, n_sems, start, finish = _comm_plan(*comm)
    xs = list(comm[1])
    sems = [pltpu.SemaphoreType.DMA((n_sems,)), pltpu.SemaphoreType.DMA((n_sems,))]
    return xs, [_HBM] * len(xs), outs, sems, start, finish


def _rwkv_scan_fwd(r, w, k, a, b, v, name, comm=None):
    t_len, kd, ln = r.shape
    vh = v.shape[1]
    tc = SCAN_CHUNK
    nc = t_len // tc
    cx, cx_specs, c_outs, c_sems, c_start, c_finish = _carried(comm)
    nx = len(cx)

    def body(r_ref, w_ref, k_ref, a_ref, b_ref, v_ref, *rest):
        c_ins, (y_ref, st_ref, sa_ref), c_out_refs = rest[:nx], rest[nx:nx + 3], rest[nx + 3:nx + 3 + len(c_outs)]
        s_ref = rest[nx + 3 + len(c_outs)]

        @pl.when(pl.program_id(0) == 0)
        def _():
            s_ref[...] = jnp.zeros_like(s_ref)
            if nx:
                c_start(c_ins, c_out_refs, rest[-2], rest[-1])

        st_ref[0, 0] = s_ref[...]

        def step(t, carry):
            a_t, w_t, b_t, k_t, r_t = a_ref[t], w_ref[t], b_ref[t], k_ref[t], r_ref[t]
            for i in range(vh):
                s_new, sa = _scan_step(st_ref[0, t, i], a_t, w_t, b_t, k_t, v_ref[t, pl.ds(i, 1), :])
                st_ref[0, t + 1, i] = s_new
                sa_ref[t, pl.ds(i, 1), :] = sa
                y_ref[t, pl.ds(i, 1), :] = jnp.sum(s_new * r_t, axis=0, keepdims=True)
            return carry

        lax.fori_loop(0, tc, step, 0)
        s_ref[...] = st_ref[0, tc]

        if nx:
            @pl.when(pl.program_id(0) == nc - 1)
            def _():
                c_finish(c_ins, c_out_refs, rest[-2], rest[-1])

    kblk = pl.BlockSpec((tc, kd, ln), lambda c: (c, 0, 0))
    vblk = pl.BlockSpec((tc, vh, ln), lambda c: (c, 0, 0))
    vsd = jax.ShapeDtypeStruct((t_len, vh, ln), F32)
    return _call(
        body, name=name, grid=(nc,), in_specs=[kblk] * 5 + [vblk] + cx_specs,
        out_specs=[vblk, pl.BlockSpec((1, tc + 1, vh, kd, ln), lambda c: (c, 0, 0, 0, 0)), vblk] + [_HBM] * len(c_outs),
        out_shape=[vsd, jax.ShapeDtypeStruct((nc, tc + 1, vh, kd, ln), F32), vsd] + c_outs,
        scratch_shapes=[pltpu.VMEM((vh, kd, ln), F32)] + c_sems,
        compiler_params=_cp("arbitrary"),
    )(r, w, k, a, b, v, *cx)


def _rwkv_scan_bwd(r, w, k, a, b, v, dy, states, sa, name, comm=None):
    t_len, kd, ln = r.shape
    vh = v.shape[1]
    tc = SCAN_CHUNK
    nc = t_len // tc
    half = ln // 2
    cx, cx_specs, c_outs, c_sems, c_start, c_finish = _carried(comm)
    nx = len(cx)

    def body(r_ref, w_ref, k_ref, a_ref, b_ref, v_ref, dy_ref, st_ref, sa_ref, *rest):
        c_ins = rest[:nx]
        dr_o, dw_o, dk_o, da_o, db_o, dv_o = rest[nx:nx + 6]
        c_out_refs = rest[nx + 6:nx + 6 + len(c_outs)]
        ds_ref = rest[nx + 6 + len(c_outs)]

        @pl.when(pl.program_id(0) == 0)
        def _():
            ds_ref[...] = jnp.zeros_like(ds_ref)
            if nx:
                c_start(c_ins, c_out_refs, rest[-2], rest[-1])

        def bwd(tt, carry):
            t = tc - 1 - tt
            a_t, w_t, b_t, k_t, r_t = a_ref[t], w_ref[t], b_ref[t], k_ref[t], r_ref[t]
            z = jnp.zeros((kd, ln), F32)
            dr, dw, dk, da, db = z, z, z, z, z
            for i in range(vh):
                dy_i = dy_ref[t, pl.ds(i, 1), :]
                s_t = st_ref[0, t + 1, i]
                s_p = st_ref[0, t, i]
                d = ds_ref[i] + dy_i * r_t
                dr = dr + s_t * dy_i
                dv_o[t, pl.ds(i, 1), :] = jnp.sum(d * k_t, axis=0, keepdims=True)
                dk = dk + d * v_ref[t, pl.ds(i, 1), :]
                dsa = jnp.sum(d * b_t, axis=0, keepdims=True)
                db = db + d * sa_ref[t, pl.ds(i, 1), :]
                dw = dw + d * s_p
                da = da + s_p * dsa
                ds_ref[i] = d * w_t + dsa * a_t

            def both(x):
                return x + pltpu.roll(x, half, 1)

            dr_o[t] = both(dr)
            dw_o[t] = both(dw)
            dk_o[t] = both(dk)
            da_o[t] = both(da)
            db_o[t] = both(db)
            return carry

        lax.fori_loop(0, tc, bwd, 0)

        if nx:
            @pl.when(pl.program_id(0) == nc - 1)
            def _():
                c_finish(c_ins, c_out_refs, rest[-2], rest[-1])

    kblk = pl.BlockSpec((tc, kd, ln), lambda c: (nc - 1 - c, 0, 0))
    vblk = pl.BlockSpec((tc, vh, ln), lambda c: (nc - 1 - c, 0, 0))
    ksd = jax.ShapeDtypeStruct((t_len, kd, ln), F32)
    return _call(
        body, name=name, grid=(nc,),
        in_specs=[kblk] * 5 + [vblk, vblk, pl.BlockSpec((1, tc + 1, vh, kd, ln), lambda c: (nc - 1 - c, 0, 0, 0, 0)),
                  vblk] + cx_specs,
        out_specs=[kblk] * 5 + [vblk] + [_HBM] * len(c_outs),
        out_shape=[ksd] * 5 + [jax.ShapeDtypeStruct((t_len, vh, ln), F32)] + c_outs,
        scratch_shapes=[pltpu.VMEM((vh, kd, ln), F32)] + c_sems,
        compiler_params=_cp("arbitrary"),
    )(r, w, k, a, b, v, dy, states, sa, *cx)


def _post_common(y_ref, r_ref, k_ref, v_ref, lng_ref, lnb_ref, rk_ref, e, et):
    y = y_ref[...]
    inv_n = 1.0 / RWKV_HEAD
    mean = _headsum(y, e, et) * inv_n
    yc = y - mean
    var = _headsum(yc * yc, e, et) * inv_n
    rstd = lax.rsqrt(var + GN_EPS)
    yh = yc * rstd
    yn = yh * lng_ref[...] + lnb_ref[...]
    bonus = _headsum(r_ref[...] * k_ref[...] * rk_ref[...], e, et)
    return yh, rstd, yn, bonus


def _rwkv_post_fwd(y, r, k, v, g, ln_g, ln_b, r_k, name):
    n, w = y.shape
    hp = max(w // RWKV_HEAD, LANES)
    tb = _tile(n, 256, SUBLANES)

    def body(y_ref, r_ref, k_ref, v_ref, g_ref, lng_ref, lnb_ref, rk_ref, o_ref):
        e, et = _head_mats(w, hp)
        _, _, yn, bonus = _post_common(y_ref, r_ref, k_ref, v_ref, lng_ref, lnb_ref, rk_ref, e, et)
        o_ref[...] = ((yn + bonus * v_ref[...]) * g_ref[...]).astype(BF16)

    blk = pl.BlockSpec((tb, w), lambda i: (i, 0))
    vec = pl.BlockSpec((1, w), lambda i: (0, 0))
    return _call(body, name=name, grid=(n // tb,), in_specs=[blk] * 5 + [vec] * 3, out_specs=blk,
                 out_shape=jax.ShapeDtypeStruct((n, w), BF16),
                 compiler_params=_cp("parallel"))(y, r, k, v, g, ln_g, ln_b, r_k)


def _rwkv_post_bwd(y, r, k, v, g, ln_g, ln_b, r_k, do_cat, name):
    n, w = y.shape
    hp = max(w // RWKV_HEAD, LANES)
    tb = _tile(n, 128, SUBLANES)

    def body(y_ref, r_ref, k_ref, v_ref, g_ref, lng_ref, lnb_ref, rk_ref, do_ref,
             dy_o, dr_o, dk_o, dv_o, dg_o, dlng_o, dlnb_o, drk_o):
        e, et = _head_mats(w, hp)
        yh, rstd, yn, bonus = _post_common(y_ref, r_ref, k_ref, v_ref, lng_ref, lnb_ref, rk_ref, e, et)
        do = do_ref[...]
        vv, rv, kv, rk = v_ref[...], r_ref[...], k_ref[...], rk_ref[...]
        dg_o[...] = do * (yn + bonus * vv)
        dz = do * g_ref[...]
        dbonus = _headsum(dz * vv, e, et)
        dv_o[...] = dz * bonus
        dr_o[...] = dbonus * kv * rk
        dk_o[...] = dbonus * rv * rk
        dyh = dz * lng_ref[...]
        inv_n = 1.0 / RWKV_HEAD
        dy_o[...] = rstd * (dyh - _headsum(dyh, e, et) * inv_n - yh * (_headsum(dyh * yh, e, et) * inv_n))

        @pl.when(pl.program_id(0) == 0)
        def _():
            for o in (dlng_o, dlnb_o, drk_o):
                o[...] = jnp.zeros_like(o)

        dlng_o[...] += jnp.sum(dz * yh, axis=0, keepdims=True)
        dlnb_o[...] += jnp.sum(dz, axis=0, keepdims=True)
        drk_o[...] += jnp.sum(dbonus * rv * kv, axis=0, keepdims=True)

    blk = pl.BlockSpec((tb, w), lambda i: (i, 0))
    vec = pl.BlockSpec((1, w), lambda i: (0, 0))
    sds = jax.ShapeDtypeStruct
    return _call(body, name=name, grid=(n // tb,), in_specs=[blk] * 5 + [vec] * 3 + [blk],
                 out_specs=[blk] * 5 + [vec] * 3,
                 out_shape=[sds((n, w), F32)] * 5 + [sds((1, w), F32)] * 3,
                 compiler_params=_cp("arbitrary"))(y, r, k, v, g, ln_g, ln_b, r_k, do_cat)


def _lru_gates(xb, prev8, gate, cw_ref, cb_ref, wr_ref, br_ref, wi_ref, bi_ref, lam_ref, is_t0):
    c = xb.shape[1]
    nblk = c // LRU_BLOCK_W
    xs = [xb] + [_shift_down(xb, prev8, j) for j in range(1, CONV_WIDTH)]
    xc = cb_ref[...]
    for j in range(CONV_WIDTH):
        xc = xc + xs[CONV_WIDTH - 1 - j] * cw_ref[pl.ds(j, 1), :]
    xcb = xc.astype(BF16)

    def blockmm(w_ref):
        return jnp.concatenate(
            [jnp.dot(xcb[:, h * LRU_BLOCK_W:(h + 1) * LRU_BLOCK_W], w_ref[h], preferred_element_type=F32)
             for h in range(nblk)], axis=1)

    rg = _sig(blockmm(wr_ref) + br_ref[...])
    ig = _sig(blockmm(wi_ref) + bi_ref[...])
    sp = _softplus(-lam_ref[...])
    la = -LRU_C * rg * sp
    av = jnp.exp(la)
    mult = jnp.where(is_t0, 1.0, jnp.sqrt(_neg_expm1(2.0 * la)))
    ge, th = _gelu_parts(gate)
    return dict(xs=xs, xc=xc, xcb=xcb, rg=rg, ig=ig, sp=sp, a=av, mult=mult, ge=ge, th=th)


def _lru_specs(tb, c, nb, rev):
    nb8 = tb // SUBLANES

    def blk_i(i):
        return nb - 1 - i if rev else i

    xb = pl.BlockSpec((tb, c), lambda b, i: (b * nb + blk_i(i), 0))
    gate = pl.BlockSpec((tb, c), lambda b, i: (b * nb + blk_i(i), 1))
    prev = pl.BlockSpec((SUBLANES, c), lambda b, i: (jnp.maximum((b * nb + blk_i(i)) * nb8 - 1, 0), 0))
    return xb, gate, prev


def _lru_fwd(p_lru, prm, t_len, name):
    n, c2 = p_lru.shape
    c = c2 // 2
    nblk = c // LRU_BLOCK_W
    tb = _tile(t_len, 256, SUBLANES)
    nb = t_len // tb
    bsz = n // t_len

    def body(xb_ref, gate_ref, prev_ref, cw_ref, cb_ref, wr_ref, br_ref, wi_ref, bi_ref, lam_ref, ng_ref,
             y_o, h_o, carry):
        i = pl.program_id(1)
        prev8 = jnp.where(i == 0, jnp.zeros_like(prev_ref[...]), prev_ref[...])
        row = lax.broadcasted_iota(jnp.int32, (tb, c), 0)
        f = _lru_gates(xb_ref[...], prev8, gate_ref[...], cw_ref, cb_ref, wr_ref, br_ref, wi_ref, bi_ref, lam_ref,
                       jnp.logical_and(i == 0, row == 0))
        acc_a = f["a"]
        acc_b = f["mult"] * f["ig"] * f["xc"]
        s = 1
        while s < tb:
            keep = row >= s
            a_sh = jnp.where(keep, pltpu.roll(acc_a, s, 0), 1.0)
            b_sh = jnp.where(keep, pltpu.roll(acc_b, s, 0), 0.0)
            acc_b = acc_a * b_sh + acc_b
            acc_a = acc_a * a_sh
            s *= 2

        @pl.when(i == 0)
        def _():
            carry[...] = jnp.zeros_like(carry)

        h = acc_b + acc_a * carry[0:1, :]
        carry[0:1, :] = h[tb - 1:tb, :]
        h_o[...] = h
        y = h * f["ge"]
        rstd = lax.rsqrt(jnp.mean(y * y, axis=-1, keepdims=True) + NORM_EPS)
        y_o[...] = (y * rstd * ng_ref[...]).astype(BF16)

    xb_s, gate_s, prev_s = _lru_specs(tb, c, nb, False)
    c0 = lambda b, i: (0, 0)
    vec = pl.BlockSpec((1, c), c0)
    wsp = pl.BlockSpec((nblk, LRU_BLOCK_W, LRU_BLOCK_W), lambda b, i: (0, 0, 0))
    out = pl.BlockSpec((tb, c), lambda b, i: (b * nb + i, 0))
    return _call(
        body, name=name, grid=(bsz, nb),
        in_specs=[xb_s, gate_s, prev_s, pl.BlockSpec((CONV_WIDTH, c), c0), vec, wsp, vec, wsp, vec, vec, vec],
        out_specs=[out, out],
        out_shape=[jax.ShapeDtypeStruct((n, c), BF16), jax.ShapeDtypeStruct((n, c), F32)],
        scratch_shapes=[pltpu.VMEM((SUBLANES, c), F32)],
        compiler_params=_cp("arbitrary", "arbitrary"),
    )(p_lru, p_lru, p_lru, prm["conv_w"], prm["conv_b"], prm["wr"], prm["br"], prm["wi"], prm["bi"],
      prm["lam"], prm["norm_g"])


def _lru_bwd(p_lru, h, do_cat, prm, t_len, name):
    n, c2 = p_lru.shape
    c = c2 // 2
    nblk = c // LRU_BLOCK_W
    tb = _tile(t_len, 128, SUBLANES)
    nb = t_len // tb
    bsz = n // t_len

    def body(xb_ref, gate_ref, prev_ref, h_ref, hprev_ref, do_ref,
             cw_ref, cb_ref, wr_ref, br_ref, wi_ref, bi_ref, lam_ref, ng_ref,
             dp_o, dcw_o, dcb_o, dwr_o, dbr_o, dwi_o, dbi_o, dlam_o, dng_o,
             a_next, g_next, dxc_next):
        b = pl.program_id(0)
        i = pl.program_id(1)
        blk = nb - 1 - i
        first = blk == 0
        prev8 = jnp.where(first, jnp.zeros_like(prev_ref[...]), prev_ref[...])
        hprev8 = jnp.where(first, jnp.zeros_like(hprev_ref[...]), hprev_ref[...])
        row = lax.broadcasted_iota(jnp.int32, (tb, c), 0)
        is_t0 = jnp.logical_and(first, row == 0)
        gate = gate_ref[...]
        f = _lru_gates(xb_ref[...], prev8, gate, cw_ref, cb_ref, wr_ref, br_ref, wi_ref, bi_ref, lam_ref, is_t0)

        @pl.when(i == 0)
        def _():
            a_next[...] = jnp.zeros_like(a_next)
            g_next[...] = jnp.zeros_like(g_next)
            dxc_next[...] = jnp.zeros_like(dxc_next)

        @pl.when(jnp.logical_and(b == 0, i == 0))
        def _():
            for o in (dcw_o, dcb_o, dwr_o, dbr_o, dwi_o, dbi_o, dlam_o, dng_o):
                o[...] = jnp.zeros_like(o)

        def rsum(x):
            return jnp.sum(x, axis=0, keepdims=True)

        hv = h_ref[...]
        hprev = _shift_down(hv, hprev8, 1)
        ge = f["ge"]
        y = hv * ge
        rstd = lax.rsqrt(jnp.mean(y * y, axis=-1, keepdims=True) + NORM_EPS)
        yh = y * rstd
        dyn = do_ref[...]
        t = dyn * ng_ref[...]
        dy = rstd * (t - yh * jnp.mean(t * yh, axis=-1, keepdims=True))
        dng_o[...] += rsum(dyn * yh)
        dgate = dy * hv * _gelu_grad(gate, f["th"])

        av = f["a"]
        acc_c = _shift_up(av, a_next[...], 1)
        acc_g = dy * ge
        s = 1
        while s < tb:
            keep = row < tb - s
            c_sh = jnp.where(keep, pltpu.roll(acc_c, tb - s, 0), 1.0)
            g_sh = jnp.where(keep, pltpu.roll(acc_g, tb - s, 0), 0.0)
            acc_g = acc_g + acc_c * g_sh
            acc_c = acc_c * c_sh
            s *= 2
        gtot = acc_g + acc_c * g_next[0:1, :]
        a_next[0:1, :] = av[0:1, :]
        g_next[0:1, :] = gtot[0:1, :]

        xc, ig, rg, mult = f["xc"], f["ig"], f["rg"], f["mult"]
        da = gtot * hprev
        dmult = gtot * ig * xc
        dig = gtot * mult * xc
        dxc = gtot * mult * ig
        da = da + jnp.where(is_t0, 0.0, -dmult * av / mult)
        dla = da * av
        drg = dla * (-LRU_C) * f["sp"]
        dlam_o[...] += rsum(dla * rg) * LRU_C * _sig(-lam_ref[...])
        dzr = drg * rg * (1.0 - rg)
        dzi = dig * ig * (1.0 - ig)
        dbr_o[...] += rsum(dzr)
        dbi_o[...] += rsum(dzi)
        dzrb, dzib = dzr.astype(BF16), dzi.astype(BF16)
        xcb = f["xcb"]
        back = []
        for hh in range(nblk):
            sl = slice(hh * LRU_BLOCK_W, (hh + 1) * LRU_BLOCK_W)
            dwr_o[hh] += _dot_tn(xcb[:, sl], dzrb[:, sl])
            dwi_o[hh] += _dot_tn(xcb[:, sl], dzib[:, sl])
            back.append(_dot_nt(dzrb[:, sl], wr_ref[hh]) + _dot_nt(dzib[:, sl], wi_ref[hh]))
        dxc = dxc + jnp.concatenate(back, axis=1)
        dcb_o[...] += rsum(dxc)
        xs = f["xs"]
        dcw_o[...] += jnp.concatenate([rsum(dxc * xs[CONV_WIDTH - 1 - j]) for j in range(CONV_WIDTH)], axis=0)
        nxt = dxc_next[...]
        dxb = dxc * cw_ref[pl.ds(CONV_WIDTH - 1, 1), :]
        for j in range(1, CONV_WIDTH):
            dxb = dxb + _shift_up(dxc, nxt, j) * cw_ref[pl.ds(CONV_WIDTH - 1 - j, 1), :]
        dxc_next[...] = dxc[0:SUBLANES, :]
        dp_o[:, 0:c] = dxb.astype(BF16)
        dp_o[:, c:2 * c] = dgate.astype(BF16)

    xb_s, gate_s, prev_s = _lru_specs(tb, c, nb, True)
    c0 = lambda b, i: (0, 0)
    vec = pl.BlockSpec((1, c), c0)
    wsp = pl.BlockSpec((nblk, LRU_BLOCK_W, LRU_BLOCK_W), lambda b, i: (0, 0, 0))
    cwsp = pl.BlockSpec((CONV_WIDTH, c), c0)
    sds = jax.ShapeDtypeStruct
    return _call(
        body, name=name, grid=(bsz, nb),
        in_specs=[xb_s, gate_s, prev_s, xb_s, prev_s, gate_s, cwsp, vec, wsp, vec, wsp, vec, vec, vec],
        out_specs=[pl.BlockSpec((tb, 2 * c), lambda b, i: (b * nb + nb - 1 - i, 0)),
                   cwsp, vec, wsp, vec, wsp, vec, vec, vec],
        out_shape=[sds((n, 2 * c), BF16), sds((CONV_WIDTH, c), F32), sds((1, c), F32),
                   sds((nblk, LRU_BLOCK_W, LRU_BLOCK_W), F32), sds((1, c), F32),
                   sds((nblk, LRU_BLOCK_W, LRU_BLOCK_W), F32), sds((1, c), F32), sds((1, c), F32), sds((1, c), F32)],
        scratch_shapes=[pltpu.VMEM((SUBLANES, c), F32)] * 3,
        compiler_params=_cp("arbitrary", "arbitrary"),
    )(p_lru, p_lru, p_lru, h, h, do_cat, prm["conv_w"], prm["conv_b"], prm["wr"], prm["br"], prm["wi"], prm["bi"],
      prm["lam"], prm["norm_g"])


def _adamw(g, w, m, v, name):
    rows, cols = g.shape
    tb = _tile(rows, 128, SUBLANES)

    def body(g_ref, w_ref, m_ref, v_ref, d_o, m_o, v_o):
        gv = g_ref[...]
        mn = ADAM_B1 * m_ref[...] + (1.0 - ADAM_B1) * gv
        vn = ADAM_B2 * v_ref[...] + (1.0 - ADAM_B2) * (gv * gv)
        m_o[...] = mn
        v_o[...] = vn
        d_o[...] = -ADAM_LR * ((mn / _BC1) / (jnp.sqrt(vn / _BC2) + ADAM_EPS) + ADAM_WD * w_ref[...])

    blk = pl.BlockSpec((tb, cols), lambda i: (i, 0))
    return _call(body, name=name, grid=(rows // tb,), in_specs=[blk] * 4, out_specs=[blk] * 3,
                 out_shape=[jax.ShapeDtypeStruct((rows, cols), F32)] * 3, compiler_params=_cp("parallel"))(g, w, m, v)


def _adamw_halves(mine, theirs, w, m, v, name):
    a, b = mine.shape
    tc = _col_tile(a, b)
    w, m, v = (t.reshape(2, a, b) for t in (w, m, v))

    def body(mine_ref, theirs_ref, w_ref, m_ref, v_ref, g_o, d_o, m_o, v_o):
        gv = jnp.where(pl.program_id(0) == lax.axis_index("c"), mine_ref[...], theirs_ref[...])
        mn = ADAM_B1 * m_ref[...] + (1.0 - ADAM_B1) * gv
        vn = ADAM_B2 * v_ref[...] + (1.0 - ADAM_B2) * (gv * gv)
        g_o[...] = gv
        m_o[...] = mn
        v_o[...] = vn
        d_o[...] = -ADAM_LR * ((mn / _BC1) / (jnp.sqrt(vn / _BC2) + ADAM_EPS) + ADAM_WD * w_ref[...])

    half = pl.BlockSpec((a, tc), lambda h, j: (0, j))
    blk = pl.BlockSpec((None, a, tc), lambda h, j: (h, 0, j))
    return _call(body, name=name, grid=(2, b // tc), in_specs=[half, half, blk, blk, blk], out_specs=[blk] * 4,
                 out_shape=[jax.ShapeDtypeStruct((2, a, b), F32)] * 4,
                 compiler_params=_cp("parallel", "parallel"))(mine, theirs, w, m, v)


def _pair_sum(x4, recv, name):
    _, _, a, b = x4.shape
    tc = _col_tile(a, b)

    def body(x_ref, r_ref, o_ref):
        mine = x_ref[lax.axis_index("c")]
        o_ref[...] = (mine.astype(F32) + r_ref[...].astype(F32)).astype(BF16)

    return _call(
        body, name=name, grid=(4, b // tc),
        in_specs=[pl.BlockSpec((None, 2, a, tc), lambda j, i: (j, 0, 0, i)),
                  pl.BlockSpec((None, a, tc), lambda j, i: (j, 0, i))],
        out_specs=pl.BlockSpec((None, a, tc), lambda j, i: (j, 0, i)),
        out_shape=jax.ShapeDtypeStruct((4, a, b), BF16), compiler_params=_cp("parallel", "parallel"))(x4, recv)


def _chip_sum(x4, name):
    _, a, b = x4.shape
    ta = _tile(a, 256, SUBLANES)

    def body(x_ref, o_ref):
        acc = x_ref[0] + x_ref[1]
        acc = acc + x_ref[2]
        o_ref[...] = acc + x_ref[3]

    return _call(
        body, name=name, grid=(a // ta,),
        in_specs=[pl.BlockSpec((4, ta, b), lambda i: (0, i, 0))],
        out_specs=pl.BlockSpec((ta, b), lambda i: (i, 0)),
        out_shape=jax.ShapeDtypeStruct((a, b), F32), compiler_params=_cp("parallel"))(x4)


def _peer_sum(own4, parts, name):
    _, a, b = parts.shape
    tc = _col_tile(a, b)

    def body(own_ref, p_ref, o_ref):
        me = 2 * lax.axis_index("x") + lax.axis_index("y")
        acc = own_ref[me].astype(F32) + p_ref[0].astype(F32)
        acc = acc + p_ref[1].astype(F32)
        o_ref[...] = acc + p_ref[2].astype(F32)

    return _call(
        body, name=name, grid=(b // tc,),
        in_specs=[pl.BlockSpec((4, a, tc), lambda i: (0, 0, i)), pl.BlockSpec((3, a, tc), lambda i: (0, 0, i))],
        out_specs=pl.BlockSpec((a, tc), lambda i: (0, i)),
        out_shape=jax.ShapeDtypeStruct((a, b), F32), compiler_params=_cp("parallel"))(own4, parts)


def _add2(x, y, name):
    rows, cols = x.shape
    tb = _tile(rows, 512, SUBLANES)

    def body(x_ref, y_ref, o_ref):
        o_ref[...] = x_ref[...] + y_ref[...]

    blk = pl.BlockSpec((tb, cols), lambda i: (i, 0))
    return _call(body, name=name, grid=(rows // tb,), in_specs=[blk, blk], out_specs=blk,
                 out_shape=jax.ShapeDtypeStruct((rows, cols), x.dtype), compiler_params=_cp("parallel"))(x, y)


_HBM = pl.BlockSpec(memory_space=pltpu.HBM)


def _place():
    x, y, c = lax.axis_index("x"), lax.axis_index("y"), lax.axis_index("c")
    chips = [(1 - x, y), (x, 1 - y), (1 - x, 1 - y)]
    return x, y, c, chips


def _comm_call(body, name, xs, out_shapes, n_sems):
    return _call(
        body, name=name, in_specs=[_HBM] * len(xs), out_specs=[_HBM] * len(out_shapes), out_shape=out_shapes,
        scratch_shapes=[pltpu.SemaphoreType.DMA((n_sems,)), pltpu.SemaphoreType.DMA((n_sems,)),
                        pltpu.SemaphoreType.DMA((len(xs),))],
    )(*xs)


def _all_gather_chips(xs, name):
    n = len(xs)

    def body(*refs):
        ins, outs = refs[:n], refs[n:2 * n]
        ssem, rsem, _ = refs[2 * n:]
        _gather_start(ins, outs, ssem, rsem)
        _gather_finish(ins, outs, ssem, rsem)

    outs = [jax.ShapeDtypeStruct((4,) + v.shape, v.dtype) for v in xs]
    return _comm_call(body, name, xs, outs, GATHER_SEMS * n)


GATHER_SEMS = 7
PEER_SEMS = 3


def _remote(src, dst, ssem, rsem, k, dev):
    return pltpu.make_async_remote_copy(src_ref=src, dst_ref=dst, send_sem=ssem.at[k], recv_sem=rsem.at[k],
                                        device_id=dev, device_id_type=MESH)


def _gather_start(ins, outs, ssem, rsem):
    x, y, c, chips = _place()
    me = 2 * x + y
    for i in range(len(ins)):
        for j, (px, py) in enumerate(chips):
            _remote(ins[i].at[c], outs[i].at[me, c], ssem, rsem, GATHER_SEMS * i + j, (px, py, c)).start()
        _remote(ins[i], outs[i].at[me], ssem, rsem, GATHER_SEMS * i + 6, (x, y, 1 - c)).start()


def _gather_finish(ins, outs, ssem, rsem):
    x, y, c, chips = _place()
    me = 2 * x + y
    sib = (x, y, 1 - c)
    n = len(ins)
    for i in range(n):
        for j, (px, py) in enumerate(chips):
            slot = outs[i].at[2 * px + py, c]
            _remote(slot, slot, ssem, rsem, GATHER_SEMS * i + j, (px, py, c)).wait_recv()
            _remote(slot, slot, ssem, rsem, GATHER_SEMS * i + 3 + j, sib).start()
    for i in range(n):
        own = outs[i].at[me]
        _remote(own, own, ssem, rsem, GATHER_SEMS * i + 6, sib).wait_recv()
        for j, (px, py) in enumerate(chips):
            slot = outs[i].at[2 * px + py, 1 - c]
            _remote(slot, slot, ssem, rsem, GATHER_SEMS * i + 3 + j, sib).wait_recv()
    for i in range(n):
        for j, (px, py) in enumerate(chips):
            slot = outs[i].at[2 * px + py, c]
            _remote(ins[i].at[c], outs[i].at[me, c], ssem, rsem, GATHER_SEMS * i + j, (px, py, c)).wait_send()
            _remote(slot, slot, ssem, rsem, GATHER_SEMS * i + 3 + j, sib).wait_send()
        _remote(ins[i], outs[i].at[me], ssem, rsem, GATHER_SEMS * i + 6, sib).wait_send()


def _peer_copies(ins, outs, ssem, rsem):
    x, y, c, chips = _place()
    return [_remote(ins[i].at[2 * px + py], outs[i].at[j], ssem, rsem, PEER_SEMS * i + j, (px, py, c))
            for i in range(len(ins)) for j, (px, py) in enumerate(chips)]


def _comm_plan(kind, xs):
    if kind == "gather":
        outs = [jax.ShapeDtypeStruct((4,) + v.shape, v.dtype) for v in xs]
        return outs, GATHER_SEMS * len(xs), _gather_start, _gather_finish
    if kind == "swap_half":
        def swaps(ins, outs, ssem, rsem):
            x, y, c, _ = _place()
            return [_remote(ins[i].at[:, 1 - c], outs[i], ssem, rsem, i, (x, y, 1 - c)) for i in range(len(ins))]

        def start_swaps(ins, outs, ssem, rsem):
            for cp in swaps(ins, outs, ssem, rsem):
                cp.start()

        def finish_swaps(ins, outs, ssem, rsem):
            for cp in swaps(ins, outs, ssem, rsem):
                cp.wait()

        outs = [jax.ShapeDtypeStruct((v.shape[0],) + v.shape[2:], v.dtype) for v in xs]
        return outs, len(xs), start_swaps, finish_swaps

    def start(ins, outs, ssem, rsem):
        for cp in _peer_copies(ins, outs, ssem, rsem):
            cp.start()

    def finish(ins, outs, ssem, rsem):
        for cp in _peer_copies(ins, outs, ssem, rsem):
            cp.wait()

    outs = [jax.ShapeDtypeStruct((3,) + v.shape[1:], v.dtype) for v in xs]
    return outs, PEER_SEMS * len(xs), start, finish


def _sibling_swap(xs, pick_half, name):
    n = len(xs)

    def body(*refs):
        ins, outs = refs[:n], refs[n:2 * n]
        ssem, rsem, _ = refs[2 * n:]
        x, y, c, _ = _place()
        cps = []
        for i in range(n):
            src = ins[i].at[:, 1 - c] if pick_half else ins[i]
            cp = pltpu.make_async_remote_copy(src_ref=src, dst_ref=outs[i], send_sem=ssem.at[i], recv_sem=rsem.at[i],
                                              device_id=(x, y, 1 - c), device_id_type=MESH)
            cp.start()
            cps.append(cp)
        for cp in cps:
            cp.wait()

    outs = [jax.ShapeDtypeStruct((v.shape[0],) + v.shape[2:] if pick_half else v.shape, v.dtype) for v in xs]
    return _comm_call(body, name, xs, outs, n)


def _chip_broadcast(xs, name):
    n = len(xs)

    def body(*refs):
        ins, outs = refs[:n], refs[n:2 * n]
        ssem, rsem, lsem = refs[2 * n:]
        x, y, c, chips = _place()
        me = 2 * x + y
        cps = []
        for i in range(n):
            cp = pltpu.make_async_copy(ins[i], outs[i].at[me], lsem.at[i])
            cp.start()
            cps.append(cp)
            for j, (px, py) in enumerate(chips):
                cp = pltpu.make_async_remote_copy(
                    src_ref=ins[i], dst_ref=outs[i].at[me], send_sem=ssem.at[3 * i + j], recv_sem=rsem.at[3 * i + j],
                    device_id=(px, py, c), device_id_type=MESH)
                cp.start()
                cps.append(cp)
        for i in range(n):
            for j, (px, py) in enumerate(chips):
                slot = outs[i].at[2 * px + py]
                pltpu.make_async_remote_copy(
                    src_ref=slot, dst_ref=slot, send_sem=ssem.at[3 * i + j], recv_sem=rsem.at[3 * i + j],
                    device_id=(px, py, c), device_id_type=MESH).wait_recv()
        for i in range(n):
            cps[4 * i].wait()
            for j in range(3):
                cps[4 * i + 1 + j].wait_send()

    outs = [jax.ShapeDtypeStruct((4,) + v.shape, v.dtype) for v in xs]
    return _comm_call(body, name, xs, outs, 3 * n)


def _to_scan_k(x, bsz, t_len):
    h = x.shape[1] // RWKV_HEAD
    y = jnp.broadcast_to(x.reshape(1, bsz, t_len, h, RWKV_HEAD), (2, bsz, t_len, h, RWKV_HEAD))
    return y.transpose(2, 4, 0, 1, 3).reshape(t_len, RWKV_HEAD, 2 * bsz * h)


def _to_scan_v(x, bsz, t_len):
    h = x.shape[1] // RWKV_HEAD
    y = x.reshape(bsz, t_len, h, 2, RWKV_HEAD // 2).transpose(1, 4, 3, 0, 2)
    return y.reshape(t_len, RWKV_HEAD // 2, 2 * bsz * h)


def _from_scan_k(x, bsz, t_len):
    h = x.shape[2] // (2 * bsz)
    y = x[:, :, :bsz * h].reshape(t_len, RWKV_HEAD, bsz, h).transpose(2, 0, 3, 1)
    return y.reshape(bsz * t_len, h * RWKV_HEAD)


def _from_scan_v(x, bsz, t_len):
    h = x.shape[2] // (2 * bsz)
    y = x.reshape(t_len, RWKV_HEAD // 2, 2, bsz, h).transpose(3, 0, 4, 2, 1)
    return y.reshape(bsz * t_len, h * RWKV_HEAD)


def _pad_rows(x, rows):
    return jnp.pad(x, ((0, rows - x.shape[0]), (0, 0)))


def _pad_cols(x, cols):
    return jnp.pad(x, ((0, 0), (0, cols - x.shape[1])))


def _cols_from_shards(g4):
    _, r, cs = g4.shape
    return g4.transpose(1, 0, 2).reshape(r, 4 * cs)


def _cols_to_shards(g):
    r, cols = g.shape
    return g.reshape(r, 4, cols // 4).transpose(1, 0, 2)


def kernel(x, norm_mix_g, w_in, mu_shift, rwkv_w0, rwkv_w2, rwkv_a0, rwkv_a2, rwkv_g2, rwkv_k_k, rwkv_k_a, rwkv_r_k, rwkv_ln_g, rwkv_ln_b, conv_w, conv_b, lru_wr, lru_br, lru_wi, lru_bi, lru_lambda, lru_norm_g, w_out, norm_ffn_g, ffn_w_gate, ffn_w_up, ffn_w_down, norm_final_g, loss_target, m_norm_mix_g, m_w_in, m_mu_shift, m_rwkv_w0, m_rwkv_w2, m_rwkv_a0, m_rwkv_a2, m_rwkv_g2, m_rwkv_k_k, m_rwkv_k_a, m_rwkv_r_k, m_rwkv_ln_g, m_rwkv_ln_b, m_conv_w, m_conv_b, m_lru_wr, m_lru_br, m_lru_wi, m_lru_bi, m_lru_lambda, m_lru_norm_g, m_w_out, m_norm_ffn_g, m_ffn_w_gate, m_ffn_w_up, m_ffn_w_down, m_norm_final_g, v_norm_mix_g, v_w_in, v_mu_shift, v_rwkv_w0, v_rwkv_w2, v_rwkv_a0, v_rwkv_a2, v_rwkv_g2, v_rwkv_k_k, v_rwkv_k_a, v_rwkv_r_k, v_rwkv_ln_g, v_rwkv_ln_b, v_conv_w, v_conv_b, v_lru_wr, v_lru_br, v_lru_wi, v_lru_bi, v_lru_lambda, v_lru_norm_g, v_w_out, v_norm_ffn_g, v_ffn_w_gate, v_ffn_w_up, v_ffn_w_down, v_norm_final_g):
    names = ['norm_mix_g', 'w_in', 'mu_shift', 'rwkv_w0', 'rwkv_w2', 'rwkv_a0', 'rwkv_a2', 'rwkv_g2', 'rwkv_k_k',
             'rwkv_k_a', 'rwkv_r_k', 'rwkv_ln_g', 'rwkv_ln_b', 'conv_w', 'conv_b', 'lru_wr', 'lru_br', 'lru_wi',
             'lru_bi', 'lru_lambda', 'lru_norm_g', 'w_out', 'norm_ffn_g', 'ffn_w_gate', 'ffn_w_up', 'ffn_w_down',
             'norm_final_g']
    env = locals()
    wts = {k: env[k] for k in names}
    mom_m = {k: env["m_" + k] for k in names}
    mom_v = {k: env["v_" + k] for k in names}

    bsz, t_len, d = x.shape
    n = bsz * t_len
    w = rwkv_w0.shape[1]
    lw = lru_br.shape[1]
    dl, al, gl = rwkv_w2.shape[1], rwkv_a2.shape[1], rwkv_g2.shape[1]
    dlp, alp, glp = _ceil_to(dl, LANES), _ceil_to(al, LANES), _ceil_to(gl, LANES)
    lp = dlp + alp + glp
    rc = 3 * w + dl + al + gl
    chip = 2 * lax.axis_index("x") + lax.axis_index("y")

    big = ['w_in', 'w_out', 'ffn_w_gate', 'ffn_w_up', 'ffn_w_down']
    small_sh = ['rwkv_w2', 'rwkv_a2', 'rwkv_g2', 'conv_w']

    def halves(a2d):
        return a2d.reshape(2, a2d.shape[0] // 2, a2d.shape[1])

    col_sharded = ('w_in', 'ffn_w_gate', 'ffn_w_up')

    def work(k, t):
        return jnp.swapaxes(t[0], 0, 1) if k in col_sharded else t[0]

    def unwork(k, t2):
        return (jnp.swapaxes(t2, 0, 1) if k in col_sharded else t2)[None]

    def rows_of(g):
        return g.reshape(g.shape[0] * g.shape[1] * g.shape[2], g.shape[3])

    send = [halves(work('w_in', w_in).astype(BF16))] + [halves(wts[k][0]) for k in small_sh]
    got = _all_gather_chips(send, "gather_w_in")
    later = ['w_out', 'ffn_w_gate', 'ffn_w_up', 'ffn_w_down']
    send_later = [halves(work(k, wts[k]).astype(BF16)) for k in later]
    full = {}
    for k, g in zip(small_sh, got[1:]):
        full[k] = _cols_from_shards(g.reshape(4, g.shape[1] * g.shape[2], g.shape[3]))
    wi_t = rows_of(got[0])
    w_rkv = wi_t[:3 * w]
    w_lru = wi_t[rc:]
    o = 3 * w
    w_lora = jnp.concatenate([_pad_rows(wi_t[o:o + dl], dlp), _pad_rows(wi_t[o + dl:o + dl + al], alp),
                              _pad_rows(wi_t[o + dl + al:rc], glp)], axis=0)
    mu = mu_shift
    prm_r = dict(
        mu_rkv=mu[:, :3 * w],
        mu_lora=jnp.concatenate([_pad_cols(mu[:, o:o + dl], dlp), _pad_cols(mu[:, o + dl:o + dl + al], alp),
                                 _pad_cols(mu[:, o + dl + al:rc], glp)], axis=1),
        w0=rwkv_w0, a0=rwkv_a0, k_k=rwkv_k_k, k_a=rwkv_k_a,
        w2=_pad_rows(full['rwkv_w2'], dlp).astype(BF16), a2=_pad_rows(full['rwkv_a2'], alp).astype(BF16),
        g2=_pad_rows(full['rwkv_g2'], glp).astype(BF16))
    ln_g, ln_b, r_k = rwkv_ln_g, rwkv_ln_b, rwkv_r_k.reshape(1, w)
    prm_l = dict(conv_w=full['conv_w'], conv_b=conv_b, wr=lru_wr[0].astype(BF16), br=lru_br,
                 wi=lru_wi[0].astype(BF16), bi=lru_bi, lam=lru_lambda, norm_g=lru_norm_g)
    g_final = norm_final_g.reshape(1, d)

    x2 = x.reshape(n, d)
    u1 = _rmsnorm_fwd(x2, norm_mix_g, "norm_mix")
    p_rkv = _mm(u1, w_rkv, name="in_rkv", tb=True)
    p_lru = _mm(u1, w_lru, name="in_lru", tb=True)
    p_lora = _mm(u1, w_lora, name="in_lora", tb=True)
    r_t, dec_t, k_t, v_t, na_t, nb_t, g_t = _rwkv_prep_fwd(p_rkv, p_lora, prm_r, t_len, "rwkv_prep")
    sk = [_to_scan_k(a, bsz, t_len) for a in (r_t, dec_t, k_t, na_t, nb_t)]
    sv = _to_scan_v(v_t, bsz, t_len)
    y_s, states, sa_s, got_wo, got_wg =_rwkv_scan_fwd(*sk, sv, name="rwkv_scan", comm=("gather", send_later[:2]))
    wo, wg = rows_of(got_wo), rows_of(got_wg)
    y_t = _from_scan_v(y_s, bsz, t_len)
    y_a = _rwkv_post_fwd(y_t, r_t, k_t, v_t, g_t, ln_g, ln_b, r_k, "rwkv_post")
    y_b, h_lru = _lru_fwd(p_lru, prm_l, t_len, "lru_fwd")
    h1 = _mm(y_a, wo[:w], name="out_a", res=x2)
    h1 = _mm(y_b, wo[w:], name="out_b", res=h1)
    u2 = _rmsnorm_fwd(h1, norm_ffn_g, "norm_ffn")
    ffc = (1024, 256, 4096)
    gate, got_wu = _mm(u2, wg, name="ffn_gate", tb=True, caps=ffc, comm=("gather", send_later[2:3]))
    wu = rows_of(got_wu)
    up, got_wd = _mm(u2, wu, name="ffn_up", tb=True, caps=ffc, comm=("gather", send_later[3:4]))
    wd = rows_of(got_wd)
    act = _swiglu_fwd(gate, up, "swiglu")
    h2 = _mm(act, wd, name="ffn_down", res=h1, caps=(512, 256, 11008))

    dh2, dh2b, g_norm_final, loss_vec = _loss_head(h2, g_final, loss_target.reshape(n, d), "loss_head")
    loss = lax.psum(loss_vec[0, 0], ("x", "y", "c"))
    dact = _mm(dh2b, wd, name="d_act", tb=True, caps=(1024, 256, 4096))
    dgate, dup = _swiglu_bwd(gate, up, dact, "swiglu_bwd")
    shards = lambda g: g.reshape(4, 2, g.shape[0] // 8, g.shape[1])
    dwc = dict(ta=True, out_dtype=BF16, n_outer=True, caps=(256, 1024, 4096))
    gw_down = _mm(act, dh2b, name="dw_down", **dwc)
    gw_gate = _mm(dgate, u2, name="dw_gate", **dwc)
    gw_up = _mm(dup, u2, name="dw_up", **dwc)
    g4f = [shards(g) for g in (gw_gate, gw_up, gw_down)]
    du2, *sib_f = _mm(dgate, wg, name="du2_gate", caps=(512, 256, 11008), comm=("swap_half", g4f))
    du2 = _mm(dup, wu, name="du2_up", res=du2, caps=(512, 256, 11008))
    dh1, dh1b, g_norm_ffn = _rmsnorm_bwd(du2, h1, norm_ffn_g, dh2, "norm_ffn_bwd")
    dcat = _mm(dh1b, wo, name="d_cat", tb=True)
    gw_out = jnp.concatenate([_mm(y_a, dh1b, name="dw_out_a", ta=True, out_dtype=BF16),
                              _mm(y_b, dh1b, name="dw_out_b", ta=True, out_dtype=BF16)], axis=0)
    g4o = [shards(gw_out)]
    sib_o = _sibling_swap(g4o, True, "grad_sibling_out")
    pair_a = [_pair_sum(a4, s, "grad_pair_sum_%d" % i) for i, (a4, s) in enumerate(zip(g4o + g4f, list(sib_o) + list(sib_f)))]
    (dp_lru, g_conv_w, g_conv_b, g_wr, g_br, g_wi, g_bi, g_lam, g_lng) = _lru_bwd(
        p_lru, h_lru, dcat, prm_l, t_len, "lru_bwd")
    dy_t, dr_p, dk_p, dv_p, dg_t, g_ln_g, g_ln_b, g_r_k = _rwkv_post_bwd(
        y_t, r_t, k_t, v_t, g_t, ln_g, ln_b, r_k, dcat, "rwkv_post_bwd")
    dr_s, dw_s, dk_s, da_s, db_s, dv_s, *parts_a = _rwkv_scan_bwd(
        *sk, sv, _to_scan_v(dy_t, bsz, t_len), states, sa_s, name="rwkv_scan_bwd", comm=("peer", pair_a))
    grads = [_from_scan_k(dr_s, bsz, t_len), dr_p, _from_scan_k(dw_s, bsz, t_len), _from_scan_k(dk_s, bsz, t_len),
             dk_p, _from_scan_v(dv_s, bsz, t_len), dv_p, _from_scan_k(da_s, bsz, t_len),
             _from_scan_k(db_s, bsz, t_len), dg_t]
    (dq_r, dq_l, g_mu_r, g_mu_l, g_w0, g_a0, g_kk, g_ka, g_w2, g_a2, g_g2) = _rwkv_prep_bwd(
        p_rkv, p_lora, prm_r, grads, t_len, "rwkv_prep_bwd")
    dp_rkv = _shift_combine(dq_r, prm_r["mu_rkv"], t_len, "shift_bwd_rkv")
    dp_lora = _shift_combine(dq_l, prm_r["mu_lora"], t_len, "shift_bwd_lora")
    gi_rkv = _mm(dp_rkv, u1, name="dw_in_rkv", ta=True, out_dtype=BF16)
    gi_lru = _mm(dp_lru, u1, name="dw_in_lru", ta=True, out_dtype=BF16)
    gi_lora = _mm(dp_lora, u1, name="dw_in_lora", ta=True, out_dtype=BF16)
    gw_in = jnp.concatenate([gi_rkv, gi_lora[:dl], gi_lora[dlp:dlp + al], gi_lora[dlp + alp:dlp + alp + gl], gi_lru],
                            axis=0)
    g4b = [shards(gw_in)]
    sib_b = _sibling_swap(g4b, True, "grad_sibling_in")
    pair_b = [_pair_sum(g4b[0], sib_b[0], "grad_pair_sum_in")]
    du1, *parts_b = _mm(dp_rkv, w_rkv, name="du1_rkv", comm=("peer", pair_b))
    du1 = _mm(dp_lru, w_lru, name="du1_lru", res=du1)
    du1 = _mm(dp_lora, w_lora, name="du1_lora", res=du1)
    gx, _, g_norm_mix = _rmsnorm_bwd(du1, x2, norm_mix_g, dh1, "norm_mix_bwd")

    pair, parts = pair_b + pair_a, list(parts_b) + list(parts_a)
    mine = [_peer_sum(own4, p3, "grad_chip_sum_%d" % i) for i, (own4, p3) in enumerate(zip(pair, parts))]
    theirs = _sibling_swap(mine, False, "grad_share")

    g_mu = jnp.concatenate([g_mu_r, g_mu_l[:, :dl], g_mu_l[:, dlp:dlp + al], g_mu_l[:, dlp + alp:dlp + alp + gl]],
                           axis=1)
    small = dict(norm_mix_g=g_norm_mix, mu_shift=g_mu, rwkv_w0=g_w0, rwkv_w2=g_w2[:dl], rwkv_a0=g_a0,
                 rwkv_a2=g_a2[:al], rwkv_g2=g_g2[:gl], rwkv_k_k=g_kk, rwkv_k_a=g_ka, rwkv_r_k=g_r_k,
                 rwkv_ln_g=g_ln_g, rwkv_ln_b=g_ln_b, conv_w=g_conv_w, conv_b=g_conv_b, lru_wr=g_wr, lru_br=g_br,
                 lru_wi=g_wi, lru_bi=g_bi, lru_lambda=g_lam, lru_norm_g=g_lng, norm_ffn_g=g_norm_ffn,
                 norm_final_g=g_norm_final)
    small_names = list(small)
    sizes = [small[k].size for k in small_names]
    total = sum(sizes)
    padded = _ceil_to(total, 512 * LANES)

    def pack(arrs):
        flat = jnp.concatenate([a.reshape(-1) for a in arrs] + [jnp.zeros((padded - sum(a.size for a in arrs),), F32)])
        return flat.reshape(padded // LANES, LANES)

    packed = pack([small[k] for k in small_names])
    other = _sibling_swap([packed], False, "small_sibling")[0]
    chip_sum = _add2(packed, other, "small_pair_sum")
    all4 = _chip_broadcast([chip_sum], "small_chips")[0]
    red = _chip_sum(all4, "small_chip_sum").reshape(-1)
    small_g = {}
    off = 0
    for k, sz in zip(small_names, sizes):
        full_g = red[off:off + sz].reshape(small[k].shape)
        off += sz
        if k in small_sh:
            cs = full_g.shape[1] // 4
            full_g = lax.dynamic_slice_in_dim(full_g, chip * cs, cs, axis=1)
        small_g[k] = full_g.reshape(wts[k].shape)

    grad_w, delta_w, new_m, new_v = {}, {}, {}, {}
    for k, g_mine, g_theirs in zip(big, mine, theirs):
        res = _adamw_halves(g_mine, g_theirs, work(k, wts[k]), work(k, mom_m[k]), work(k, mom_v[k]), "adamw_" + k)
        grad_w[k], delta_w[k], new_m[k], new_v[k] = (unwork(k, t.reshape(2 * t.shape[1], t.shape[2])) for t in res)
    lsizes = [small_g[k].size for k in small_names]
    lpad = _ceil_to(sum(lsizes), 128 * LANES)

    def lpack(tree):
        arrs = [tree[k].reshape(-1) for k in small_names]
        flat = jnp.concatenate(arrs + [jnp.zeros((lpad - sum(lsizes),), F32)])
        return flat.reshape(lpad // LANES, LANES)

    dlt, mn, vn = _adamw(lpack(small_g), lpack(wts), lpack(mom_m), lpack(mom_v), "adamw_small")
    off = 0
    for k, sz in zip(small_names, lsizes):
        shp = wts[k].shape
        grad_w[k] = small_g[k]
        delta_w[k] = dlt.reshape(-1)[off:off + sz].reshape(shp)
        new_m[k] = mn.reshape(-1)[off:off + sz].reshape(shp)
        new_v[k] = vn.reshape(-1)[off:off + sz].reshape(shp)
        off += sz

    return (loss, gx.reshape(bsz, t_len, d), *[grad_w[k] for k in names], *[delta_w[k] for k in names],
            *[new_m[k] for k in names], *[new_v[k] for k in names])
```

```python
import jax
import jax.numpy as jnp
from jax import lax
from jax.experimental import pallas as pl
from jax.experimental.pallas import tpu as pltpu

F32 = jnp.float32
BF16 = jnp.bfloat16
MESH = pl.DeviceIdType.MESH
_call = pl.pallas_call

V7X_VMEM_LIMIT = 56 * 1024 * 1024
LANES = 128
SUBLANES = 8

RWKV_HEAD = 64
LRU_BLOCK_W = 128
CONV_WIDTH = 4
LRU_C = 8.0
NORM_EPS = 1e-6
GN_EPS = 64e-5
KK_EPS = 1e-24
SCAN_CHUNK = 16

ADAM_LR = 0.001
ADAM_B1 = 0.9
ADAM_B2 = 0.999
ADAM_EPS = 1e-08
ADAM_WD = 0.01
ADAM_STEP = 10
_BC1 = 1.0 - ADAM_B1 ** ADAM_STEP
_BC2 = 1.0 - ADAM_B2 ** ADAM_STEP


def _cp(*sem):
    return pltpu.CompilerParams(dimension_semantics=tuple(sem), vmem_limit_bytes=V7X_VMEM_LIMIT)


def _tile(n, cap, unit=LANES):
    if n <= cap:
        return n
    best = None
    d = unit
    while d <= cap:
        if n % d == 0:
            best = d
        d += unit
    return n if best is None else best


def _ceil_to(n, m):
    return -(-n // m) * m


ELEMENTWISE_BLOCK_BYTES = 3 * 512 * 1024


def _col_tile(rows, cols):
    cap = max(LANES, ELEMENTWISE_BLOCK_BYTES // (4 * rows) // LANES * LANES)
    return _tile(cols, cap)


def _sig(x):
    return 1.0 / (1.0 + jnp.exp(-x))


def _log1p(x):
    return jnp.where(x < 0.01, x * (1.0 - x * (0.5 - x * (1.0 / 3.0))), jnp.log(1.0 + x))


def _softplus(x):
    return jnp.maximum(x, 0.0) + _log1p(jnp.exp(-jnp.abs(x)))


def _neg_expm1(x):
    small = -x * (1.0 + x * (0.5 + x * (1.0 / 6.0)))
    return jnp.where(x > -0.01, small, 1.0 - jnp.exp(x))


_GELU_K = 0.7978845608028654
_GELU_C = 0.044715


def _gelu_parts(x):
    th = jnp.tanh(_GELU_K * (x + _GELU_C * x * x * x))
    return 0.5 * x * (1.0 + th), th


def _gelu_grad(x, th):
    return 0.5 * (1.0 + th) + 0.5 * x * (1.0 - th * th) * _GELU_K * (1.0 + 3.0 * _GELU_C * x * x)


def _shift_down(x, prev8, j):
    tb = x.shape[0]
    xr = pltpu.roll(x, j, 0)
    pr = pltpu.roll(prev8, j, 0)
    row = lax.broadcasted_iota(jnp.int32, prev8.shape, 0)
    first = jnp.where(row < j, pr, xr[0:SUBLANES])
    if tb == SUBLANES:
        return first
    return jnp.concatenate([first, xr[SUBLANES:]], axis=0)


def _shift_up(x, next8, j):
    tb = x.shape[0]
    xr = pltpu.roll(x, tb - j, 0)
    nr = pltpu.roll(next8, SUBLANES - j, 0)
    row = lax.broadcasted_iota(jnp.int32, next8.shape, 0)
    last = jnp.where(row >= SUBLANES - j, nr, xr[tb - SUBLANES:])
    if tb == SUBLANES:
        return last
    return jnp.concatenate([xr[:tb - SUBLANES], last], axis=0)


def _head_mats(width, heads_pad):
    e = (lax.broadcasted_iota(jnp.int32, (width, heads_pad), 0) // RWKV_HEAD
         == lax.broadcasted_iota(jnp.int32, (width, heads_pad), 1)).astype(BF16)
    et = (lax.broadcasted_iota(jnp.int32, (heads_pad, width), 1) // RWKV_HEAD
          == lax.broadcasted_iota(jnp.int32, (heads_pad, width), 0)).astype(BF16)
    return e, et


def _dot_exact01(x, m):
    hi = x.astype(BF16)
    r1 = x - hi.astype(F32)
    mid = r1.astype(BF16)
    lo = (r1 - mid.astype(F32)).astype(BF16)
    return (jnp.dot(lo, m, preferred_element_type=F32) + jnp.dot(mid, m, preferred_element_type=F32)
            + jnp.dot(hi, m, preferred_element_type=F32))


def _headsum(x, e, et):
    return _dot_exact01(_dot_exact01(x, e), et)


def _dot(a, b):
    return jnp.dot(a.astype(BF16), b.astype(BF16), preferred_element_type=F32)


def _dot_tn(a, b):
    return lax.dot_general(a.astype(BF16), b.astype(BF16), (((0,), (0,)), ((), ())), preferred_element_type=F32)


def _dot_nt(a, b):
    return lax.dot_general(a.astype(BF16), b.astype(BF16), (((1,), (1,)), ((), ())), preferred_element_type=F32)


def _mm(a, b, *, name, ta=False, tb=False, out_dtype=F32, res=None, n_outer=False, caps=(1024, 512, 4096),
        comm=None, epi=None):
    m = a.shape[1] if ta else a.shape[0]
    kd = a.shape[0] if ta else a.shape[1]
    n = b.shape[0] if tb else b.shape[1]
    assert kd == (b.shape[1] if tb else b.shape[0])
    tm, tn, tk = _tile(m, caps[0]), _tile(n, caps[1]), _tile(kd, caps[2])
    gm, gn, gk = m // tm, n // tn, kd // tk
    dims = (((0 if ta else 1,), (1 if tb else 0,)), ((), ()))
    grid = (gn, gm, gk) if n_outer else (gm, gn, gk)
    cx, cx_specs, c_outs, c_sems, c_start, c_finish = _carried(comm)
    epi_fn, epi_ins, out_dtypes = epi if epi is not None else (None, [], [out_dtype])
    n_epi, n_out = len(epi_ins), len(out_dtypes)
    nx, n_in = len(cx), (3 if res is not None else 2) + n_epi

    def ij(g0, g1):
        return (g1, g0) if n_outer else (g0, g1)

    def a_map(g0, g1, k):
        i, _ = ij(g0, g1)
        return (k, i) if ta else (i, k)

    def b_map(g0, g1, k):
        _, j = ij(g0, g1)
        return (j, k) if tb else (k, j)

    def o_map(g0, g1, k):
        return ij(g0, g1)

    has_res = res is not None

    def body(*refs):
        a_ref, b_ref = refs[0], refs[1]
        res_ref = refs[2] if has_res else None
        epi_refs = refs[n_in - n_epi:n_in]
        c_ins = refs[n_in:n_in + nx]
        o_refs = refs[n_in + nx:n_in + nx + n_out]
        c_out_refs = refs[n_in + nx + n_out:n_in + nx + n_out + len(c_outs)]
        acc_ref = refs[n_in + nx + n_out + len(c_outs)] if gk > 1 else None
        steps = [pl.program_id(ax) for ax in range(3)]
        if nx:
            @pl.when(jnp.logical_and(jnp.logical_and(steps[0] == 0, steps[1] == 0), steps[2] == 0))
            def _():
                c_start(c_ins, c_out_refs, refs[-2], refs[-1])

        prod = lax.dot_general(a_ref[...], b_ref[...], dims, preferred_element_type=F32)

        def finish(acc):
            if has_res:
                acc = acc + res_ref[...]
            tiles = [acc] if epi_fn is None else epi_fn(acc, *[e_ref[...] for e_ref in epi_refs])
            for o_ref, tile, dt in zip(o_refs, tiles, out_dtypes):
                o_ref[...] = tile.astype(dt)

        if gk == 1:
            finish(prod)
        else:
            k = steps[2]

            @pl.when(k == 0)
            def _():
                acc_ref[...] = prod

            @pl.when(k > 0)
            def _():
                acc_ref[...] += prod

            @pl.when(k == gk - 1)
            def _():
                finish(acc_ref[...])

        if nx:
            @pl.when(jnp.logical_and(jnp.logical_and(steps[0] == grid[0] - 1, steps[1] == grid[1] - 1),
                                     steps[2] == grid[2] - 1))
            def _():
                c_finish(c_ins, c_out_refs, refs[-2], refs[-1])

    in_specs = [pl.BlockSpec((tk, tm) if ta else (tm, tk), a_map),
                pl.BlockSpec((tn, tk) if tb else (tk, tn), b_map)]
    args = [a, b]
    for extra in ([res] if has_res else []) + list(epi_ins):
        in_specs.append(pl.BlockSpec((tm, tn), o_map))
        args.append(extra)
    out = _call(
        body, name=name, grid=grid, in_specs=in_specs + cx_specs,
        out_specs=[pl.BlockSpec((tm, tn), o_map)] * n_out + [_HBM] * len(c_outs),
        out_shape=[jax.ShapeDtypeStruct((m, n), dt) for dt in out_dtypes] + c_outs,
        scratch_shapes=([pltpu.VMEM((tm, tn), F32)] if gk > 1 else []) + c_sems,
        compiler_params=_cp(*(("arbitrary",) * 3 if nx else ("parallel", "parallel", "arbitrary"))),
    )(*args, *cx)
    return out if (nx or epi is not None) else out[0]


def _rmsnorm_fwd(x, g, name):
    n, d = x.shape
    tb = _tile(n, 256, SUBLANES)

    def body(x_ref, g_ref, u_ref):
        xv = x_ref[...]
        rstd = lax.rsqrt(jnp.mean(xv * xv, axis=-1, keepdims=True) + NORM_EPS)
        u_ref[...] = (xv * rstd * g_ref[...]).astype(BF16)

    row = pl.BlockSpec((tb, d), lambda i: (i, 0))
    vec = pl.BlockSpec((1, d), lambda i: (0, 0))
    return _call(body, name=name, grid=(n // tb,), in_specs=[row, vec], out_specs=row,
                 out_shape=jax.ShapeDtypeStruct((n, d), BF16), compiler_params=_cp("parallel"))(x, g)


def _rmsnorm_bwd(du, x, g, dres, name):
    n, d = x.shape
    tb = _tile(n, 256, SUBLANES)

    def body(du_ref, x_ref, g_ref, dres_ref, dx_ref, dxb_ref, dg_ref):
        xv = x_ref[...]
        rstd = lax.rsqrt(jnp.mean(xv * xv, axis=-1, keepdims=True) + NORM_EPS)
        xh = xv * rstd
        duv = du_ref[...]
        t = duv * g_ref[...]
        dx = dres_ref[...] + rstd * (t - xh * jnp.mean(t * xh, axis=-1, keepdims=True))
        dx_ref[...] = dx
        dxb_ref[...] = dx.astype(BF16)

        @pl.when(pl.program_id(0) == 0)
        def _():
            dg_ref[...] = jnp.zeros_like(dg_ref)

        dg_ref[...] += jnp.sum(duv * xh, axis=0, keepdims=True)

    row = pl.BlockSpec((tb, d), lambda i: (i, 0))
    vec = pl.BlockSpec((1, d), lambda i: (0, 0))
    return _call(body, name=name, grid=(n // tb,), in_specs=[row, row, vec, row], out_specs=[row, row, vec],
                 out_shape=[jax.ShapeDtypeStruct((n, d), F32), jax.ShapeDtypeStruct((n, d), BF16),
                            jax.ShapeDtypeStruct((1, d), F32)],
                 compiler_params=_cp("arbitrary"))(du, x, g, dres)


def _loss_head(h, g, target, name):
    n, d = h.shape
    tb = _tile(n, 256, SUBLANES)

    def body(h_ref, g_ref, t_ref, dh_ref, dhb_ref, dg_ref, loss_ref):
        hv = h_ref[...]
        gv = g_ref[...]
        rstd = lax.rsqrt(jnp.mean(hv * hv, axis=-1, keepdims=True) + NORM_EPS)
        hh = hv * rstd
        err = hh * gv - t_ref[...]
        dy = err * (1.0 / d)
        dhh = dy * gv
        dh = rstd * (dhh - hh * jnp.mean(dhh * hh, axis=-1, keepdims=True))
        dh_ref[...] = dh
        dhb_ref[...] = dh.astype(BF16)

        @pl.when(pl.program_id(0) == 0)
        def _():
            dg_ref[...] = jnp.zeros_like(dg_ref)
            loss_ref[...] = jnp.zeros_like(loss_ref)

        dg_ref[...] += jnp.sum(dy * hh, axis=0, keepdims=True)
        loss_ref[...] += jnp.sum(err * err) * (0.5 / d)

    row = pl.BlockSpec((tb, d), lambda i: (i, 0))
    vec = pl.BlockSpec((1, d), lambda i: (0, 0))
    lvec = pl.BlockSpec((1, LANES), lambda i: (0, 0))
    return _call(body, name=name, grid=(n // tb,), in_specs=[row, vec, row], out_specs=[row, row, vec, lvec],
                 out_shape=[jax.ShapeDtypeStruct((n, d), F32), jax.ShapeDtypeStruct((n, d), BF16),
                            jax.ShapeDtypeStruct((1, d), F32), jax.ShapeDtypeStruct((1, LANES), F32)],
                 compiler_params=_cp("arbitrary"))(h, g, target)


def _swiglu_tile(up, gate):
    return [up, gate * _sig(gate) * up]


def _swiglu_bwd_tile(dact, gate, up):
    s = _sig(gate)
    return [dact * up * s * (1.0 + gate * (1.0 - s)), dact * gate * s]


def _prep_common(prkv_ref, prkvp_ref, plo_ref, plop_ref, mur_ref, mul_ref, w0_ref, a0_ref, kk_ref, ka_ref,
                 w2_ref, a2_ref, g2_ref, seq_start, w, dlp, alp):
    z8r = jnp.zeros_like(prkvp_ref[...])
    z8l = jnp.zeros_like(plop_ref[...])
    prev_r = jnp.where(seq_start, z8r, prkvp_ref[...])
    prev_l = jnp.where(seq_start, z8l, plop_ref[...])
    p_r = prkv_ref[...]
    p_l = plo_ref[...]
    dif_r = _shift_down(p_r, prev_r, 1) - p_r
    dif_l = _shift_down(p_l, prev_l, 1) - p_l
    q_r = p_r + dif_r * mur_ref[...]
    q_l = p_l + dif_l * mul_ref[...]
    r, k, v = q_r[:, 0:w], q_r[:, w:2 * w], q_r[:, 2 * w:3 * w]
    wd, ad, gd = q_l[:, 0:dlp], q_l[:, dlp:dlp + alp], q_l[:, dlp + alp:]
    tw = jnp.tanh(wd)
    zw = w0_ref[...] + _dot(tw, w2_ref[...])
    wlog = -_softplus(-zw) - 0.5
    ew = jnp.exp(wlog)
    dec = jnp.exp(-ew)
    za = a0_ref[...] + _dot(ad, a2_ref[...])
    av = _sig(za)
    sg = _sig(gd)
    g = _dot(sg, g2_ref[...])
    return dict(dif_r=dif_r, dif_l=dif_l, r=r, k=k, v=v, ad=ad, tw=tw, zw=zw, ew=ew, dec=dec, av=av, sg=sg, g=g)


def _rwkv_prep_specs(n, tb, w, lp, t_len):
    nb8 = tb // SUBLANES
    row3 = pl.BlockSpec((tb, 3 * w), lambda i: (i, 0))
    prev3 = pl.BlockSpec((SUBLANES, 3 * w), lambda i: (jnp.maximum(i * nb8 - 1, 0), 0))
    rowl = pl.BlockSpec((tb, lp), lambda i: (i, 0))
    prevl = pl.BlockSpec((SUBLANES, lp), lambda i: (jnp.maximum(i * nb8 - 1, 0), 0))
    return row3, prev3, rowl, prevl


def _rwkv_prep_fwd(p_rkv, p_lora, prm, t_len, name):
    n, w3 = p_rkv.shape
    w = w3 // 3
    lp = p_lora.shape[1]
    dlp, alp = prm["w2"].shape[0], prm["a2"].shape[0]
    glp = lp - dlp - alp
    hp = max(w // RWKV_HEAD, LANES)
    tb = _tile(min(n, t_len), 128, SUBLANES)
    bps = t_len // tb

    def body(prkv_ref, prkvp_ref, plo_ref, plop_ref, mur_ref, mul_ref, w0_ref, a0_ref, kk_ref, ka_ref,
             w2_ref, a2_ref, g2_ref, r_o, dec_o, k_o, v_o, na_o, nb_o, g_o):
        seq_start = (pl.program_id(0) % bps) == 0
        f = _prep_common(prkv_ref, prkvp_ref, plo_ref, plop_ref, mur_ref, mul_ref, w0_ref, a0_ref, kk_ref, ka_ref,
                         w2_ref, a2_ref, g2_ref, seq_start, w, dlp, alp)
        e, et = _head_mats(w, hp)
        kk0 = f["k"] * kk_ref[...]
        inv = lax.rsqrt(jnp.maximum(_headsum(kk0 * kk0, e, et), KK_EPS))
        kk = kk0 * inv
        r_o[...] = f["r"]
        dec_o[...] = f["dec"]
        k_o[...] = f["k"] * (1.0 + (f["av"] - 1.0) * ka_ref[...])
        v_o[...] = f["v"]
        na_o[...] = -kk
        nb_o[...] = kk * f["av"]
        g_o[...] = f["g"]

    row3, prev3, rowl, prevl = _rwkv_prep_specs(n, tb, w, lp, t_len)
    c0 = lambda i: (0, 0)
    vec3 = pl.BlockSpec((1, 3 * w), c0)
    vecl = pl.BlockSpec((1, lp), c0)
    vec = pl.BlockSpec((1, w), c0)
    out = pl.BlockSpec((tb, w), lambda i: (i, 0))
    return _call(
        body, name=name, grid=(n // tb,),
        in_specs=[row3, prev3, rowl, prevl, vec3, vecl, vec, vec, vec, vec,
                  pl.BlockSpec((dlp, w), c0), pl.BlockSpec((alp, w), c0), pl.BlockSpec((glp, w), c0)],
        out_specs=[out] * 7, out_shape=[jax.ShapeDtypeStruct((n, w), F32)] * 7,
        compiler_params=_cp("parallel"),
    )(p_rkv, p_rkv, p_lora, p_lora, prm["mu_rkv"], prm["mu_lora"], prm["w0"], prm["a0"], prm["k_k"], prm["k_a"],
      prm["w2"], prm["a2"], prm["g2"])


def _rwkv_prep_bwd(p_rkv, p_lora, prm, grads, t_len, name):
    n, w3 = p_rkv.shape
    w = w3 // 3
    lp = p_lora.shape[1]
    dlp, alp = prm["w2"].shape[0], prm["a2"].shape[0]
    glp = lp - dlp - alp
    hp = max(w // RWKV_HEAD, LANES)
    tb = _tile(min(n, t_len), 64, SUBLANES)
    bps = t_len // tb

    def body(prkv_ref, prkvp_ref, plo_ref, plop_ref, mur_ref, mul_ref, w0_ref, a0_ref, kk_ref, ka_ref,
             w2_ref, a2_ref, g2_ref,
             drs_ref, drp_ref, ddec_ref, dks_ref, dkp_ref, dvs_ref, dvp_ref, dna_ref, dnb_ref, dg_ref,
             dqr_o, dql_o, dmur_o, dmul_o, dw0_o, da0_o, dkk_o, dka_o, dw2_o, da2_o, dg2_o):
        seq_start = (pl.program_id(0) % bps) == 0
        f = _prep_common(prkv_ref, prkvp_ref, plo_ref, plop_ref, mur_ref, mul_ref, w0_ref, a0_ref, kk_ref, ka_ref,
                         w2_ref, a2_ref, g2_ref, seq_start, w, dlp, alp)
        e, et = _head_mats(w, hp)
        k, av = f["k"], f["av"]
        k_k, k_a = kk_ref[...], ka_ref[...]
        kk0 = k * k_k
        n2 = _headsum(kk0 * kk0, e, et)
        inv = lax.rsqrt(jnp.maximum(n2, KK_EPS))
        kk = kk0 * inv
        dk2 = dks_ref[...] + dkp_ref[...]
        dnb = dnb_ref[...]
        dkk = dnb * av - dna_ref[...]
        dav = dnb * kk + dk2 * k * k_a
        dk = dk2 * (1.0 + (av - 1.0) * k_a)
        dka = dk2 * k * (av - 1.0)
        proj = jnp.where(n2 > KK_EPS, _headsum(dkk * kk, e, et), 0.0)
        dkk0 = inv * (dkk - kk * proj)
        dk = dk + dkk0 * k_k
        dkkp = dkk0 * k
        dgv = dg_ref[...]
        sg = f["sg"]
        dgd = _dot_nt(dgv, g2_ref[...]) * sg * (1.0 - sg)
        dza = dav * av * (1.0 - av)
        dad = _dot_nt(dza, a2_ref[...])
        dzw = ddec_ref[...] * f["dec"] * (-f["ew"]) * _sig(-f["zw"])
        tw = f["tw"]
        dwd = _dot_nt(dzw, w2_ref[...]) * (1.0 - tw * tw)
        dq_r = jnp.concatenate([drs_ref[...] + drp_ref[...], dk, dvs_ref[...] + dvp_ref[...]], axis=1)
        dq_l = jnp.concatenate([dwd, dad, dgd], axis=1)
        dqr_o[...] = dq_r
        dql_o[...] = dq_l

        @pl.when(pl.program_id(0) == 0)
        def _():
            for o in (dmur_o, dmul_o, dw0_o, da0_o, dkk_o, dka_o, dw2_o, da2_o, dg2_o):
                o[...] = jnp.zeros_like(o)

        def rsum(x):
            return jnp.sum(x, axis=0, keepdims=True)

        dmur_o[...] += rsum(dq_r * f["dif_r"])
        dmul_o[...] += rsum(dq_l * f["dif_l"])
        dw0_o[...] += rsum(dzw)
        da0_o[...] += rsum(dza)
        dkk_o[...] += rsum(dkkp)
        dka_o[...] += rsum(dka)
        dw2_o[...] += _dot_tn(tw, dzw)
        da2_o[...] += _dot_tn(f["ad"], dza)
        dg2_o[...] += _dot_tn(sg, dgv)

    row3, prev3, rowl, prevl = _rwkv_prep_specs(n, tb, w, lp, t_len)
    c0 = lambda i: (0, 0)
    vec3 = pl.BlockSpec((1, 3 * w), c0)
    vecl = pl.BlockSpec((1, lp), c0)
    vec = pl.BlockSpec((1, w), c0)
    blk = pl.BlockSpec((tb, w), lambda i: (i, 0))
    m2, ma, mg = pl.BlockSpec((dlp, w), c0), pl.BlockSpec((alp, w), c0), pl.BlockSpec((glp, w), c0)
    sds = jax.ShapeDtypeStruct
    return _call(
        body, name=name, grid=(n // tb,),
        in_specs=[row3, prev3, rowl, prevl, vec3, vecl, vec, vec, vec, vec, m2, ma, mg] + [blk] * 10,
        out_specs=[row3, rowl, vec3, vecl, vec, vec, vec, vec, m2, ma, mg],
        out_shape=[sds((n, 3 * w), F32), sds((n, lp), F32), sds((1, 3 * w), F32), sds((1, lp), F32),
                   sds((1, w), F32), sds((1, w), F32), sds((1, w), F32), sds((1, w), F32),
                   sds((dlp, w), F32), sds((alp, w), F32), sds((glp, w), F32)],
        compiler_params=_cp("arbitrary"),
    )(p_rkv, p_rkv, p_lora, p_lora, prm["mu_rkv"], prm["mu_lora"], prm["w0"], prm["a0"], prm["k_k"], prm["k_a"],
      prm["w2"], prm["a2"], prm["g2"], *grads)


def _shift_combine(dq, mu, t_len, name):
    n, c = dq.shape
    tb = _tile(min(n, t_len), 256, SUBLANES)
    bps = t_len // tb
    nb8 = tb // SUBLANES
    last8 = n // SUBLANES - 1

    def body(x_ref, nx_ref, mu_ref, o_ref):
        seq_end = (pl.program_id(0) % bps) == bps - 1
        nxt = jnp.where(seq_end, jnp.zeros_like(nx_ref[...]), nx_ref[...])
        x = x_ref[...]
        muv = mu_ref[...]
        o_ref[...] = ((1.0 - muv) * x + muv * _shift_up(x, nxt, 1)).astype(BF16)

    row = pl.BlockSpec((tb, c), lambda i: (i, 0))
    nxt = pl.BlockSpec((SUBLANES, c), lambda i: (jnp.minimum((i + 1) * nb8, last8), 0))
    vec = pl.BlockSpec((1, c), lambda i: (0, 0))
    return _call(body, name=name, grid=(n // tb,), in_specs=[row, nxt, vec], out_specs=row,
                 out_shape=jax.ShapeDtypeStruct((n, c), BF16), compiler_params=_cp("parallel"))(dq, dq, mu)


def _scan_step(s_i, a_t, w_t, b_t, k_t, v_i):
    sa = jnp.sum(s_i * a_t, axis=0, keepdims=True)
    return s_i * w_t + sa * b_t + v_i * k_t, sa


def _carried(comm):
    if comm is None:
        return [], [], [], [], None, None
    outs, n_sems, start, finish = _comm_plan(*comm)
    xs = list(comm[1])
    sems = [pltpu.SemaphoreType.DMA((n_sems,)), pltpu.SemaphoreType.DMA((n_sems,))]
    return xs, [_HBM] * len(xs), outs, sems, start, finish


def _rwkv_scan_fwd(r, w, k, a, b, v, name, comm=None):
    t_len, kd, ln = r.shape
    vh = v.shape[1]
    tc = SCAN_CHUNK
    nc = t_len // tc
    cx, cx_specs, c_outs, c_sems, c_start, c_finish = _carried(comm)
    nx = len(cx)

    def body(r_ref, w_ref, k_ref, a_ref, b_ref, v_ref, *rest):
        c_ins, (y_ref, st_ref, sa_ref), c_out_refs = rest[:nx], rest[nx:nx + 3], rest[nx + 3:nx + 3 + len(c_outs)]
        s_ref = rest[nx + 3 + len(c_outs)]

        @pl.when(pl.program_id(0) == 0)
        def _():
            s_ref[...] = jnp.zeros_like(s_ref)
            if nx:
                c_start(c_ins, c_out_refs, rest[-2], rest[-1])

        st_ref[0, 0] = s_ref[...]

        def step(t, carry):
            a_t, w_t, b_t, k_t, r_t = a_ref[t], w_ref[t], b_ref[t], k_ref[t], r_ref[t]
            for i in range(vh):
                s_new, sa = _scan_step(st_ref[0, t, i], a_t, w_t, b_t, k_t, v_ref[t, pl.ds(i, 1), :])
                st_ref[0, t + 1, i] = s_new
                sa_ref[t, pl.ds(i, 1), :] = sa
                y_ref[t, pl.ds(i, 1), :] = jnp.sum(s_new * r_t, axis=0, keepdims=True)
            return carry

        lax.fori_loop(0, tc, step, 0)
        s_ref[...] = st_ref[0, tc]

        if nx:
            @pl.when(pl.program_id(0) == nc - 1)
            def _():
                c_finish(c_ins, c_out_refs, rest[-2], rest[-1])

    kblk = pl.BlockSpec((tc, kd, ln), lambda c: (c, 0, 0))
    vblk = pl.BlockSpec((tc, vh, ln), lambda c: (c, 0, 0))
    vsd = jax.ShapeDtypeStruct((t_len, vh, ln), F32)
    return _call(
        body, name=name, grid=(nc,), in_specs=[kblk] * 5 + [vblk] + cx_specs,
        out_specs=[vblk, pl.BlockSpec((1, tc + 1, vh, kd, ln), lambda c: (c, 0, 0, 0, 0)), vblk] + [_HBM] * len(c_outs),
        out_shape=[vsd, jax.ShapeDtypeStruct((nc, tc + 1, vh, kd, ln), F32), vsd] + c_outs,
        scratch_shapes=[pltpu.VMEM((vh, kd, ln), F32)] + c_sems,
        compiler_params=_cp("arbitrary"),
    )(r, w, k, a, b, v, *cx)


def _rwkv_scan_bwd(r, w, k, a, b, v, dy, states, sa, name, comm=None):
    t_len, kd, ln = r.shape
    vh = v.shape[1]
    tc = SCAN_CHUNK
    nc = t_len // tc
    half = ln // 2
    cx, cx_specs, c_outs, c_sems, c_start, c_finish = _carried(comm)
    nx = len(cx)

    def body(r_ref, w_ref, k_ref, a_ref, b_ref, v_ref, dy_ref, st_ref, sa_ref, *rest):
        c_ins = rest[:nx]
        dr_o, dw_o, dk_o, da_o, db_o, dv_o = rest[nx:nx + 6]
        c_out_refs = rest[nx + 6:nx + 6 + len(c_outs)]
        ds_ref = rest[nx + 6 + len(c_outs)]

        @pl.when(pl.program_id(0) == 0)
        def _():
            ds_ref[...] = jnp.zeros_like(ds_ref)
            if nx:
                c_start(c_ins, c_out_refs, rest[-2], rest[-1])

        def bwd(tt, carry):
            t = tc - 1 - tt
            a_t, w_t, b_t, k_t, r_t = a_ref[t], w_ref[t], b_ref[t], k_ref[t], r_ref[t]
            z = jnp.zeros((kd, ln), F32)
            dr, dw, dk, da, db = z, z, z, z, z
            for i in range(vh):
                dy_i = dy_ref[t, pl.ds(i, 1), :]
                s_t = st_ref[0, t + 1, i]
                s_p = st_ref[0, t, i]
                d = ds_ref[i] + dy_i * r_t
                dr = dr + s_t * dy_i
                dv_o[t, pl.ds(i, 1), :] = jnp.sum(d * k_t, axis=0, keepdims=True)
                dk = dk + d * v_ref[t, pl.ds(i, 1), :]
                dsa = jnp.sum(d * b_t, axis=0, keepdims=True)
                db = db + d * sa_ref[t, pl.ds(i, 1), :]
                dw = dw + d * s_p
                da = da + s_p * dsa
                ds_ref[i] = d * w_t + dsa * a_t

            def both(x):
                return x + pltpu.roll(x, half, 1)

            dr_o[t] = both(dr)
            dw_o[t] = both(dw)
            dk_o[t] = both(dk)
            da_o[t] = both(da)
            db_o[t] = both(db)
            return carry

        lax.fori_loop(0, tc, bwd, 0)

        if nx:
            @pl.when(pl.program_id(0) == nc - 1)
            def _():
                c_finish(c_ins, c_out_refs, rest[-2], rest[-1])

    kblk = pl.BlockSpec((tc, kd, ln), lambda c: (nc - 1 - c, 0, 0))
    vblk = pl.BlockSpec((tc, vh, ln), lambda c: (nc - 1 - c, 0, 0))
    ksd = jax.ShapeDtypeStruct((t_len, kd, ln), F32)
    return _call(
        body, name=name, grid=(nc,),
        in_specs=[kblk] * 5 + [vblk, vblk, pl.BlockSpec((1, tc + 1, vh, kd, ln), lambda c: (nc - 1 - c, 0, 0, 0, 0)),
                  vblk] + cx_specs,
        out_specs=[kblk] * 5 + [vblk] + [_HBM] * len(c_outs),
        out_shape=[ksd] * 5 + [jax.ShapeDtypeStruct((t_len, vh, ln), F32)] + c_outs,
        scratch_shapes=[pltpu.VMEM((vh, kd, ln), F32)] + c_sems,
        compiler_params=_cp("arbitrary"),
    )(r, w, k, a, b, v, dy, states, sa, *cx)


def _post_common(y_ref, r_ref, k_ref, v_ref, lng_ref, lnb_ref, rk_ref, e, et):
    y = y_ref[...]
    inv_n = 1.0 / RWKV_HEAD
    mean = _headsum(y, e, et) * inv_n
    yc = y - mean
    var = _headsum(yc * yc, e, et) * inv_n
    rstd = lax.rsqrt(var + GN_EPS)
    yh = yc * rstd
    yn = yh * lng_ref[...] + lnb_ref[...]
    bonus = _headsum(r_ref[...] * k_ref[...] * rk_ref[...], e, et)
    return yh, rstd, yn, bonus


def _rwkv_post_fwd(y, r, k, v, g, ln_g, ln_b, r_k, name):
    n, w = y.shape
    hp = max(w // RWKV_HEAD, LANES)
    tb = _tile(n, 256, SUBLANES)

    def body(y_ref, r_ref, k_ref, v_ref, g_ref, lng_ref, lnb_ref, rk_ref, o_ref):
        e, et = _head_mats(w, hp)
        _, _, yn, bonus = _post_common(y_ref, r_ref, k_ref, v_ref, lng_ref, lnb_ref, rk_ref, e, et)
        o_ref[...] = ((yn + bonus * v_ref[...]) * g_ref[...]).astype(BF16)

    blk = pl.BlockSpec((tb, w), lambda i: (i, 0))
    vec = pl.BlockSpec((1, w), lambda i: (0, 0))
    return _call(body, name=name, grid=(n // tb,), in_specs=[blk] * 5 + [vec] * 3, out_specs=blk,
                 out_shape=jax.ShapeDtypeStruct((n, w), BF16),
                 compiler_params=_cp("parallel"))(y, r, k, v, g, ln_g, ln_b, r_k)


def _rwkv_post_bwd(y, r, k, v, g, ln_g, ln_b, r_k, do_cat, name):
    n, w = y.shape
    hp = max(w // RWKV_HEAD, LANES)
    tb = _tile(n, 128, SUBLANES)

    def body(y_ref, r_ref, k_ref, v_ref, g_ref, lng_ref, lnb_ref, rk_ref, do_ref,
             dy_o, dr_o, dk_o, dv_o, dg_o, dlng_o, dlnb_o, drk_o):
        e, et = _head_mats(w, hp)
        yh, rstd, yn, bonus = _post_common(y_ref, r_ref, k_ref, v_ref, lng_ref, lnb_ref, rk_ref, e, et)
        do = do_ref[...]
        vv, rv, kv, rk = v_ref[...], r_ref[...], k_ref[...], rk_ref[...]
        dg_o[...] = do * (yn + bonus * vv)
        dz = do * g_ref[...]
        dbonus = _headsum(dz * vv, e, et)
        dv_o[...] = dz * bonus
        dr_o[...] = dbonus * kv * rk
        dk_o[...] = dbonus * rv * rk
        dyh = dz * lng_ref[...]
        inv_n = 1.0 / RWKV_HEAD
        dy_o[...] = rstd * (dyh - _headsum(dyh, e, et) * inv_n - yh * (_headsum(dyh * yh, e, et) * inv_n))

        @pl.when(pl.program_id(0) == 0)
        def _():
            for o in (dlng_o, dlnb_o, drk_o):
                o[...] = jnp.zeros_like(o)

        dlng_o[...] += jnp.sum(dz * yh, axis=0, keepdims=True)
        dlnb_o[...] += jnp.sum(dz, axis=0, keepdims=True)
        drk_o[...] += jnp.sum(dbonus * rv * kv, axis=0, keepdims=True)

    blk = pl.BlockSpec((tb, w), lambda i: (i, 0))
    vec = pl.BlockSpec((1, w), lambda i: (0, 0))
    sds = jax.ShapeDtypeStruct
    return _call(body, name=name, grid=(n // tb,), in_specs=[blk] * 5 + [vec] * 3 + [blk],
                 out_specs=[blk] * 5 + [vec] * 3,
                 out_shape=[sds((n, w), F32)] * 5 + [sds((1, w), F32)] * 3,
                 compiler_params=_cp("arbitrary"))(y, r, k, v, g, ln_g, ln_b, r_k, do_cat)


def _lru_gates(xb, prev8, gate, cw_ref, cb_ref, wr_ref, br_ref, wi_ref, bi_ref, lam_ref, is_t0):
    c = xb.shape[1]
    nblk = c // LRU_BLOCK_W
    xs = [xb] + [_shift_down(xb, prev8, j) for j in range(1, CONV_WIDTH)]
    xc = cb_ref[...]
    for j in range(CONV_WIDTH):
        xc = xc + xs[CONV_WIDTH - 1 - j] * cw_ref[pl.ds(j, 1), :]
    xcb = xc.astype(BF16)

    def blockmm(w_ref):
        return jnp.concatenate(
            [jnp.dot(xcb[:, h * LRU_BLOCK_W:(h + 1) * LRU_BLOCK_W], w_ref[h], preferred_element_type=F32)
             for h in range(nblk)], axis=1)

    rg = _sig(blockmm(wr_ref) + br_ref[...])
    ig = _sig(blockmm(wi_ref) + bi_ref[...])
    sp = _softplus(-lam_ref[...])
    la = -LRU_C * rg * sp
    av = jnp.exp(la)
    mult = jnp.where(is_t0, 1.0, jnp.sqrt(_neg_expm1(2.0 * la)))
    ge, th = _gelu_parts(gate)
    return dict(xs=xs, xc=xc, xcb=xcb, rg=rg, ig=ig, sp=sp, a=av, mult=mult, ge=ge, th=th)


def _lru_specs(tb, c, nb, rev):
    nb8 = tb // SUBLANES

    def blk_i(i):
        return nb - 1 - i if rev else i

    xb = pl.BlockSpec((tb, c), lambda b, i: (b * nb + blk_i(i), 0))
    gate = pl.BlockSpec((tb, c), lambda b, i: (b * nb + blk_i(i), 1))
    prev = pl.BlockSpec((SUBLANES, c), lambda b, i: (jnp.maximum((b * nb + blk_i(i)) * nb8 - 1, 0), 0))
    return xb, gate, prev


def _lru_fwd(p_lru, prm, t_len, name):
    n, c2 = p_lru.shape
    c = c2 // 2
    nblk = c // LRU_BLOCK_W
    tb = _tile(t_len, 256, SUBLANES)
    nb = t_len // tb
    bsz = n // t_len

    def body(xb_ref, gate_ref, prev_ref, cw_ref, cb_ref, wr_ref, br_ref, wi_ref, bi_ref, lam_ref, ng_ref,
             y_o, h_o, carry):
        i = pl.program_id(1)
        prev8 = jnp.where(i == 0, jnp.zeros_like(prev_ref[...]), prev_ref[...])
        row = lax.broadcasted_iota(jnp.int32, (tb, c), 0)
        f = _lru_gates(xb_ref[...], prev8, gate_ref[...], cw_ref, cb_ref, wr_ref, br_ref, wi_ref, bi_ref, lam_ref,
                       jnp.logical_and(i == 0, row == 0))
        acc_a = f["a"]
        acc_b = f["mult"] * f["ig"] * f["xc"]
        s = 1
        while s < tb:
            keep = row >= s
            a_sh = jnp.where(keep, pltpu.roll(acc_a, s, 0), 1.0)
            b_sh = jnp.where(keep, pltpu.roll(acc_b, s, 0), 0.0)
            acc_b = acc_a * b_sh + acc_b
            acc_a = acc_a * a_sh
            s *= 2

        @pl.when(i == 0)
        def _():
            carry[...] = jnp.zeros_like(carry)

        h = acc_b + acc_a * carry[0:1, :]
        carry[0:1, :] = h[tb - 1:tb, :]
        h_o[...] = h
        y = h * f["ge"]
        rstd = lax.rsqrt(jnp.mean(y * y, axis=-1, keepdims=True) + NORM_EPS)
        y_o[...] = (y * rstd * ng_ref[...]).astype(BF16)

    xb_s, gate_s, prev_s = _lru_specs(tb, c, nb, False)
    c0 = lambda b, i: (0, 0)
    vec = pl.BlockSpec((1, c), c0)
    wsp = pl.BlockSpec((nblk, LRU_BLOCK_W, LRU_BLOCK_W), lambda b, i: (0, 0, 0))
    out = pl.BlockSpec((tb, c), lambda b, i: (b * nb + i, 0))
    return _call(
        body, name=name, grid=(bsz, nb),
        in_specs=[xb_s, gate_s, prev_s, pl.BlockSpec((CONV_WIDTH, c), c0), vec, wsp, vec, wsp, vec, vec, vec],
        out_specs=[out, out],
        out_shape=[jax.ShapeDtypeStruct((n, c), BF16), jax.ShapeDtypeStruct((n, c), F32)],
        scratch_shapes=[pltpu.VMEM((SUBLANES, c), F32)],
        compiler_params=_cp("arbitrary", "arbitrary"),
    )(p_lru, p_lru, p_lru, prm["conv_w"], prm["conv_b"], prm["wr"], prm["br"], prm["wi"], prm["bi"],
      prm["lam"], prm["norm_g"])


def _lru_bwd(p_lru, h, do_cat, prm, t_len, name):
    n, c2 = p_lru.shape
    c = c2 // 2
    nblk = c // LRU_BLOCK_W
    tb = _tile(t_len, 128, SUBLANES)
    nb = t_len // tb
    bsz = n // t_len

    def body(xb_ref, gate_ref, prev_ref, h_ref, hprev_ref, do_ref,
             cw_ref, cb_ref, wr_ref, br_ref, wi_ref, bi_ref, lam_ref, ng_ref,
             dp_o, dcw_o, dcb_o, dwr_o, dbr_o, dwi_o, dbi_o, dlam_o, dng_o,
             a_next, g_next, dxc_next):
        b = pl.program_id(0)
        i = pl.program_id(1)
        blk = nb - 1 - i
        first = blk == 0
        prev8 = jnp.where(first, jnp.zeros_like(prev_ref[...]), prev_ref[...])
        hprev8 = jnp.where(first, jnp.zeros_like(hprev_ref[...]), hprev_ref[...])
        row = lax.broadcasted_iota(jnp.int32, (tb, c), 0)
        is_t0 = jnp.logical_and(first, row == 0)
        gate = gate_ref[...]
        f = _lru_gates(xb_ref[...], prev8, gate, cw_ref, cb_ref, wr_ref, br_ref, wi_ref, bi_ref, lam_ref, is_t0)

        @pl.when(i == 0)
        def _():
            a_next[...] = jnp.zeros_like(a_next)
            g_next[...] = jnp.zeros_like(g_next)
            dxc_next[...] = jnp.zeros_like(dxc_next)

        @pl.when(jnp.logical_and(b == 0, i == 0))
        def _():
            for o in (dcw_o, dcb_o, dwr_o, dbr_o, dwi_o, dbi_o, dlam_o, dng_o):
                o[...] = jnp.zeros_like(o)

        def rsum(x):
            return jnp.sum(x, axis=0, keepdims=True)

        hv = h_ref[...]
        hprev = _shift_down(hv, hprev8, 1)
        ge = f["ge"]
        y = hv * ge
        rstd = lax.rsqrt(jnp.mean(y * y, axis=-1, keepdims=True) + NORM_EPS)
        yh = y * rstd
        dyn = do_ref[...]
        t = dyn * ng_ref[...]
        dy = rstd * (t - yh * jnp.mean(t * yh, axis=-1, keepdims=True))
        dng_o[...] += rsum(dyn * yh)
        dgate = dy * hv * _gelu_grad(gate, f["th"])

        av = f["a"]
        acc_c = _shift_up(av, a_next[...], 1)
        acc_g = dy * ge
        s = 1
        while s < tb:
            keep = row < tb - s
            c_sh = jnp.where(keep, pltpu.roll(acc_c, tb - s, 0), 1.0)
            g_sh = jnp.where(keep, pltpu.roll(acc_g, tb - s, 0), 0.0)
            acc_g = acc_g + acc_c * g_sh
            acc_c = acc_c * c_sh
            s *= 2
        gtot = acc_g + acc_c * g_next[0:1, :]
        a_next[0:1, :] = av[0:1, :]
        g_next[0:1, :] = gtot[0:1, :]

        xc, ig, rg, mult = f["xc"], f["ig"], f["rg"], f["mult"]
        da = gtot * hprev
        dmult = gtot * ig * xc
        dig = gtot * mult * xc
        dxc = gtot * mult * ig
        da = da + jnp.where(is_t0, 0.0, -dmult * av / mult)
        dla = da * av
        drg = dla * (-LRU_C) * f["sp"]
        dlam_o[...] += rsum(dla * rg) * LRU_C * _sig(-lam_ref[...])
        dzr = drg * rg * (1.0 - rg)
        dzi = dig * ig * (1.0 - ig)
        dbr_o[...] += rsum(dzr)
        dbi_o[...] += rsum(dzi)
        dzrb, dzib = dzr.astype(BF16), dzi.astype(BF16)
        xcb = f["xcb"]
        back = []
        for hh in range(nblk):
            sl = slice(hh * LRU_BLOCK_W, (hh + 1) * LRU_BLOCK_W)
            dwr_o[hh] += _dot_tn(xcb[:, sl], dzrb[:, sl])
            dwi_o[hh] += _dot_tn(xcb[:, sl], dzib[:, sl])
            back.append(_dot_nt(dzrb[:, sl], wr_ref[hh]) + _dot_nt(dzib[:, sl], wi_ref[hh]))
        dxc = dxc + jnp.concatenate(back, axis=1)
        dcb_o[...] += rsum(dxc)
        xs = f["xs"]
        dcw_o[...] += jnp.concatenate([rsum(dxc * xs[CONV_WIDTH - 1 - j]) for j in range(CONV_WIDTH)], axis=0)
        nxt = dxc_next[...]
        dxb = dxc * cw_ref[pl.ds(CONV_WIDTH - 1, 1), :]
        for j in range(1, CONV_WIDTH):
            dxb = dxb + _shift_up(dxc, nxt, j) * cw_ref[pl.ds(CONV_WIDTH - 1 - j, 1), :]
        dxc_next[...] = dxc[0:SUBLANES, :]
        dp_o[:, 0:c] = dxb.astype(BF16)
        dp_o[:, c:2 * c] = dgate.astype(BF16)

    xb_s, gate_s, prev_s = _lru_specs(tb, c, nb, True)
    c0 = lambda b, i: (0, 0)
    vec = pl.BlockSpec((1, c), c0)
    wsp = pl.BlockSpec((nblk, LRU_BLOCK_W, LRU_BLOCK_W), lambda b, i: (0, 0, 0))
    cwsp = pl.BlockSpec((CONV_WIDTH, c), c0)
    sds = jax.ShapeDtypeStruct
    return _call(
        body, name=name, grid=(bsz, nb),
        in_specs=[xb_s, gate_s, prev_s, xb_s, prev_s, gate_s, cwsp, vec, wsp, vec, wsp, vec, vec, vec],
        out_specs=[pl.BlockSpec((tb, 2 * c), lambda b, i: (b * nb + nb - 1 - i, 0)),
                   cwsp, vec, wsp, vec, wsp, vec, vec, vec],
        out_shape=[sds((n, 2 * c), BF16), sds((CONV_WIDTH, c), F32), sds((1, c), F32),
                   sds((nblk, LRU_BLOCK_W, LRU_BLOCK_W), F32), sds((1, c), F32),
                   sds((nblk, LRU_BLOCK_W, LRU_BLOCK_W), F32), sds((1, c), F32), sds((1, c), F32), sds((1, c), F32)],
        scratch_shapes=[pltpu.VMEM((SUBLANES, c), F32)] * 3,
        compiler_params=_cp("arbitrary", "arbitrary"),
    )(p_lru, p_lru, p_lru, h, h, do_cat, prm["conv_w"], prm["conv_b"], prm["wr"], prm["br"], prm["wi"], prm["bi"],
      prm["lam"], prm["norm_g"])


def _adamw(g, w, m, v, name):
    rows, cols = g.shape
    tb = _tile(rows, 128, SUBLANES)

    def body(g_ref, w_ref, m_ref, v_ref, d_o, m_o, v_o):
        gv = g_ref[...]
        mn = ADAM_B1 * m_ref[...] + (1.0 - ADAM_B1) * gv
        vn = ADAM_B2 * v_ref[...] + (1.0 - ADAM_B2) * (gv * gv)
        m_o[...] = mn
        v_o[...] = vn
        d_o[...] = -ADAM_LR * ((mn / _BC1) / (jnp.sqrt(vn / _BC2) + ADAM_EPS) + ADAM_WD * w_ref[...])

    blk = pl.BlockSpec((tb, cols), lambda i: (i, 0))
    return _call(body, name=name, grid=(rows // tb,), in_specs=[blk] * 4, out_specs=[blk] * 3,
                 out_shape=[jax.ShapeDtypeStruct((rows, cols), F32)] * 3, compiler_params=_cp("parallel"))(g, w, m, v)


def _adamw_halves(mine, theirs, w, m, v, name):
    a, b = mine.shape
    tc = _col_tile(a, b)
    w, m, v = (t.reshape(2, a, b) for t in (w, m, v))

    def body(mine_ref, theirs_ref, w_ref, m_ref, v_ref, g_o, d_o, m_o, v_o):
        gv = jnp.where(pl.program_id(0) == lax.axis_index("c"), mine_ref[...], theirs_ref[...])
        mn = ADAM_B1 * m_ref[...] + (1.0 - ADAM_B1) * gv
        vn = ADAM_B2 * v_ref[...] + (1.0 - ADAM_B2) * (gv * gv)
        g_o[...] = gv
        m_o[...] = mn
        v_o[...] = vn
        d_o[...] = -ADAM_LR * ((mn / _BC1) / (jnp.sqrt(vn / _BC2) + ADAM_EPS) + ADAM_WD * w_ref[...])

    half = pl.BlockSpec((a, tc), lambda h, j: (0, j))
    blk = pl.BlockSpec((None, a, tc), lambda h, j: (h, 0, j))
    return _call(body, name=name, grid=(2, b // tc), in_specs=[half, half, blk, blk, blk], out_specs=[blk] * 4,
                 out_shape=[jax.ShapeDtypeStruct((2, a, b), F32)] * 4,
                 compiler_params=_cp("parallel", "parallel"))(mine, theirs, w, m, v)


def _pair_sum(x4, recv, name):
    _, _, a, b = x4.shape
    tc = _col_tile(a, b)

    def body(x_ref, r_ref, o_ref):
        mine = x_ref[lax.axis_index("c")]
        o_ref[...] = (mine.astype(F32) + r_ref[...].astype(F32)).astype(BF16)

    return _call(
        body, name=name, grid=(4, b // tc),
        in_specs=[pl.BlockSpec((None, 2, a, tc), lambda j, i: (j, 0, 0, i)),
                  pl.BlockSpec((None, a, tc), lambda j, i: (j, 0, i))],
        out_specs=pl.BlockSpec((None, a, tc), lambda j, i: (j, 0, i)),
        out_shape=jax.ShapeDtypeStruct((4, a, b), BF16), compiler_params=_cp("parallel", "parallel"))(x4, recv)


def _chip_sum(x4, name):
    _, a, b = x4.shape
    ta = _tile(a, 256, SUBLANES)

    def body(x_ref, o_ref):
        acc = x_ref[0] + x_ref[1]
        acc = acc + x_ref[2]
        o_ref[...] = acc + x_ref[3]

    return _call(
        body, name=name, grid=(a // ta,),
        in_specs=[pl.BlockSpec((4, ta, b), lambda i: (0, i, 0))],
        out_specs=pl.BlockSpec((ta, b), lambda i: (i, 0)),
        out_shape=jax.ShapeDtypeStruct((a, b), F32), compiler_params=_cp("parallel"))(x4)


def _peer_sum(own4, parts, name):
    _, a, b = parts.shape
    tc = _col_tile(a, b)

    def body(own_ref, p_ref, o_ref):
        me = 2 * lax.axis_index("x") + lax.axis_index("y")
        acc = own_ref[me].astype(F32) + p_ref[0].astype(F32)
        acc = acc + p_ref[1].astype(F32)
        o_ref[...] = acc + p_ref[2].astype(F32)

    return _call(
        body, name=name, grid=(b // tc,),
        in_specs=[pl.BlockSpec((4, a, tc), lambda i: (0, 0, i)), pl.BlockSpec((3, a, tc), lambda i: (0, 0, i))],
        out_specs=pl.BlockSpec((a, tc), lambda i: (0, i)),
        out_shape=jax.ShapeDtypeStruct((a, b), F32), compiler_params=_cp("parallel"))(own4, parts)


def _add2(x, y, name):
    rows, cols = x.shape
    tb = _tile(rows, 512, SUBLANES)

    def body(x_ref, y_ref, o_ref):
        o_ref[...] = x_ref[...] + y_ref[...]

    blk = pl.BlockSpec((tb, cols), lambda i: (i, 0))
    return _call(body, name=name, grid=(rows // tb,), in_specs=[blk, blk], out_specs=blk,
                 out_shape=jax.ShapeDtypeStruct((rows, cols), x.dtype), compiler_params=_cp("parallel"))(x, y)


_HBM = pl.BlockSpec(memory_space=pltpu.HBM)


def _place():
    x, y, c = lax.axis_index("x"), lax.axis_index("y"), lax.axis_index("c")
    chips = [(1 - x, y), (x, 1 - y), (1 - x, 1 - y)]
    return x, y, c, chips


def _comm_call(body, name, xs, out_shapes, n_sems):
    return _call(
        body, name=name, in_specs=[_HBM] * len(xs), out_specs=[_HBM] * len(out_shapes), out_shape=out_shapes,
        scratch_shapes=[pltpu.SemaphoreType.DMA((n_sems,)), pltpu.SemaphoreType.DMA((n_sems,)),
                        pltpu.SemaphoreType.DMA((len(xs),))],
    )(*xs)


def _all_gather_chips(xs, name):
    n = len(xs)

    def body(*refs):
        ins, outs = refs[:n], refs[n:2 * n]
        ssem, rsem, _ = refs[2 * n:]
        _gather_start(ins, outs, ssem, rsem)
        _gather_finish(ins, outs, ssem, rsem)

    outs = [jax.ShapeDtypeStruct((4,) + v.shape, v.dtype) for v in xs]
    return _comm_call(body, name, xs, outs, GATHER_SEMS * n)


GATHER_SEMS = 7
PEER_SEMS = 3


def _remote(src, dst, ssem, rsem, k, dev):
    return pltpu.make_async_remote_copy(src_ref=src, dst_ref=dst, send_sem=ssem.at[k], recv_sem=rsem.at[k],
                                        device_id=dev, device_id_type=MESH)


def _gather_start(ins, outs, ssem, rsem):
    x, y, c, chips = _place()
    me = 2 * x + y
    for i in range(len(ins)):
        for j, (px, py) in enumerate(chips):
            _remote(ins[i].at[c], outs[i].at[me, c], ssem, rsem, GATHER_SEMS * i + j, (px, py, c)).start()
        _remote(ins[i], outs[i].at[me], ssem, rsem, GATHER_SEMS * i + 6, (x, y, 1 - c)).start()


def _gather_finish(ins, outs, ssem, rsem):
    x, y, c, chips = _place()
    me = 2 * x + y
    sib = (x, y, 1 - c)
    n = len(ins)
    for i in range(n):
        for j, (px, py) in enumerate(chips):
            slot = outs[i].at[2 * px + py, c]
            _remote(slot, slot, ssem, rsem, GATHER_SEMS * i + j, (px, py, c)).wait_recv()
            _remote(slot, slot, ssem, rsem, GATHER_SEMS * i + 3 + j, sib).start()
    for i in range(n):
        own = outs[i].at[me]
        _remote(own, own, ssem, rsem, GATHER_SEMS * i + 6, sib).wait_recv()
        for j, (px, py) in enumerate(chips):
            slot = outs[i].at[2 * px + py, 1 - c]
            _remote(slot, slot, ssem, rsem, GATHER_SEMS * i + 3 + j, sib).wait_recv()
    for i in range(n):
        for j, (px, py) in enumerate(chips):
            slot = outs[i].at[2 * px + py, c]
            _remote(ins[i].at[c], outs[i].at[me, c], ssem, rsem, GATHER_SEMS * i + j, (px, py, c)).wait_send()
            _remote(slot, slot, ssem, rsem, GATHER_SEMS * i + 3 + j, sib).wait_send()
        _remote(ins[i], outs[i].at[me], ssem, rsem, GATHER_SEMS * i + 6, sib).wait_send()


def _peer_copies(ins, outs, ssem, rsem):
    x, y, c, chips = _place()
    return [_remote(ins[i].at[2 * px + py], outs[i].at[j], ssem, rsem, PEER_SEMS * i + j, (px, py, c))
            for i in range(len(ins)) for j, (px, py) in enumerate(chips)]


def _comm_plan(kind, xs):
    if kind == "gather":
        outs = [jax.ShapeDtypeStruct((4,) + v.shape, v.dtype) for v in xs]
        return outs, GATHER_SEMS * len(xs), _gather_start, _gather_finish
    if kind == "swap_half":
        def swaps(ins, outs, ssem, rsem):
            x, y, c, _ = _place()
            return [_remote(ins[i].at[:, 1 - c], outs[i], ssem, rsem, i, (x, y, 1 - c)) for i in range(len(ins))]

        def start_swaps(ins, outs, ssem, rsem):
            for cp in swaps(ins, outs, ssem, rsem):
                cp.start()

        def finish_swaps(ins, outs, ssem, rsem):
            for cp in swaps(ins, outs, ssem, rsem):
                cp.wait()

        outs = [jax.ShapeDtypeStruct((v.shape[0],) + v.shape[2:], v.dtype) for v in xs]
        return outs, len(xs), start_swaps, finish_swaps

    def start(ins, outs, ssem, rsem):
        for cp in _peer_copies(ins, outs, ssem, rsem):
            cp.start()

    def finish(ins, outs, ssem, rsem):
        for cp in _peer_copies(ins, outs, ssem, rsem):
            cp.wait()

    outs = [jax.ShapeDtypeStruct((3,) + v.shape[1:], v.dtype) for v in xs]
    return outs, PEER_SEMS * len(xs), start, finish


def _sibling_swap(xs, pick_half, name):
    n = len(xs)

    def body(*refs):
        ins, outs = refs[:n], refs[n:2 * n]
        ssem, rsem, _ = refs[2 * n:]
        x, y, c, _ = _place()
        cps = []
        for i in range(n):
            src = ins[i].at[:, 1 - c] if pick_half else ins[i]
            cp = pltpu.make_async_remote_copy(src_ref=src, dst_ref=outs[i], send_sem=ssem.at[i], recv_sem=rsem.at[i],
                                              device_id=(x, y, 1 - c), device_id_type=MESH)
            cp.start()
            cps.append(cp)
        for cp in cps:
            cp.wait()

    outs = [jax.ShapeDtypeStruct((v.shape[0],) + v.shape[2:] if pick_half else v.shape, v.dtype) for v in xs]
    return _comm_call(body, name, xs, outs, n)


def _chip_broadcast(xs, name):
    n = len(xs)

    def body(*refs):
        ins, outs = refs[:n], refs[n:2 * n]
        ssem, rsem, lsem = refs[2 * n:]
        x, y, c, chips = _place()
        me = 2 * x + y
        cps = []
        for i in range(n):
            cp = pltpu.make_async_copy(ins[i], outs[i].at[me], lsem.at[i])
            cp.start()
            cps.append(cp)
            for j, (px, py) in enumerate(chips):
                cp = pltpu.make_async_remote_copy(
                    src_ref=ins[i], dst_ref=outs[i].at[me], send_sem=ssem.at[3 * i + j], recv_sem=rsem.at[3 * i + j],
                    device_id=(px, py, c), device_id_type=MESH)
                cp.start()
                cps.append(cp)
        for i in range(n):
            for j, (px, py) in enumerate(chips):
                slot = outs[i].at[2 * px + py]
                pltpu.make_async_remote_copy(
                    src_ref=slot, dst_ref=slot, send_sem=ssem.at[3 * i + j], recv_sem=rsem.at[3 * i + j],
                    device_id=(px, py, c), device_id_type=MESH).wait_recv()
        for i in range(n):
            cps[4 * i].wait()
            for j in range(3):
                cps[4 * i + 1 + j].wait_send()

    outs = [jax.ShapeDtypeStruct((4,) + v.shape, v.dtype) for v in xs]
    return _comm_call(body, name, xs, outs, 3 * n)


def _to_scan_k(x, bsz, t_len):
    h = x.shape[1] // RWKV_HEAD
    y = jnp.broadcast_to(x.reshape(1, bsz, t_len, h, RWKV_HEAD), (2, bsz, t_len, h, RWKV_HEAD))
    return y.transpose(2, 4, 0, 1, 3).reshape(t_len, RWKV_HEAD, 2 * bsz * h)


def _to_scan_v(x, bsz, t_len):
    h = x.shape[1] // RWKV_HEAD
    y = x.reshape(bsz, t_len, h, 2, RWKV_HEAD // 2).transpose(1, 4, 3, 0, 2)
    return y.reshape(t_len, RWKV_HEAD // 2, 2 * bsz * h)


def _from_scan_k(x, bsz, t_len):
    h = x.shape[2] // (2 * bsz)
    y = x[:, :, :bsz * h].reshape(t_len, RWKV_HEAD, bsz, h).transpose(2, 0, 3, 1)
    return y.reshape(bsz * t_len, h * RWKV_HEAD)


def _from_scan_v(x, bsz, t_len):
    h = x.shape[2] // (2 * bsz)
    y = x.reshape(t_len, RWKV_HEAD // 2, 2, bsz, h).transpose(3, 0, 4, 2, 1)
    return y.reshape(bsz * t_len, h * RWKV_HEAD)


def _pad_rows(x, rows):
    return jnp.pad(x, ((0, rows - x.shape[0]), (0, 0)))


def _pad_cols(x, cols):
    return jnp.pad(x, ((0, 0), (0, cols - x.shape[1])))


def _cols_from_shards(g4):
    _, r, cs = g4.shape
    return g4.transpose(1, 0, 2).reshape(r, 4 * cs)


def _cols_to_shards(g):
    r, cols = g.shape
    return g.reshape(r, 4, cols // 4).transpose(1, 0, 2)


def kernel(x, norm_mix_g, w_in, mu_shift, rwkv_w0, rwkv_w2, rwkv_a0, rwkv_a2, rwkv_g2, rwkv_k_k, rwkv_k_a, rwkv_r_k, rwkv_ln_g, rwkv_ln_b, conv_w, conv_b, lru_wr, lru_br, lru_wi, lru_bi, lru_lambda, lru_norm_g, w_out, norm_ffn_g, ffn_w_gate, ffn_w_up, ffn_w_down, norm_final_g, loss_target, m_norm_mix_g, m_w_in, m_mu_shift, m_rwkv_w0, m_rwkv_w2, m_rwkv_a0, m_rwkv_a2, m_rwkv_g2, m_rwkv_k_k, m_rwkv_k_a, m_rwkv_r_k, m_rwkv_ln_g, m_rwkv_ln_b, m_conv_w, m_conv_b, m_lru_wr, m_lru_br, m_lru_wi, m_lru_bi, m_lru_lambda, m_lru_norm_g, m_w_out, m_norm_ffn_g, m_ffn_w_gate, m_ffn_w_up, m_ffn_w_down, m_norm_final_g, v_norm_mix_g, v_w_in, v_mu_shift, v_rwkv_w0, v_rwkv_w2, v_rwkv_a0, v_rwkv_a2, v_rwkv_g2, v_rwkv_k_k, v_rwkv_k_a, v_rwkv_r_k, v_rwkv_ln_g, v_rwkv_ln_b, v_conv_w, v_conv_b, v_lru_wr, v_lru_br, v_lru_wi, v_lru_bi, v_lru_lambda, v_lru_norm_g, v_w_out, v_norm_ffn_g, v_ffn_w_gate, v_ffn_w_up, v_ffn_w_down, v_norm_final_g):
    names = ['norm_mix_g', 'w_in', 'mu_shift', 'rwkv_w0', 'rwkv_w2', 'rwkv_a0', 'rwkv_a2', 'rwkv_g2', 'rwkv_k_k',
             'rwkv_k_a', 'rwkv_r_k', 'rwkv_ln_g', 'rwkv_ln_b', 'conv_w', 'conv_b', 'lru_wr', 'lru_br', 'lru_wi',
             'lru_bi', 'lru_lambda', 'lru_norm_g', 'w_out', 'norm_ffn_g', 'ffn_w_gate', 'ffn_w_up', 'ffn_w_down',
             'norm_final_g']
    env = locals()
    wts = {k: env[k] for k in names}
    mom_m = {k: env["m_" + k] for k in names}
    mom_v = {k: env["v_" + k] for k in names}

    bsz, t_len, d = x.shape
    n = bsz * t_len
    w = rwkv_w0.shape[1]
    lw = lru_br.shape[1]
    dl, al, gl = rwkv_w2.shape[1], rwkv_a2.shape[1], rwkv_g2.shape[1]
    dlp, alp, glp = _ceil_to(dl, LANES), _ceil_to(al, LANES), _ceil_to(gl, LANES)
    lp = dlp + alp + glp
    rc = 3 * w + dl + al + gl
    chip = 2 * lax.axis_index("x") + lax.axis_index("y")

    big = ['w_in', 'w_out', 'ffn_w_gate', 'ffn_w_up', 'ffn_w_down']
    small_sh = ['rwkv_w2', 'rwkv_a2', 'rwkv_g2', 'conv_w']

    def halves(a2d):
        return a2d.reshape(2, a2d.shape[0] // 2, a2d.shape[1])

    col_sharded = ('w_in', 'ffn_w_gate', 'ffn_w_up')

    def work(k, t):
        return jnp.swapaxes(t[0], 0, 1) if k in col_sharded else t[0]

    def unwork(k, t2):
        return (jnp.swapaxes(t2, 0, 1) if k in col_sharded else t2)[None]

    def rows_of(g):
        return g.reshape(g.shape[0] * g.shape[1] * g.shape[2], g.shape[3])

    send = [halves(work('w_in', w_in).astype(BF16))] + [halves(wts[k][0]) for k in small_sh]
    got = _all_gather_chips(send, "gather_w_in")
    later = ['w_out', 'ffn_w_gate', 'ffn_w_up', 'ffn_w_down']
    send_later = [halves(work(k, wts[k]).astype(BF16)) for k in later]
    full = {}
    for k, g in zip(small_sh, got[1:]):
        full[k] = _cols_from_shards(g.reshape(4, g.shape[1] * g.shape[2], g.shape[3]))
    wi_t = rows_of(got[0])
    w_rkv = wi_t[:3 * w]
    w_lru = wi_t[rc:]
    o = 3 * w
    w_lora = jnp.concatenate([_pad_rows(wi_t[o:o + dl], dlp), _pad_rows(wi_t[o + dl:o + dl + al], alp),
                              _pad_rows(wi_t[o + dl + al:rc], glp)], axis=0)
    mu = mu_shift
    prm_r = dict(
        mu_rkv=mu[:, :3 * w],
        mu_lora=jnp.concatenate([_pad_cols(mu[:, o:o + dl], dlp), _pad_cols(mu[:, o + dl:o + dl + al], alp),
                                 _pad_cols(mu[:, o + dl + al:rc], glp)], axis=1),
        w0=rwkv_w0, a0=rwkv_a0, k_k=rwkv_k_k, k_a=rwkv_k_a,
        w2=_pad_rows(full['rwkv_w2'], dlp).astype(BF16), a2=_pad_rows(full['rwkv_a2'], alp).astype(BF16),
        g2=_pad_rows(full['rwkv_g2'], glp).astype(BF16))
    ln_g, ln_b, r_k = rwkv_ln_g, rwkv_ln_b, rwkv_r_k.reshape(1, w)
    prm_l = dict(conv_w=full['conv_w'], conv_b=conv_b, wr=lru_wr[0].astype(BF16), br=lru_br,
                 wi=lru_wi[0].astype(BF16), bi=lru_bi, lam=lru_lambda, norm_g=lru_norm_g)
    g_final = norm_final_g.reshape(1, d)

    x2 = x.reshape(n, d)
    u1 = _rmsnorm_fwd(x2, norm_mix_g, "norm_mix")
    p_rkv = _mm(u1, w_rkv, name="in_rkv", tb=True)
    p_lru = _mm(u1, w_lru, name="in_lru", tb=True)
    p_lora = _mm(u1, w_lora, name="in_lora", tb=True)
    r_t, dec_t, k_t, v_t, na_t, nb_t, g_t = _rwkv_prep_fwd(p_rkv, p_lora, prm_r, t_len, "rwkv_prep")
    sk = [_to_scan_k(a, bsz, t_len) for a in (r_t, dec_t, k_t, na_t, nb_t)]
    sv = _to_scan_v(v_t, bsz, t_len)
    y_s, states, sa_s, got_wo, got_wg =_rwkv_scan_fwd(*sk, sv, name="rwkv_scan", comm=("gather", send_later[:2]))
    wo, wg = rows_of(got_wo), rows_of(got_wg)
    y_t = _from_scan_v(y_s, bsz, t_len)
    y_a = _rwkv_post_fwd(y_t, r_t, k_t, v_t, g_t, ln_g, ln_b, r_k, "rwkv_post")
    y_b, h_lru = _lru_fwd(p_lru, prm_l, t_len, "lru_fwd")
    h1 = _mm(y_a, wo[:w], name="out_a", res=x2)
    h1 = _mm(y_b, wo[w:], name="out_b", res=h1)
    u2 = _rmsnorm_fwd(h1, norm_ffn_g, "norm_ffn")
    ffc = (1024, 256, 4096)
    gate, got_wu = _mm(u2, wg, name="ffn_gate", tb=True, caps=ffc, comm=("gather", send_later[2:3]))
    wu = rows_of(got_wu)
    up, act, got_wd = _mm(u2, wu, name="ffn_up", tb=True, caps=ffc, comm=("gather", send_later[3:4]),
                          epi=(_swiglu_tile, [gate], [F32, BF16]))
    wd = rows_of(got_wd)
    h2 = _mm(act, wd, name="ffn_down", res=h1, caps=(512, 256, 11008))

    dh2, dh2b, g_norm_final, loss_vec = _loss_head(h2, g_final, loss_target.reshape(n, d), "loss_head")
    loss = lax.psum(loss_vec[0, 0], ("x", "y", "c"))
    dgate, dup = _mm(dh2b, wd, name="d_act", tb=True, caps=(1024, 256, 4096),
                     epi=(_swiglu_bwd_tile, [gate, up], [BF16, BF16]))
    shards = lambda g: g.reshape(4, 2, g.shape[0] // 8, g.shape[1])
    dwc = dict(ta=True, out_dtype=BF16, n_outer=True, caps=(256, 1024, 4096))
    gw_down = _mm(act, dh2b, name="dw_down", **dwc)
    gw_gate = _mm(dgate, u2, name="dw_gate", **dwc)
    gw_up = _mm(dup, u2, name="dw_up", **dwc)
    g4f = [shards(g) for g in (gw_gate, gw_up, gw_down)]
    du2, *sib_f = _mm(dgate, wg, name="du2_gate", caps=(512, 256, 11008), comm=("swap_half", g4f))
    du2 = _mm(dup, wu, name="du2_up", res=du2, caps=(512, 256, 11008))
    dh1, dh1b, g_norm_ffn = _rmsnorm_bwd(du2, h1, norm_ffn_g, dh2, "norm_ffn_bwd")
    dcat = _mm(dh1b, wo, name="d_cat", tb=True)
    gw_out = jnp.concatenate([_mm(y_a, dh1b, name="dw_out_a", ta=True, out_dtype=BF16),
                              _mm(y_b, dh1b, name="dw_out_b", ta=True, out_dtype=BF16)], axis=0)
    g4o = [shards(gw_out)]
    sib_o = _sibling_swap(g4o, True, "grad_sibling_out")
    pair_a = [_pair_sum(a4, s, "grad_pair_sum_%d" % i) for i, (a4, s) in enumerate(zip(g4o + g4f, list(sib_o) + list(sib_f)))]
    (dp_lru, g_conv_w, g_conv_b, g_wr, g_br, g_wi, g_bi, g_lam, g_lng) = _lru_bwd(
        p_lru, h_lru, dcat, prm_l, t_len, "lru_bwd")
    dy_t, dr_p, dk_p, dv_p, dg_t, g_ln_g, g_ln_b, g_r_k = _rwkv_post_bwd(
        y_t, r_t, k_t, v_t, g_t, ln_g, ln_b, r_k, dcat, "rwkv_post_bwd")
    dr_s, dw_s, dk_s, da_s, db_s, dv_s, *parts_a = _rwkv_scan_bwd(
        *sk, sv, _to_scan_v(dy_t, bsz, t_len), states, sa_s, name="rwkv_scan_bwd", comm=("peer", pair_a))
    grads = [_from_scan_k(dr_s, bsz, t_len), dr_p, _from_scan_k(dw_s, bsz, t_len), _from_scan_k(dk_s, bsz, t_len),
             dk_p, _from_scan_v(dv_s, bsz, t_len), dv_p, _from_scan_k(da_s, bsz, t_len),
             _from_scan_k(db_s, bsz, t_len), dg_t]
    (dq_r, dq_l, g_mu_r, g_mu_l, g_w0, g_a0, g_kk, g_ka, g_w2, g_a2, g_g2) = _rwkv_prep_bwd(
        p_rkv, p_lora, prm_r, grads, t_len, "rwkv_prep_bwd")
    dp_rkv = _shift_combine(dq_r, prm_r["mu_rkv"], t_len, "shift_bwd_rkv")
    dp_lora = _shift_combine(dq_l, prm_r["mu_lora"], t_len, "shift_bwd_lora")
    gi_rkv = _mm(dp_rkv, u1, name="dw_in_rkv", ta=True, out_dtype=BF16)
    gi_lru = _mm(dp_lru, u1, name="dw_in_lru", ta=True, out_dtype=BF16)
    gi_lora = _mm(dp_lora, u1, name="dw_in_lora", ta=True, out_dtype=BF16)
    gw_in = jnp.concatenate([gi_rkv, gi_lora[:dl], gi_lora[dlp:dlp + al], gi_lora[dlp + alp:dlp + alp + gl], gi_lru],
                            axis=0)
    g4b = [shards(gw_in)]
    sib_b = _sibling_swap(g4b, True, "grad_sibling_in")
    pair_b = [_pair_sum(g4b[0], sib_b[0], "grad_pair_sum_in")]
    du1, *parts_b = _mm(dp_rkv, w_rkv, name="du1_rkv", comm=("peer", pair_b))
    du1 = _mm(dp_lru, w_lru, name="du1_lru", res=du1)
    du1 = _mm(dp_lora, w_lora, name="du1_lora", res=du1)
    gx, _, g_norm_mix = _rmsnorm_bwd(du1, x2, norm_mix_g, dh1, "norm_mix_bwd")

    pair, parts = pair_b + pair_a, list(parts_b) + list(parts_a)
    mine = [_peer_sum(own4, p3, "grad_chip_sum_%d" % i) for i, (own4, p3) in enumerate(zip(pair, parts))]
    theirs = _sibling_swap(mine, False, "grad_share")

    g_mu = jnp.concatenate([g_mu_r, g_mu_l[:, :dl], g_mu_l[:, dlp:dlp + al], g_mu_l[:, dlp + alp:dlp + alp + gl]],
                           axis=1)
    small = dict(norm_mix_g=g_norm_mix, mu_shift=g_mu, rwkv_w0=g_w0, rwkv_w2=g_w2[:dl], rwkv_a0=g_a0,
                 rwkv_a2=g_a2[:al], rwkv_g2=g_g2[:gl], rwkv_k_k=g_kk, rwkv_k_a=g_ka, rwkv_r_k=g_r_k,
                 rwkv_ln_g=g_ln_g, rwkv_ln_b=g_ln_b, conv_w=g_conv_w, conv_b=g_conv_b, lru_wr=g_wr, lru_br=g_br,
                 lru_wi=g_wi, lru_bi=g_bi, lru_lambda=g_lam, lru_norm_g=g_lng, norm_ffn_g=g_norm_ffn,
                 norm_final_g=g_norm_final)
    small_names = list(small)
    sizes = [small[k].size for k in small_names]
    total = sum(sizes)
    padded = _ceil_to(total, 512 * LANES)

    def pack(arrs):
        flat = jnp.concatenate([a.reshape(-1) for a in arrs] + [jnp.zeros((padded - sum(a.size for a in arrs),), F32)])
        return flat.reshape(padded // LANES, LANES)

    packed = pack([small[k] for k in small_names])
    other = _sibling_swap([packed], False, "small_sibling")[0]
    chip_sum = _add2(packed, other, "small_pair_sum")
    all4 = _chip_broadcast([chip_sum], "small_chips")[0]
    red = _chip_sum(all4, "small_chip_sum").reshape(-1)
    small_g = {}
    off = 0
    for k, sz in zip(small_names, sizes):
        full_g = red[off:off + sz].reshape(small[k].shape)
        off += sz
        if k in small_sh:
            cs = full_g.shape[1] // 4
            full_g = lax.dynamic_slice_in_dim(full_g, chip * cs, cs, axis=1)
        small_g[k] = full_g.reshape(wts[k].shape)

    grad_w, delta_w, new_m, new_v = {}, {}, {}, {}
    for k, g_mine, g_theirs in zip(big, mine, theirs):
        res = _adamw_halves(g_mine, g_theirs, work(k, wts[k]), work(k, mom_m[k]), work(k, mom_v[k]), "adamw_" + k)
        grad_w[k], delta_w[k], new_m[k], new_v[k] = (unwork(k, t.reshape(2 * t.shape[1], t.shape[2])) for t in res)
    lsizes = [small_g[k].size for k in small_names]
    lpad = _ceil_to(sum(lsizes), 128 * LANES)

    def lpack(tree):
        arrs = [tree[k].reshape(-1) for k in small_names]
        flat = jnp.concatenate(arrs + [jnp.zeros((lpad - sum(lsizes),), F32)])
        return flat.reshape(lpad // LANES, LANES)

    dlt, mn, vn = _adamw(lpack(small_g), lpack(wts), lpack(mom_m), lpack(mom_v), "adamw_small")
    off = 0
    for k, sz in zip(small_names, lsizes):
        shp = wts[k].shape
        grad_w[k] = small_g[k]
        delta_w[k] = dlt.reshape(-1)[off:off + sz].reshape(shp)
        new_m[k] = mn.reshape(-1)[off:off + sz].reshape(shp)
        new_v[k] = vn.reshape(-1)[off:off + sz].reshape(shp)
        off += sz

    return (loss, gx.reshape(bsz, t_len, d), *[grad_w[k] for k in names], *[delta_w[k] for k in names],
            *[new_m[k] for k in names], *[new_v[k] for k in names])
```

```python
import jax
import jax.numpy as jnp
from jax import lax
from jax.experimental import pallas as pl
from jax.experimental.pallas import tpu as pltpu

F32 = jnp.float32
BF16 = jnp.bfloat16
MESH = pl.DeviceIdType.MESH
_call = pl.pallas_call

V7X_VMEM_LIMIT = 56 * 1024 * 1024
LANES = 128
SUBLANES = 8

RWKV_HEAD = 64
LRU_BLOCK_W = 128
CONV_WIDTH = 4
LRU_C = 8.0
NORM_EPS = 1e-6
GN_EPS = 64e-5
KK_EPS = 1e-24
SCAN_CHUNK = 16

ADAM_LR = 0.001
ADAM_B1 = 0.9
ADAM_B2 = 0.999
ADAM_EPS = 1e-08
ADAM_WD = 0.01
ADAM_STEP = 10
_BC1 = 1.0 - ADAM_B1 ** ADAM_STEP
_BC2 = 1.0 - ADAM_B2 ** ADAM_STEP


def _cp(*sem):
    return pltpu.CompilerParams(dimension_semantics=tuple(sem), vmem_limit_bytes=V7X_VMEM_LIMIT)


def _tile(n, cap, unit=LANES):
    if n <= cap:
        return n
    best = None
    d = unit
    while d <= cap:
        if n % d == 0:
            best = d
        d += unit
    return n if best is None else best


def _ceil_to(n, m):
    return -(-n // m) * m


ELEMENTWISE_BLOCK_BYTES = 3 * 512 * 1024


def _col_tile(rows, cols):
    cap = max(LANES, ELEMENTWISE_BLOCK_BYTES // (4 * rows) // LANES * LANES)
    return _tile(cols, cap)


def _sig(x):
    return 1.0 / (1.0 + jnp.exp(-x))


def _log1p(x):
    return jnp.where(x < 0.01, x * (1.0 - x * (0.5 - x * (1.0 / 3.0))), jnp.log(1.0 + x))


def _softplus(x):
    return jnp.maximum(x, 0.0) + _log1p(jnp.exp(-jnp.abs(x)))


def _neg_expm1(x):
    small = -x * (1.0 + x * (0.5 + x * (1.0 / 6.0)))
    return jnp.where(x > -0.01, small, 1.0 - jnp.exp(x))


_GELU_K = 0.7978845608028654
_GELU_C = 0.044715


def _gelu_parts(x):
    th = jnp.tanh(_GELU_K * (x + _GELU_C * x * x * x))
    return 0.5 * x * (1.0 + th), th


def _gelu_grad(x, th):
    return 0.5 * (1.0 + th) + 0.5 * x * (1.0 - th * th) * _GELU_K * (1.0 + 3.0 * _GELU_C * x * x)


def _shift_down(x, prev8, j):
    tb = x.shape[0]
    xr = pltpu.roll(x, j, 0)
    pr = pltpu.roll(prev8, j, 0)
    row = lax.broadcasted_iota(jnp.int32, prev8.shape, 0)
    first = jnp.where(row < j, pr, xr[0:SUBLANES])
    if tb == SUBLANES:
        return first
    return jnp.concatenate([first, xr[SUBLANES:]], axis=0)


def _shift_up(x, next8, j):
    tb = x.shape[0]
    xr = pltpu.roll(x, tb - j, 0)
    nr = pltpu.roll(next8, SUBLANES - j, 0)
    row = lax.broadcasted_iota(jnp.int32, next8.shape, 0)
    last = jnp.where(row >= SUBLANES - j, nr, xr[tb - SUBLANES:])
    if tb == SUBLANES:
        return last
    return jnp.concatenate([xr[:tb - SUBLANES], last], axis=0)


def _head_mats(width, heads_pad):
    e = (lax.broadcasted_iota(jnp.int32, (width, heads_pad), 0) // RWKV_HEAD
         == lax.broadcasted_iota(jnp.int32, (width, heads_pad), 1)).astype(BF16)
    et = (lax.broadcasted_iota(jnp.int32, (heads_pad, width), 1) // RWKV_HEAD
          == lax.broadcasted_iota(jnp.int32, (heads_pad, width), 0)).astype(BF16)
    return e, et


def _dot_exact01(x, m):
    hi = x.astype(BF16)
    r1 = x - hi.astype(F32)
    mid = r1.astype(BF16)
    lo = (r1 - mid.astype(F32)).astype(BF16)
    return (jnp.dot(lo, m, preferred_element_type=F32) + jnp.dot(mid, m, preferred_element_type=F32)
            + jnp.dot(hi, m, preferred_element_type=F32))


def _headsum(x, e, et):
    return _dot_exact01(_dot_exact01(x, e), et)


def _dot(a, b):
    return jnp.dot(a.astype(BF16), b.astype(BF16), preferred_element_type=F32)


def _dot_tn(a, b):
    return lax.dot_general(a.astype(BF16), b.astype(BF16), (((0,), (0,)), ((), ())), preferred_element_type=F32)


def _dot_nt(a, b):
    return lax.dot_general(a.astype(BF16), b.astype(BF16), (((1,), (1,)), ((), ())), preferred_element_type=F32)


def _mm(a, b, *, name, ta=False, tb=False, out_dtype=F32, res=None, n_outer=False, caps=(1024, 512, 4096),
        comm=None, epi=None, resident=None):
    m = a.shape[1] if ta else a.shape[0]
    kd = a.shape[0] if ta else a.shape[1]
    n = b.shape[0] if tb else b.shape[1]
    assert kd == (b.shape[1] if tb else b.shape[0])
    tm, tn, tk = _tile(m, caps[0]), _tile(n, caps[1]), _tile(kd, caps[2])
    gm, gn, gk = m // tm, n // tn, kd // tk
    dims = (((0 if ta else 1,), (1 if tb else 0,)), ((), ()))
    grid = (gn, gm, gk) if n_outer else (gm, gn, gk)
    cx, cx_specs, c_outs, c_sems, c_start, c_finish = _carried(comm)
    epi_fn, epi_ins, out_dtypes = epi if epi is not None else (None, [], [out_dtype])
    n_epi, n_out = len(epi_ins), len(out_dtypes)
    nx, n_in = len(cx), (3 if res is not None else 2) + n_epi

    def ij(g0, g1):
        return (g1, g0) if n_outer else (g0, g1)

    def a_map(g0, g1, k):
        i, _ = ij(g0, g1)
        return (k, i) if ta else (i, k)

    def b_map(g0, g1, k):
        _, j = ij(g0, g1)
        return (j, k) if tb else (k, j)

    def o_map(g0, g1, k):
        return ij(g0, g1)

    has_res = res is not None

    def body(*refs):
        a_ref, b_ref = refs[0], refs[1]
        res_ref = refs[2] if has_res else None
        epi_refs = refs[n_in - n_epi:n_in]
        c_ins = refs[n_in:n_in + nx]
        o_refs = refs[n_in + nx:n_in + nx + n_out]
        c_out_refs = refs[n_in + nx + n_out:n_in + nx + n_out + len(c_outs)]
        acc_ref = refs[n_in + nx + n_out + len(c_outs)] if gk > 1 else None
        steps = [pl.program_id(ax) for ax in range(3)]
        if nx:
            @pl.when(jnp.logical_and(jnp.logical_and(steps[0] == 0, steps[1] == 0), steps[2] == 0))
            def _():
                c_start(c_ins, c_out_refs, refs[-2], refs[-1])

        prod = lax.dot_general(a_ref[...], b_ref[...], dims, preferred_element_type=F32)

        def finish(acc):
            if has_res:
                acc = acc + res_ref[...]
            tiles = [acc] if epi_fn is None else epi_fn(acc, *[e_ref[...] for e_ref in epi_refs])
            for o_ref, tile, dt in zip(o_refs, tiles, out_dtypes):
                o_ref[...] = tile.astype(dt)

        if gk == 1:
            finish(prod)
        else:
            k = steps[2]

            @pl.when(k == 0)
            def _():
                acc_ref[...] = prod

            @pl.when(k > 0)
            def _():
                acc_ref[...] += prod

            @pl.when(k == gk - 1)
            def _():
                finish(acc_ref[...])

        if nx:
            @pl.when(jnp.logical_and(jnp.logical_and(steps[0] == grid[0] - 1, steps[1] == grid[1] - 1),
                                     steps[2] == grid[2] - 1))
            def _():
                c_finish(c_ins, c_out_refs, refs[-2], refs[-1])

    one = pl.Buffered(1)
    in_specs = [pl.BlockSpec((tk, tm) if ta else (tm, tk), a_map, pipeline_mode=one if resident == "a" else None),
                pl.BlockSpec((tn, tk) if tb else (tk, tn), b_map, pipeline_mode=one if resident == "b" else None)]
    args = [a, b]
    for extra in ([res] if has_res else []) + list(epi_ins):
        in_specs.append(pl.BlockSpec((tm, tn), o_map))
        args.append(extra)
    out = _call(
        body, name=name, grid=grid, in_specs=in_specs + cx_specs,
        out_specs=[pl.BlockSpec((tm, tn), o_map)] * n_out + [_HBM] * len(c_outs),
        out_shape=[jax.ShapeDtypeStruct((m, n), dt) for dt in out_dtypes] + c_outs,
        scratch_shapes=([pltpu.VMEM((tm, tn), F32)] if gk > 1 else []) + c_sems,
        compiler_params=_cp(*(("arbitrary",) * 3 if nx else ("parallel", "parallel", "arbitrary"))),
    )(*args, *cx)
    return out if (nx or epi is not None) else out[0]


def _rmsnorm_fwd(x, g, name):
    n, d = x.shape
    tb = _tile(n, 256, SUBLANES)

    def body(x_ref, g_ref, u_ref):
        xv = x_ref[...]
        rstd = lax.rsqrt(jnp.mean(xv * xv, axis=-1, keepdims=True) + NORM_EPS)
        u_ref[...] = (xv * rstd * g_ref[...]).astype(BF16)

    row = pl.BlockSpec((tb, d), lambda i: (i, 0))
    vec = pl.BlockSpec((1, d), lambda i: (0, 0))
    return _call(body, name=name, grid=(n // tb,), in_specs=[row, vec], out_specs=row,
                 out_shape=jax.ShapeDtypeStruct((n, d), BF16), compiler_params=_cp("parallel"))(x, g)


def _rmsnorm_bwd(du, x, g, dres, name):
    n, d = x.shape
    tb = _tile(n, 256, SUBLANES)

    def body(du_ref, x_ref, g_ref, dres_ref, dx_ref, dxb_ref, dg_ref):
        xv = x_ref[...]
        rstd = lax.rsqrt(jnp.mean(xv * xv, axis=-1, keepdims=True) + NORM_EPS)
        xh = xv * rstd
        duv = du_ref[...]
        t = duv * g_ref[...]
        dx = dres_ref[...] + rstd * (t - xh * jnp.mean(t * xh, axis=-1, keepdims=True))
        dx_ref[...] = dx
        dxb_ref[...] = dx.astype(BF16)

        @pl.when(pl.program_id(0) == 0)
        def _():
            dg_ref[...] = jnp.zeros_like(dg_ref)

        dg_ref[...] += jnp.sum(duv * xh, axis=0, keepdims=True)

    row = pl.BlockSpec((tb, d), lambda i: (i, 0))
    vec = pl.BlockSpec((1, d), lambda i: (0, 0))
    return _call(body, name=name, grid=(n // tb,), in_specs=[row, row, vec, row], out_specs=[row, row, vec],
                 out_shape=[jax.ShapeDtypeStruct((n, d), F32), jax.ShapeDtypeStruct((n, d), BF16),
                            jax.ShapeDtypeStruct((1, d), F32)],
                 compiler_params=_cp("arbitrary"))(du, x, g, dres)


def _loss_head(h, g, target, name):
    n, d = h.shape
    tb = _tile(n, 256, SUBLANES)

    def body(h_ref, g_ref, t_ref, dh_ref, dhb_ref, dg_ref, loss_ref):
        hv = h_ref[...]
        gv = g_ref[...]
        rstd = lax.rsqrt(jnp.mean(hv * hv, axis=-1, keepdims=True) + NORM_EPS)
        hh = hv * rstd
        err = hh * gv - t_ref[...]
        dy = err * (1.0 / d)
        dhh = dy * gv
        dh = rstd * (dhh - hh * jnp.mean(dhh * hh, axis=-1, keepdims=True))
        dh_ref[...] = dh
        dhb_ref[...] = dh.astype(BF16)

        @pl.when(pl.program_id(0) == 0)
        def _():
            dg_ref[...] = jnp.zeros_like(dg_ref)
            loss_ref[...] = jnp.zeros_like(loss_ref)

        dg_ref[...] += jnp.sum(dy * hh, axis=0, keepdims=True)
        loss_ref[...] += jnp.sum(err * err) * (0.5 / d)

    row = pl.BlockSpec((tb, d), lambda i: (i, 0))
    vec = pl.BlockSpec((1, d), lambda i: (0, 0))
    lvec = pl.BlockSpec((1, LANES), lambda i: (0, 0))
    return _call(body, name=name, grid=(n // tb,), in_specs=[row, vec, row], out_specs=[row, row, vec, lvec],
                 out_shape=[jax.ShapeDtypeStruct((n, d), F32), jax.ShapeDtypeStruct((n, d), BF16),
                            jax.ShapeDtypeStruct((1, d), F32), jax.ShapeDtypeStruct((1, LANES), F32)],
                 compiler_params=_cp("arbitrary"))(h, g, target)


def _swiglu_tile(up, gate):
    return [up, gate * _sig(gate) * up]


def _swiglu_bwd_tile(dact, gate, up):
    s = _sig(gate)
    return [dact * up * s * (1.0 + gate * (1.0 - s)), dact * gate * s]


def _prep_common(prkv_ref, prkvp_ref, plo_ref, plop_ref, mur_ref, mul_ref, w0_ref, a0_ref, kk_ref, ka_ref,
                 w2_ref, a2_ref, g2_ref, seq_start, w, dlp, alp):
    z8r = jnp.zeros_like(prkvp_ref[...])
    z8l = jnp.zeros_like(plop_ref[...])
    prev_r = jnp.where(seq_start, z8r, prkvp_ref[...])
    prev_l = jnp.where(seq_start, z8l, plop_ref[...])
    p_r = prkv_ref[...]
    p_l = plo_ref[...]
    dif_r = _shift_down(p_r, prev_r, 1) - p_r
    dif_l = _shift_down(p_l, prev_l, 1) - p_l
    q_r = p_r + dif_r * mur_ref[...]
    q_l = p_l + dif_l * mul_ref[...]
    r, k, v = q_r[:, 0:w], q_r[:, w:2 * w], q_r[:, 2 * w:3 * w]
    wd, ad, gd = q_l[:, 0:dlp], q_l[:, dlp:dlp + alp], q_l[:, dlp + alp:]
    tw = jnp.tanh(wd)
    zw = w0_ref[...] + _dot(tw, w2_ref[...])
    wlog = -_softplus(-zw) - 0.5
    ew = jnp.exp(wlog)
    dec = jnp.exp(-ew)
    za = a0_ref[...] + _dot(ad, a2_ref[...])
    av = _sig(za)
    sg = _sig(gd)
    g = _dot(sg, g2_ref[...])
    return dict(dif_r=dif_r, dif_l=dif_l, r=r, k=k, v=v, ad=ad, tw=tw, zw=zw, ew=ew, dec=dec, av=av, sg=sg, g=g)


def _rwkv_prep_specs(n, tb, w, lp, t_len):
    nb8 = tb // SUBLANES
    row3 = pl.BlockSpec((tb, 3 * w), lambda i: (i, 0))
    prev3 = pl.BlockSpec((SUBLANES, 3 * w), lambda i: (jnp.maximum(i * nb8 - 1, 0), 0))
    rowl = pl.BlockSpec((tb, lp), lambda i: (i, 0))
    prevl = pl.BlockSpec((SUBLANES, lp), lambda i: (jnp.maximum(i * nb8 - 1, 0), 0))
    return row3, prev3, rowl, prevl


def _rwkv_prep_fwd(p_rkv, p_lora, prm, t_len, name):
    n, w3 = p_rkv.shape
    w = w3 // 3
    lp = p_lora.shape[1]
    dlp, alp = prm["w2"].shape[0], prm["a2"].shape[0]
    glp = lp - dlp - alp
    hp = max(w // RWKV_HEAD, LANES)
    tb = _tile(min(n, t_len), 128, SUBLANES)
    bps = t_len // tb

    def body(prkv_ref, prkvp_ref, plo_ref, plop_ref, mur_ref, mul_ref, w0_ref, a0_ref, kk_ref, ka_ref,
             w2_ref, a2_ref, g2_ref, r_o, dec_o, k_o, v_o, na_o, nb_o, g_o):
        seq_start = (pl.program_id(0) % bps) == 0
        f = _prep_common(prkv_ref, prkvp_ref, plo_ref, plop_ref, mur_ref, mul_ref, w0_ref, a0_ref, kk_ref, ka_ref,
                         w2_ref, a2_ref, g2_ref, seq_start, w, dlp, alp)
        e, et = _head_mats(w, hp)
        kk0 = f["k"] * kk_ref[...]
        inv = lax.rsqrt(jnp.maximum(_headsum(kk0 * kk0, e, et), KK_EPS))
        kk = kk0 * inv
        r_o[...] = f["r"]
        dec_o[...] = f["dec"]
        k_o[...] = f["k"] * (1.0 + (f["av"] - 1.0) * ka_ref[...])
        v_o[...] = f["v"]
        na_o[...] = -kk
        nb_o[...] = kk * f["av"]
        g_o[...] = f["g"]

    row3, prev3, rowl, prevl = _rwkv_prep_specs(n, tb, w, lp, t_len)
    c0 = lambda i: (0, 0)
    vec3 = pl.BlockSpec((1, 3 * w), c0)
    vecl = pl.BlockSpec((1, lp), c0)
    vec = pl.BlockSpec((1, w), c0)
    out = pl.BlockSpec((tb, w), lambda i: (i, 0))
    return _call(
        body, name=name, grid=(n // tb,),
        in_specs=[row3, prev3, rowl, prevl, vec3, vecl, vec, vec, vec, vec,
                  pl.BlockSpec((dlp, w), c0), pl.BlockSpec((alp, w), c0), pl.BlockSpec((glp, w), c0)],
        out_specs=[out] * 7, out_shape=[jax.ShapeDtypeStruct((n, w), F32)] * 7,
        compiler_params=_cp("parallel"),
    )(p_rkv, p_rkv, p_lora, p_lora, prm["mu_rkv"], prm["mu_lora"], prm["w0"], prm["a0"], prm["k_k"], prm["k_a"],
      prm["w2"], prm["a2"], prm["g2"])


def _rwkv_prep_bwd(p_rkv, p_lora, prm, grads, t_len, name):
    n, w3 = p_rkv.shape
    w = w3 // 3
    lp = p_lora.shape[1]
    dlp, alp = prm["w2"].shape[0], prm["a2"].shape[0]
    glp = lp - dlp - alp
    hp = max(w // RWKV_HEAD, LANES)
    tb = _tile(min(n, t_len), 64, SUBLANES)
    bps = t_len // tb

    def body(prkv_ref, prkvp_ref, plo_ref, plop_ref, mur_ref, mul_ref, w0_ref, a0_ref, kk_ref, ka_ref,
             w2_ref, a2_ref, g2_ref,
             drs_ref, drp_ref, ddec_ref, dks_ref, dkp_ref, dvs_ref, dvp_ref, dna_ref, dnb_ref, dg_ref,
             dqr_o, dql_o, dmur_o, dmul_o, dw0_o, da0_o, dkk_o, dka_o, dw2_o, da2_o, dg2_o):
        seq_start = (pl.program_id(0) % bps) == 0
        f = _prep_common(prkv_ref, prkvp_ref, plo_ref, plop_ref, mur_ref, mul_ref, w0_ref, a0_ref, kk_ref, ka_ref,
                         w2_ref, a2_ref, g2_ref, seq_start, w, dlp, alp)
        e, et = _head_mats(w, hp)
        k, av = f["k"], f["av"]
        k_k, k_a = kk_ref[...], ka_ref[...]
        kk0 = k * k_k
        n2 = _headsum(kk0 * kk0, e, et)
        inv = lax.rsqrt(jnp.maximum(n2, KK_EPS))
        kk = kk0 * inv
        dk2 = dks_ref[...] + dkp_ref[...]
        dnb = dnb_ref[...]
        dkk = dnb * av - dna_ref[...]
        dav = dnb * kk + dk2 * k * k_a
        dk = dk2 * (1.0 + (av - 1.0) * k_a)
        dka = dk2 * k * (av - 1.0)
        proj = jnp.where(n2 > KK_EPS, _headsum(dkk * kk, e, et), 0.0)
        dkk0 = inv * (dkk - kk * proj)
        dk = dk + dkk0 * k_k
        dkkp = dkk0 * k
        dgv = dg_ref[...]
        sg = f["sg"]
        dgd = _dot_nt(dgv, g2_ref[...]) * sg * (1.0 - sg)
        dza = dav * av * (1.0 - av)
        dad = _dot_nt(dza, a2_ref[...])
        dzw = ddec_ref[...] * f["dec"] * (-f["ew"]) * _sig(-f["zw"])
        tw = f["tw"]
        dwd = _dot_nt(dzw, w2_ref[...]) * (1.0 - tw * tw)
        dq_r = jnp.concatenate([drs_ref[...] + drp_ref[...], dk, dvs_ref[...] + dvp_ref[...]], axis=1)
        dq_l = jnp.concatenate([dwd, dad, dgd], axis=1)
        dqr_o[...] = dq_r
        dql_o[...] = dq_l

        @pl.when(pl.program_id(0) == 0)
        def _():
            for o in (dmur_o, dmul_o, dw0_o, da0_o, dkk_o, dka_o, dw2_o, da2_o, dg2_o):
                o[...] = jnp.zeros_like(o)

        def rsum(x):
            return jnp.sum(x, axis=0, keepdims=True)

        dmur_o[...] += rsum(dq_r * f["dif_r"])
        dmul_o[...] += rsum(dq_l * f["dif_l"])
        dw0_o[...] += rsum(dzw)
        da0_o[...] += rsum(dza)
        dkk_o[...] += rsum(dkkp)
        dka_o[...] += rsum(dka)
        dw2_o[...] += _dot_tn(tw, dzw)
        da2_o[...] += _dot_tn(f["ad"], dza)
        dg2_o[...] += _dot_tn(sg, dgv)

    row3, prev3, rowl, prevl = _rwkv_prep_specs(n, tb, w, lp, t_len)
    c0 = lambda i: (0, 0)
    vec3 = pl.BlockSpec((1, 3 * w), c0)
    vecl = pl.BlockSpec((1, lp), c0)
    vec = pl.BlockSpec((1, w), c0)
    blk = pl.BlockSpec((tb, w), lambda i: (i, 0))
    m2, ma, mg = pl.BlockSpec((dlp, w), c0), pl.BlockSpec((alp, w), c0), pl.BlockSpec((glp, w), c0)
    sds = jax.ShapeDtypeStruct
    return _call(
        body, name=name, grid=(n // tb,),
        in_specs=[row3, prev3, rowl, prevl, vec3, vecl, vec, vec, vec, vec, m2, ma, mg] + [blk] * 10,
        out_specs=[row3, rowl, vec3, vecl, vec, vec, vec, vec, m2, ma, mg],
        out_shape=[sds((n, 3 * w), F32), sds((n, lp), F32), sds((1, 3 * w), F32), sds((1, lp), F32),
                   sds((1, w), F32), sds((1, w), F32), sds((1, w), F32), sds((1, w), F32),
                   sds((dlp, w), F32), sds((alp, w), F32), sds((glp, w), F32)],
        compiler_params=_cp("arbitrary"),
    )(p_rkv, p_rkv, p_lora, p_lora, prm["mu_rkv"], prm["mu_lora"], prm["w0"], prm["a0"], prm["k_k"], prm["k_a"],
      prm["w2"], prm["a2"], prm["g2"], *grads)


def _shift_combine(dq, mu, t_len, name):
    n, c = dq.shape
    tb = _tile(min(n, t_len), 256, SUBLANES)
    bps = t_len // tb
    nb8 = tb // SUBLANES
    last8 = n // SUBLANES - 1

    def body(x_ref, nx_ref, mu_ref, o_ref):
        seq_end = (pl.program_id(0) % bps) == bps - 1
        nxt = jnp.where(seq_end, jnp.zeros_like(nx_ref[...]), nx_ref[...])
        x = x_ref[...]
        muv = mu_ref[...]
        o_ref[...] = ((1.0 - muv) * x + muv * _shift_up(x, nxt, 1)).astype(BF16)

    row = pl.BlockSpec((tb, c), lambda i: (i, 0))
    nxt = pl.BlockSpec((SUBLANES, c), lambda i: (jnp.minimum((i + 1) * nb8, last8), 0))
    vec = pl.BlockSpec((1, c), lambda i: (0, 0))
    return _call(body, name=name, grid=(n // tb,), in_specs=[row, nxt, vec], out_specs=row,
                 out_shape=jax.ShapeDtypeStruct((n, c), BF16), compiler_params=_cp("parallel"))(dq, dq, mu)


def _scan_step(s_i, a_t, w_t, b_t, k_t, v_i):
    sa = jnp.sum(s_i * a_t, axis=0, keepdims=True)
    return s_i * w_t + sa * b_t + v_i * k_t, sa


def _carried(comm):
    if comm is None:
        return [], [], [], [], None, None
    outs, n_sems, start, finish = _comm_plan(*comm)
    xs = list(comm[1])
    sems = [pltpu.SemaphoreType.DMA((n_sems,)), pltpu.SemaphoreType.DMA((n_sems,))]
    return xs, [_HBM] * len(xs), outs, sems, start, finish


def _rwkv_scan_fwd(r, w, k, a, b, v, name, comm=None):
    t_len, kd, ln = r.shape
    vh = v.shape[1]
    tc = SCAN_CHUNK
    nc = t_len // tc
    cx, cx_specs, c_outs, c_sems, c_start, c_finish = _carried(comm)
    nx = len(cx)

    def body(r_ref, w_ref, k_ref, a_ref, b_ref, v_ref, *rest):
        c_ins, (y_ref, st_ref, sa_ref), c_out_refs = rest[:nx], rest[nx:nx + 3], rest[nx + 3:nx + 3 + len(c_outs)]
        s_ref = rest[nx + 3 + len(c_outs)]

        @pl.when(pl.program_id(0) == 0)
        def _():
            s_ref[...] = jnp.zeros_like(s_ref)
            if nx:
                c_start(c_ins, c_out_refs, rest[-2], rest[-1])

        st_ref[0, 0] = s_ref[...]

        def step(t, carry):
            a_t, w_t, b_t, k_t, r_t = a_ref[t], w_ref[t], b_ref[t], k_ref[t], r_ref[t]
            for i in range(vh):
                s_new, sa = _scan_step(st_ref[0, t, i], a_t, w_t, b_t, k_t, v_ref[t, pl.ds(i, 1), :])
                st_ref[0, t + 1, i] = s_new
                sa_ref[t, pl.ds(i, 1), :] = sa
                y_ref[t, pl.ds(i, 1), :] = jnp.sum(s_new * r_t, axis=0, keepdims=True)
            return carry

        lax.fori_loop(0, tc, step, 0)
        s_ref[...] = st_ref[0, tc]

        if nx:
            @pl.when(pl.program_id(0) == nc - 1)
            def _():
                c_finish(c_ins, c_out_refs, rest[-2], rest[-1])

    kblk = pl.BlockSpec((tc, kd, ln), lambda c: (c, 0, 0))
    vblk = pl.BlockSpec((tc, vh, ln), lambda c: (c, 0, 0))
    vsd = jax.ShapeDtypeStruct((t_len, vh, ln), F32)
    return _call(
        body, name=name, grid=(nc,), in_specs=[kblk] * 5 + [vblk] + cx_specs,
        out_specs=[vblk, pl.BlockSpec((1, tc + 1, vh, kd, ln), lambda c: (c, 0, 0, 0, 0)), vblk] + [_HBM] * len(c_outs),
        out_shape=[vsd, jax.ShapeDtypeStruct((nc, tc + 1, vh, kd, ln), F32), vsd] + c_outs,
        scratch_shapes=[pltpu.VMEM((vh, kd, ln), F32)] + c_sems,
        compiler_params=_cp("arbitrary"),
    )(r, w, k, a, b, v, *cx)


def _rwkv_scan_bwd(r, w, k, a, b, v, dy, states, sa, name, comm=None):
    t_len, kd, ln = r.shape
    vh = v.shape[1]
    tc = SCAN_CHUNK
    nc = t_len // tc
    half = ln // 2
    cx, cx_specs, c_outs, c_sems, c_start, c_finish = _carried(comm)
    nx = len(cx)

    def body(r_ref, w_ref, k_ref, a_ref, b_ref, v_ref, dy_ref, st_ref, sa_ref, *rest):
        c_ins = rest[:nx]
        dr_o, dw_o, dk_o, da_o, db_o, dv_o = rest[nx:nx + 6]
        c_out_refs = rest[nx + 6:nx + 6 + len(c_outs)]
        ds_ref = rest[nx + 6 + len(c_outs)]

        @pl.when(pl.program_id(0) == 0)
        def _():
            ds_ref[...] = jnp.zeros_like(ds_ref)
            if nx:
                c_start(c_ins, c_out_refs, rest[-2], rest[-1])

        def bwd(tt, carry):
            t = tc - 1 - tt
            a_t, w_t, b_t, k_t, r_t = a_ref[t], w_ref[t], b_ref[t], k_ref[t], r_ref[t]
            z = jnp.zeros((kd, ln), F32)
            dr, dw, dk, da, db = z, z, z, z, z
            for i in range(vh):
                dy_i = dy_ref[t, pl.ds(i, 1), :]
                s_t = st_ref[0, t + 1, i]
                s_p = st_ref[0, t, i]
                d = ds_ref[i] + dy_i * r_t
                dr = dr + s_t * dy_i
                dv_o[t, pl.ds(i, 1), :] = jnp.sum(d * k_t, axis=0, keepdims=True)
                dk = dk + d * v_ref[t, pl.ds(i, 1), :]
                dsa = jnp.sum(d * b_t, axis=0, keepdims=True)
                db = db + d * sa_ref[t, pl.ds(i, 1), :]
                dw = dw + d * s_p
                da = da + s_p * dsa
                ds_ref[i] = d * w_t + dsa * a_t

            def both(x):
                return x + pltpu.roll(x, half, 1)

            dr_o[t] = both(dr)
            dw_o[t] = both(dw)
            dk_o[t] = both(dk)
            da_o[t] = both(da)
            db_o[t] = both(db)
            return carry

        lax.fori_loop(0, tc, bwd, 0)

        if nx:
            @pl.when(pl.program_id(0) == nc - 1)
            def _():
                c_finish(c_ins, c_out_refs, rest[-2], rest[-1])

    kblk = pl.BlockSpec((tc, kd, ln), lambda c: (nc - 1 - c, 0, 0))
    vblk = pl.BlockSpec((tc, vh, ln), lambda c: (nc - 1 - c, 0, 0))
    ksd = jax.ShapeDtypeStruct((t_len, kd, ln), F32)
    return _call(
        body, name=name, grid=(nc,),
        in_specs=[kblk] * 5 + [vblk, vblk, pl.BlockSpec((1, tc + 1, vh, kd, ln), lambda c: (nc - 1 - c, 0, 0, 0, 0)),
                  vblk] + cx_specs,
        out_specs=[kblk] * 5 + [vblk] + [_HBM] * len(c_outs),
        out_shape=[ksd] * 5 + [jax.ShapeDtypeStruct((t_len, vh, ln), F32)] + c_outs,
        scratch_shapes=[pltpu.VMEM((vh, kd, ln), F32)] + c_sems,
        compiler_params=_cp("arbitrary"),
    )(r, w, k, a, b, v, dy, states, sa, *cx)


def _post_common(y_ref, r_ref, k_ref, v_ref, lng_ref, lnb_ref, rk_ref, e, et):
    y = y_ref[...]
    inv_n = 1.0 / RWKV_HEAD
    mean = _headsum(y, e, et) * inv_n
    yc = y - mean
    var = _headsum(yc * yc, e, et) * inv_n
    rstd = lax.rsqrt(var + GN_EPS)
    yh = yc * rstd
    yn = yh * lng_ref[...] + lnb_ref[...]
    bonus = _headsum(r_ref[...] * k_ref[...] * rk_ref[...], e, et)
    return yh, rstd, yn, bonus


def _rwkv_post_fwd(y, r, k, v, g, ln_g, ln_b, r_k, name):
    n, w = y.shape
    hp = max(w // RWKV_HEAD, LANES)
    tb = _tile(n, 256, SUBLANES)

    def body(y_ref, r_ref, k_ref, v_ref, g_ref, lng_ref, lnb_ref, rk_ref, o_ref):
        e, et = _head_mats(w, hp)
        _, _, yn, bonus = _post_common(y_ref, r_ref, k_ref, v_ref, lng_ref, lnb_ref, rk_ref, e, et)
        o_ref[...] = ((yn + bonus * v_ref[...]) * g_ref[...]).astype(BF16)

    blk = pl.BlockSpec((tb, w), lambda i: (i, 0))
    vec = pl.BlockSpec((1, w), lambda i: (0, 0))
    return _call(body, name=name, grid=(n // tb,), in_specs=[blk] * 5 + [vec] * 3, out_specs=blk,
                 out_shape=jax.ShapeDtypeStruct((n, w), BF16),
                 compiler_params=_cp("parallel"))(y, r, k, v, g, ln_g, ln_b, r_k)


def _rwkv_post_bwd(y, r, k, v, g, ln_g, ln_b, r_k, do_cat, name):
    n, w = y.shape
    hp = max(w // RWKV_HEAD, LANES)
    tb = _tile(n, 128, SUBLANES)

    def body(y_ref, r_ref, k_ref, v_ref, g_ref, lng_ref, lnb_ref, rk_ref, do_ref,
             dy_o, dr_o, dk_o, dv_o, dg_o, dlng_o, dlnb_o, drk_o):
        e, et = _head_mats(w, hp)
        yh, rstd, yn, bonus = _post_common(y_ref, r_ref, k_ref, v_ref, lng_ref, lnb_ref, rk_ref, e, et)
        do = do_ref[...]
        vv, rv, kv, rk = v_ref[...], r_ref[...], k_ref[...], rk_ref[...]
        dg_o[...] = do * (yn + bonus * vv)
        dz = do * g_ref[...]
        dbonus = _headsum(dz * vv, e, et)
        dv_o[...] = dz * bonus
        dr_o[...] = dbonus * kv * rk
        dk_o[...] = dbonus * rv * rk
        dyh = dz * lng_ref[...]
        inv_n = 1.0 / RWKV_HEAD
        dy_o[...] = rstd * (dyh - _headsum(dyh, e, et) * inv_n - yh * (_headsum(dyh * yh, e, et) * inv_n))

        @pl.when(pl.program_id(0) == 0)
        def _():
            for o in (dlng_o, dlnb_o, drk_o):
                o[...] = jnp.zeros_like(o)

        dlng_o[...] += jnp.sum(dz * yh, axis=0, keepdims=True)
        dlnb_o[...] += jnp.sum(dz, axis=0, keepdims=True)
        drk_o[...] += jnp.sum(dbonus * rv * kv, axis=0, keepdims=True)

    blk = pl.BlockSpec((tb, w), lambda i: (i, 0))
    vec = pl.BlockSpec((1, w), lambda i: (0, 0))
    sds = jax.ShapeDtypeStruct
    return _call(body, name=name, grid=(n // tb,), in_specs=[blk] * 5 + [vec] * 3 + [blk],
                 out_specs=[blk] * 5 + [vec] * 3,
                 out_shape=[sds((n, w), F32)] * 5 + [sds((1, w), F32)] * 3,
                 compiler_params=_cp("arbitrary"))(y, r, k, v, g, ln_g, ln_b, r_k, do_cat)


def _lru_gates(xb, prev8, gate, cw_ref, cb_ref, wr_ref, br_ref, wi_ref, bi_ref, lam_ref, is_t0):
    c = xb.shape[1]
    nblk = c // LRU_BLOCK_W
    xs = [xb] + [_shift_down(xb, prev8, j) for j in range(1, CONV_WIDTH)]
    xc = cb_ref[...]
    for j in range(CONV_WIDTH):
        xc = xc + xs[CONV_WIDTH - 1 - j] * cw_ref[pl.ds(j, 1), :]
    xcb = xc.astype(BF16)

    def blockmm(w_ref):
        return jnp.concatenate(
            [jnp.dot(xcb[:, h * LRU_BLOCK_W:(h + 1) * LRU_BLOCK_W], w_ref[h], preferred_element_type=F32)
             for h in range(nblk)], axis=1)

    rg = _sig(blockmm(wr_ref) + br_ref[...])
    ig = _sig(blockmm(wi_ref) + bi_ref[...])
    sp = _softplus(-lam_ref[...])
    la = -LRU_C * rg * sp
    av = jnp.exp(la)
    mult = jnp.where(is_t0, 1.0, jnp.sqrt(_neg_expm1(2.0 * la)))
    ge, th = _gelu_parts(gate)
    return dict(xs=xs, xc=xc, xcb=xcb, rg=rg, ig=ig, sp=sp, a=av, mult=mult, ge=ge, th=th)


def _lru_specs(tb, c, nb, rev):
    nb8 = tb // SUBLANES

    def blk_i(i):
        return nb - 1 - i if rev else i

    xb = pl.BlockSpec((tb, c), lambda b, i: (b * nb + blk_i(i), 0))
    gate = pl.BlockSpec((tb, c), lambda b, i: (b * nb + blk_i(i), 1))
    prev = pl.BlockSpec((SUBLANES, c), lambda b, i: (jnp.maximum((b * nb + blk_i(i)) * nb8 - 1, 0), 0))
    return xb, gate, prev


def _lru_fwd(p_lru, prm, t_len, name):
    n, c2 = p_lru.shape
    c = c2 // 2
    nblk = c // LRU_BLOCK_W
    tb = _tile(t_len, 256, SUBLANES)
    nb = t_len // tb
    bsz = n // t_len

    def body(xb_ref, gate_ref, prev_ref, cw_ref, cb_ref, wr_ref, br_ref, wi_ref, bi_ref, lam_ref, ng_ref,
             y_o, h_o, carry):
        i = pl.program_id(1)
        prev8 = jnp.where(i == 0, jnp.zeros_like(prev_ref[...]), prev_ref[...])
        row = lax.broadcasted_iota(jnp.int32, (tb, c), 0)
        f = _lru_gates(xb_ref[...], prev8, gate_ref[...], cw_ref, cb_ref, wr_ref, br_ref, wi_ref, bi_ref, lam_ref,
                       jnp.logical_and(i == 0, row == 0))
        acc_a = f["a"]
        acc_b = f["mult"] * f["ig"] * f["xc"]
        s = 1
        while s < tb:
            keep = row >= s
            a_sh = jnp.where(keep, pltpu.roll(acc_a, s, 0), 1.0)
            b_sh = jnp.where(keep, pltpu.roll(acc_b, s, 0), 0.0)
            acc_b = acc_a * b_sh + acc_b
            acc_a = acc_a * a_sh
            s *= 2

        @pl.when(i == 0)
        def _():
            carry[...] = jnp.zeros_like(carry)

        h = acc_b + acc_a * carry[0:1, :]
        carry[0:1, :] = h[tb - 1:tb, :]
        h_o[...] = h
        y = h * f["ge"]
        rstd = lax.rsqrt(jnp.mean(y * y, axis=-1, keepdims=True) + NORM_EPS)
        y_o[...] = (y * rstd * ng_ref[...]).astype(BF16)

    xb_s, gate_s, prev_s = _lru_specs(tb, c, nb, False)
    c0 = lambda b, i: (0, 0)
    vec = pl.BlockSpec((1, c), c0)
    wsp = pl.BlockSpec((nblk, LRU_BLOCK_W, LRU_BLOCK_W), lambda b, i: (0, 0, 0))
    out = pl.BlockSpec((tb, c), lambda b, i: (b * nb + i, 0))
    return _call(
        body, name=name, grid=(bsz, nb),
        in_specs=[xb_s, gate_s, prev_s, pl.BlockSpec((CONV_WIDTH, c), c0), vec, wsp, vec, wsp, vec, vec, vec],
        out_specs=[out, out],
        out_shape=[jax.ShapeDtypeStruct((n, c), BF16), jax.ShapeDtypeStruct((n, c), F32)],
        scratch_shapes=[pltpu.VMEM((SUBLANES, c), F32)],
        compiler_params=_cp("arbitrary", "arbitrary"),
    )(p_lru, p_lru, p_lru, prm["conv_w"], prm["conv_b"], prm["wr"], prm["br"], prm["wi"], prm["bi"],
      prm["lam"], prm["norm_g"])


def _lru_bwd(p_lru, h, do_cat, prm, t_len, name):
    n, c2 = p_lru.shape
    c = c2 // 2
    nblk = c // LRU_BLOCK_W
    tb = _tile(t_len, 128, SUBLANES)
    nb = t_len // tb
    bsz = n // t_len

    def body(xb_ref, gate_ref, prev_ref, h_ref, hprev_ref, do_ref,
             cw_ref, cb_ref, wr_ref, br_ref, wi_ref, bi_ref, lam_ref, ng_ref,
             dp_o, dcw_o, dcb_o, dwr_o, dbr_o, dwi_o, dbi_o, dlam_o, dng_o,
             a_next, g_next, dxc_next):
        b = pl.program_id(0)
        i = pl.program_id(1)
        blk = nb - 1 - i
        first = blk == 0
        prev8 = jnp.where(first, jnp.zeros_like(prev_ref[...]), prev_ref[...])
        hprev8 = jnp.where(first, jnp.zeros_like(hprev_ref[...]), hprev_ref[...])
        row = lax.broadcasted_iota(jnp.int32, (tb, c), 0)
        is_t0 = jnp.logical_and(first, row == 0)
        gate = gate_ref[...]
        f = _lru_gates(xb_ref[...], prev8, gate, cw_ref, cb_ref, wr_ref, br_ref, wi_ref, bi_ref, lam_ref, is_t0)

        @pl.when(i == 0)
        def _():
            a_next[...] = jnp.zeros_like(a_next)
            g_next[...] = jnp.zeros_like(g_next)
            dxc_next[...] = jnp.zeros_like(dxc_next)

        @pl.when(jnp.logical_and(b == 0, i == 0))
        def _():
            for o in (dcw_o, dcb_o, dwr_o, dbr_o, dwi_o, dbi_o, dlam_o, dng_o):
                o[...] = jnp.zeros_like(o)

        def rsum(x):
            return jnp.sum(x, axis=0, keepdims=True)

        hv = h_ref[...]
        hprev = _shift_down(hv, hprev8, 1)
        ge = f["ge"]
        y = hv * ge
        rstd = lax.rsqrt(jnp.mean(y * y, axis=-1, keepdims=True) + NORM_EPS)
        yh = y * rstd
        dyn = do_ref[...]
        t = dyn * ng_ref[...]
        dy = rstd * (t - yh * jnp.mean(t * yh, axis=-1, keepdims=True))
        dng_o[...] += rsum(dyn * yh)
        dgate = dy * hv * _gelu_grad(gate, f["th"])

        av = f["a"]
        acc_c = _shift_up(av, a_next[...], 1)
        acc_g = dy * ge
        s = 1
        while s < tb:
            keep = row < tb - s
            c_sh = jnp.where(keep, pltpu.roll(acc_c, tb - s, 0), 1.0)
            g_sh = jnp.where(keep, pltpu.roll(acc_g, tb - s, 0), 0.0)
            acc_g = acc_g + acc_c * g_sh
            acc_c = acc_c * c_sh
            s *= 2
        gtot = acc_g + acc_c * g_next[0:1, :]
        a_next[0:1, :] = av[0:1, :]
        g_next[0:1, :] = gtot[0:1, :]

        xc, ig, rg, mult = f["xc"], f["ig"], f["rg"], f["mult"]
        da = gtot * hprev
        dmult = gtot * ig * xc
        dig = gtot * mult * xc
        dxc = gtot * mult * ig
        da = da + jnp.where(is_t0, 0.0, -dmult * av / mult)
        dla = da * av
        drg = dla * (-LRU_C) * f["sp"]
        dlam_o[...] += rsum(dla * rg) * LRU_C * _sig(-lam_ref[...])
        dzr = drg * rg * (1.0 - rg)
        dzi = dig * ig * (1.0 - ig)
        dbr_o[...] += rsum(dzr)
        dbi_o[...] += rsum(dzi)
        dzrb, dzib = dzr.astype(BF16), dzi.astype(BF16)
        xcb = f["xcb"]
        back = []
        for hh in range(nblk):
            sl = slice(hh * LRU_BLOCK_W, (hh + 1) * LRU_BLOCK_W)
            dwr_o[hh] += _dot_tn(xcb[:, sl], dzrb[:, sl])
            dwi_o[hh] += _dot_tn(xcb[:, sl], dzib[:, sl])
            back.append(_dot_nt(dzrb[:, sl], wr_ref[hh]) + _dot_nt(dzib[:, sl], wi_ref[hh]))
        dxc = dxc + jnp.concatenate(back, axis=1)
        dcb_o[...] += rsum(dxc)
        xs = f["xs"]
        dcw_o[...] += jnp.concatenate([rsum(dxc * xs[CONV_WIDTH - 1 - j]) for j in range(CONV_WIDTH)], axis=0)
        nxt = dxc_next[...]
        dxb = dxc * cw_ref[pl.ds(CONV_WIDTH - 1, 1), :]
        for j in range(1, CONV_WIDTH):
            dxb = dxb + _shift_up(dxc, nxt, j) * cw_ref[pl.ds(CONV_WIDTH - 1 - j, 1), :]
        dxc_next[...] = dxc[0:SUBLANES, :]
        dp_o[:, 0:c] = dxb.astype(BF16)
        dp_o[:, c:2 * c] = dgate.astype(BF16)

    xb_s, gate_s, prev_s = _lru_specs(tb, c, nb, True)
    c0 = lambda b, i: (0, 0)
    vec = pl.BlockSpec((1, c), c0)
    wsp = pl.BlockSpec((nblk, LRU_BLOCK_W, LRU_BLOCK_W), lambda b, i: (0, 0, 0))
    cwsp = pl.BlockSpec((CONV_WIDTH, c), c0)
    sds = jax.ShapeDtypeStruct
    return _call(
        body, name=name, grid=(bsz, nb),
        in_specs=[xb_s, gate_s, prev_s, xb_s, prev_s, gate_s, cwsp, vec, wsp, vec, wsp, vec, vec, vec],
        out_specs=[pl.BlockSpec((tb, 2 * c), lambda b, i: (b * nb + nb - 1 - i, 0)),
                   cwsp, vec, wsp, vec, wsp, vec, vec, vec],
        out_shape=[sds((n, 2 * c), BF16), sds((CONV_WIDTH, c), F32), sds((1, c), F32),
                   sds((nblk, LRU_BLOCK_W, LRU_BLOCK_W), F32), sds((1, c), F32),
                   sds((nblk, LRU_BLOCK_W, LRU_BLOCK_W), F32), sds((1, c), F32), sds((1, c), F32), sds((1, c), F32)],
        scratch_shapes=[pltpu.VMEM((SUBLANES, c), F32)] * 3,
        compiler_params=_cp("arbitrary", "arbitrary"),
    )(p_lru, p_lru, p_lru, h, h, do_cat, prm["conv_w"], prm["conv_b"], prm["wr"], prm["br"], prm["wi"], prm["bi"],
      prm["lam"], prm["norm_g"])


def _adamw(g, w, m, v, name):
    rows, cols = g.shape
    tb = _tile(rows, 128, SUBLANES)

    def body(g_ref, w_ref, m_ref, v_ref, d_o, m_o, v_o):
        gv = g_ref[...]
        mn = ADAM_B1 * m_ref[...] + (1.0 - ADAM_B1) * gv
        vn = ADAM_B2 * v_ref[...] + (1.0 - ADAM_B2) * (gv * gv)
        m_o[...] = mn
        v_o[...] = vn
        d_o[...] = -ADAM_LR * ((mn / _BC1) / (jnp.sqrt(vn / _BC2) + ADAM_EPS) + ADAM_WD * w_ref[...])

    blk = pl.BlockSpec((tb, cols), lambda i: (i, 0))
    return _call(body, name=name, grid=(rows // tb,), in_specs=[blk] * 4, out_specs=[blk] * 3,
                 out_shape=[jax.ShapeDtypeStruct((rows, cols), F32)] * 3, compiler_params=_cp("parallel"))(g, w, m, v)


def _adamw_halves(mine, theirs, w, m, v, name):
    a, b = mine.shape
    tc = _col_tile(a, b)
    w, m, v = (t.reshape(2, a, b) for t in (w, m, v))

    def body(mine_ref, theirs_ref, w_ref, m_ref, v_ref, g_o, d_o, m_o, v_o):
        gv = jnp.where(pl.program_id(0) == lax.axis_index("c"), mine_ref[...], theirs_ref[...])
        mn = ADAM_B1 * m_ref[...] + (1.0 - ADAM_B1) * gv
        vn = ADAM_B2 * v_ref[...] + (1.0 - ADAM_B2) * (gv * gv)
        g_o[...] = gv
        m_o[...] = mn
        v_o[...] = vn
        d_o[...] = -ADAM_LR * ((mn / _BC1) / (jnp.sqrt(vn / _BC2) + ADAM_EPS) + ADAM_WD * w_ref[...])

    half = pl.BlockSpec((a, tc), lambda h, j: (0, j))
    blk = pl.BlockSpec((None, a, tc), lambda h, j: (h, 0, j))
    return _call(body, name=name, grid=(2, b // tc), in_specs=[half, half, blk, blk, blk], out_specs=[blk] * 4,
                 out_shape=[jax.ShapeDtypeStruct((2, a, b), F32)] * 4,
                 compiler_params=_cp("parallel", "parallel"))(mine, theirs, w, m, v)


def _pair_sum(x4, recv, name):
    _, _, a, b = x4.shape
    tc = _col_tile(a, b)

    def body(x_ref, r_ref, o_ref):
        mine = x_ref[lax.axis_index("c")]
        o_ref[...] = (mine.astype(F32) + r_ref[...].astype(F32)).astype(BF16)

    return _call(
        body, name=name, grid=(4, b // tc),
        in_specs=[pl.BlockSpec((None, 2, a, tc), lambda j, i: (j, 0, 0, i)),
                  pl.BlockSpec((None, a, tc), lambda j, i: (j, 0, i))],
        out_specs=pl.BlockSpec((None, a, tc), lambda j, i: (j, 0, i)),
        out_shape=jax.ShapeDtypeStruct((4, a, b), BF16), compiler_params=_cp("parallel", "parallel"))(x4, recv)


def _chip_sum(x4, name):
    _, a, b = x4.shape
    ta = _tile(a, 256, SUBLANES)

    def body(x_ref, o_ref):
        acc = x_ref[0] + x_ref[1]
        acc = acc + x_ref[2]
        o_ref[...] = acc + x_ref[3]

    return _call(
        body, name=name, grid=(a // ta,),
        in_specs=[pl.BlockSpec((4, ta, b), lambda i: (0, i, 0))],
        out_specs=pl.BlockSpec((ta, b), lambda i: (i, 0)),
        out_shape=jax.ShapeDtypeStruct((a, b), F32), compiler_params=_cp("parallel"))(x4)


def _peer_sum(own4, parts, name):
    _, a, b = parts.shape
    tc = _col_tile(a, b)

    def body(own_ref, p_ref, o_ref):
        me = 2 * lax.axis_index("x") + lax.axis_index("y")
        acc = own_ref[me].astype(F32) + p_ref[0].astype(F32)
        acc = acc + p_ref[1].astype(F32)
        o_ref[...] = acc + p_ref[2].astype(F32)

    return _call(
        body, name=name, grid=(b // tc,),
        in_specs=[pl.BlockSpec((4, a, tc), lambda i: (0, 0, i)), pl.BlockSpec((3, a, tc), lambda i: (0, 0, i))],
        out_specs=pl.BlockSpec((a, tc), lambda i: (0, i)),
        out_shape=jax.ShapeDtypeStruct((a, b), F32), compiler_params=_cp("parallel"))(own4, parts)


def _add2(x, y, name):
    rows, cols = x.shape
    tb = _tile(rows, 512, SUBLANES)

    def body(x_ref, y_ref, o_ref):
        o_ref[...] = x_ref[...] + y_ref[...]

    blk = pl.BlockSpec((tb, cols), lambda i: (i, 0))
    return _call(body, name=name, grid=(rows // tb,), in_specs=[blk, blk], out_specs=blk,
                 out_shape=jax.ShapeDtypeStruct((rows, cols), x.dtype), compiler_params=_cp("parallel"))(x, y)


_HBM = pl.BlockSpec(memory_space=pltpu.HBM)


def _place():
    x, y, c = lax.axis_index("x"), lax.axis_index("y"), lax.axis_index("c")
    chips = [(1 - x, y), (x, 1 - y), (1 - x, 1 - y)]
    return x, y, c, chips


def _comm_call(body, name, xs, out_shapes, n_sems):
    return _call(
        body, name=name, in_specs=[_HBM] * len(xs), out_specs=[_HBM] * len(out_shapes), out_shape=out_shapes,
        scratch_shapes=[pltpu.SemaphoreType.DMA((n_sems,)), pltpu.SemaphoreType.DMA((n_sems,)),
                        pltpu.SemaphoreType.DMA((len(xs),))],
    )(*xs)


def _all_gather_chips(xs, name):
    n = len(xs)

    def body(*refs):
        ins, outs = refs[:n], refs[n:2 * n]
        ssem, rsem, _ = refs[2 * n:]
        _gather_start(ins, outs, ssem, rsem)
        _gather_finish(ins, outs, ssem, rsem)

    outs = [jax.ShapeDtypeStruct((4,) + v.shape, v.dtype) for v in xs]
    return _comm_call(body, name, xs, outs, GATHER_SEMS * n)


GATHER_SEMS = 7
PEER_SEMS = 3


def _remote(src, dst, ssem, rsem, k, dev):
    return pltpu.make_async_remote_copy(src_ref=src, dst_ref=dst, send_sem=ssem.at[k], recv_sem=rsem.at[k],
                                        device_id=dev, device_id_type=MESH)


def _gather_start(ins, outs, ssem, rsem):
    x, y, c, chips = _place()
    me = 2 * x + y
    for i in range(len(ins)):
        for j, (px, py) in enumerate(chips):
            _remote(ins[i].at[c], outs[i].at[me, c], ssem, rsem, GATHER_SEMS * i + j, (px, py, c)).start()
        _remote(ins[i], outs[i].at[me], ssem, rsem, GATHER_SEMS * i + 6, (x, y, 1 - c)).start()


def _gather_finish(ins, outs, ssem, rsem):
    x, y, c, chips = _place()
    me = 2 * x + y
    sib = (x, y, 1 - c)
    n = len(ins)
    for i in range(n):
        for j, (px, py) in enumerate(chips):
            slot = outs[i].at[2 * px + py, c]
            _remote(slot, slot, ssem, rsem, GATHER_SEMS * i + j, (px, py, c)).wait_recv()
            _remote(slot, slot, ssem, rsem, GATHER_SEMS * i + 3 + j, sib).start()
    for i in range(n):
        own = outs[i].at[me]
        _remote(own, own, ssem, rsem, GATHER_SEMS * i + 6, sib).wait_recv()
        for j, (px, py) in enumerate(chips):
            slot = outs[i].at[2 * px + py, 1 - c]
            _remote(slot, slot, ssem, rsem, GATHER_SEMS * i + 3 + j, sib).wait_recv()
    for i in range(n):
        for j, (px, py) in enumerate(chips):
            slot = outs[i].at[2 * px + py, c]
            _remote(ins[i].at[c], outs[i].at[me, c], ssem, rsem, GATHER_SEMS * i + j, (px, py, c)).wait_send()
            _remote(slot, slot, ssem, rsem, GATHER_SEMS * i + 3 + j, sib).wait_send()
        _remote(ins[i], outs[i].at[me], ssem, rsem, GATHER_SEMS * i + 6, sib).wait_send()


def _peer_copies(ins, outs, ssem, rsem):
    x, y, c, chips = _place()
    return [_remote(ins[i].at[2 * px + py], outs[i].at[j], ssem, rsem, PEER_SEMS * i + j, (px, py, c))
            for i in range(len(ins)) for j, (px, py) in enumerate(chips)]


def _comm_plan(kind, xs):
    if kind == "gather":
        outs = [jax.ShapeDtypeStruct((4,) + v.shape, v.dtype) for v in xs]
        return outs, GATHER_SEMS * len(xs), _gather_start, _gather_finish
    if kind == "swap_half":
        def swaps(ins, outs, ssem, rsem):
            x, y, c, _ = _place()
            return [_remote(ins[i].at[:, 1 - c], outs[i], ssem, rsem, i, (x, y, 1 - c)) for i in range(len(ins))]

        def start_swaps(ins, outs, ssem, rsem):
            for cp in swaps(ins, outs, ssem, rsem):
                cp.start()

        def finish_swaps(ins, outs, ssem, rsem):
            for cp in swaps(ins, outs, ssem, rsem):
                cp.wait()

        outs = [jax.ShapeDtypeStruct((v.shape[0],) + v.shape[2:], v.dtype) for v in xs]
        return outs, len(xs), start_swaps, finish_swaps

    def start(ins, outs, ssem, rsem):
        for cp in _peer_copies(ins, outs, ssem, rsem):
            cp.start()

    def finish(ins, outs, ssem, rsem):
        for cp in _peer_copies(ins, outs, ssem, rsem):
            cp.wait()

    outs = [jax.ShapeDtypeStruct((3,) + v.shape[1:], v.dtype) for v in xs]
    return outs, PEER_SEMS * len(xs), start, finish


def _sibling_swap(xs, pick_half, name):
    n = len(xs)

    def body(*refs):
        ins, outs = refs[:n], refs[n:2 * n]
        ssem, rsem, _ = refs[2 * n:]
        x, y, c, _ = _place()
        cps = []
        for i in range(n):
            src = ins[i].at[:, 1 - c] if pick_half else ins[i]
            cp = pltpu.make_async_remote_copy(src_ref=src, dst_ref=outs[i], send_sem=ssem.at[i], recv_sem=rsem.at[i],
                                              device_id=(x, y, 1 - c), device_id_type=MESH)
            cp.start()
            cps.append(cp)
        for cp in cps:
            cp.wait()

    outs = [jax.ShapeDtypeStruct((v.shape[0],) + v.shape[2:] if pick_half else v.shape, v.dtype) for v in xs]
    return _comm_call(body, name, xs, outs, n)


def _chip_broadcast(xs, name):
    n = len(xs)

    def body(*refs):
        ins, outs = refs[:n], refs[n:2 * n]
        ssem, rsem, lsem = refs[2 * n:]
        x, y, c, chips = _place()
        me = 2 * x + y
        cps = []
        for i in range(n):
            cp = pltpu.make_async_copy(ins[i], outs[i].at[me], lsem.at[i])
            cp.start()
            cps.append(cp)
            for j, (px, py) in enumerate(chips):
                cp = pltpu.make_async_remote_copy(
                    src_ref=ins[i], dst_ref=outs[i].at[me], send_sem=ssem.at[3 * i + j], recv_sem=rsem.at[3 * i + j],
                    device_id=(px, py, c), device_id_type=MESH)
                cp.start()
                cps.append(cp)
        for i in range(n):
            for j, (px, py) in enumerate(chips):
                slot = outs[i].at[2 * px + py]
                pltpu.make_async_remote_copy(
                    src_ref=slot, dst_ref=slot, send_sem=ssem.at[3 * i + j], recv_sem=rsem.at[3 * i + j],
                    device_id=(px, py, c), device_id_type=MESH).wait_recv()
        for i in range(n):
            cps[4 * i].wait()
            for j in range(3):
                cps[4 * i + 1 + j].wait_send()

    outs = [jax.ShapeDtypeStruct((4,) + v.shape, v.dtype) for v in xs]
    return _comm_call(body, name, xs, outs, 3 * n)


def _to_scan_k(x, bsz, t_len):
    h = x.shape[1] // RWKV_HEAD
    y = jnp.broadcast_to(x.reshape(1, bsz, t_len, h, RWKV_HEAD), (2, bsz, t_len, h, RWKV_HEAD))
    return y.transpose(2, 4, 0, 1, 3).reshape(t_len, RWKV_HEAD, 2 * bsz * h)


def _to_scan_v(x, bsz, t_len):
    h = x.shape[1] // RWKV_HEAD
    y = x.reshape(bsz, t_len, h, 2, RWKV_HEAD // 2).transpose(1, 4, 3, 0, 2)
    return y.reshape(t_len, RWKV_HEAD // 2, 2 * bsz * h)


def _from_scan_k(x, bsz, t_len):
    h = x.shape[2] // (2 * bsz)
    y = x[:, :, :bsz * h].reshape(t_len, RWKV_HEAD, bsz, h).transpose(2, 0, 3, 1)
    return y.reshape(bsz * t_len, h * RWKV_HEAD)


def _from_scan_v(x, bsz, t_len):
    h = x.shape[2] // (2 * bsz)
    y = x.reshape(t_len, RWKV_HEAD // 2, 2, bsz, h).transpose(3, 0, 4, 2, 1)
    return y.reshape(bsz * t_len, h * RWKV_HEAD)


def _pad_rows(x, rows):
    return jnp.pad(x, ((0, rows - x.shape[0]), (0, 0)))


def _pad_cols(x, cols):
    return jnp.pad(x, ((0, 0), (0, cols - x.shape[1])))


def _cols_from_shards(g4):
    _, r, cs = g4.shape
    return g4.transpose(1, 0, 2).reshape(r, 4 * cs)


def _cols_to_shards(g):
    r, cols = g.shape
    return g.reshape(r, 4, cols // 4).transpose(1, 0, 2)


def kernel(x, norm_mix_g, w_in, mu_shift, rwkv_w0, rwkv_w2, rwkv_a0, rwkv_a2, rwkv_g2, rwkv_k_k, rwkv_k_a, rwkv_r_k, rwkv_ln_g, rwkv_ln_b, conv_w, conv_b, lru_wr, lru_br, lru_wi, lru_bi, lru_lambda, lru_norm_g, w_out, norm_ffn_g, ffn_w_gate, ffn_w_up, ffn_w_down, norm_final_g, loss_target, m_norm_mix_g, m_w_in, m_mu_shift, m_rwkv_w0, m_rwkv_w2, m_rwkv_a0, m_rwkv_a2, m_rwkv_g2, m_rwkv_k_k, m_rwkv_k_a, m_rwkv_r_k, m_rwkv_ln_g, m_rwkv_ln_b, m_conv_w, m_conv_b, m_lru_wr, m_lru_br, m_lru_wi, m_lru_bi, m_lru_lambda, m_lru_norm_g, m_w_out, m_norm_ffn_g, m_ffn_w_gate, m_ffn_w_up, m_ffn_w_down, m_norm_final_g, v_norm_mix_g, v_w_in, v_mu_shift, v_rwkv_w0, v_rwkv_w2, v_rwkv_a0, v_rwkv_a2, v_rwkv_g2, v_rwkv_k_k, v_rwkv_k_a, v_rwkv_r_k, v_rwkv_ln_g, v_rwkv_ln_b, v_conv_w, v_conv_b, v_lru_wr, v_lru_br, v_lru_wi, v_lru_bi, v_lru_lambda, v_lru_norm_g, v_w_out, v_norm_ffn_g, v_ffn_w_gate, v_ffn_w_up, v_ffn_w_down, v_norm_final_g):
    names = ['norm_mix_g', 'w_in', 'mu_shift', 'rwkv_w0', 'rwkv_w2', 'rwkv_a0', 'rwkv_a2', 'rwkv_g2', 'rwkv_k_k',
             'rwkv_k_a', 'rwkv_r_k', 'rwkv_ln_g', 'rwkv_ln_b', 'conv_w', 'conv_b', 'lru_wr', 'lru_br', 'lru_wi',
             'lru_bi', 'lru_lambda', 'lru_norm_g', 'w_out', 'norm_ffn_g', 'ffn_w_gate', 'ffn_w_up', 'ffn_w_down',
             'norm_final_g']
    env = locals()
    wts = {k: env[k] for k in names}
    mom_m = {k: env["m_" + k] for k in names}
    mom_v = {k: env["v_" + k] for k in names}

    bsz, t_len, d = x.shape
    n = bsz * t_len
    w = rwkv_w0.shape[1]
    lw = lru_br.shape[1]
    dl, al, gl = rwkv_w2.shape[1], rwkv_a2.shape[1], rwkv_g2.shape[1]
    dlp, alp, glp = _ceil_to(dl, LANES), _ceil_to(al, LANES), _ceil_to(gl, LANES)
    lp = dlp + alp + glp
    rc = 3 * w + dl + al + gl
    chip = 2 * lax.axis_index("x") + lax.axis_index("y")

    big = ['w_in', 'w_out', 'ffn_w_gate', 'ffn_w_up', 'ffn_w_down']
    small_sh = ['rwkv_w2', 'rwkv_a2', 'rwkv_g2', 'conv_w']

    def halves(a2d):
        return a2d.reshape(2, a2d.shape[0] // 2, a2d.shape[1])

    col_sharded = ('w_in', 'ffn_w_gate', 'ffn_w_up')

    def work(k, t):
        return jnp.swapaxes(t[0], 0, 1) if k in col_sharded else t[0]

    def unwork(k, t2):
        return (jnp.swapaxes(t2, 0, 1) if k in col_sharded else t2)[None]

    def rows_of(g):
        return g.reshape(g.shape[0] * g.shape[1] * g.shape[2], g.shape[3])

    send = [halves(work('w_in', w_in).astype(BF16))] + [halves(wts[k][0]) for k in small_sh]
    got = _all_gather_chips(send, "gather_w_in")
    later = ['w_out', 'ffn_w_gate', 'ffn_w_up', 'ffn_w_down']
    send_later = [halves(work(k, wts[k]).astype(BF16)) for k in later]
    full = {}
    for k, g in zip(small_sh, got[1:]):
        full[k] = _cols_from_shards(g.reshape(4, g.shape[1] * g.shape[2], g.shape[3]))
    wi_t = rows_of(got[0])
    w_rkv = wi_t[:3 * w]
    w_lru = wi_t[rc:]
    o = 3 * w
    w_lora = jnp.concatenate([_pad_rows(wi_t[o:o + dl], dlp), _pad_rows(wi_t[o + dl:o + dl + al], alp),
                              _pad_rows(wi_t[o + dl + al:rc], glp)], axis=0)
    mu = mu_shift
    prm_r = dict(
        mu_rkv=mu[:, :3 * w],
        mu_lora=jnp.concatenate([_pad_cols(mu[:, o:o + dl], dlp), _pad_cols(mu[:, o + dl:o + dl + al], alp),
                                 _pad_cols(mu[:, o + dl + al:rc], glp)], axis=1),
        w0=rwkv_w0, a0=rwkv_a0, k_k=rwkv_k_k, k_a=rwkv_k_a,
        w2=_pad_rows(full['rwkv_w2'], dlp).astype(BF16), a2=_pad_rows(full['rwkv_a2'], alp).astype(BF16),
        g2=_pad_rows(full['rwkv_g2'], glp).astype(BF16))
    ln_g, ln_b, r_k = rwkv_ln_g, rwkv_ln_b, rwkv_r_k.reshape(1, w)
    prm_l = dict(conv_w=full['conv_w'], conv_b=conv_b, wr=lru_wr[0].astype(BF16), br=lru_br,
                 wi=lru_wi[0].astype(BF16), bi=lru_bi, lam=lru_lambda, norm_g=lru_norm_g)
    g_final = norm_final_g.reshape(1, d)

    x2 = x.reshape(n, d)
    u1 = _rmsnorm_fwd(x2, norm_mix_g, "norm_mix")
    p_rkv = _mm(u1, w_rkv, name="in_rkv", tb=True)
    p_lru = _mm(u1, w_lru, name="in_lru", tb=True)
    p_lora = _mm(u1, w_lora, name="in_lora", tb=True)
    r_t, dec_t, k_t, v_t, na_t, nb_t, g_t = _rwkv_prep_fwd(p_rkv, p_lora, prm_r, t_len, "rwkv_prep")
    sk = [_to_scan_k(a, bsz, t_len) for a in (r_t, dec_t, k_t, na_t, nb_t)]
    sv = _to_scan_v(v_t, bsz, t_len)
    y_s, states, sa_s, got_wo, got_wg =_rwkv_scan_fwd(*sk, sv, name="rwkv_scan", comm=("gather", send_later[:2]))
    wo, wg = rows_of(got_wo), rows_of(got_wg)
    y_t = _from_scan_v(y_s, bsz, t_len)
    y_a = _rwkv_post_fwd(y_t, r_t, k_t, v_t, g_t, ln_g, ln_b, r_k, "rwkv_post")
    y_b, h_lru = _lru_fwd(p_lru, prm_l, t_len, "lru_fwd")
    h1 = _mm(y_a, wo[:w], name="out_a", res=x2)
    h1 = _mm(y_b, wo[w:], name="out_b", res=h1)
    u2 = _rmsnorm_fwd(h1, norm_ffn_g, "norm_ffn")
    ffc = (2048, 256, 4096)
    gate, got_wu = _mm(u2, wg, name="ffn_gate", tb=True, caps=ffc, comm=("gather", send_later[2:3]))
    wu = rows_of(got_wu)
    up, act, got_wd = _mm(u2, wu, name="ffn_up", tb=True, caps=(1024, 256, 4096), comm=("gather", send_later[3:4]),
                          epi=(_swiglu_tile, [gate], [F32, BF16]))
    wd = rows_of(got_wd)
    h2 = _mm(act, wd, name="ffn_down", res=h1, caps=(1024, 256, 11008), resident="a")

    dh2, dh2b, g_norm_final, loss_vec = _loss_head(h2, g_final, loss_target.reshape(n, d), "loss_head")
    loss = lax.psum(loss_vec[0, 0], ("x", "y", "c"))
    dgate, dup = _mm(dh2b, wd, name="d_act", tb=True, caps=(2048, 256, 4096), resident="a",
                     epi=(_swiglu_bwd_tile, [gate, up], [BF16, BF16]))
    shards = lambda g: g.reshape(4, 2, g.shape[0] // 8, g.shape[1])
    dwc = dict(ta=True, out_dtype=BF16, n_outer=True, caps=(256, 2048, 4096), resident="b")
    gw_down = _mm(act, dh2b, name="dw_down", **dwc)
    gw_gate = _mm(dgate, u2, name="dw_gate", **dwc)
    gw_up = _mm(dup, u2, name="dw_up", **dwc)
    g4f = [shards(g) for g in (gw_gate, gw_up, gw_down)]
    du2, *sib_f = _mm(dgate, wg, name="du2_gate", caps=(512, 256, 11008), comm=("swap_half", g4f))
    du2 = _mm(dup, wu, name="du2_up", res=du2, caps=(1024, 256, 11008), resident="a")
    dh1, dh1b, g_norm_ffn = _rmsnorm_bwd(du2, h1, norm_ffn_g, dh2, "norm_ffn_bwd")
    dcat = _mm(dh1b, wo, name="d_cat", tb=True)
    gw_out = jnp.concatenate([_mm(y_a, dh1b, name="dw_out_a", ta=True, out_dtype=BF16),
                              _mm(y_b, dh1b, name="dw_out_b", ta=True, out_dtype=BF16)], axis=0)
    g4o = [shards(gw_out)]
    sib_o = _sibling_swap(g4o, True, "grad_sibling_out")
    pair_a = [_pair_sum(a4, s, "grad_pair_sum_%d" % i) for i, (a4, s) in enumerate(zip(g4o + g4f, list(sib_o) + list(sib_f)))]
    (dp_lru, g_conv_w, g_conv_b, g_wr, g_br, g_wi, g_bi, g_lam, g_lng) = _lru_bwd(
        p_lru, h_lru, dcat, prm_l, t_len, "lru_bwd")
    dy_t, dr_p, dk_p, dv_p, dg_t, g_ln_g, g_ln_b, g_r_k = _rwkv_post_bwd(
        y_t, r_t, k_t, v_t, g_t, ln_g, ln_b, r_k, dcat, "rwkv_post_bwd")
    dr_s, dw_s, dk_s, da_s, db_s, dv_s, *parts_a = _rwkv_scan_bwd(
        *sk, sv, _to_scan_v(dy_t, bsz, t_len), states, sa_s, name="rwkv_scan_bwd", comm=("peer", pair_a))
    grads = [_from_scan_k(dr_s, bsz, t_len), dr_p, _from_scan_k(dw_s, bsz, t_len), _from_scan_k(dk_s, bsz, t_len),
             dk_p, _from_scan_v(dv_s, bsz, t_len), dv_p, _from_scan_k(da_s, bsz, t_len),
             _from_scan_k(db_s, bsz, t_len), dg_t]
    (dq_r, dq_l, g_mu_r, g_mu_l, g_w0, g_a0, g_kk, g_ka, g_w2, g_a2, g_g2) = _rwkv_prep_bwd(
        p_rkv, p_lora, prm_r, grads, t_len, "rwkv_prep_bwd")
    dp_rkv = _shift_combine(dq_r, prm_r["mu_rkv"], t_len, "shift_bwd_rkv")
    dp_lora = _shift_combine(dq_l, prm_r["mu_lora"], t_len, "shift_bwd_lora")
    gi_rkv = _mm(dp_rkv, u1, name="dw_in_rkv", ta=True, out_dtype=BF16)
    gi_lru = _mm(dp_lru, u1, name="dw_in_lru", ta=True, out_dtype=BF16)
    gi_lora = _mm(dp_lora, u1, name="dw_in_lora", ta=True, out_dtype=BF16)
    gw_in = jnp.concatenate([gi_rkv, gi_lora[:dl], gi_lora[dlp:dlp + al], gi_lora[dlp + alp:dlp + alp + gl], gi_lru],
                            axis=0)
    g4b = [shards(gw_in)]
    sib_b = _sibling_swap(g4b, True, "grad_sibling_in")
    pair_b = [_pair_sum(g4b[0], sib_b[0], "grad_pair_sum_in")]
    du1, *parts_b = _mm(dp_rkv, w_rkv, name="du1_rkv", comm=("peer", pair_b))
    du1 = _mm(dp_lru, w_lru, name="du1_lru", res=du1)
    du1 = _mm(dp_lora, w_lora, name="du1_lora", res=du1)
    gx, _, g_norm_mix = _rmsnorm_bwd(du1, x2, norm_mix_g, dh1, "norm_mix_bwd")

    pair, parts = pair_b + pair_a, list(parts_b) + list(parts_a)
    mine = [_peer_sum(own4, p3, "grad_chip_sum_%d" % i) for i, (own4, p3) in enumerate(zip(pair, parts))]
    theirs = _sibling_swap(mine, False, "grad_share")

    g_mu = jnp.concatenate([g_mu_r, g_mu_l[:, :dl], g_mu_l[:, dlp:dlp + al], g_mu_l[:, dlp + alp:dlp + alp + gl]],
                           axis=1)
    small = dict(norm_mix_g=g_norm_mix, mu_shift=g_mu, rwkv_w0=g_w0, rwkv_w2=g_w2[:dl], rwkv_a0=g_a0,
                 rwkv_a2=g_a2[:al], rwkv_g2=g_g2[:gl], rwkv_k_k=g_kk, rwkv_k_a=g_ka, rwkv_r_k=g_r_k,
                 rwkv_ln_g=g_ln_g, rwkv_ln_b=g_ln_b, conv_w=g_conv_w, conv_b=g_conv_b, lru_wr=g_wr, lru_br=g_br,
                 lru_wi=g_wi, lru_bi=g_bi, lru_lambda=g_lam, lru_norm_g=g_lng, norm_ffn_g=g_norm_ffn,
                 norm_final_g=g_norm_final)
    small_names = list(small)
    sizes = [small[k].size for k in small_names]
    total = sum(sizes)
    padded = _ceil_to(total, 512 * LANES)

    def pack(arrs):
        flat = jnp.concatenate([a.reshape(-1) for a in arrs] + [jnp.zeros((padded - sum(a.size for a in arrs),), F32)])
        return flat.reshape(padded // LANES, LANES)

    packed = pack([small[k] for k in small_names])
    other = _sibling_swap([packed], False, "small_sibling")[0]
    chip_sum = _add2(packed, other, "small_pair_sum")
    all4 = _chip_broadcast([chip_sum], "small_chips")[0]
    red = _chip_sum(all4, "small_chip_sum").reshape(-1)
    small_g = {}
    off = 0
    for k, sz in zip(small_names, sizes):
        full_g = red[off:off + sz].reshape(small[k].shape)
        off += sz
        if k in small_sh:
            cs = full_g.shape[1] // 4
            full_g = lax.dynamic_slice_in_dim(full_g, chip * cs, cs, axis=1)
        small_g[k] = full_g.reshape(wts[k].shape)

    grad_w, delta_w, new_m, new_v = {}, {}, {}, {}
    for k, g_mine, g_theirs in zip(big, mine, theirs):
        res = _adamw_halves(g_mine, g_theirs, work(k, wts[k]), work(k, mom_m[k]), work(k, mom_v[k]), "adamw_" + k)
        grad_w[k], delta_w[k], new_m[k], new_v[k] = (unwork(k, t.reshape(2 * t.shape[1], t.shape[2])) for t in res)
    lsizes = [small_g[k].size for k in small_names]
    lpad = _ceil_to(sum(lsizes), 128 * LANES)

    def lpack(tree):
        arrs = [tree[k].reshape(-1) for k in small_names]
        flat = jnp.concatenate(arrs + [jnp.zeros((lpad - sum(lsizes),), F32)])
        return flat.reshape(lpad // LANES, LANES)

    dlt, mn, vn = _adamw(lpack(small_g), lpack(wts), lpack(mom_m), lpack(mom_v), "adamw_small")
    off = 0
    for k, sz in zip(small_names, lsizes):
        shp = wts[k].shape
        grad_w[k] = small_g[k]
        delta_w[k] = dlt.reshape(-1)[off:off + sz].reshape(shp)
        new_m[k] = mn.reshape(-1)[off:off + sz].reshape(shp)
        new_v[k] = vn.reshape(-1)[off:off + sz].reshape(shp)
        off += sz

    return (loss, gx.reshape(bsz, t_len, d), *[grad_w[k] for k in names], *[delta_w[k] for k in names],
            *[new_m[k] for k in names], *[new_v[k] for k in names])
```

```python
import jax
import jax.numpy as jnp
from jax import lax
from jax.experimental import pallas as pl
from jax.experimental.pallas import tpu as pltpu

F32 = jnp.float32
BF16 = jnp.bfloat16
MESH = pl.DeviceIdType.MESH
_call = pl.pallas_call

V7X_VMEM_LIMIT = 56 * 1024 * 1024
LANES = 128
SUBLANES = 8

RWKV_HEAD = 64
LRU_BLOCK_W = 128
CONV_WIDTH = 4
LRU_C = 8.0
NORM_EPS = 1e-6
GN_EPS = 64e-5
KK_EPS = 1e-24
SCAN_CHUNK = 16

ADAM_LR = 0.001
ADAM_B1 = 0.9
ADAM_B2 = 0.999
ADAM_EPS = 1e-08
ADAM_WD = 0.01
ADAM_STEP = 10
_BC1 = 1.0 - ADAM_B1 ** ADAM_STEP
_BC2 = 1.0 - ADAM_B2 ** ADAM_STEP


def _cp(*sem):
    return pltpu.CompilerParams(dimension_semantics=tuple(sem), vmem_limit_bytes=V7X_VMEM_LIMIT)


def _tile(n, cap, unit=LANES):
    if n <= cap:
        return n
    best = None
    d = unit
    while d <= cap:
        if n % d == 0:
            best = d
        d += unit
    return n if best is None else best


def _ceil_to(n, m):
    return -(-n // m) * m


ELEMENTWISE_BLOCK_BYTES = 3 * 512 * 1024


def _col_tile(rows, cols):
    cap = max(LANES, ELEMENTWISE_BLOCK_BYTES // (4 * rows) // LANES * LANES)
    return _tile(cols, cap)


def _sig(x):
    return 1.0 / (1.0 + jnp.exp(-x))


def _log1p(x):
    return jnp.where(x < 0.01, x * (1.0 - x * (0.5 - x * (1.0 / 3.0))), jnp.log(1.0 + x))


def _softplus(x):
    return jnp.maximum(x, 0.0) + _log1p(jnp.exp(-jnp.abs(x)))


def _neg_expm1(x):
    small = -x * (1.0 + x * (0.5 + x * (1.0 / 6.0)))
    return jnp.where(x > -0.01, small, 1.0 - jnp.exp(x))


_GELU_K = 0.7978845608028654
_GELU_C = 0.044715


def _gelu_parts(x):
    th = jnp.tanh(_GELU_K * (x + _GELU_C * x * x * x))
    return 0.5 * x * (1.0 + th), th


def _gelu_grad(x, th):
    return 0.5 * (1.0 + th) + 0.5 * x * (1.0 - th * th) * _GELU_K * (1.0 + 3.0 * _GELU_C * x * x)


def _shift_down(x, prev8, j):
    tb = x.shape[0]
    xr = pltpu.roll(x, j, 0)
    pr = pltpu.roll(prev8, j, 0)
    row = lax.broadcasted_iota(jnp.int32, prev8.shape, 0)
    first = jnp.where(row < j, pr, xr[0:SUBLANES])
    if tb == SUBLANES:
        return first
    return jnp.concatenate([first, xr[SUBLANES:]], axis=0)


def _shift_up(x, next8, j):
    tb = x.shape[0]
    xr = pltpu.roll(x, tb - j, 0)
    nr = pltpu.roll(next8, SUBLANES - j, 0)
    row = lax.broadcasted_iota(jnp.int32, next8.shape, 0)
    last = jnp.where(row >= SUBLANES - j, nr, xr[tb - SUBLANES:])
    if tb == SUBLANES:
        return last
    return jnp.concatenate([xr[:tb - SUBLANES], last], axis=0)


def _head_mats(width, heads_pad):
    e = (lax.broadcasted_iota(jnp.int32, (width, heads_pad), 0) // RWKV_HEAD
         == lax.broadcasted_iota(jnp.int32, (width, heads_pad), 1)).astype(BF16)
    et = (lax.broadcasted_iota(jnp.int32, (heads_pad, width), 1) // RWKV_HEAD
          == lax.broadcasted_iota(jnp.int32, (heads_pad, width), 0)).astype(BF16)
    return e, et


def _dot_exact01(x, m):
    hi = x.astype(BF16)
    r1 = x - hi.astype(F32)
    mid = r1.astype(BF16)
    lo = (r1 - mid.astype(F32)).astype(BF16)
    return (jnp.dot(lo, m, preferred_element_type=F32) + jnp.dot(mid, m, preferred_element_type=F32)
            + jnp.dot(hi, m, preferred_element_type=F32))


def _headsum(x, e, et):
    return _dot_exact01(_dot_exact01(x, e), et)


def _dot(a, b):
    return jnp.dot(a.astype(BF16), b.astype(BF16), preferred_element_type=F32)


def _dot_tn(a, b):
    return lax.dot_general(a.astype(BF16), b.astype(BF16), (((0,), (0,)), ((), ())), preferred_element_type=F32)


def _dot_nt(a, b):
    return lax.dot_general(a.astype(BF16), b.astype(BF16), (((1,), (1,)), ((), ())), preferred_element_type=F32)


def _mm(a, b, *, name, ta=False, tb=False, out_dtype=F32, res=None, n_outer=False, caps=(1024, 512, 4096),
        comm=None, epi=None, resident=None):
    m = a.shape[1] if ta else a.shape[0]
    kd = a.shape[0] if ta else a.shape[1]
    n = b.shape[0] if tb else b.shape[1]
    assert kd == (b.shape[1] if tb else b.shape[0])
    tm, tn, tk = _tile(m, caps[0]), _tile(n, caps[1]), _tile(kd, caps[2])
    gm, gn, gk = m // tm, n // tn, kd // tk
    dims = (((0 if ta else 1,), (1 if tb else 0,)), ((), ()))
    grid = (gn, gm, gk) if n_outer else (gm, gn, gk)
    cx, cx_specs, c_outs, c_sems, c_start, c_finish = _carried(comm)
    epi_fn, epi_ins, out_dtypes = epi if epi is not None else (None, [], [out_dtype])
    n_epi, n_out = len(epi_ins), len(out_dtypes)
    nx, n_in = len(cx), (3 if res is not None else 2) + n_epi

    def ij(g0, g1):
        return (g1, g0) if n_outer else (g0, g1)

    def a_map(g0, g1, k):
        i, _ = ij(g0, g1)
        return (k, i) if ta else (i, k)

    def b_map(g0, g1, k):
        _, j = ij(g0, g1)
        return (j, k) if tb else (k, j)

    def o_map(g0, g1, k):
        return ij(g0, g1)

    has_res = res is not None

    def body(*refs):
        a_ref, b_ref = refs[0], refs[1]
        res_ref = refs[2] if has_res else None
        epi_refs = refs[n_in - n_epi:n_in]
        c_ins = refs[n_in:n_in + nx]
        o_refs = refs[n_in + nx:n_in + nx + n_out]
        c_out_refs = refs[n_in + nx + n_out:n_in + nx + n_out + len(c_outs)]
        acc_ref = refs[n_in + nx + n_out + len(c_outs)] if gk > 1 else None
        steps = [pl.program_id(ax) for ax in range(3)]
        if nx:
            @pl.when(jnp.logical_and(jnp.logical_and(steps[0] == 0, steps[1] == 0), steps[2] == 0))
            def _():
                c_start(c_ins, c_out_refs, refs[-2], refs[-1])

        prod = lax.dot_general(a_ref[...], b_ref[...], dims, preferred_element_type=F32)

        def finish(acc):
            if has_res:
                acc = acc + res_ref[...]
            tiles = [acc] if epi_fn is None else epi_fn(acc, *[e_ref[...] for e_ref in epi_refs])
            for o_ref, tile, dt in zip(o_refs, tiles, out_dtypes):
                o_ref[...] = tile.astype(dt)

        if gk == 1:
            finish(prod)
        else:
            k = steps[2]

            @pl.when(k == 0)
            def _():
                acc_ref[...] = prod

            @pl.when(k > 0)
            def _():
                acc_ref[...] += prod

            @pl.when(k == gk - 1)
            def _():
                finish(acc_ref[...])

        if nx:
            @pl.when(jnp.logical_and(jnp.logical_and(steps[0] == grid[0] - 1, steps[1] == grid[1] - 1),
                                     steps[2] == grid[2] - 1))
            def _():
                c_finish(c_ins, c_out_refs, refs[-2], refs[-1])

    one = pl.Buffered(1)
    in_specs = [pl.BlockSpec((tk, tm) if ta else (tm, tk), a_map, pipeline_mode=one if resident == "a" else None),
                pl.BlockSpec((tn, tk) if tb else (tk, tn), b_map, pipeline_mode=one if resident == "b" else None)]
    args = [a, b]
    for extra in ([res] if has_res else []) + list(epi_ins):
        in_specs.append(pl.BlockSpec((tm, tn), o_map))
        args.append(extra)
    out = _call(
        body, name=name, grid=grid, in_specs=in_specs + cx_specs,
        out_specs=[pl.BlockSpec((tm, tn), o_map)] * n_out + [_HBM] * len(c_outs),
        out_shape=[jax.ShapeDtypeStruct((m, n), dt) for dt in out_dtypes] + c_outs,
        scratch_shapes=([pltpu.VMEM((tm, tn), F32)] if gk > 1 else []) + c_sems,
        compiler_params=_cp(*(("arbitrary",) * 3 if nx else ("parallel", "parallel", "arbitrary"))),
    )(*args, *cx)
    return out if (nx or epi is not None) else out[0]


def _rmsnorm_fwd(x, g, name):
    n, d = x.shape
    tb = _tile(n, 256, SUBLANES)

    def body(x_ref, g_ref, u_ref):
        xv = x_ref[...]
        rstd = lax.rsqrt(jnp.mean(xv * xv, axis=-1, keepdims=True) + NORM_EPS)
        u_ref[...] = (xv * rstd * g_ref[...]).astype(BF16)

    row = pl.BlockSpec((tb, d), lambda i: (i, 0))
    vec = pl.BlockSpec((1, d), lambda i: (0, 0))
    return _call(body, name=name, grid=(n // tb,), in_specs=[row, vec], out_specs=row,
                 out_shape=jax.ShapeDtypeStruct((n, d), BF16), compiler_params=_cp("parallel"))(x, g)


def _rmsnorm_bwd(du, x, g, dres, name):
    n, d = x.shape
    tb = _tile(n, 256, SUBLANES)

    def body(du_ref, x_ref, g_ref, dres_ref, dx_ref, dxb_ref, dg_ref):
        xv = x_ref[...]
        rstd = lax.rsqrt(jnp.mean(xv * xv, axis=-1, keepdims=True) + NORM_EPS)
        xh = xv * rstd
        duv = du_ref[...]
        t = duv * g_ref[...]
        dx = dres_ref[...] + rstd * (t - xh * jnp.mean(t * xh, axis=-1, keepdims=True))
        dx_ref[...] = dx
        dxb_ref[...] = dx.astype(BF16)

        @pl.when(pl.program_id(0) == 0)
        def _():
            dg_ref[...] = jnp.zeros_like(dg_ref)

        dg_ref[...] += jnp.sum(duv * xh, axis=0, keepdims=True)

    row = pl.BlockSpec((tb, d), lambda i: (i, 0))
    vec = pl.BlockSpec((1, d), lambda i: (0, 0))
    return _call(body, name=name, grid=(n // tb,), in_specs=[row, row, vec, row], out_specs=[row, row, vec],
                 out_shape=[jax.ShapeDtypeStruct((n, d), F32), jax.ShapeDtypeStruct((n, d), BF16),
                            jax.ShapeDtypeStruct((1, d), F32)],
                 compiler_params=_cp("arbitrary"))(du, x, g, dres)


def _loss_head(h, g, target, name):
    n, d = h.shape
    tb = _tile(n, 256, SUBLANES)

    def body(h_ref, g_ref, t_ref, dh_ref, dhb_ref, dg_ref, loss_ref):
        hv = h_ref[...]
        gv = g_ref[...]
        rstd = lax.rsqrt(jnp.mean(hv * hv, axis=-1, keepdims=True) + NORM_EPS)
        hh = hv * rstd
        err = hh * gv - t_ref[...]
        dy = err * (1.0 / d)
        dhh = dy * gv
        dh = rstd * (dhh - hh * jnp.mean(dhh * hh, axis=-1, keepdims=True))
        dh_ref[...] = dh
        dhb_ref[...] = dh.astype(BF16)

        @pl.when(pl.program_id(0) == 0)
        def _():
            dg_ref[...] = jnp.zeros_like(dg_ref)
            loss_ref[...] = jnp.zeros_like(loss_ref)

        dg_ref[...] += jnp.sum(dy * hh, axis=0, keepdims=True)
        loss_ref[...] += jnp.sum(err * err) * (0.5 / d)

    row = pl.BlockSpec((tb, d), lambda i: (i, 0))
    vec = pl.BlockSpec((1, d), lambda i: (0, 0))
    lvec = pl.BlockSpec((1, LANES), lambda i: (0, 0))
    return _call(body, name=name, grid=(n // tb,), in_specs=[row, vec, row], out_specs=[row, row, vec, lvec],
                 out_shape=[jax.ShapeDtypeStruct((n, d), F32), jax.ShapeDtypeStruct((n, d), BF16),
                            jax.ShapeDtypeStruct((1, d), F32), jax.ShapeDtypeStruct((1, LANES), F32)],
                 compiler_params=_cp("arbitrary"))(h, g, target)


def _swiglu_tile(up, gate):
    return [up, gate * _sig(gate) * up]


def _swiglu_bwd_tile(dact, gate, up):
    s = _sig(gate)
    return [dact * up * s * (1.0 + gate * (1.0 - s)), dact * gate * s]


def _prep_common(prkv_ref, prkvp_ref, plo_ref, plop_ref, mur_ref, mul_ref, w0_ref, a0_ref, kk_ref, ka_ref,
                 w2_ref, a2_ref, g2_ref, seq_start, w, dlp, alp):
    z8r = jnp.zeros_like(prkvp_ref[...])
    z8l = jnp.zeros_like(plop_ref[...])
    prev_r = jnp.where(seq_start, z8r, prkvp_ref[...])
    prev_l = jnp.where(seq_start, z8l, plop_ref[...])
    p_r = prkv_ref[...]
    p_l = plo_ref[...]
    dif_r = _shift_down(p_r, prev_r, 1) - p_r
    dif_l = _shift_down(p_l, prev_l, 1) - p_l
    q_r = p_r + dif_r * mur_ref[...]
    q_l = p_l + dif_l * mul_ref[...]
    r, k, v = q_r[:, 0:w], q_r[:, w:2 * w], q_r[:, 2 * w:3 * w]
    wd, ad, gd = q_l[:, 0:dlp], q_l[:, dlp:dlp + alp], q_l[:, dlp + alp:]
    tw = jnp.tanh(wd)
    zw = w0_ref[...] + _dot(tw, w2_ref[...])
    wlog = -_softplus(-zw) - 0.5
    ew = jnp.exp(wlog)
    dec = jnp.exp(-ew)
    za = a0_ref[...] + _dot(ad, a2_ref[...])
    av = _sig(za)
    sg = _sig(gd)
    g = _dot(sg, g2_ref[...])
    return dict(dif_r=dif_r, dif_l=dif_l, r=r, k=k, v=v, ad=ad, tw=tw, zw=zw, ew=ew, dec=dec, av=av, sg=sg, g=g)


def _rwkv_prep_specs(n, tb, w, lp, t_len):
    nb8 = tb // SUBLANES
    row3 = pl.BlockSpec((tb, 3 * w), lambda i: (i, 0))
    prev3 = pl.BlockSpec((SUBLANES, 3 * w), lambda i: (jnp.maximum(i * nb8 - 1, 0), 0))
    rowl = pl.BlockSpec((tb, lp), lambda i: (i, 0))
    prevl = pl.BlockSpec((SUBLANES, lp), lambda i: (jnp.maximum(i * nb8 - 1, 0), 0))
    return row3, prev3, rowl, prevl


def _rwkv_prep_fwd(p_rkv, p_lora, prm, t_len, name):
    n, w3 = p_rkv.shape
    w = w3 // 3
    lp = p_lora.shape[1]
    dlp, alp = prm["w2"].shape[0], prm["a2"].shape[0]
    glp = lp - dlp - alp
    hp = max(w // RWKV_HEAD, LANES)
    tb = _tile(min(n, t_len), 128, SUBLANES)
    bps = t_len // tb

    def body(prkv_ref, prkvp_ref, plo_ref, plop_ref, mur_ref, mul_ref, w0_ref, a0_ref, kk_ref, ka_ref,
             w2_ref, a2_ref, g2_ref, r_o, dec_o, k_o, v_o, na_o, nb_o, g_o):
        seq_start = (pl.program_id(0) % bps) == 0
        f = _prep_common(prkv_ref, prkvp_ref, plo_ref, plop_ref, mur_ref, mul_ref, w0_ref, a0_ref, kk_ref, ka_ref,
                         w2_ref, a2_ref, g2_ref, seq_start, w, dlp, alp)
        e, et = _head_mats(w, hp)
        kk0 = f["k"] * kk_ref[...]
        inv = lax.rsqrt(jnp.maximum(_headsum(kk0 * kk0, e, et), KK_EPS))
        kk = kk0 * inv
        r_o[...] = f["r"]
        dec_o[...] = f["dec"]
        k_o[...] = f["k"] * (1.0 + (f["av"] - 1.0) * ka_ref[...])
        v_o[...] = f["v"]
        na_o[...] = -kk
        nb_o[...] = kk * f["av"]
        g_o[...] = f["g"]

    row3, prev3, rowl, prevl = _rwkv_prep_specs(n, tb, w, lp, t_len)
    c0 = lambda i: (0, 0)
    vec3 = pl.BlockSpec((1, 3 * w), c0)
    vecl = pl.BlockSpec((1, lp), c0)
    vec = pl.BlockSpec((1, w), c0)
    out = pl.BlockSpec((tb, w), lambda i: (i, 0))
    return _call(
        body, name=name, grid=(n // tb,),
        in_specs=[row3, prev3, rowl, prevl, vec3, vecl, vec, vec, vec, vec,
                  pl.BlockSpec((dlp, w), c0), pl.BlockSpec((alp, w), c0), pl.BlockSpec((glp, w), c0)],
        out_specs=[out] * 7, out_shape=[jax.ShapeDtypeStruct((n, w), F32)] * 7,
        compiler_params=_cp("parallel"),
    )(p_rkv, p_rkv, p_lora, p_lora, prm["mu_rkv"], prm["mu_lora"], prm["w0"], prm["a0"], prm["k_k"], prm["k_a"],
      prm["w2"], prm["a2"], prm["g2"])


def _rwkv_prep_bwd(p_rkv, p_lora, prm, grads, t_len, name):
    n, w3 = p_rkv.shape
    w = w3 // 3
    lp = p_lora.shape[1]
    dlp, alp = prm["w2"].shape[0], prm["a2"].shape[0]
    glp = lp - dlp - alp
    hp = max(w // RWKV_HEAD, LANES)
    tb = _tile(min(n, t_len), 64, SUBLANES)
    bps = t_len // tb

    def body(prkv_ref, prkvp_ref, plo_ref, plop_ref, mur_ref, mul_ref, w0_ref, a0_ref, kk_ref, ka_ref,
             w2_ref, a2_ref, g2_ref,
             drs_ref, drp_ref, ddec_ref, dks_ref, dkp_ref, dvs_ref, dvp_ref, dna_ref, dnb_ref, dg_ref,
             dqr_o, dql_o, dmur_o, dmul_o, dw0_o, da0_o, dkk_o, dka_o, dw2_o, da2_o, dg2_o):
        seq_start = (pl.program_id(0) % bps) == 0
        f = _prep_common(prkv_ref, prkvp_ref, plo_ref, plop_ref, mur_ref, mul_ref, w0_ref, a0_ref, kk_ref, ka_ref,
                         w2_ref, a2_ref, g2_ref, seq_start, w, dlp, alp)
        e, et = _head_mats(w, hp)
        k, av = f["k"], f["av"]
        k_k, k_a = kk_ref[...], ka_ref[...]
        kk0 = k * k_k
        n2 = _headsum(kk0 * kk0, e, et)
        inv = lax.rsqrt(jnp.maximum(n2, KK_EPS))
        kk = kk0 * inv
        dk2 = dks_ref[...] + dkp_ref[...]
        dnb = dnb_ref[...]
        dkk = dnb * av - dna_ref[...]
        dav = dnb * kk + dk2 * k * k_a
        dk = dk2 * (1.0 + (av - 1.0) * k_a)
        dka = dk2 * k * (av - 1.0)
        proj = jnp.where(n2 > KK_EPS, _headsum(dkk * kk, e, et), 0.0)
        dkk0 = inv * (dkk - kk * proj)
        dk = dk + dkk0 * k_k
        dkkp = dkk0 * k
        dgv = dg_ref[...]
        sg = f["sg"]
        dgd = _dot_nt(dgv, g2_ref[...]) * sg * (1.0 - sg)
        dza = dav * av * (1.0 - av)
        dad = _dot_nt(dza, a2_ref[...])
        dzw = ddec_ref[...] * f["dec"] * (-f["ew"]) * _sig(-f["zw"])
        tw = f["tw"]
        dwd = _dot_nt(dzw, w2_ref[...]) * (1.0 - tw * tw)
        dq_r = jnp.concatenate([drs_ref[...] + drp_ref[...], dk, dvs_ref[...] + dvp_ref[...]], axis=1)
        dq_l = jnp.concatenate([dwd, dad, dgd], axis=1)
        dqr_o[...] = dq_r
        dql_o[...] = dq_l

        @pl.when(pl.program_id(0) == 0)
        def _():
            for o in (dmur_o, dmul_o, dw0_o, da0_o, dkk_o, dka_o, dw2_o, da2_o, dg2_o):
                o[...] = jnp.zeros_like(o)

        def rsum(x):
            return jnp.sum(x, axis=0, keepdims=True)

        dmur_o[...] += rsum(dq_r * f["dif_r"])
        dmul_o[...] += rsum(dq_l * f["dif_l"])
        dw0_o[...] += rsum(dzw)
        da0_o[...] += rsum(dza)
        dkk_o[...] += rsum(dkkp)
        dka_o[...] += rsum(dka)
        dw2_o[...] += _dot_tn(tw, dzw)
        da2_o[...] += _dot_tn(f["ad"], dza)
        dg2_o[...] += _dot_tn(sg, dgv)

    row3, prev3, rowl, prevl = _rwkv_prep_specs(n, tb, w, lp, t_len)
    c0 = lambda i: (0, 0)
    vec3 = pl.BlockSpec((1, 3 * w), c0)
    vecl = pl.BlockSpec((1, lp), c0)
    vec = pl.BlockSpec((1, w), c0)
    blk = pl.BlockSpec((tb, w), lambda i: (i, 0))
    m2, ma, mg = pl.BlockSpec((dlp, w), c0), pl.BlockSpec((alp, w), c0), pl.BlockSpec((glp, w), c0)
    sds = jax.ShapeDtypeStruct
    return _call(
        body, name=name, grid=(n // tb,),
        in_specs=[row3, prev3, rowl, prevl, vec3, vecl, vec, vec, vec, vec, m2, ma, mg] + [blk] * 10,
        out_specs=[row3, rowl, vec3, vecl, vec, vec, vec, vec, m2, ma, mg],
        out_shape=[sds((n, 3 * w), F32), sds((n, lp), F32), sds((1, 3 * w), F32), sds((1, lp), F32),
                   sds((1, w), F32), sds((1, w), F32), sds((1, w), F32), sds((1, w), F32),
                   sds((dlp, w), F32), sds((alp, w), F32), sds((glp, w), F32)],
        compiler_params=_cp("arbitrary"),
    )(p_rkv, p_rkv, p_lora, p_lora, prm["mu_rkv"], prm["mu_lora"], prm["w0"], prm["a0"], prm["k_k"], prm["k_a"],
      prm["w2"], prm["a2"], prm["g2"], *grads)


def _shift_combine(dq, mu, t_len, name):
    n, c = dq.shape
    tb = _tile(min(n, t_len), 256, SUBLANES)
    bps = t_len // tb
    nb8 = tb // SUBLANES
    last8 = n // SUBLANES - 1

    def body(x_ref, nx_ref, mu_ref, o_ref):
        seq_end = (pl.program_id(0) % bps) == bps - 1
        nxt = jnp.where(seq_end, jnp.zeros_like(nx_ref[...]), nx_ref[...])
        x = x_ref[...]
        muv = mu_ref[...]
        o_ref[...] = ((1.0 - muv) * x + muv * _shift_up(x, nxt, 1)).astype(BF16)

    row = pl.BlockSpec((tb, c), lambda i: (i, 0))
    nxt = pl.BlockSpec((SUBLANES, c), lambda i: (jnp.minimum((i + 1) * nb8, last8), 0))
    vec = pl.BlockSpec((1, c), lambda i: (0, 0))
    return _call(body, name=name, grid=(n // tb,), in_specs=[row, nxt, vec], out_specs=row,
                 out_shape=jax.ShapeDtypeStruct((n, c), BF16), compiler_params=_cp("parallel"))(dq, dq, mu)


def _scan_step(s_i, a_t, w_t, b_t, k_t, v_i):
    sa = jnp.sum(s_i * a_t, axis=0, keepdims=True)
    return s_i * w_t + sa * b_t + v_i * k_t, sa


def _carried(comm):
    if comm is None:
        return [], [], [], [], None, None
    outs, n_sems, start, finish = _comm_plan(*comm)
    xs = list(comm[1])
    sems = [pltpu.SemaphoreType.DMA((n_sems,)), pltpu.SemaphoreType.DMA((n_sems,))]
    return xs, [_HBM] * len(xs), outs, sems, start, finish


def _rwkv_scan_fwd(r, w, k, a, b, v, name, comm=None):
    t_len, kd, ln = r.shape
    vh = v.shape[1]
    tc = SCAN_CHUNK
    nc = t_len // tc
    cx, cx_specs, c_outs, c_sems, c_start, c_finish = _carried(comm)
    nx = len(cx)

    def body(r_ref, w_ref, k_ref, a_ref, b_ref, v_ref, *rest):
        c_ins, (y_ref, st_ref, sa_ref), c_out_refs = rest[:nx], rest[nx:nx + 3], rest[nx + 3:nx + 3 + len(c_outs)]
        s_ref = rest[nx + 3 + len(c_outs)]

        @pl.when(pl.program_id(0) == 0)
        def _():
            s_ref[...] = jnp.zeros_like(s_ref)
            if nx:
                c_start(c_ins, c_out_refs, rest[-2], rest[-1])

        st_ref[0, 0] = s_ref[...]

        def step(t, carry):
            a_t, w_t, b_t, k_t, r_t = a_ref[t], w_ref[t], b_ref[t], k_ref[t], r_ref[t]
            for i in range(vh):
                s_new, sa = _scan_step(st_ref[0, t, i], a_t, w_t, b_t, k_t, v_ref[t, pl.ds(i, 1), :])
                st_ref[0, t + 1, i] = s_new
                sa_ref[t, pl.ds(i, 1), :] = sa
                y_ref[t, pl.ds(i, 1), :] = jnp.sum(s_new * r_t, axis=0, keepdims=True)
            return carry

        lax.fori_loop(0, tc, step, 0)
        s_ref[...] = st_ref[0, tc]

        if nx:
            @pl.when(pl.program_id(0) == nc - 1)
            def _():
                c_finish(c_ins, c_out_refs, rest[-2], rest[-1])

    kblk = pl.BlockSpec((tc, kd, ln), lambda c: (c, 0, 0))
    vblk = pl.BlockSpec((tc, vh, ln), lambda c: (c, 0, 0))
    vsd = jax.ShapeDtypeStruct((t_len, vh, ln), F32)
    return _call(
        body, name=name, grid=(nc,), in_specs=[kblk] * 5 + [vblk] + cx_specs,
        out_specs=[vblk, pl.BlockSpec((1, tc + 1, vh, kd, ln), lambda c: (c, 0, 0, 0, 0)), vblk] + [_HBM] * len(c_outs),
        out_shape=[vsd, jax.ShapeDtypeStruct((nc, tc + 1, vh, kd, ln), F32), vsd] + c_outs,
        scratch_shapes=[pltpu.VMEM((vh, kd, ln), F32)] + c_sems,
        compiler_params=_cp("arbitrary"),
    )(r, w, k, a, b, v, *cx)


def _rwkv_scan_bwd(r, w, k, a, b, v, dy, states, sa, name, comm=None):
    t_len, kd, ln = r.shape
    vh = v.shape[1]
    tc = SCAN_CHUNK
    nc = t_len // tc
    half = ln // 2
    cx, cx_specs, c_outs, c_sems, c_start, c_finish = _carried(comm)
    nx = len(cx)

    def body(r_ref, w_ref, k_ref, a_ref, b_ref, v_ref, dy_ref, st_ref, sa_ref, *rest):
        c_ins = rest[:nx]
        dr_o, dw_o, dk_o, da_o, db_o, dv_o = rest[nx:nx + 6]
        c_out_refs = rest[nx + 6:nx + 6 + len(c_outs)]
        ds_ref = rest[nx + 6 + len(c_outs)]

        @pl.when(pl.program_id(0) == 0)
        def _():
            ds_ref[...] = jnp.zeros_like(ds_ref)
            if nx:
                c_start(c_ins, c_out_refs, rest[-2], rest[-1])

        def bwd(tt, carry):
            t = tc - 1 - tt
            a_t, w_t, b_t, k_t, r_t = a_ref[t], w_ref[t], b_ref[t], k_ref[t], r_ref[t]
            z = jnp.zeros((kd, ln), F32)
            dr, dw, dk, da, db = z, z, z, z, z
            for i in range(vh):
                dy_i = dy_ref[t, pl.ds(i, 1), :]
                s_t = st_ref[0, t + 1, i]
                s_p = st_ref[0, t, i]
                d = ds_ref[i] + dy_i * r_t
                dr = dr + s_t * dy_i
                dv_o[t, pl.ds(i, 1), :] = jnp.sum(d * k_t, axis=0, keepdims=True)
                dk = dk + d * v_ref[t, pl.ds(i, 1), :]
                dsa = jnp.sum(d * b_t, axis=0, keepdims=True)
                db = db + d * sa_ref[t, pl.ds(i, 1), :]
                dw = dw + d * s_p
                da = da + s_p * dsa
                ds_ref[i] = d * w_t + dsa * a_t

            def both(x):
                return x + pltpu.roll(x, half, 1)

            dr_o[t] = both(dr)
            dw_o[t] = both(dw)
            dk_o[t] = both(dk)
            da_o[t] = both(da)
            db_o[t] = both(db)
            return carry

        lax.fori_loop(0, tc, bwd, 0)

        if nx:
            @pl.when(pl.program_id(0) == nc - 1)
            def _():
                c_finish(c_ins, c_out_refs, rest[-2], rest[-1])

    kblk = pl.BlockSpec((tc, kd, ln), lambda c: (nc - 1 - c, 0, 0))
    vblk = pl.BlockSpec((tc, vh, ln), lambda c: (nc - 1 - c, 0, 0))
    ksd = jax.ShapeDtypeStruct((t_len, kd, ln), F32)
    return _call(
        body, name=name, grid=(nc,),
        in_specs=[kblk] * 5 + [vblk, vblk, pl.BlockSpec((1, tc + 1, vh, kd, ln), lambda c: (nc - 1 - c, 0, 0, 0, 0)),
                  vblk] + cx_specs,
        out_specs=[kblk] * 5 + [vblk] + [_HBM] * len(c_outs),
        out_shape=[ksd] * 5 + [jax.ShapeDtypeStruct((t_len, vh, ln), F32)] + c_outs,
        scratch_shapes=[pltpu.VMEM((vh, kd, ln), F32)] + c_sems,
        compiler_params=_cp("arbitrary"),
    )(r, w, k, a, b, v, dy, states, sa, *cx)


def _post_common(y_ref, r_ref, k_ref, v_ref, lng_ref, lnb_ref, rk_ref, e, et):
    y = y_ref[...]
    inv_n = 1.0 / RWKV_HEAD
    mean = _headsum(y, e, et) * inv_n
    yc = y - mean
    var = _headsum(yc * yc, e, et) * inv_n
    rstd = lax.rsqrt(var + GN_EPS)
    yh = yc * rstd
    yn = yh * lng_ref[...] + lnb_ref[...]
    bonus = _headsum(r_ref[...] * k_ref[...] * rk_ref[...], e, et)
    return yh, rstd, yn, bonus


def _rwkv_post_fwd(y, r, k, v, g, ln_g, ln_b, r_k, name):
    n, w = y.shape
    hp = max(w // RWKV_HEAD, LANES)
    tb = _tile(n, 256, SUBLANES)

    def body(y_ref, r_ref, k_ref, v_ref, g_ref, lng_ref, lnb_ref, rk_ref, o_ref):
        e, et = _head_mats(w, hp)
        _, _, yn, bonus = _post_common(y_ref, r_ref, k_ref, v_ref, lng_ref, lnb_ref, rk_ref, e, et)
        o_ref[...] = ((yn + bonus * v_ref[...]) * g_ref[...]).astype(BF16)

    blk = pl.BlockSpec((tb, w), lambda i: (i, 0))
    vec = pl.BlockSpec((1, w), lambda i: (0, 0))
    return _call(body, name=name, grid=(n // tb,), in_specs=[blk] * 5 + [vec] * 3, out_specs=blk,
                 out_shape=jax.ShapeDtypeStruct((n, w), BF16),
                 compiler_params=_cp("parallel"))(y, r, k, v, g, ln_g, ln_b, r_k)


def _rwkv_post_bwd(y, r, k, v, g, ln_g, ln_b, r_k, do_cat, name):
    n, w = y.shape
    hp = max(w // RWKV_HEAD, LANES)
    tb = _tile(n, 128, SUBLANES)

    def body(y_ref, r_ref, k_ref, v_ref, g_ref, lng_ref, lnb_ref, rk_ref, do_ref,
             dy_o, dr_o, dk_o, dv_o, dg_o, dlng_o, dlnb_o, drk_o):
        e, et = _head_mats(w, hp)
        yh, rstd, yn, bonus = _post_common(y_ref, r_ref, k_ref, v_ref, lng_ref, lnb_ref, rk_ref, e, et)
        do = do_ref[...]
        vv, rv, kv, rk = v_ref[...], r_ref[...], k_ref[...], rk_ref[...]
        dg_o[...] = do * (yn + bonus * vv)
        dz = do * g_ref[...]
        dbonus = _headsum(dz * vv, e, et)
        dv_o[...] = dz * bonus
        dr_o[...] = dbonus * kv * rk
        dk_o[...] = dbonus * rv * rk
        dyh = dz * lng_ref[...]
        inv_n = 1.0 / RWKV_HEAD
        dy_o[...] = rstd * (dyh - _headsum(dyh, e, et) * inv_n - yh * (_headsum(dyh * yh, e, et) * inv_n))

        @pl.when(pl.program_id(0) == 0)
        def _():
            for o in (dlng_o, dlnb_o, drk_o):
                o[...] = jnp.zeros_like(o)

        dlng_o[...] += jnp.sum(dz * yh, axis=0, keepdims=True)
        dlnb_o[...] += jnp.sum(dz, axis=0, keepdims=True)
        drk_o[...] += jnp.sum(dbonus * rv * kv, axis=0, keepdims=True)

    blk = pl.BlockSpec((tb, w), lambda i: (i, 0))
    vec = pl.BlockSpec((1, w), lambda i: (0, 0))
    sds = jax.ShapeDtypeStruct
    return _call(body, name=name, grid=(n // tb,), in_specs=[blk] * 5 + [vec] * 3 + [blk],
                 out_specs=[blk] * 5 + [vec] * 3,
                 out_shape=[sds((n, w), F32)] * 5 + [sds((1, w), F32)] * 3,
                 compiler_params=_cp("arbitrary"))(y, r, k, v, g, ln_g, ln_b, r_k, do_cat)


def _lru_gates(xb, prev8, gate, cw_ref, cb_ref, wr_ref, br_ref, wi_ref, bi_ref, lam_ref, is_t0):
    c = xb.shape[1]
    nblk = c // LRU_BLOCK_W
    xs = [xb] + [_shift_down(xb, prev8, j) for j in range(1, CONV_WIDTH)]
    xc = cb_ref[...]
    for j in range(CONV_WIDTH):
        xc = xc + xs[CONV_WIDTH - 1 - j] * cw_ref[pl.ds(j, 1), :]
    xcb = xc.astype(BF16)

    def blockmm(w_ref):
        return jnp.concatenate(
            [jnp.dot(xcb[:, h * LRU_BLOCK_W:(h + 1) * LRU_BLOCK_W], w_ref[h], preferred_element_type=F32)
             for h in range(nblk)], axis=1)

    rg = _sig(blockmm(wr_ref) + br_ref[...])
    ig = _sig(blockmm(wi_ref) + bi_ref[...])
    sp = _softplus(-lam_ref[...])
    la = -LRU_C * rg * sp
    av = jnp.exp(la)
    mult = jnp.where(is_t0, 1.0, jnp.sqrt(_neg_expm1(2.0 * la)))
    ge, th = _gelu_parts(gate)
    return dict(xs=xs, xc=xc, xcb=xcb, rg=rg, ig=ig, sp=sp, a=av, mult=mult, ge=ge, th=th)


def _lru_specs(tb, c, nb, rev):
    nb8 = tb // SUBLANES

    def blk_i(i):
        return nb - 1 - i if rev else i

    xb = pl.BlockSpec((tb, c), lambda b, i: (b * nb + blk_i(i), 0))
    gate = pl.BlockSpec((tb, c), lambda b, i: (b * nb + blk_i(i), 1))
    prev = pl.BlockSpec((SUBLANES, c), lambda b, i: (jnp.maximum((b * nb + blk_i(i)) * nb8 - 1, 0), 0))
    return xb, gate, prev


def _lru_fwd(p_lru, prm, t_len, name):
    n, c2 = p_lru.shape
    c = c2 // 2
    nblk = c // LRU_BLOCK_W
    tb = _tile(t_len, 256, SUBLANES)
    nb = t_len // tb
    bsz = n // t_len

    def body(xb_ref, gate_ref, prev_ref, cw_ref, cb_ref, wr_ref, br_ref, wi_ref, bi_ref, lam_ref, ng_ref,
             y_o, h_o, carry):
        i = pl.program_id(1)
        prev8 = jnp.where(i == 0, jnp.zeros_like(prev_ref[...]), prev_ref[...])
        row = lax.broadcasted_iota(jnp.int32, (tb, c), 0)
        f = _lru_gates(xb_ref[...], prev8, gate_ref[...], cw_ref, cb_ref, wr_ref, br_ref, wi_ref, bi_ref, lam_ref,
                       jnp.logical_and(i == 0, row == 0))
        acc_a = f["a"]
        acc_b = f["mult"] * f["ig"] * f["xc"]
        s = 1
        while s < tb:
            keep = row >= s
            a_sh = jnp.where(keep, pltpu.roll(acc_a, s, 0), 1.0)
            b_sh = jnp.where(keep, pltpu.roll(acc_b, s, 0), 0.0)
            acc_b = acc_a * b_sh + acc_b
            acc_a = acc_a * a_sh
            s *= 2

        @pl.when(i == 0)
        def _():
            carry[...] = jnp.zeros_like(carry)

        h = acc_b + acc_a * carry[0:1, :]
        carry[0:1, :] = h[tb - 1:tb, :]
        h_o[...] = h
        y = h * f["ge"]
        rstd = lax.rsqrt(jnp.mean(y * y, axis=-1, keepdims=True) + NORM_EPS)
        y_o[...] = (y * rstd * ng_ref[...]).astype(BF16)

    xb_s, gate_s, prev_s = _lru_specs(tb, c, nb, False)
    c0 = lambda b, i: (0, 0)
    vec = pl.BlockSpec((1, c), c0)
    wsp = pl.BlockSpec((nblk, LRU_BLOCK_W, LRU_BLOCK_W), lambda b, i: (0, 0, 0))
    out = pl.BlockSpec((tb, c), lambda b, i: (b * nb + i, 0))
    return _call(
        body, name=name, grid=(bsz, nb),
        in_specs=[xb_s, gate_s, prev_s, pl.BlockSpec((CONV_WIDTH, c), c0), vec, wsp, vec, wsp, vec, vec, vec],
        out_specs=[out, out],
        out_shape=[jax.ShapeDtypeStruct((n, c), BF16), jax.ShapeDtypeStruct((n, c), F32)],
        scratch_shapes=[pltpu.VMEM((SUBLANES, c), F32)],
        compiler_params=_cp("arbitrary", "arbitrary"),
    )(p_lru, p_lru, p_lru, prm["conv_w"], prm["conv_b"], prm["wr"], prm["br"], prm["wi"], prm["bi"],
      prm["lam"], prm["norm_g"])


def _lru_bwd(p_lru, h, do_cat, prm, t_len, name):
    n, c2 = p_lru.shape
    c = c2 // 2
    nblk = c // LRU_BLOCK_W
    tb = _tile(t_len, 128, SUBLANES)
    nb = t_len // tb
    bsz = n // t_len

    def body(xb_ref, gate_ref, prev_ref, h_ref, hprev_ref, do_ref,
             cw_ref, cb_ref, wr_ref, br_ref, wi_ref, bi_ref, lam_ref, ng_ref,
             dp_o, dcw_o, dcb_o, dwr_o, dbr_o, dwi_o, dbi_o, dlam_o, dng_o,
             a_next, g_next, dxc_next):
        b = pl.program_id(0)
        i = pl.program_id(1)
        blk = nb - 1 - i
        first = blk == 0
        prev8 = jnp.where(first, jnp.zeros_like(prev_ref[...]), prev_ref[...])
        hprev8 = jnp.where(first, jnp.zeros_like(hprev_ref[...]), hprev_ref[...])
        row = lax.broadcasted_iota(jnp.int32, (tb, c), 0)
        is_t0 = jnp.logical_and(first, row == 0)
        gate = gate_ref[...]
        f = _lru_gates(xb_ref[...], prev8, gate, cw_ref, cb_ref, wr_ref, br_ref, wi_ref, bi_ref, lam_ref, is_t0)

        @pl.when(i == 0)
        def _():
            a_next[...] = jnp.zeros_like(a_next)
            g_next[...] = jnp.zeros_like(g_next)
            dxc_next[...] = jnp.zeros_like(dxc_next)

        @pl.when(jnp.logical_and(b == 0, i == 0))
        def _():
            for o in (dcw_o, dcb_o, dwr_o, dbr_o, dwi_o, dbi_o, dlam_o, dng_o):
                o[...] = jnp.zeros_like(o)

        def rsum(x):
            return jnp.sum(x, axis=0, keepdims=True)

        hv = h_ref[...]
        hprev = _shift_down(hv, hprev8, 1)
        ge = f["ge"]
        y = hv * ge
        rstd = lax.rsqrt(jnp.mean(y * y, axis=-1, keepdims=True) + NORM_EPS)
        yh = y * rstd
        dyn = do_ref[...]
        t = dyn * ng_ref[...]
        dy = rstd * (t - yh * jnp.mean(t * yh, axis=-1, keepdims=True))
        dng_o[...] += rsum(dyn * yh)
        dgate = dy * hv * _gelu_grad(gate, f["th"])

        av = f["a"]
        acc_c = _shift_up(av, a_next[...], 1)
        acc_g = dy * ge
        s = 1
        while s < tb:
            keep = row < tb - s
            c_sh = jnp.where(keep, pltpu.roll(acc_c, tb - s, 0), 1.0)
            g_sh = jnp.where(keep, pltpu.roll(acc_g, tb - s, 0), 0.0)
            acc_g = acc_g + acc_c * g_sh
            acc_c = acc_c * c_sh
            s *= 2
        gtot = acc_g + acc_c * g_next[0:1, :]
        a_next[0:1, :] = av[0:1, :]
        g_next[0:1, :] = gtot[0:1, :]

        xc, ig, rg, mult = f["xc"], f["ig"], f["rg"], f["mult"]
        da = gtot * hprev
        dmult = gtot * ig * xc
        dig = gtot * mult * xc
        dxc = gtot * mult * ig
        da = da + jnp.where(is_t0, 0.0, -dmult * av / mult)
        dla = da * av
        drg = dla * (-LRU_C) * f["sp"]
        dlam_o[...] += rsum(dla * rg) * LRU_C * _sig(-lam_ref[...])
        dzr = drg * rg * (1.0 - rg)
        dzi = dig * ig * (1.0 - ig)
        dbr_o[...] += rsum(dzr)
        dbi_o[...] += rsum(dzi)
        dzrb, dzib = dzr.astype(BF16), dzi.astype(BF16)
        xcb = f["xcb"]
        back = []
        for hh in range(nblk):
            sl = slice(hh * LRU_BLOCK_W, (hh + 1) * LRU_BLOCK_W)
            dwr_o[hh] += _dot_tn(xcb[:, sl], dzrb[:, sl])
            dwi_o[hh] += _dot_tn(xcb[:, sl], dzib[:, sl])
            back.append(_dot_nt(dzrb[:, sl], wr_ref[hh]) + _dot_nt(dzib[:, sl], wi_ref[hh]))
        dxc = dxc + jnp.concatenate(back, axis=1)
        dcb_o[...] += rsum(dxc)
        xs = f["xs"]
        dcw_o[...] += jnp.concatenate([rsum(dxc * xs[CONV_WIDTH - 1 - j]) for j in range(CONV_WIDTH)], axis=0)
        nxt = dxc_next[...]
        dxb = dxc * cw_ref[pl.ds(CONV_WIDTH - 1, 1), :]
        for j in range(1, CONV_WIDTH):
            dxb = dxb + _shift_up(dxc, nxt, j) * cw_ref[pl.ds(CONV_WIDTH - 1 - j, 1), :]
        dxc_next[...] = dxc[0:SUBLANES, :]
        dp_o[:, 0:c] = dxb.astype(BF16)
        dp_o[:, c:2 * c] = dgate.astype(BF16)

    xb_s, gate_s, prev_s = _lru_specs(tb, c, nb, True)
    c0 = lambda b, i: (0, 0)
    vec = pl.BlockSpec((1, c), c0)
    wsp = pl.BlockSpec((nblk, LRU_BLOCK_W, LRU_BLOCK_W), lambda b, i: (0, 0, 0))
    cwsp = pl.BlockSpec((CONV_WIDTH, c), c0)
    sds = jax.ShapeDtypeStruct
    return _call(
        body, name=name, grid=(bsz, nb),
        in_specs=[xb_s, gate_s, prev_s, xb_s, prev_s, gate_s, cwsp, vec, wsp, vec, wsp, vec, vec, vec],
        out_specs=[pl.BlockSpec((tb, 2 * c), lambda b, i: (b * nb + nb - 1 - i, 0)),
                   cwsp, vec, wsp, vec, wsp, vec, vec, vec],
        out_shape=[sds((n, 2 * c), BF16), sds((CONV_WIDTH, c), F32), sds((1, c), F32),
                   sds((nblk, LRU_BLOCK_W, LRU_BLOCK_W), F32), sds((1, c), F32),
                   sds((nblk, LRU_BLOCK_W, LRU_BLOCK_W), F32), sds((1, c), F32), sds((1, c), F32), sds((1, c), F32)],
        scratch_shapes=[pltpu.VMEM((SUBLANES, c), F32)] * 3,
        compiler_params=_cp("arbitrary", "arbitrary"),
    )(p_lru, p_lru, p_lru, h, h, do_cat, prm["conv_w"], prm["conv_b"], prm["wr"], prm["br"], prm["wi"], prm["bi"],
      prm["lam"], prm["norm_g"])


def _adamw(g, w, m, v, name):
    rows, cols = g.shape
    tb = _tile(rows, 128, SUBLANES)

    def body(g_ref, w_ref, m_ref, v_ref, d_o, m_o, v_o):
        gv = g_ref[...]
        mn = ADAM_B1 * m_ref[...] + (1.0 - ADAM_B1) * gv
        vn = ADAM_B2 * v_ref[...] + (1.0 - ADAM_B2) * (gv * gv)
        m_o[...] = mn
        v_o[...] = vn
        d_o[...] = -ADAM_LR * ((mn / _BC1) / (jnp.sqrt(vn / _BC2) + ADAM_EPS) + ADAM_WD * w_ref[...])

    blk = pl.BlockSpec((tb, cols), lambda i: (i, 0))
    return _call(body, name=name, grid=(rows // tb,), in_specs=[blk] * 4, out_specs=[blk] * 3,
                 out_shape=[jax.ShapeDtypeStruct((rows, cols), F32)] * 3, compiler_params=_cp("parallel"))(g, w, m, v)


def _adamw_halves(mine, theirs, w, m, v, name):
    a, b = mine.shape
    tc = _col_tile(a, b)
    w, m, v = (t.reshape(2, a, b) for t in (w, m, v))

    def body(mine_ref, theirs_ref, w_ref, m_ref, v_ref, g_o, d_o, m_o, v_o):
        gv = jnp.where(pl.program_id(0) == lax.axis_index("c"), mine_ref[...], theirs_ref[...])
        mn = ADAM_B1 * m_ref[...] + (1.0 - ADAM_B1) * gv
        vn = ADAM_B2 * v_ref[...] + (1.0 - ADAM_B2) * (gv * gv)
        g_o[...] = gv
        m_o[...] = mn
        v_o[...] = vn
        d_o[...] = -ADAM_LR * ((mn / _BC1) / (jnp.sqrt(vn / _BC2) + ADAM_EPS) + ADAM_WD * w_ref[...])

    half = pl.BlockSpec((a, tc), lambda h, j: (0, j))
    blk = pl.BlockSpec((None, a, tc), lambda h, j: (h, 0, j))
    return _call(body, name=name, grid=(2, b // tc), in_specs=[half, half, blk, blk, blk], out_specs=[blk] * 4,
                 out_shape=[jax.ShapeDtypeStruct((2, a, b), F32)] * 4,
                 compiler_params=_cp("parallel", "parallel"))(mine, theirs, w, m, v)


def _pair_sum(x4, recv, name):
    _, _, a, b = x4.shape
    tc = _col_tile(a, b)

    def body(x_ref, r_ref, o_ref):
        mine = x_ref[lax.axis_index("c")]
        o_ref[...] = (mine.astype(F32) + r_ref[...].astype(F32)).astype(BF16)

    return _call(
        body, name=name, grid=(4, b // tc),
        in_specs=[pl.BlockSpec((None, 2, a, tc), lambda j, i: (j, 0, 0, i)),
                  pl.BlockSpec((None, a, tc), lambda j, i: (j, 0, i))],
        out_specs=pl.BlockSpec((None, a, tc), lambda j, i: (j, 0, i)),
        out_shape=jax.ShapeDtypeStruct((4, a, b), BF16), compiler_params=_cp("parallel", "parallel"))(x4, recv)


def _chip_sum(x4, name):
    _, a, b = x4.shape
    ta = _tile(a, 256, SUBLANES)

    def body(x_ref, o_ref):
        acc = x_ref[0] + x_ref[1]
        acc = acc + x_ref[2]
        o_ref[...] = acc + x_ref[3]

    return _call(
        body, name=name, grid=(a // ta,),
        in_specs=[pl.BlockSpec((4, ta, b), lambda i: (0, i, 0))],
        out_specs=pl.BlockSpec((ta, b), lambda i: (i, 0)),
        out_shape=jax.ShapeDtypeStruct((a, b), F32), compiler_params=_cp("parallel"))(x4)


def _peer_sum(own4, parts, name):
    _, a, b = parts.shape
    tc = _col_tile(a, b)

    def body(own_ref, p_ref, o_ref):
        me = 2 * lax.axis_index("x") + lax.axis_index("y")
        acc = own_ref[me].astype(F32) + p_ref[0].astype(F32)
        acc = acc + p_ref[1].astype(F32)
        o_ref[...] = acc + p_ref[2].astype(F32)

    return _call(
        body, name=name, grid=(b // tc,),
        in_specs=[pl.BlockSpec((4, a, tc), lambda i: (0, 0, i)), pl.BlockSpec((3, a, tc), lambda i: (0, 0, i))],
        out_specs=pl.BlockSpec((a, tc), lambda i: (0, i)),
        out_shape=jax.ShapeDtypeStruct((a, b), F32), compiler_params=_cp("parallel"))(own4, parts)


def _add2(x, y, name):
    rows, cols = x.shape
    tb = _tile(rows, 512, SUBLANES)

    def body(x_ref, y_ref, o_ref):
        o_ref[...] = x_ref[...] + y_ref[...]

    blk = pl.BlockSpec((tb, cols), lambda i: (i, 0))
    return _call(body, name=name, grid=(rows // tb,), in_specs=[blk, blk], out_specs=blk,
                 out_shape=jax.ShapeDtypeStruct((rows, cols), x.dtype), compiler_params=_cp("parallel"))(x, y)


_HBM = pl.BlockSpec(memory_space=pltpu.HBM)


def _place():
    x, y, c = lax.axis_index("x"), lax.axis_index("y"), lax.axis_index("c")
    chips = [(1 - x, y), (x, 1 - y), (1 - x, 1 - y)]
    return x, y, c, chips


def _comm_call(body, name, xs, out_shapes, n_sems):
    return _call(
        body, name=name, in_specs=[_HBM] * len(xs), out_specs=[_HBM] * len(out_shapes), out_shape=out_shapes,
        scratch_shapes=[pltpu.SemaphoreType.DMA((n_sems,)), pltpu.SemaphoreType.DMA((n_sems,)),
                        pltpu.SemaphoreType.DMA((len(xs),))],
    )(*xs)


def _all_gather_chips(xs, name):
    n = len(xs)

    def body(*refs):
        ins, outs = refs[:n], refs[n:2 * n]
        ssem, rsem, _ = refs[2 * n:]
        _gather_start(ins, outs, ssem, rsem)
        _gather_finish(ins, outs, ssem, rsem)

    outs = [jax.ShapeDtypeStruct((4,) + v.shape, v.dtype) for v in xs]
    return _comm_call(body, name, xs, outs, GATHER_SEMS * n)


GATHER_SEMS = 7
PEER_SEMS = 3


def _remote(src, dst, ssem, rsem, k, dev):
    return pltpu.make_async_remote_copy(src_ref=src, dst_ref=dst, send_sem=ssem.at[k], recv_sem=rsem.at[k],
                                        device_id=dev, device_id_type=MESH)


def _gather_start(ins, outs, ssem, rsem):
    x, y, c, chips = _place()
    me = 2 * x + y
    for i in range(len(ins)):
        for j, (px, py) in enumerate(chips):
            _remote(ins[i].at[c], outs[i].at[me, c], ssem, rsem, GATHER_SEMS * i + j, (px, py, c)).start()
        _remote(ins[i], outs[i].at[me], ssem, rsem, GATHER_SEMS * i + 6, (x, y, 1 - c)).start()


def _gather_finish(ins, outs, ssem, rsem):
    x, y, c, chips = _place()
    me = 2 * x + y
    sib = (x, y, 1 - c)
    n = len(ins)
    for i in range(n):
        for j, (px, py) in enumerate(chips):
            slot = outs[i].at[2 * px + py, c]
            _remote(slot, slot, ssem, rsem, GATHER_SEMS * i + j, (px, py, c)).wait_recv()
            _remote(slot, slot, ssem, rsem, GATHER_SEMS * i + 3 + j, sib).start()
    for i in range(n):
        own = outs[i].at[me]
        _remote(own, own, ssem, rsem, GATHER_SEMS * i + 6, sib).wait_recv()
        for j, (px, py) in enumerate(chips):
            slot = outs[i].at[2 * px + py, 1 - c]
            _remote(slot, slot, ssem, rsem, GATHER_SEMS * i + 3 + j, sib).wait_recv()
    for i in range(n):
        for j, (px, py) in enumerate(chips):
            slot = outs[i].at[2 * px + py, c]
            _remote(ins[i].at[c], outs[i].at[me, c], ssem, rsem, GATHER_SEMS * i + j, (px, py, c)).wait_send()
            _remote(slot, slot, ssem, rsem, GATHER_SEMS * i + 3 + j, sib).wait_send()
        _remote(ins[i], outs[i].at[me], ssem, rsem, GATHER_SEMS * i + 6, sib).wait_send()


def _peer_copies(ins, outs, ssem, rsem):
    x, y, c, chips = _place()
    return [_remote(ins[i].at[2 * px + py], outs[i].at[j], ssem, rsem, PEER_SEMS * i + j, (px, py, c))
            for i in range(len(ins)) for j, (px, py) in enumerate(chips)]


def _comm_plan(kind, xs):
    if kind == "gather":
        outs = [jax.ShapeDtypeStruct((4,) + v.shape, v.dtype) for v in xs]
        return outs, GATHER_SEMS * len(xs), _gather_start, _gather_finish
    if kind == "swap_half":
        def swaps(ins, outs, ssem, rsem):
            x, y, c, _ = _place()
            return [_remote(ins[i].at[:, 1 - c], outs[i], ssem, rsem, i, (x, y, 1 - c)) for i in range(len(ins))]

        def start_swaps(ins, outs, ssem, rsem):
            for cp in swaps(ins, outs, ssem, rsem):
                cp.start()

        def finish_swaps(ins, outs, ssem, rsem):
            for cp in swaps(ins, outs, ssem, rsem):
                cp.wait()

        outs = [jax.ShapeDtypeStruct((v.shape[0],) + v.shape[2:], v.dtype) for v in xs]
        return outs, len(xs), start_swaps, finish_swaps

    def start(ins, outs, ssem, rsem):
        for cp in _peer_copies(ins, outs, ssem, rsem):
            cp.start()

    def finish(ins, outs, ssem, rsem):
        for cp in _peer_copies(ins, outs, ssem, rsem):
            cp.wait()

    outs = [jax.ShapeDtypeStruct((3,) + v.shape[1:], v.dtype) for v in xs]
    return outs, PEER_SEMS * len(xs), start, finish


def _sibling_swap(xs, pick_half, name):
    n = len(xs)

    def body(*refs):
        ins, outs = refs[:n], refs[n:2 * n]
        ssem, rsem, _ = refs[2 * n:]
        x, y, c, _ = _place()
        cps = []
        for i in range(n):
            src = ins[i].at[:, 1 - c] if pick_half else ins[i]
            cp = pltpu.make_async_remote_copy(src_ref=src, dst_ref=outs[i], send_sem=ssem.at[i], recv_sem=rsem.at[i],
                                              device_id=(x, y, 1 - c), device_id_type=MESH)
            cp.start()
            cps.append(cp)
        for cp in cps:
            cp.wait()

    outs = [jax.ShapeDtypeStruct((v.shape[0],) + v.shape[2:] if pick_half else v.shape, v.dtype) for v in xs]
    return _comm_call(body, name, xs, outs, n)


def _to_scan_k(x, bsz, t_len):
    h = x.shape[1] // RWKV_HEAD
    y = jnp.broadcast_to(x.reshape(1, bsz, t_len, h, RWKV_HEAD), (2, bsz, t_len, h, RWKV_HEAD))
    return y.transpose(2, 4, 0, 1, 3).reshape(t_len, RWKV_HEAD, 2 * bsz * h)


def _to_scan_v(x, bsz, t_len):
    h = x.shape[1] // RWKV_HEAD
    y = x.reshape(bsz, t_len, h, 2, RWKV_HEAD // 2).transpose(1, 4, 3, 0, 2)
    return y.reshape(t_len, RWKV_HEAD // 2, 2 * bsz * h)


def _from_scan_k(x, bsz, t_len):
    h = x.shape[2] // (2 * bsz)
    y = x[:, :, :bsz * h].reshape(t_len, RWKV_HEAD, bsz, h).transpose(2, 0, 3, 1)
    return y.reshape(bsz * t_len, h * RWKV_HEAD)


def _from_scan_v(x, bsz, t_len):
    h = x.shape[2] // (2 * bsz)
    y = x.reshape(t_len, RWKV_HEAD // 2, 2, bsz, h).transpose(3, 0, 4, 2, 1)
    return y.reshape(bsz * t_len, h * RWKV_HEAD)


def _pad_rows(x, rows):
    return jnp.pad(x, ((0, rows - x.shape[0]), (0, 0)))


def _pad_cols(x, cols):
    return jnp.pad(x, ((0, 0), (0, cols - x.shape[1])))


def _cols_from_shards(g4):
    _, r, cs = g4.shape
    return g4.transpose(1, 0, 2).reshape(r, 4 * cs)


def _cols_to_shards(g):
    r, cols = g.shape
    return g.reshape(r, 4, cols // 4).transpose(1, 0, 2)


def kernel(x, norm_mix_g, w_in, mu_shift, rwkv_w0, rwkv_w2, rwkv_a0, rwkv_a2, rwkv_g2, rwkv_k_k, rwkv_k_a, rwkv_r_k, rwkv_ln_g, rwkv_ln_b, conv_w, conv_b, lru_wr, lru_br, lru_wi, lru_bi, lru_lambda, lru_norm_g, w_out, norm_ffn_g, ffn_w_gate, ffn_w_up, ffn_w_down, norm_final_g, loss_target, m_norm_mix_g, m_w_in, m_mu_shift, m_rwkv_w0, m_rwkv_w2, m_rwkv_a0, m_rwkv_a2, m_rwkv_g2, m_rwkv_k_k, m_rwkv_k_a, m_rwkv_r_k, m_rwkv_ln_g, m_rwkv_ln_b, m_conv_w, m_conv_b, m_lru_wr, m_lru_br, m_lru_wi, m_lru_bi, m_lru_lambda, m_lru_norm_g, m_w_out, m_norm_ffn_g, m_ffn_w_gate, m_ffn_w_up, m_ffn_w_down, m_norm_final_g, v_norm_mix_g, v_w_in, v_mu_shift, v_rwkv_w0, v_rwkv_w2, v_rwkv_a0, v_rwkv_a2, v_rwkv_g2, v_rwkv_k_k, v_rwkv_k_a, v_rwkv_r_k, v_rwkv_ln_g, v_rwkv_ln_b, v_conv_w, v_conv_b, v_lru_wr, v_lru_br, v_lru_wi, v_lru_bi, v_lru_lambda, v_lru_norm_g, v_w_out, v_norm_ffn_g, v_ffn_w_gate, v_ffn_w_up, v_ffn_w_down, v_norm_final_g):
    names = ['norm_mix_g', 'w_in', 'mu_shift', 'rwkv_w0', 'rwkv_w2', 'rwkv_a0', 'rwkv_a2', 'rwkv_g2', 'rwkv_k_k',
             'rwkv_k_a', 'rwkv_r_k', 'rwkv_ln_g', 'rwkv_ln_b', 'conv_w', 'conv_b', 'lru_wr', 'lru_br', 'lru_wi',
             'lru_bi', 'lru_lambda', 'lru_norm_g', 'w_out', 'norm_ffn_g', 'ffn_w_gate', 'ffn_w_up', 'ffn_w_down',
             'norm_final_g']
    env = locals()
    wts = {k: env[k] for k in names}
    mom_m = {k: env["m_" + k] for k in names}
    mom_v = {k: env["v_" + k] for k in names}

    bsz, t_len, d = x.shape
    n = bsz * t_len
    w = rwkv_w0.shape[1]
    lw = lru_br.shape[1]
    dl, al, gl = rwkv_w2.shape[1], rwkv_a2.shape[1], rwkv_g2.shape[1]
    dlp, alp, glp = _ceil_to(dl, LANES), _ceil_to(al, LANES), _ceil_to(gl, LANES)
    lp = dlp + alp + glp
    rc = 3 * w + dl + al + gl
    chip = 2 * lax.axis_index("x") + lax.axis_index("y")

    big = ['w_in', 'w_out', 'ffn_w_gate', 'ffn_w_up', 'ffn_w_down']
    small_sh = ['rwkv_w2', 'rwkv_a2', 'rwkv_g2', 'conv_w']

    def halves(a2d):
        return a2d.reshape(2, a2d.shape[0] // 2, a2d.shape[1])

    col_sharded = ('w_in', 'ffn_w_gate', 'ffn_w_up')

    def work(k, t):
        return jnp.swapaxes(t[0], 0, 1) if k in col_sharded else t[0]

    def unwork(k, t2):
        return (jnp.swapaxes(t2, 0, 1) if k in col_sharded else t2)[None]

    def rows_of(g):
        return g.reshape(g.shape[0] * g.shape[1] * g.shape[2], g.shape[3])

    send = [halves(work('w_in', w_in).astype(BF16))] + [halves(wts[k][0]) for k in small_sh]
    got = _all_gather_chips(send, "gather_w_in")
    later = ['w_out', 'ffn_w_gate', 'ffn_w_up', 'ffn_w_down']
    send_later = [halves(work(k, wts[k]).astype(BF16)) for k in later]
    full = {}
    for k, g in zip(small_sh, got[1:]):
        full[k] = _cols_from_shards(g.reshape(4, g.shape[1] * g.shape[2], g.shape[3]))
    wi_t = rows_of(got[0])
    w_rkv = wi_t[:3 * w]
    w_lru = wi_t[rc:]
    o = 3 * w
    w_lora = jnp.concatenate([_pad_rows(wi_t[o:o + dl], dlp), _pad_rows(wi_t[o + dl:o + dl + al], alp),
                              _pad_rows(wi_t[o + dl + al:rc], glp)], axis=0)
    mu = mu_shift
    prm_r = dict(
        mu_rkv=mu[:, :3 * w],
        mu_lora=jnp.concatenate([_pad_cols(mu[:, o:o + dl], dlp), _pad_cols(mu[:, o + dl:o + dl + al], alp),
                                 _pad_cols(mu[:, o + dl + al:rc], glp)], axis=1),
        w0=rwkv_w0, a0=rwkv_a0, k_k=rwkv_k_k, k_a=rwkv_k_a,
        w2=_pad_rows(full['rwkv_w2'], dlp).astype(BF16), a2=_pad_rows(full['rwkv_a2'], alp).astype(BF16),
        g2=_pad_rows(full['rwkv_g2'], glp).astype(BF16))
    ln_g, ln_b, r_k = rwkv_ln_g, rwkv_ln_b, rwkv_r_k.reshape(1, w)
    prm_l = dict(conv_w=full['conv_w'], conv_b=conv_b, wr=lru_wr[0].astype(BF16), br=lru_br,
                 wi=lru_wi[0].astype(BF16), bi=lru_bi, lam=lru_lambda, norm_g=lru_norm_g)
    g_final = norm_final_g.reshape(1, d)

    x2 = x.reshape(n, d)
    u1 = _rmsnorm_fwd(x2, norm_mix_g, "norm_mix")
    p_rkv = _mm(u1, w_rkv, name="in_rkv", tb=True)
    p_lru = _mm(u1, w_lru, name="in_lru", tb=True)
    p_lora = _mm(u1, w_lora, name="in_lora", tb=True)
    r_t, dec_t, k_t, v_t, na_t, nb_t, g_t = _rwkv_prep_fwd(p_rkv, p_lora, prm_r, t_len, "rwkv_prep")
    sk = [_to_scan_k(a, bsz, t_len) for a in (r_t, dec_t, k_t, na_t, nb_t)]
    sv = _to_scan_v(v_t, bsz, t_len)
    y_s, states, sa_s, got_wo, got_wg =_rwkv_scan_fwd(*sk, sv, name="rwkv_scan", comm=("gather", send_later[:2]))
    wo, wg = rows_of(got_wo), rows_of(got_wg)
    y_t = _from_scan_v(y_s, bsz, t_len)
    y_a = _rwkv_post_fwd(y_t, r_t, k_t, v_t, g_t, ln_g, ln_b, r_k, "rwkv_post")
    y_b, h_lru = _lru_fwd(p_lru, prm_l, t_len, "lru_fwd")
    h1 = _mm(y_a, wo[:w], name="out_a", res=x2)
    h1 = _mm(y_b, wo[w:], name="out_b", res=h1)
    u2 = _rmsnorm_fwd(h1, norm_ffn_g, "norm_ffn")
    ffc = (1024, 256, 4096)
    gate, got_wu = _mm(u2, wg, name="ffn_gate", tb=True, caps=ffc, comm=("gather", send_later[2:3]))
    wu = rows_of(got_wu)
    up, act, got_wd = _mm(u2, wu, name="ffn_up", tb=True, caps=(1024, 256, 4096), comm=("gather", send_later[3:4]),
                          epi=(_swiglu_tile, [gate], [F32, BF16]))
    wd = rows_of(got_wd)
    h2 = _mm(act, wd, name="ffn_down", res=h1, caps=(1024, 256, 11008), resident="a")

    dh2, dh2b, g_norm_final, loss_vec = _loss_head(h2, g_final, loss_target.reshape(n, d), "loss_head")
    loss = lax.psum(loss_vec[0, 0], ("x", "y", "c"))
    dgate, dup = _mm(dh2b, wd, name="d_act", tb=True, caps=(1024, 256, 4096),
                     epi=(_swiglu_bwd_tile, [gate, up], [BF16, BF16]))
    shards = lambda g: g.reshape(4, 2, g.shape[0] // 8, g.shape[1])
    dwc = dict(ta=True, out_dtype=BF16, n_outer=True, caps=(256, 2048, 4096), resident="b")
    gw_down = _mm(act, dh2b, name="dw_down", **dwc)
    gw_gate = _mm(dgate, u2, name="dw_gate", **dwc)
    gw_up = _mm(dup, u2, name="dw_up", **dwc)
    g4f = [shards(g) for g in (gw_gate, gw_up, gw_down)]
    du2, *sib_f = _mm(dgate, wg, name="du2_gate", caps=(512, 256, 11008), comm=("swap_half", g4f))
    du2 = _mm(dup, wu, name="du2_up", res=du2, caps=(1024, 256, 11008), resident="a")
    dh1, dh1b, g_norm_ffn = _rmsnorm_bwd(du2, h1, norm_ffn_g, dh2, "norm_ffn_bwd")
    dcat = _mm(dh1b, wo, name="d_cat", tb=True)
    gw_out = jnp.concatenate([_mm(y_a, dh1b, name="dw_out_a", ta=True, out_dtype=BF16),
                              _mm(y_b, dh1b, name="dw_out_b", ta=True, out_dtype=BF16)], axis=0)
    g4o = [shards(gw_out)]
    sib_o = _sibling_swap(g4o, True, "grad_sibling_out")
    pair_a = [_pair_sum(a4, s, "grad_pair_sum_%d" % i) for i, (a4, s) in enumerate(zip(g4o + g4f, list(sib_o) + list(sib_f)))]
    (dp_lru, g_conv_w, g_conv_b, g_wr, g_br, g_wi, g_bi, g_lam, g_lng) = _lru_bwd(
        p_lru, h_lru, dcat, prm_l, t_len, "lru_bwd")
    dy_t, dr_p, dk_p, dv_p, dg_t, g_ln_g, g_ln_b, g_r_k = _rwkv_post_bwd(
        y_t, r_t, k_t, v_t, g_t, ln_g, ln_b, r_k, dcat, "rwkv_post_bwd")
    dr_s, dw_s, dk_s, da_s, db_s, dv_s, *parts_a = _rwkv_scan_bwd(
        *sk, sv, _to_scan_v(dy_t, bsz, t_len), states, sa_s, name="rwkv_scan_bwd", comm=("peer", pair_a))
    grads = [_from_scan_k(dr_s, bsz, t_len), dr_p, _from_scan_k(dw_s, bsz, t_len), _from_scan_k(dk_s, bsz, t_len),
             dk_p, _from_scan_v(dv_s, bsz, t_len), dv_p, _from_scan_k(da_s, bsz, t_len),
             _from_scan_k(db_s, bsz, t_len), dg_t]
    (dq_r, dq_l, g_mu_r, g_mu_l, g_w0, g_a0, g_kk, g_ka, g_w2, g_a2, g_g2) = _rwkv_prep_bwd(
        p_rkv, p_lora, prm_r, grads, t_len, "rwkv_prep_bwd")
    dp_rkv = _shift_combine(dq_r, prm_r["mu_rkv"], t_len, "shift_bwd_rkv")
    dp_lora = _shift_combine(dq_l, prm_r["mu_lora"], t_len, "shift_bwd_lora")
    gi_rkv = _mm(dp_rkv, u1, name="dw_in_rkv", ta=True, out_dtype=BF16)
    gi_lru = _mm(dp_lru, u1, name="dw_in_lru", ta=True, out_dtype=BF16)
    gi_lora = _mm(dp_lora, u1, name="dw_in_lora", ta=True, out_dtype=BF16)
    gw_in = jnp.concatenate([gi_rkv, gi_lora[:dl], gi_lora[dlp:dlp + al], gi_lora[dlp + alp:dlp + alp + gl], gi_lru],
                            axis=0)
    g4b = [shards(gw_in)]
    sib_b = _sibling_swap(g4b, True, "grad_sibling_in")
    pair_b = [_pair_sum(g4b[0], sib_b[0], "grad_pair_sum_in")]
    du1, *parts_b = _mm(dp_rkv, w_rkv, name="du1_rkv", comm=("peer", pair_b))
    du1 = _mm(dp_lru, w_lru, name="du1_lru", res=du1)
    du1 = _mm(dp_lora, w_lora, name="du1_lora", res=du1)
    gx, _, g_norm_mix = _rmsnorm_bwd(du1, x2, norm_mix_g, dh1, "norm_mix_bwd")

    pair, parts = pair_b + pair_a, list(parts_b) + list(parts_a)
    mine = [_peer_sum(own4, p3, "grad_chip_sum_%d" % i) for i, (own4, p3) in enumerate(zip(pair, parts))]
    theirs = _sibling_swap(mine, False, "grad_share")

    g_mu = jnp.concatenate([g_mu_r, g_mu_l[:, :dl], g_mu_l[:, dlp:dlp + al], g_mu_l[:, dlp + alp:dlp + alp + gl]],
                           axis=1)
    small = dict(norm_mix_g=g_norm_mix, mu_shift=g_mu, rwkv_w0=g_w0, rwkv_w2=g_w2[:dl], rwkv_a0=g_a0,
                 rwkv_a2=g_a2[:al], rwkv_g2=g_g2[:gl], rwkv_k_k=g_kk, rwkv_k_a=g_ka, rwkv_r_k=g_r_k,
                 rwkv_ln_g=g_ln_g, rwkv_ln_b=g_ln_b, conv_w=g_conv_w, conv_b=g_conv_b, lru_wr=g_wr, lru_br=g_br,
                 lru_wi=g_wi, lru_bi=g_bi, lru_lambda=g_lam, lru_norm_g=g_lng, norm_ffn_g=g_norm_ffn,
                 norm_final_g=g_norm_final)
    small_names = list(small)
    sizes = [small[k].size for k in small_names]
    total = sum(sizes)
    padded = _ceil_to(total, 512 * LANES)

    def pack(arrs):
        flat = jnp.concatenate([a.reshape(-1) for a in arrs] + [jnp.zeros((padded - sum(a.size for a in arrs),), F32)])
        return flat.reshape(padded // LANES, LANES)

    packed = pack([small[k] for k in small_names])
    other = _sibling_swap([packed], False, "small_sibling")[0]
    chip_sum = _add2(packed, other, "small_pair_sum")
    all4 = _all_gather_chips([halves(chip_sum)], "small_chips")[0].reshape((4,) + chip_sum.shape)
    red = _chip_sum(all4, "small_chip_sum").reshape(-1)
    small_g = {}
    off = 0
    for k, sz in zip(small_names, sizes):
        full_g = red[off:off + sz].reshape(small[k].shape)
        off += sz
        if k in small_sh:
            cs = full_g.shape[1] // 4
            full_g = lax.dynamic_slice_in_dim(full_g, chip * cs, cs, axis=1)
        small_g[k] = full_g.reshape(wts[k].shape)

    grad_w, delta_w, new_m, new_v = {}, {}, {}, {}
    for k, g_mine, g_theirs in zip(big, mine, theirs):
        res = _adamw_halves(g_mine, g_theirs, work(k, wts[k]), work(k, mom_m[k]), work(k, mom_v[k]), "adamw_" + k)
        grad_w[k], delta_w[k], new_m[k], new_v[k] = (unwork(k, t.reshape(2 * t.shape[1], t.shape[2])) for t in res)
    lsizes = [small_g[k].size for k in small_names]
    lpad = _ceil_to(sum(lsizes), 128 * LANES)

    def lpack(tree):
        arrs = [tree[k].reshape(-1) for k in small_names]
        flat = jnp.concatenate(arrs + [jnp.zeros((lpad - sum(lsizes),), F32)])
        return flat.reshape(lpad // LANES, LANES)

    dlt, mn, vn = _adamw(lpack(small_g), lpack(wts), lpack(mom_m), lpack(mom_v), "adamw_small")
    off = 0
    for k, sz in zip(small_names, lsizes):
        shp = wts[k].shape
        grad_w[k] = small_g[k]
        delta_w[k] = dlt.reshape(-1)[off:off + sz].reshape(shp)
        new_m[k] = mn.reshape(-1)[off:off + sz].reshape(shp)
        new_v[k] = vn.reshape(-1)[off:off + sz].reshape(shp)
        off += sz

    return (loss, gx.reshape(bsz, t_len, d), *[grad_w[k] for k in names], *[delta_w[k] for k in names],
            *[new_m[k] for k in names], *[new_v[k] for k in names])
```

```python
import jax
import jax.numpy as jnp
from jax import lax
from jax.experimental import pallas as pl
from jax.experimental.pallas import tpu as pltpu

F32 = jnp.float32
BF16 = jnp.bfloat16
MESH = pl.DeviceIdType.MESH
_call = pl.pallas_call

V7X_VMEM_LIMIT = 56 * 1024 * 1024
LANES = 128
SUBLANES = 8

RWKV_HEAD = 64
LRU_BLOCK_W = 128
CONV_WIDTH = 4
LRU_C = 8.0
NORM_EPS = 1e-6
GN_EPS = 64e-5
KK_EPS = 1e-24
SCAN_CHUNK = 16

ADAM_LR = 0.001
ADAM_B1 = 0.9
ADAM_B2 = 0.999
ADAM_EPS = 1e-08
ADAM_WD = 0.01
ADAM_STEP = 10
_BC1 = 1.0 - ADAM_B1 ** ADAM_STEP
_BC2 = 1.0 - ADAM_B2 ** ADAM_STEP


def _cp(*sem):
    return pltpu.CompilerParams(dimension_semantics=tuple(sem), vmem_limit_bytes=V7X_VMEM_LIMIT)


def _tile(n, cap, unit=LANES):
    if n <= cap:
        return n
    best = None
    d = unit
    while d <= cap:
        if n % d == 0:
            best = d
        d += unit
    return n if best is None else best


def _ceil_to(n, m):
    return -(-n // m) * m


ELEMENTWISE_BLOCK_BYTES = 3 * 512 * 1024


def _col_tile(rows, cols):
    cap = max(LANES, ELEMENTWISE_BLOCK_BYTES // (4 * rows) // LANES * LANES)
    return _tile(cols, cap)


def _sig(x):
    return 1.0 / (1.0 + jnp.exp(-x))


def _log1p(x):
    return jnp.where(x < 0.01, x * (1.0 - x * (0.5 - x * (1.0 / 3.0))), jnp.log(1.0 + x))


def _softplus(x):
    return jnp.maximum(x, 0.0) + _log1p(jnp.exp(-jnp.abs(x)))


def _neg_expm1(x):
    small = -x * (1.0 + x * (0.5 + x * (1.0 / 6.0)))
    return jnp.where(x > -0.01, small, 1.0 - jnp.exp(x))


_GELU_K = 0.7978845608028654
_GELU_C = 0.044715


def _gelu_parts(x):
    th = jnp.tanh(_GELU_K * (x + _GELU_C * x * x * x))
    return 0.5 * x * (1.0 + th), th


def _gelu_grad(x, th):
    return 0.5 * (1.0 + th) + 0.5 * x * (1.0 - th * th) * _GELU_K * (1.0 + 3.0 * _GELU_C * x * x)


def _shift_down(x, prev8, j):
    tb = x.shape[0]
    xr = pltpu.roll(x, j, 0)
    pr = pltpu.roll(prev8, j, 0)
    row = lax.broadcasted_iota(jnp.int32, prev8.shape, 0)
    first = jnp.where(row < j, pr, xr[0:SUBLANES])
    if tb == SUBLANES:
        return first
    return jnp.concatenate([first, xr[SUBLANES:]], axis=0)


def _shift_up(x, next8, j):
    tb = x.shape[0]
    xr = pltpu.roll(x, tb - j, 0)
    nr = pltpu.roll(next8, SUBLANES - j, 0)
    row = lax.broadcasted_iota(jnp.int32, next8.shape, 0)
    last = jnp.where(row >= SUBLANES - j, nr, xr[tb - SUBLANES:])
    if tb == SUBLANES:
        return last
    return jnp.concatenate([xr[:tb - SUBLANES], last], axis=0)


def _head_mats(width, heads_pad):
    e = (lax.broadcasted_iota(jnp.int32, (width, heads_pad), 0) // RWKV_HEAD
         == lax.broadcasted_iota(jnp.int32, (width, heads_pad), 1)).astype(BF16)
    et = (lax.broadcasted_iota(jnp.int32, (heads_pad, width), 1) // RWKV_HEAD
          == lax.broadcasted_iota(jnp.int32, (heads_pad, width), 0)).astype(BF16)
    return e, et


def _dot_exact01(x, m):
    hi = x.astype(BF16)
    r1 = x - hi.astype(F32)
    mid = r1.astype(BF16)
    lo = (r1 - mid.astype(F32)).astype(BF16)
    return (jnp.dot(lo, m, preferred_element_type=F32) + jnp.dot(mid, m, preferred_element_type=F32)
            + jnp.dot(hi, m, preferred_element_type=F32))


def _headsum(x, e, et):
    return _dot_exact01(_dot_exact01(x, e), et)


def _dot(a, b):
    return jnp.dot(a.astype(BF16), b.astype(BF16), preferred_element_type=F32)


def _dot_tn(a, b):
    return lax.dot_general(a.astype(BF16), b.astype(BF16), (((0,), (0,)), ((), ())), preferred_element_type=F32)


def _dot_nt(a, b):
    return lax.dot_general(a.astype(BF16), b.astype(BF16), (((1,), (1,)), ((), ())), preferred_element_type=F32)


def _mm(a, b, *, name, ta=False, tb=False, out_dtype=F32, res=None, n_outer=False, caps=(1024, 512, 4096),
        comm=None, epi=None, resident=None):
    m = a.shape[1] if ta else a.shape[0]
    kd = a.shape[0] if ta else a.shape[1]
    n = b.shape[0] if tb else b.shape[1]
    assert kd == (b.shape[1] if tb else b.shape[0])
    tm, tn, tk = _tile(m, caps[0]), _tile(n, caps[1]), _tile(kd, caps[2])
    gm, gn, gk = m // tm, n // tn, kd // tk
    dims = (((0 if ta else 1,), (1 if tb else 0,)), ((), ()))
    grid = (gn, gm, gk) if n_outer else (gm, gn, gk)
    cx, cx_specs, c_outs, c_sems, c_start, c_finish = _carried(comm)
    epi_fn, epi_ins, out_dtypes = epi if epi is not None else (None, [], [out_dtype])
    n_epi, n_out = len(epi_ins), len(out_dtypes)
    nx, n_in = len(cx), (3 if res is not None else 2) + n_epi

    def ij(g0, g1):
        return (g1, g0) if n_outer else (g0, g1)

    def a_map(g0, g1, k):
        i, _ = ij(g0, g1)
        return (k, i) if ta else (i, k)

    def b_map(g0, g1, k):
        _, j = ij(g0, g1)
        return (j, k) if tb else (k, j)

    def o_map(g0, g1, k):
        return ij(g0, g1)

    has_res = res is not None

    def body(*refs):
        a_ref, b_ref = refs[0], refs[1]
        res_ref = refs[2] if has_res else None
        epi_refs = refs[n_in - n_epi:n_in]
        c_ins = refs[n_in:n_in + nx]
        o_refs = refs[n_in + nx:n_in + nx + n_out]
        c_out_refs = refs[n_in + nx + n_out:n_in + nx + n_out + len(c_outs)]
        acc_ref = refs[n_in + nx + n_out + len(c_outs)] if gk > 1 else None
        steps = [pl.program_id(ax) for ax in range(3)]
        if nx:
            @pl.when(jnp.logical_and(jnp.logical_and(steps[0] == 0, steps[1] == 0), steps[2] == 0))
            def _():
                c_start(c_ins, c_out_refs, refs[-2], refs[-1])

        prod = lax.dot_general(a_ref[...], b_ref[...], dims, preferred_element_type=F32)

        def finish(acc):
            if has_res:
                acc = acc + res_ref[...]
            tiles = [acc] if epi_fn is None else epi_fn(acc, *[e_ref[...] for e_ref in epi_refs])
            for o_ref, tile, dt in zip(o_refs, tiles, out_dtypes):
                o_ref[...] = tile.astype(dt)

        if gk == 1:
            finish(prod)
        else:
            k = steps[2]

            @pl.when(k == 0)
            def _():
                acc_ref[...] = prod

            @pl.when(k > 0)
            def _():
                acc_ref[...] += prod

            @pl.when(k == gk - 1)
            def _():
                finish(acc_ref[...])

        if nx:
            @pl.when(jnp.logical_and(jnp.logical_and(steps[0] == grid[0] - 1, steps[1] == grid[1] - 1),
                                     steps[2] == grid[2] - 1))
            def _():
                c_finish(c_ins, c_out_refs, refs[-2], refs[-1])

    one = pl.Buffered(1)
    in_specs = [pl.BlockSpec((tk, tm) if ta else (tm, tk), a_map, pipeline_mode=one if resident == "a" else None),
                pl.BlockSpec((tn, tk) if tb else (tk, tn), b_map, pipeline_mode=one if resident == "b" else None)]
    args = [a, b]
    for extra in ([res] if has_res else []) + list(epi_ins):
        in_specs.append(pl.BlockSpec((tm, tn), o_map))
        args.append(extra)
    out = _call(
        body, name=name, grid=grid, in_specs=in_specs + cx_specs,
        out_specs=[pl.BlockSpec((tm, tn), o_map)] * n_out + [_HBM] * len(c_outs),
        out_shape=[jax.ShapeDtypeStruct((m, n), dt) for dt in out_dtypes] + c_outs,
        scratch_shapes=([pltpu.VMEM((tm, tn), F32)] if gk > 1 else []) + c_sems,
        compiler_params=_cp(*(("arbitrary",) * 3 if nx else ("parallel", "parallel", "arbitrary"))),
    )(*args, *cx)
    return out if (nx or epi is not None) else out[0]


def _rmsnorm_fwd(x, g, name):
    n, d = x.shape
    tb = _tile(n, 256, SUBLANES)

    def body(x_ref, g_ref, u_ref):
        xv = x_ref[...]
        rstd = lax.rsqrt(jnp.mean(xv * xv, axis=-1, keepdims=True) + NORM_EPS)
        u_ref[...] = (xv * rstd * g_ref[...]).astype(BF16)

    row = pl.BlockSpec((tb, d), lambda i: (i, 0))
    vec = pl.BlockSpec((1, d), lambda i: (0, 0))
    return _call(body, name=name, grid=(n // tb,), in_specs=[row, vec], out_specs=row,
                 out_shape=jax.ShapeDtypeStruct((n, d), BF16), compiler_params=_cp("parallel"))(x, g)


def _rmsnorm_bwd(du, x, g, dres, name):
    n, d = x.shape
    tb = _tile(n, 256, SUBLANES)

    def body(du_ref, x_ref, g_ref, dres_ref, dx_ref, dxb_ref, dg_ref):
        xv = x_ref[...]
        rstd = lax.rsqrt(jnp.mean(xv * xv, axis=-1, keepdims=True) + NORM_EPS)
        xh = xv * rstd
        duv = du_ref[...]
        t = duv * g_ref[...]
        dx = dres_ref[...] + rstd * (t - xh * jnp.mean(t * xh, axis=-1, keepdims=True))
        dx_ref[...] = dx
        dxb_ref[...] = dx.astype(BF16)

        @pl.when(pl.program_id(0) == 0)
        def _():
            dg_ref[...] = jnp.zeros_like(dg_ref)

        dg_ref[...] += jnp.sum(duv * xh, axis=0, keepdims=True)

    row = pl.BlockSpec((tb, d), lambda i: (i, 0))
    vec = pl.BlockSpec((1, d), lambda i: (0, 0))
    return _call(body, name=name, grid=(n // tb,), in_specs=[row, row, vec, row], out_specs=[row, row, vec],
                 out_shape=[jax.ShapeDtypeStruct((n, d), F32), jax.ShapeDtypeStruct((n, d), BF16),
                            jax.ShapeDtypeStruct((1, d), F32)],
                 compiler_params=_cp("arbitrary"))(du, x, g, dres)


def _loss_head(h, g, target, name):
    n, d = h.shape
    tb = _tile(n, 256, SUBLANES)

    def body(h_ref, g_ref, t_ref, dh_ref, dhb_ref, dg_ref, loss_ref):
        hv = h_ref[...]
        gv = g_ref[...]
        rstd = lax.rsqrt(jnp.mean(hv * hv, axis=-1, keepdims=True) + NORM_EPS)
        hh = hv * rstd
        err = hh * gv - t_ref[...]
        dy = err * (1.0 / d)
        dhh = dy * gv
        dh = rstd * (dhh - hh * jnp.mean(dhh * hh, axis=-1, keepdims=True))
        dh_ref[...] = dh
        dhb_ref[...] = dh.astype(BF16)

        @pl.when(pl.program_id(0) == 0)
        def _():
            dg_ref[...] = jnp.zeros_like(dg_ref)
            loss_ref[...] = jnp.zeros_like(loss_ref)

        dg_ref[...] += jnp.sum(dy * hh, axis=0, keepdims=True)
        loss_ref[...] += jnp.sum(err * err) * (0.5 / d)

    row = pl.BlockSpec((tb, d), lambda i: (i, 0))
    vec = pl.BlockSpec((1, d), lambda i: (0, 0))
    lvec = pl.BlockSpec((1, LANES), lambda i: (0, 0))
    return _call(body, name=name, grid=(n // tb,), in_specs=[row, vec, row], out_specs=[row, row, vec, lvec],
                 out_shape=[jax.ShapeDtypeStruct((n, d), F32), jax.ShapeDtypeStruct((n, d), BF16),
                            jax.ShapeDtypeStruct((1, d), F32), jax.ShapeDtypeStruct((1, LANES), F32)],
                 compiler_params=_cp("arbitrary"))(h, g, target)


def _swiglu_tile(up, gate):
    return [up, gate * _sig(gate) * up]


def _swiglu_bwd_tile(dact, gate, up):
    s = _sig(gate)
    return [dact * up * s * (1.0 + gate * (1.0 - s)), dact * gate * s]


def _prep_common(prkv_ref, prkvp_ref, plo_ref, plop_ref, mur_ref, mul_ref, w0_ref, a0_ref, kk_ref, ka_ref,
                 w2_ref, a2_ref, g2_ref, seq_start, w, dlp, alp):
    z8r = jnp.zeros_like(prkvp_ref[...])
    z8l = jnp.zeros_like(plop_ref[...])
    prev_r = jnp.where(seq_start, z8r, prkvp_ref[...])
    prev_l = jnp.where(seq_start, z8l, plop_ref[...])
    p_r = prkv_ref[...]
    p_l = plo_ref[...]
    dif_r = _shift_down(p_r, prev_r, 1) - p_r
    dif_l = _shift_down(p_l, prev_l, 1) - p_l
    q_r = p_r + dif_r * mur_ref[...]
    q_l = p_l + dif_l * mul_ref[...]
    r, k, v = q_r[:, 0:w], q_r[:, w:2 * w], q_r[:, 2 * w:3 * w]
    wd, ad, gd = q_l[:, 0:dlp], q_l[:, dlp:dlp + alp], q_l[:, dlp + alp:]
    tw = jnp.tanh(wd)
    zw = w0_ref[...] + _dot(tw, w2_ref[...])
    wlog = -_softplus(-zw) - 0.5
    ew = jnp.exp(wlog)
    dec = jnp.exp(-ew)
    za = a0_ref[...] + _dot(ad, a2_ref[...])
    av = _sig(za)
    sg = _sig(gd)
    g = _dot(sg, g2_ref[...])
    return dict(dif_r=dif_r, dif_l=dif_l, r=r, k=k, v=v, ad=ad, tw=tw, zw=zw, ew=ew, dec=dec, av=av, sg=sg, g=g)


def _rwkv_prep_specs(n, tb, w, lp, t_len):
    nb8 = tb // SUBLANES
    row3 = pl.BlockSpec((tb, 3 * w), lambda i: (i, 0))
    prev3 = pl.BlockSpec((SUBLANES, 3 * w), lambda i: (jnp.maximum(i * nb8 - 1, 0), 0))
    rowl = pl.BlockSpec((tb, lp), lambda i: (i, 0))
    prevl = pl.BlockSpec((SUBLANES, lp), lambda i: (jnp.maximum(i * nb8 - 1, 0), 0))
    return row3, prev3, rowl, prevl


def _rwkv_prep_fwd(p_rkv, p_lora, prm, t_len, name):
    n, w3 = p_rkv.shape
    w = w3 // 3
    lp = p_lora.shape[1]
    dlp, alp = prm["w2"].shape[0], prm["a2"].shape[0]
    glp = lp - dlp - alp
    hp = max(w // RWKV_HEAD, LANES)
    tb = _tile(min(n, t_len), 128, SUBLANES)
    bps = t_len // tb

    def body(prkv_ref, prkvp_ref, plo_ref, plop_ref, mur_ref, mul_ref, w0_ref, a0_ref, kk_ref, ka_ref,
             w2_ref, a2_ref, g2_ref, r_o, dec_o, k_o, v_o, na_o, nb_o, g_o):
        seq_start = (pl.program_id(0) % bps) == 0
        f = _prep_common(prkv_ref, prkvp_ref, plo_ref, plop_ref, mur_ref, mul_ref, w0_ref, a0_ref, kk_ref, ka_ref,
                         w2_ref, a2_ref, g2_ref, seq_start, w, dlp, alp)
        e, et = _head_mats(w, hp)
        kk0 = f["k"] * kk_ref[...]
        inv = lax.rsqrt(jnp.maximum(_headsum(kk0 * kk0, e, et), KK_EPS))
        kk = kk0 * inv
        r_o[...] = f["r"]
        dec_o[...] = f["dec"]
        k_o[...] = f["k"] * (1.0 + (f["av"] - 1.0) * ka_ref[...])
        v_o[...] = f["v"]
        na_o[...] = -kk
        nb_o[...] = kk * f["av"]
        g_o[...] = f["g"]

    row3, prev3, rowl, prevl = _rwkv_prep_specs(n, tb, w, lp, t_len)
    c0 = lambda i: (0, 0)
    vec3 = pl.BlockSpec((1, 3 * w), c0)
    vecl = pl.BlockSpec((1, lp), c0)
    vec = pl.BlockSpec((1, w), c0)
    out = pl.BlockSpec((tb, w), lambda i: (i, 0))
    return _call(
        body, name=name, grid=(n // tb,),
        in_specs=[row3, prev3, rowl, prevl, vec3, vecl, vec, vec, vec, vec,
                  pl.BlockSpec((dlp, w), c0), pl.BlockSpec((alp, w), c0), pl.BlockSpec((glp, w), c0)],
        out_specs=[out] * 7, out_shape=[jax.ShapeDtypeStruct((n, w), F32)] * 7,
        compiler_params=_cp("parallel"),
    )(p_rkv, p_rkv, p_lora, p_lora, prm["mu_rkv"], prm["mu_lora"], prm["w0"], prm["a0"], prm["k_k"], prm["k_a"],
      prm["w2"], prm["a2"], prm["g2"])


def _rwkv_prep_bwd(p_rkv, p_lora, prm, grads, t_len, name):
    n, w3 = p_rkv.shape
    w = w3 // 3
    lp = p_lora.shape[1]
    dlp, alp = prm["w2"].shape[0], prm["a2"].shape[0]
    glp = lp - dlp - alp
    hp = max(w // RWKV_HEAD, LANES)
    tb = _tile(min(n, t_len), 64, SUBLANES)
    bps = t_len // tb

    def body(prkv_ref, prkvp_ref, plo_ref, plop_ref, mur_ref, mul_ref, w0_ref, a0_ref, kk_ref, ka_ref,
             w2_ref, a2_ref, g2_ref,
             drs_ref, drp_ref, ddec_ref, dks_ref, dkp_ref, dvs_ref, dvp_ref, dna_ref, dnb_ref, dg_ref,
             dqr_o, dql_o, dmur_o, dmul_o, dw0_o, da0_o, dkk_o, dka_o, dw2_o, da2_o, dg2_o):
        seq_start = (pl.program_id(0) % bps) == 0
        f = _prep_common(prkv_ref, prkvp_ref, plo_ref, plop_ref, mur_ref, mul_ref, w0_ref, a0_ref, kk_ref, ka_ref,
                         w2_ref, a2_ref, g2_ref, seq_start, w, dlp, alp)
        e, et = _head_mats(w, hp)
        k, av = f["k"], f["av"]
        k_k, k_a = kk_ref[...], ka_ref[...]
        kk0 = k * k_k
        n2 = _headsum(kk0 * kk0, e, et)
        inv = lax.rsqrt(jnp.maximum(n2, KK_EPS))
        kk = kk0 * inv
        dk2 = dks_ref[...] + dkp_ref[...]
        dnb = dnb_ref[...]
        dkk = dnb * av - dna_ref[...]
        dav = dnb * kk + dk2 * k * k_a
        dk = dk2 * (1.0 + (av - 1.0) * k_a)
        dka = dk2 * k * (av - 1.0)
        proj = jnp.where(n2 > KK_EPS, _headsum(dkk * kk, e, et), 0.0)
        dkk0 = inv * (dkk - kk * proj)
        dk = dk + dkk0 * k_k
        dkkp = dkk0 * k
        dgv = dg_ref[...]
        sg = f["sg"]
        dgd = _dot_nt(dgv, g2_ref[...]) * sg * (1.0 - sg)
        dza = dav * av * (1.0 - av)
        dad = _dot_nt(dza, a2_ref[...])
        dzw = ddec_ref[...] * f["dec"] * (-f["ew"]) * _sig(-f["zw"])
        tw = f["tw"]
        dwd = _dot_nt(dzw, w2_ref[...]) * (1.0 - tw * tw)
        dq_r = jnp.concatenate([drs_ref[...] + drp_ref[...], dk, dvs_ref[...] + dvp_ref[...]], axis=1)
        dq_l = jnp.concatenate([dwd, dad, dgd], axis=1)
        dqr_o[...] = dq_r
        dql_o[...] = dq_l

        @pl.when(pl.program_id(0) == 0)
        def _():
            for o in (dmur_o, dmul_o, dw0_o, da0_o, dkk_o, dka_o, dw2_o, da2_o, dg2_o):
                o[...] = jnp.zeros_like(o)

        def rsum(x):
            return jnp.sum(x, axis=0, keepdims=True)

        dmur_o[...] += rsum(dq_r * f["dif_r"])
        dmul_o[...] += rsum(dq_l * f["dif_l"])
        dw0_o[...] += rsum(dzw)
        da0_o[...] += rsum(dza)
        dkk_o[...] += rsum(dkkp)
        dka_o[...] += rsum(dka)
        dw2_o[...] += _dot_tn(tw, dzw)
        da2_o[...] += _dot_tn(f["ad"], dza)
        dg2_o[...] += _dot_tn(sg, dgv)

    row3, prev3, rowl, prevl = _rwkv_prep_specs(n, tb, w, lp, t_len)
    c0 = lambda i: (0, 0)
    vec3 = pl.BlockSpec((1, 3 * w), c0)
    vecl = pl.BlockSpec((1, lp), c0)
    vec = pl.BlockSpec((1, w), c0)
    blk = pl.BlockSpec((tb, w), lambda i: (i, 0))
    m2, ma, mg = pl.BlockSpec((dlp, w), c0), pl.BlockSpec((alp, w), c0), pl.BlockSpec((glp, w), c0)
    sds = jax.ShapeDtypeStruct
    return _call(
        body, name=name, grid=(n // tb,),
        in_specs=[row3, prev3, rowl, prevl, vec3, vecl, vec, vec, vec, vec, m2, ma, mg] + [blk] * 10,
        out_specs=[row3, rowl, vec3, vecl, vec, vec, vec, vec, m2, ma, mg],
        out_shape=[sds((n, 3 * w), F32), sds((n, lp), F32), sds((1, 3 * w), F32), sds((1, lp), F32),
                   sds((1, w), F32), sds((1, w), F32), sds((1, w), F32), sds((1, w), F32),
                   sds((dlp, w), F32), sds((alp, w), F32), sds((glp, w), F32)],
        compiler_params=_cp("arbitrary"),
    )(p_rkv, p_rkv, p_lora, p_lora, prm["mu_rkv"], prm["mu_lora"], prm["w0"], prm["a0"], prm["k_k"], prm["k_a"],
      prm["w2"], prm["a2"], prm["g2"], *grads)


def _shift_combine(dq, mu, t_len, name):
    n, c = dq.shape
    tb = _tile(min(n, t_len), 256, SUBLANES)
    bps = t_len // tb
    nb8 = tb // SUBLANES
    last8 = n // SUBLANES - 1

    def body(x_ref, nx_ref, mu_ref, o_ref):
        seq_end = (pl.program_id(0) % bps) == bps - 1
        nxt = jnp.where(seq_end, jnp.zeros_like(nx_ref[...]), nx_ref[...])
        x = x_ref[...]
        muv = mu_ref[...]
        o_ref[...] = ((1.0 - muv) * x + muv * _shift_up(x, nxt, 1)).astype(BF16)

    row = pl.BlockSpec((tb, c), lambda i: (i, 0))
    nxt = pl.BlockSpec((SUBLANES, c), lambda i: (jnp.minimum((i + 1) * nb8, last8), 0))
    vec = pl.BlockSpec((1, c), lambda i: (0, 0))
    return _call(body, name=name, grid=(n // tb,), in_specs=[row, nxt, vec], out_specs=row,
                 out_shape=jax.ShapeDtypeStruct((n, c), BF16), compiler_params=_cp("parallel"))(dq, dq, mu)


def _scan_step(s_i, a_t, w_t, b_t, k_t, v_i):
    sa = jnp.sum(s_i * a_t, axis=0, keepdims=True)
    return s_i * w_t + sa * b_t + v_i * k_t, sa


def _carried(comm):
    if comm is None:
        return [], [], [], [], None, None
    outs, n_sems, start, finish = _comm_plan(*comm)
    xs = list(comm[1])
    sems = [pltpu.SemaphoreType.DMA((n_sems,)), pltpu.SemaphoreType.DMA((n_sems,))]
    return xs, [_HBM] * len(xs), outs, sems, start, finish


def _rwkv_scan_fwd(r, w, k, a, b, v, name, comm=None):
    t_len, kd, ln = r.shape
    vh = v.shape[1]
    tc = SCAN_CHUNK
    nc = t_len // tc
    cx, cx_specs, c_outs, c_sems, c_start, c_finish = _carried(comm)
    nx = len(cx)

    def body(r_ref, w_ref, k_ref, a_ref, b_ref, v_ref, *rest):
        c_ins, (y_ref, st_ref, sa_ref), c_out_refs = rest[:nx], rest[nx:nx + 3], rest[nx + 3:nx + 3 + len(c_outs)]
        s_ref = rest[nx + 3 + len(c_outs)]

        @pl.when(pl.program_id(0) == 0)
        def _():
            s_ref[...] = jnp.zeros_like(s_ref)
            if nx:
                c_start(c_ins, c_out_refs, rest[-2], rest[-1])

        st_ref[0, 0] = s_ref[...]

        def step(t, carry):
            a_t, w_t, b_t, k_t, r_t = a_ref[t], w_ref[t], b_ref[t], k_ref[t], r_ref[t]
            for i in range(vh):
                s_new, sa = _scan_step(st_ref[0, t, i], a_t, w_t, b_t, k_t, v_ref[t, pl.ds(i, 1), :])
                st_ref[0, t + 1, i] = s_new
                sa_ref[t, pl.ds(i, 1), :] = sa
                y_ref[t, pl.ds(i, 1), :] = jnp.sum(s_new * r_t, axis=0, keepdims=True)
            return carry

        lax.fori_loop(0, tc, step, 0)
        s_ref[...] = st_ref[0, tc]

        if nx:
            @pl.when(pl.program_id(0) == nc - 1)
            def _():
                c_finish(c_ins, c_out_refs, rest[-2], rest[-1])

    kblk = pl.BlockSpec((tc, kd, ln), lambda c: (c, 0, 0))
    vblk = pl.BlockSpec((tc, vh, ln), lambda c: (c, 0, 0))
    vsd = jax.ShapeDtypeStruct((t_len, vh, ln), F32)
    return _call(
        body, name=name, grid=(nc,), in_specs=[kblk] * 5 + [vblk] + cx_specs,
        out_specs=[vblk, pl.BlockSpec((1, tc + 1, vh, kd, ln), lambda c: (c, 0, 0, 0, 0)), vblk] + [_HBM] * len(c_outs),
        out_shape=[vsd, jax.ShapeDtypeStruct((nc, tc + 1, vh, kd, ln), F32), vsd] + c_outs,
        scratch_shapes=[pltpu.VMEM((vh, kd, ln), F32)] + c_sems,
        compiler_params=_cp("arbitrary"),
    )(r, w, k, a, b, v, *cx)


def _rwkv_scan_bwd(r, w, k, a, b, v, dy, states, sa, name, comm=None):
    t_len, kd, ln = r.shape
    vh = v.shape[1]
    tc = SCAN_CHUNK
    nc = t_len // tc
    half = ln // 2
    cx, cx_specs, c_outs, c_sems, c_start, c_finish = _carried(comm)
    nx = len(cx)

    def body(r_ref, w_ref, k_ref, a_ref, b_ref, v_ref, dy_ref, st_ref, sa_ref, *rest):
        c_ins = rest[:nx]
        dr_o, dw_o, dk_o, da_o, db_o, dv_o = rest[nx:nx + 6]
        c_out_refs = rest[nx + 6:nx + 6 + len(c_outs)]
        ds_ref = rest[nx + 6 + len(c_outs)]

        @pl.when(pl.program_id(0) == 0)
        def _():
            ds_ref[...] = jnp.zeros_like(ds_ref)
            if nx:
                c_start(c_ins, c_out_refs, rest[-2], rest[-1])

        def bwd(tt, carry):
            t = tc - 1 - tt
            a_t, w_t, b_t, k_t, r_t = a_ref[t], w_ref[t], b_ref[t], k_ref[t], r_ref[t]
            z = jnp.zeros((kd, ln), F32)
            dr, dw, dk, da, db = z, z, z, z, z
            for i in range(vh):
                dy_i = dy_ref[t, pl.ds(i, 1), :]
                s_t = st_ref[0, t + 1, i]
                s_p = st_ref[0, t, i]
                d = ds_ref[i] + dy_i * r_t
                dr = dr + s_t * dy_i
                dv_o[t, pl.ds(i, 1), :] = jnp.sum(d * k_t, axis=0, keepdims=True)
                dk = dk + d * v_ref[t, pl.ds(i, 1), :]
                dsa = jnp.sum(d * b_t, axis=0, keepdims=True)
                db = db + d * sa_ref[t, pl.ds(i, 1), :]
                dw = dw + d * s_p
                da = da + s_p * dsa
                ds_ref[i] = d * w_t + dsa * a_t

            def both(x):
                return x + pltpu.roll(x, half, 1)

            dr_o[t] = both(dr)
            dw_o[t] = both(dw)
            dk_o[t] = both(dk)
            da_o[t] = both(da)
            db_o[t] = both(db)
            return carry

        lax.fori_loop(0, tc, bwd, 0)

        if nx:
            @pl.when(pl.program_id(0) == nc - 1)
            def _():
                c_finish(c_ins, c_out_refs, rest[-2], rest[-1])

    kblk = pl.BlockSpec((tc, kd, ln), lambda c: (nc - 1 - c, 0, 0))
    vblk = pl.BlockSpec((tc, vh, ln), lambda c: (nc - 1 - c, 0, 0))
    ksd = jax.ShapeDtypeStruct((t_len, kd, ln), F32)
    return _call(
        body, name=name, grid=(nc,),
        in_specs=[kblk] * 5 + [vblk, vblk, pl.BlockSpec((1, tc + 1, vh, kd, ln), lambda c: (nc - 1 - c, 0, 0, 0, 0)),
                  vblk] + cx_specs,
        out_specs=[kblk] * 5 + [vblk] + [_HBM] * len(c_outs),
        out_shape=[ksd] * 5 + [jax.ShapeDtypeStruct((t_len, vh, ln), F32)] + c_outs,
        scratch_shapes=[pltpu.VMEM((vh, kd, ln), F32)] + c_sems,
        compiler_params=_cp("arbitrary"),
    )(r, w, k, a, b, v, dy, states, sa, *cx)


def _post_common(y_ref, r_ref, k_ref, v_ref, lng_ref, lnb_ref, rk_ref, e, et):
    y = y_ref[...]
    inv_n = 1.0 / RWKV_HEAD
    mean = _headsum(y, e, et) * inv_n
    yc = y - mean
    var = _headsum(yc * yc, e, et) * inv_n
    rstd = lax.rsqrt(var + GN_EPS)
    yh = yc * rstd
    yn = yh * lng_ref[...] + lnb_ref[...]
    bonus = _headsum(r_ref[...] * k_ref[...] * rk_ref[...], e, et)
    return yh, rstd, yn, bonus


def _rwkv_post_fwd(y, r, k, v, g, ln_g, ln_b, r_k, name):
    n, w = y.shape
    hp = max(w // RWKV_HEAD, LANES)
    tb = _tile(n, 256, SUBLANES)

    def body(y_ref, r_ref, k_ref, v_ref, g_ref, lng_ref, lnb_ref, rk_ref, o_ref):
        e, et = _head_mats(w, hp)
        _, _, yn, bonus = _post_common(y_ref, r_ref, k_ref, v_ref, lng_ref, lnb_ref, rk_ref, e, et)
        o_ref[...] = ((yn + bonus * v_ref[...]) * g_ref[...]).astype(BF16)

    blk = pl.BlockSpec((tb, w), lambda i: (i, 0))
    vec = pl.BlockSpec((1, w), lambda i: (0, 0))
    return _call(body, name=name, grid=(n // tb,), in_specs=[blk] * 5 + [vec] * 3, out_specs=blk,
                 out_shape=jax.ShapeDtypeStruct((n, w), BF16),
                 compiler_params=_cp("parallel"))(y, r, k, v, g, ln_g, ln_b, r_k)


def _rwkv_post_bwd(y, r, k, v, g, ln_g, ln_b, r_k, do_cat, name):
    n, w = y.shape
    hp = max(w // RWKV_HEAD, LANES)
    tb = _tile(n, 128, SUBLANES)

    def body(y_ref, r_ref, k_ref, v_ref, g_ref, lng_ref, lnb_ref, rk_ref, do_ref,
             dy_o, dr_o, dk_o, dv_o, dg_o, dlng_o, dlnb_o, drk_o):
        e, et = _head_mats(w, hp)
        yh, rstd, yn, bonus = _post_common(y_ref, r_ref, k_ref, v_ref, lng_ref, lnb_ref, rk_ref, e, et)
        do = do_ref[...]
        vv, rv, kv, rk = v_ref[...], r_ref[...], k_ref[...], rk_ref[...]
        dg_o[...] = do * (yn + bonus * vv)
        dz = do * g_ref[...]
        dbonus = _headsum(dz * vv, e, et)
        dv_o[...] = dz * bonus
        dr_o[...] = dbonus * kv * rk
        dk_o[...] = dbonus * rv * rk
        dyh = dz * lng_ref[...]
        inv_n = 1.0 / RWKV_HEAD
        dy_o[...] = rstd * (dyh - _headsum(dyh, e, et) * inv_n - yh * (_headsum(dyh * yh, e, et) * inv_n))

        @pl.when(pl.program_id(0) == 0)
        def _():
            for o in (dlng_o, dlnb_o, drk_o):
                o[...] = jnp.zeros_like(o)

        dlng_o[...] += jnp.sum(dz * yh, axis=0, keepdims=True)
        dlnb_o[...] += jnp.sum(dz, axis=0, keepdims=True)
        drk_o[...] += jnp.sum(dbonus * rv * kv, axis=0, keepdims=True)

    blk = pl.BlockSpec((tb, w), lambda i: (i, 0))
    vec = pl.BlockSpec((1, w), lambda i: (0, 0))
    sds = jax.ShapeDtypeStruct
    return _call(body, name=name, grid=(n // tb,), in_specs=[blk] * 5 + [vec] * 3 + [blk],
                 out_specs=[blk] * 5 + [vec] * 3,
                 out_shape=[sds((n, w), F32)] * 5 + [sds((1, w), F32)] * 3,
                 compiler_params=_cp("arbitrary"))(y, r, k, v, g, ln_g, ln_b, r_k, do_cat)


def _lru_gates(xb, prev8, gate, cw_ref, cb_ref, wr_ref, br_ref, wi_ref, bi_ref, lam_ref, is_t0):
    c = xb.shape[1]
    nblk = c // LRU_BLOCK_W
    xs = [xb] + [_shift_down(xb, prev8, j) for j in range(1, CONV_WIDTH)]
    xc = cb_ref[...]
    for j in range(CONV_WIDTH):
        xc = xc + xs[CONV_WIDTH - 1 - j] * cw_ref[pl.ds(j, 1), :]
    xcb = xc.astype(BF16)

    def blockmm(w_ref):
        return jnp.concatenate(
            [jnp.dot(xcb[:, h * LRU_BLOCK_W:(h + 1) * LRU_BLOCK_W], w_ref[h], preferred_element_type=F32)
             for h in range(nblk)], axis=1)

    rg = _sig(blockmm(wr_ref) + br_ref[...])
    ig = _sig(blockmm(wi_ref) + bi_ref[...])
    sp = _softplus(-lam_ref[...])
    la = -LRU_C * rg * sp
    av = jnp.exp(la)
    mult = jnp.where(is_t0, 1.0, jnp.sqrt(_neg_expm1(2.0 * la)))
    ge, th = _gelu_parts(gate)
    return dict(xs=xs, xc=xc, xcb=xcb, rg=rg, ig=ig, sp=sp, a=av, mult=mult, ge=ge, th=th)


def _lru_specs(tb, c, nb, rev):
    nb8 = tb // SUBLANES

    def blk_i(i):
        return nb - 1 - i if rev else i

    xb = pl.BlockSpec((tb, c), lambda b, i: (b * nb + blk_i(i), 0))
    gate = pl.BlockSpec((tb, c), lambda b, i: (b * nb + blk_i(i), 1))
    prev = pl.BlockSpec((SUBLANES, c), lambda b, i: (jnp.maximum((b * nb + blk_i(i)) * nb8 - 1, 0), 0))
    return xb, gate, prev


def _lru_fwd(p_lru, prm, t_len, name):
    n, c2 = p_lru.shape
    c = c2 // 2
    nblk = c // LRU_BLOCK_W
    tb = _tile(t_len, 256, SUBLANES)
    nb = t_len // tb
    bsz = n // t_len

    def body(xb_ref, gate_ref, prev_ref, cw_ref, cb_ref, wr_ref, br_ref, wi_ref, bi_ref, lam_ref, ng_ref,
             y_o, h_o, carry):
        i = pl.program_id(1)
        prev8 = jnp.where(i == 0, jnp.zeros_like(prev_ref[...]), prev_ref[...])
        row = lax.broadcasted_iota(jnp.int32, (tb, c), 0)
        f = _lru_gates(xb_ref[...], prev8, gate_ref[...], cw_ref, cb_ref, wr_ref, br_ref, wi_ref, bi_ref, lam_ref,
                       jnp.logical_and(i == 0, row == 0))
        acc_a = f["a"]
        acc_b = f["mult"] * f["ig"] * f["xc"]
        s = 1
        while s < tb:
            keep = row >= s
            a_sh = jnp.where(keep, pltpu.roll(acc_a, s, 0), 1.0)
            b_sh = jnp.where(keep, pltpu.roll(acc_b, s, 0), 0.0)
            acc_b = acc_a * b_sh + acc_b
            acc_a = acc_a * a_sh
            s *= 2

        @pl.when(i == 0)
        def _():
            carry[...] = jnp.zeros_like(carry)

        h = acc_b + acc_a * carry[0:1, :]
        carry[0:1, :] = h[tb - 1:tb, :]
        h_o[...] = h
        y = h * f["ge"]
        rstd = lax.rsqrt(jnp.mean(y * y, axis=-1, keepdims=True) + NORM_EPS)
        y_o[...] = (y * rstd * ng_ref[...]).astype(BF16)

    xb_s, gate_s, prev_s = _lru_specs(tb, c, nb, False)
    c0 = lambda b, i: (0, 0)
    vec = pl.BlockSpec((1, c), c0)
    wsp = pl.BlockSpec((nblk, LRU_BLOCK_W, LRU_BLOCK_W), lambda b, i: (0, 0, 0))
    out = pl.BlockSpec((tb, c), lambda b, i: (b * nb + i, 0))
    return _call(
        body, name=name, grid=(bsz, nb),
        in_specs=[xb_s, gate_s, prev_s, pl.BlockSpec((CONV_WIDTH, c), c0), vec, wsp, vec, wsp, vec, vec, vec],
        out_specs=[out, out],
        out_shape=[jax.ShapeDtypeStruct((n, c), BF16), jax.ShapeDtypeStruct((n, c), F32)],
        scratch_shapes=[pltpu.VMEM((SUBLANES, c), F32)],
        compiler_params=_cp("arbitrary", "arbitrary"),
    )(p_lru, p_lru, p_lru, prm["conv_w"], prm["conv_b"], prm["wr"], prm["br"], prm["wi"], prm["bi"],
      prm["lam"], prm["norm_g"])


def _lru_bwd(p_lru, h, do_cat, prm, t_len, name):
    n, c2 = p_lru.shape
    c = c2 // 2
    nblk = c // LRU_BLOCK_W
    tb = _tile(t_len, 128, SUBLANES)
    nb = t_len // tb
    bsz = n // t_len

    def body(xb_ref, gate_ref, prev_ref, h_ref, hprev_ref, do_ref,
             cw_ref, cb_ref, wr_ref, br_ref, wi_ref, bi_ref, lam_ref, ng_ref,
             dp_o, dcw_o, dcb_o, dwr_o, dbr_o, dwi_o, dbi_o, dlam_o, dng_o,
             a_next, g_next, dxc_next):
        b = pl.program_id(0)
        i = pl.program_id(1)
        blk = nb - 1 - i
        first = blk == 0
        prev8 = jnp.where(first, jnp.zeros_like(prev_ref[...]), prev_ref[...])
        hprev8 = jnp.where(first, jnp.zeros_like(hprev_ref[...]), hprev_ref[...])
        row = lax.broadcasted_iota(jnp.int32, (tb, c), 0)
        is_t0 = jnp.logical_and(first, row == 0)
        gate = gate_ref[...]
        f = _lru_gates(xb_ref[...], prev8, gate, cw_ref, cb_ref, wr_ref, br_ref, wi_ref, bi_ref, lam_ref, is_t0)

        @pl.when(i == 0)
        def _():
            a_next[...] = jnp.zeros_like(a_next)
            g_next[...] = jnp.zeros_like(g_next)
            dxc_next[...] = jnp.zeros_like(dxc_next)

        @pl.when(jnp.logical_and(b == 0, i == 0))
        def _():
            for o in (dcw_o, dcb_o, dwr_o, dbr_o, dwi_o, dbi_o, dlam_o, dng_o):
                o[...] = jnp.zeros_like(o)

        def rsum(x):
            return jnp.sum(x, axis=0, keepdims=True)

        hv = h_ref[...]
        hprev = _shift_down(hv, hprev8, 1)
        ge = f["ge"]
        y = hv * ge
        rstd = lax.rsqrt(jnp.mean(y * y, axis=-1, keepdims=True) + NORM_EPS)
        yh = y * rstd
        dyn = do_ref[...]
        t = dyn * ng_ref[...]
        dy = rstd * (t - yh * jnp.mean(t * yh, axis=-1, keepdims=True))
        dng_o[...] += rsum(dyn * yh)
        dgate = dy * hv * _gelu_grad(gate, f["th"])

        av = f["a"]
        acc_c = _shift_up(av, a_next[...], 1)
        acc_g = dy * ge
        s = 1
        while s < tb:
            keep = row < tb - s
            c_sh = jnp.where(keep, pltpu.roll(acc_c, tb - s, 0), 1.0)
            g_sh = jnp.where(keep, pltpu.roll(acc_g, tb - s, 0), 0.0)
            acc_g = acc_g + acc_c * g_sh
            acc_c = acc_c * c_sh
            s *= 2
        gtot = acc_g + acc_c * g_next[0:1, :]
        a_next[0:1, :] = av[0:1, :]
        g_next[0:1, :] = gtot[0:1, :]

        xc, ig, rg, mult = f["xc"], f["ig"], f["rg"], f["mult"]
        da = gtot * hprev
        dmult = gtot * ig * xc
        dig = gtot * mult * xc
        dxc = gtot * mult * ig
        da = da + jnp.where(is_t0, 0.0, -dmult * av / mult)
        dla = da * av
        drg = dla * (-LRU_C) * f["sp"]
        dlam_o[...] += rsum(dla * rg) * LRU_C * _sig(-lam_ref[...])
        dzr = drg * rg * (1.0 - rg)
        dzi = dig * ig * (1.0 - ig)
        dbr_o[...] += rsum(dzr)
        dbi_o[...] += rsum(dzi)
        dzrb, dzib = dzr.astype(BF16), dzi.astype(BF16)
        xcb = f["xcb"]
        back = []
        for hh in range(nblk):
            sl = slice(hh * LRU_BLOCK_W, (hh + 1) * LRU_BLOCK_W)
            dwr_o[hh] += _dot_tn(xcb[:, sl], dzrb[:, sl])
            dwi_o[hh] += _dot_tn(xcb[:, sl], dzib[:, sl])
            back.append(_dot_nt(dzrb[:, sl], wr_ref[hh]) + _dot_nt(dzib[:, sl], wi_ref[hh]))
        dxc = dxc + jnp.concatenate(back, axis=1)
        dcb_o[...] += rsum(dxc)
        xs = f["xs"]
        dcw_o[...] += jnp.concatenate([rsum(dxc * xs[CONV_WIDTH - 1 - j]) for j in range(CONV_WIDTH)], axis=0)
        nxt = dxc_next[...]
        dxb = dxc * cw_ref[pl.ds(CONV_WIDTH - 1, 1), :]
        for j in range(1, CONV_WIDTH):
            dxb = dxb + _shift_up(dxc, nxt, j) * cw_ref[pl.ds(CONV_WIDTH - 1 - j, 1), :]
        dxc_next[...] = dxc[0:SUBLANES, :]
        dp_o[:, 0:c] = dxb.astype(BF16)
        dp_o[:, c:2 * c] = dgate.astype(BF16)

    xb_s, gate_s, prev_s = _lru_specs(tb, c, nb, True)
    c0 = lambda b, i: (0, 0)
    vec = pl.BlockSpec((1, c), c0)
    wsp = pl.BlockSpec((nblk, LRU_BLOCK_W, LRU_BLOCK_W), lambda b, i: (0, 0, 0))
    cwsp = pl.BlockSpec((CONV_WIDTH, c), c0)
    sds = jax.ShapeDtypeStruct
    return _call(
        body, name=name, grid=(bsz, nb),
        in_specs=[xb_s, gate_s, prev_s, xb_s, prev_s, gate_s, cwsp, vec, wsp, vec, wsp, vec, vec, vec],
        out_specs=[pl.BlockSpec((tb, 2 * c), lambda b, i: (b * nb + nb - 1 - i, 0)),
                   cwsp, vec, wsp, vec, wsp, vec, vec, vec],
        out_shape=[sds((n, 2 * c), BF16), sds((CONV_WIDTH, c), F32), sds((1, c), F32),
                   sds((nblk, LRU_BLOCK_W, LRU_BLOCK_W), F32), sds((1, c), F32),
                   sds((nblk, LRU_BLOCK_W, LRU_BLOCK_W), F32), sds((1, c), F32), sds((1, c), F32), sds((1, c), F32)],
        scratch_shapes=[pltpu.VMEM((SUBLANES, c), F32)] * 3,
        compiler_params=_cp("arbitrary", "arbitrary"),
    )(p_lru, p_lru, p_lru, h, h, do_cat, prm["conv_w"], prm["conv_b"], prm["wr"], prm["br"], prm["wi"], prm["bi"],
      prm["lam"], prm["norm_g"])


def _adamw(g, w, m, v, name):
    rows, cols = g.shape
    tb = _tile(rows, 128, SUBLANES)

    def body(g_ref, w_ref, m_ref, v_ref, d_o, m_o, v_o):
        gv = g_ref[...]
        mn = ADAM_B1 * m_ref[...] + (1.0 - ADAM_B1) * gv
        vn = ADAM_B2 * v_ref[...] + (1.0 - ADAM_B2) * (gv * gv)
        m_o[...] = mn
        v_o[...] = vn
        d_o[...] = -ADAM_LR * ((mn / _BC1) / (jnp.sqrt(vn / _BC2) + ADAM_EPS) + ADAM_WD * w_ref[...])

    blk = pl.BlockSpec((tb, cols), lambda i: (i, 0))
    return _call(body, name=name, grid=(rows // tb,), in_specs=[blk] * 4, out_specs=[blk] * 3,
                 out_shape=[jax.ShapeDtypeStruct((rows, cols), F32)] * 3, compiler_params=_cp("parallel"))(g, w, m, v)


def _adamw_halves(mine, theirs, w, m, v, name, comm=None):
    a, b = mine.shape
    tc = _col_tile(a, b)
    nb = b // tc
    w, m, v = (t.reshape(2, a, b) for t in (w, m, v))
    cx, cx_specs, c_outs, c_sems, c_start, c_finish = _carried(comm)
    nx = len(cx)

    def body(mine_ref, theirs_ref, w_ref, m_ref, v_ref, *rest):
        c_ins = rest[:nx]
        g_o, d_o, m_o, v_o = rest[nx:nx + 4]
        c_out_refs = rest[nx + 4:nx + 4 + len(c_outs)]
        if nx:
            @pl.when(jnp.logical_and(pl.program_id(0) == 0, pl.program_id(1) == 0))
            def _():
                c_start(c_ins, c_out_refs, rest[-2], rest[-1])

            @pl.when(jnp.logical_and(pl.program_id(0) == 1, pl.program_id(1) == nb - 1))
            def _():
                c_finish(c_ins, c_out_refs, rest[-2], rest[-1])

        gv = jnp.where(pl.program_id(0) == lax.axis_index("c"), mine_ref[...], theirs_ref[...])
        mn = ADAM_B1 * m_ref[...] + (1.0 - ADAM_B1) * gv
        vn = ADAM_B2 * v_ref[...] + (1.0 - ADAM_B2) * (gv * gv)
        g_o[...] = gv
        m_o[...] = mn
        v_o[...] = vn
        d_o[...] = -ADAM_LR * ((mn / _BC1) / (jnp.sqrt(vn / _BC2) + ADAM_EPS) + ADAM_WD * w_ref[...])

    half = pl.BlockSpec((a, tc), lambda h, j: (0, j))
    blk = pl.BlockSpec((None, a, tc), lambda h, j: (h, 0, j))
    return _call(body, name=name, grid=(2, nb), in_specs=[half, half, blk, blk, blk] + cx_specs,
                 out_specs=[blk] * 4 + [_HBM] * len(c_outs),
                 out_shape=[jax.ShapeDtypeStruct((2, a, b), F32)] * 4 + c_outs, scratch_shapes=c_sems,
                 compiler_params=_cp(*(("arbitrary",) * 2 if nx else ("parallel",) * 2)))(mine, theirs, w, m, v, *cx)


def _pair_sum(x4, recv, name):
    _, _, a, b = x4.shape
    tc = _col_tile(a, b)

    def body(x_ref, r_ref, o_ref):
        mine = x_ref[lax.axis_index("c")]
        o_ref[...] = (mine.astype(F32) + r_ref[...].astype(F32)).astype(BF16)

    return _call(
        body, name=name, grid=(4, b // tc),
        in_specs=[pl.BlockSpec((None, 2, a, tc), lambda j, i: (j, 0, 0, i)),
                  pl.BlockSpec((None, a, tc), lambda j, i: (j, 0, i))],
        out_specs=pl.BlockSpec((None, a, tc), lambda j, i: (j, 0, i)),
        out_shape=jax.ShapeDtypeStruct((4, a, b), BF16), compiler_params=_cp("parallel", "parallel"))(x4, recv)


def _chip_sum(x4, name):
    _, a, b = x4.shape
    ta = _tile(a, 256, SUBLANES)

    def body(x_ref, o_ref):
        acc = x_ref[0] + x_ref[1]
        acc = acc + x_ref[2]
        o_ref[...] = acc + x_ref[3]

    return _call(
        body, name=name, grid=(a // ta,),
        in_specs=[pl.BlockSpec((4, ta, b), lambda i: (0, i, 0))],
        out_specs=pl.BlockSpec((ta, b), lambda i: (i, 0)),
        out_shape=jax.ShapeDtypeStruct((a, b), F32), compiler_params=_cp("parallel"))(x4)


def _peer_sum(own4, parts, name):
    _, a, b = parts.shape
    tc = _col_tile(a, b)

    def body(own_ref, p_ref, o_ref):
        me = 2 * lax.axis_index("x") + lax.axis_index("y")
        acc = own_ref[me].astype(F32) + p_ref[0].astype(F32)
        acc = acc + p_ref[1].astype(F32)
        o_ref[...] = acc + p_ref[2].astype(F32)

    return _call(
        body, name=name, grid=(b // tc,),
        in_specs=[pl.BlockSpec((4, a, tc), lambda i: (0, 0, i)), pl.BlockSpec((3, a, tc), lambda i: (0, 0, i))],
        out_specs=pl.BlockSpec((a, tc), lambda i: (0, i)),
        out_shape=jax.ShapeDtypeStruct((a, b), F32), compiler_params=_cp("parallel"))(own4, parts)


def _add2(x, y, name):
    rows, cols = x.shape
    tb = _tile(rows, 512, SUBLANES)

    def body(x_ref, y_ref, o_ref):
        o_ref[...] = x_ref[...] + y_ref[...]

    blk = pl.BlockSpec((tb, cols), lambda i: (i, 0))
    return _call(body, name=name, grid=(rows // tb,), in_specs=[blk, blk], out_specs=blk,
                 out_shape=jax.ShapeDtypeStruct((rows, cols), x.dtype), compiler_params=_cp("parallel"))(x, y)


_HBM = pl.BlockSpec(memory_space=pltpu.HBM)


def _place():
    x, y, c = lax.axis_index("x"), lax.axis_index("y"), lax.axis_index("c")
    chips = [(1 - x, y), (x, 1 - y), (1 - x, 1 - y)]
    return x, y, c, chips


def _comm_call(body, name, xs, out_shapes, n_sems):
    return _call(
        body, name=name, in_specs=[_HBM] * len(xs), out_specs=[_HBM] * len(out_shapes), out_shape=out_shapes,
        scratch_shapes=[pltpu.SemaphoreType.DMA((n_sems,)), pltpu.SemaphoreType.DMA((n_sems,)),
                        pltpu.SemaphoreType.DMA((len(xs),))],
    )(*xs)


def _all_gather_chips(xs, name):
    n = len(xs)

    def body(*refs):
        ins, outs = refs[:n], refs[n:2 * n]
        ssem, rsem, _ = refs[2 * n:]
        _gather_start(ins, outs, ssem, rsem)
        _gather_finish(ins, outs, ssem, rsem)

    outs = [jax.ShapeDtypeStruct((4,) + v.shape, v.dtype) for v in xs]
    return _comm_call(body, name, xs, outs, GATHER_SEMS * n)


GATHER_SEMS = 7
PEER_SEMS = 3


def _remote(src, dst, ssem, rsem, k, dev):
    return pltpu.make_async_remote_copy(src_ref=src, dst_ref=dst, send_sem=ssem.at[k], recv_sem=rsem.at[k],
                                        device_id=dev, device_id_type=MESH)


def _gather_start(ins, outs, ssem, rsem):
    x, y, c, chips = _place()
    me = 2 * x + y
    for i in range(len(ins)):
        for j, (px, py) in enumerate(chips):
            _remote(ins[i].at[c], outs[i].at[me, c], ssem, rsem, GATHER_SEMS * i + j, (px, py, c)).start()
        _remote(ins[i], outs[i].at[me], ssem, rsem, GATHER_SEMS * i + 6, (x, y, 1 - c)).start()


def _gather_finish(ins, outs, ssem, rsem):
    x, y, c, chips = _place()
    me = 2 * x + y
    sib = (x, y, 1 - c)
    n = len(ins)
    for i in range(n):
        for j, (px, py) in enumerate(chips):
            slot = outs[i].at[2 * px + py, c]
            _remote(slot, slot, ssem, rsem, GATHER_SEMS * i + j, (px, py, c)).wait_recv()
            _remote(slot, slot, ssem, rsem, GATHER_SEMS * i + 3 + j, sib).start()
    for i in range(n):
        own = outs[i].at[me]
        _remote(own, own, ssem, rsem, GATHER_SEMS * i + 6, sib).wait_recv()
        for j, (px, py) in enumerate(chips):
            slot = outs[i].at[2 * px + py, 1 - c]
            _remote(slot, slot, ssem, rsem, GATHER_SEMS * i + 3 + j, sib).wait_recv()
    for i in range(n):
        for j, (px, py) in enumerate(chips):
            slot = outs[i].at[2 * px + py, c]
            _remote(ins[i].at[c], outs[i].at[me, c], ssem, rsem, GATHER_SEMS * i + j, (px, py, c)).wait_send()
            _remote(slot, slot, ssem, rsem, GATHER_SEMS * i + 3 + j, sib).wait_send()
        _remote(ins[i], outs[i].at[me], ssem, rsem, GATHER_SEMS * i + 6, sib).wait_send()


def _peer_copies(ins, outs, ssem, rsem):
    x, y, c, chips = _place()
    return [_remote(ins[i].at[2 * px + py], outs[i].at[j], ssem, rsem, PEER_SEMS * i + j, (px, py, c))
            for i in range(len(ins)) for j, (px, py) in enumerate(chips)]


def _comm_plan(kind, xs):
    if kind == "gather":
        outs = [jax.ShapeDtypeStruct((4,) + v.shape, v.dtype) for v in xs]
        return outs, GATHER_SEMS * len(xs), _gather_start, _gather_finish
    if kind == "swap_half":
        def swaps(ins, outs, ssem, rsem):
            x, y, c, _ = _place()
            return [_remote(ins[i].at[:, 1 - c], outs[i], ssem, rsem, i, (x, y, 1 - c)) for i in range(len(ins))]

        def start_swaps(ins, outs, ssem, rsem):
            for cp in swaps(ins, outs, ssem, rsem):
                cp.start()

        def finish_swaps(ins, outs, ssem, rsem):
            for cp in swaps(ins, outs, ssem, rsem):
                cp.wait()

        outs = [jax.ShapeDtypeStruct((v.shape[0],) + v.shape[2:], v.dtype) for v in xs]
        return outs, len(xs), start_swaps, finish_swaps

    def start(ins, outs, ssem, rsem):
        for cp in _peer_copies(ins, outs, ssem, rsem):
            cp.start()

    def finish(ins, outs, ssem, rsem):
        for cp in _peer_copies(ins, outs, ssem, rsem):
            cp.wait()

    outs = [jax.ShapeDtypeStruct((3,) + v.shape[1:], v.dtype) for v in xs]
    return outs, PEER_SEMS * len(xs), start, finish


def _sibling_swap(xs, pick_half, name):
    n = len(xs)

    def body(*refs):
        ins, outs = refs[:n], refs[n:2 * n]
        ssem, rsem, _ = refs[2 * n:]
        x, y, c, _ = _place()
        cps = []
        for i in range(n):
            src = ins[i].at[:, 1 - c] if pick_half else ins[i]
            cp = pltpu.make_async_remote_copy(src_ref=src, dst_ref=outs[i], send_sem=ssem.at[i], recv_sem=rsem.at[i],
                                              device_id=(x, y, 1 - c), device_id_type=MESH)
            cp.start()
            cps.append(cp)
        for cp in cps:
            cp.wait()

    outs = [jax.ShapeDtypeStruct((v.shape[0],) + v.shape[2:] if pick_half else v.shape, v.dtype) for v in xs]
    return _comm_call(body, name, xs, outs, n)


def _to_scan_k(x, bsz, t_len):
    h = x.shape[1] // RWKV_HEAD
    y = jnp.broadcast_to(x.reshape(1, bsz, t_len, h, RWKV_HEAD), (2, bsz, t_len, h, RWKV_HEAD))
    return y.transpose(2, 4, 0, 1, 3).reshape(t_len, RWKV_HEAD, 2 * bsz * h)


def _to_scan_v(x, bsz, t_len):
    h = x.shape[1] // RWKV_HEAD
    y = x.reshape(bsz, t_len, h, 2, RWKV_HEAD // 2).transpose(1, 4, 3, 0, 2)
    return y.reshape(t_len, RWKV_HEAD // 2, 2 * bsz * h)


def _from_scan_k(x, bsz, t_len):
    h = x.shape[2] // (2 * bsz)
    y = x[:, :, :bsz * h].reshape(t_len, RWKV_HEAD, bsz, h).transpose(2, 0, 3, 1)
    return y.reshape(bsz * t_len, h * RWKV_HEAD)


def _from_scan_v(x, bsz, t_len):
    h = x.shape[2] // (2 * bsz)
    y = x.reshape(t_len, RWKV_HEAD // 2, 2, bsz, h).transpose(3, 0, 4, 2, 1)
    return y.reshape(bsz * t_len, h * RWKV_HEAD)


def _pad_rows(x, rows):
    return jnp.pad(x, ((0, rows - x.shape[0]), (0, 0)))


def _pad_cols(x, cols):
    return jnp.pad(x, ((0, 0), (0, cols - x.shape[1])))


def _cols_from_shards(g4):
    _, r, cs = g4.shape
    return g4.transpose(1, 0, 2).reshape(r, 4 * cs)


def _cols_to_shards(g):
    r, cols = g.shape
    return g.reshape(r, 4, cols // 4).transpose(1, 0, 2)


def kernel(x, norm_mix_g, w_in, mu_shift, rwkv_w0, rwkv_w2, rwkv_a0, rwkv_a2, rwkv_g2, rwkv_k_k, rwkv_k_a, rwkv_r_k, rwkv_ln_g, rwkv_ln_b, conv_w, conv_b, lru_wr, lru_br, lru_wi, lru_bi, lru_lambda, lru_norm_g, w_out, norm_ffn_g, ffn_w_gate, ffn_w_up, ffn_w_down, norm_final_g, loss_target, m_norm_mix_g, m_w_in, m_mu_shift, m_rwkv_w0, m_rwkv_w2, m_rwkv_a0, m_rwkv_a2, m_rwkv_g2, m_rwkv_k_k, m_rwkv_k_a, m_rwkv_r_k, m_rwkv_ln_g, m_rwkv_ln_b, m_conv_w, m_conv_b, m_lru_wr, m_lru_br, m_lru_wi, m_lru_bi, m_lru_lambda, m_lru_norm_g, m_w_out, m_norm_ffn_g, m_ffn_w_gate, m_ffn_w_up, m_ffn_w_down, m_norm_final_g, v_norm_mix_g, v_w_in, v_mu_shift, v_rwkv_w0, v_rwkv_w2, v_rwkv_a0, v_rwkv_a2, v_rwkv_g2, v_rwkv_k_k, v_rwkv_k_a, v_rwkv_r_k, v_rwkv_ln_g, v_rwkv_ln_b, v_conv_w, v_conv_b, v_lru_wr, v_lru_br, v_lru_wi, v_lru_bi, v_lru_lambda, v_lru_norm_g, v_w_out, v_norm_ffn_g, v_ffn_w_gate, v_ffn_w_up, v_ffn_w_down, v_norm_final_g):
    names = ['norm_mix_g', 'w_in', 'mu_shift', 'rwkv_w0', 'rwkv_w2', 'rwkv_a0', 'rwkv_a2', 'rwkv_g2', 'rwkv_k_k',
             'rwkv_k_a', 'rwkv_r_k', 'rwkv_ln_g', 'rwkv_ln_b', 'conv_w', 'conv_b', 'lru_wr', 'lru_br', 'lru_wi',
             'lru_bi', 'lru_lambda', 'lru_norm_g', 'w_out', 'norm_ffn_g', 'ffn_w_gate', 'ffn_w_up', 'ffn_w_down',
             'norm_final_g']
    env = locals()
    wts = {k: env[k] for k in names}
    mom_m = {k: env["m_" + k] for k in names}
    mom_v = {k: env["v_" + k] for k in names}

    bsz, t_len, d = x.shape
    n = bsz * t_len
    w = rwkv_w0.shape[1]
    lw = lru_br.shape[1]
    dl, al, gl = rwkv_w2.shape[1], rwkv_a2.shape[1], rwkv_g2.shape[1]
    dlp, alp, glp = _ceil_to(dl, LANES), _ceil_to(al, LANES), _ceil_to(gl, LANES)
    lp = dlp + alp + glp
    rc = 3 * w + dl + al + gl
    chip = 2 * lax.axis_index("x") + lax.axis_index("y")

    big = ['w_in', 'w_out', 'ffn_w_gate', 'ffn_w_up', 'ffn_w_down']
    small_sh = ['rwkv_w2', 'rwkv_a2', 'rwkv_g2', 'conv_w']

    def halves(a2d):
        return a2d.reshape(2, a2d.shape[0] // 2, a2d.shape[1])

    col_sharded = ('w_in', 'ffn_w_gate', 'ffn_w_up')

    def work(k, t):
        return jnp.swapaxes(t[0], 0, 1) if k in col_sharded else t[0]

    def unwork(k, t2):
        return (jnp.swapaxes(t2, 0, 1) if k in col_sharded else t2)[None]

    def rows_of(g):
        return g.reshape(g.shape[0] * g.shape[1] * g.shape[2], g.shape[3])

    send = [halves(work('w_in', w_in).astype(BF16))] + [halves(wts[k][0]) for k in small_sh]
    got = _all_gather_chips(send, "gather_w_in")
    later = ['w_out', 'ffn_w_gate', 'ffn_w_up', 'ffn_w_down']
    send_later = [halves(work(k, wts[k]).astype(BF16)) for k in later]
    full = {}
    for k, g in zip(small_sh, got[1:]):
        full[k] = _cols_from_shards(g.reshape(4, g.shape[1] * g.shape[2], g.shape[3]))
    wi_t = rows_of(got[0])
    w_rkv = wi_t[:3 * w]
    w_lru = wi_t[rc:]
    o = 3 * w
    w_lora = jnp.concatenate([_pad_rows(wi_t[o:o + dl], dlp), _pad_rows(wi_t[o + dl:o + dl + al], alp),
                              _pad_rows(wi_t[o + dl + al:rc], glp)], axis=0)
    mu = mu_shift
    prm_r = dict(
        mu_rkv=mu[:, :3 * w],
        mu_lora=jnp.concatenate([_pad_cols(mu[:, o:o + dl], dlp), _pad_cols(mu[:, o + dl:o + dl + al], alp),
                                 _pad_cols(mu[:, o + dl + al:rc], glp)], axis=1),
        w0=rwkv_w0, a0=rwkv_a0, k_k=rwkv_k_k, k_a=rwkv_k_a,
        w2=_pad_rows(full['rwkv_w2'], dlp).astype(BF16), a2=_pad_rows(full['rwkv_a2'], alp).astype(BF16),
        g2=_pad_rows(full['rwkv_g2'], glp).astype(BF16))
    ln_g, ln_b, r_k = rwkv_ln_g, rwkv_ln_b, rwkv_r_k.reshape(1, w)
    prm_l = dict(conv_w=full['conv_w'], conv_b=conv_b, wr=lru_wr[0].astype(BF16), br=lru_br,
                 wi=lru_wi[0].astype(BF16), bi=lru_bi, lam=lru_lambda, norm_g=lru_norm_g)
    g_final = norm_final_g.reshape(1, d)

    x2 = x.reshape(n, d)
    u1 = _rmsnorm_fwd(x2, norm_mix_g, "norm_mix")
    p_rkv = _mm(u1, w_rkv, name="in_rkv", tb=True)
    p_lru = _mm(u1, w_lru, name="in_lru", tb=True)
    p_lora = _mm(u1, w_lora, name="in_lora", tb=True)
    r_t, dec_t, k_t, v_t, na_t, nb_t, g_t = _rwkv_prep_fwd(p_rkv, p_lora, prm_r, t_len, "rwkv_prep")
    sk = [_to_scan_k(a, bsz, t_len) for a in (r_t, dec_t, k_t, na_t, nb_t)]
    sv = _to_scan_v(v_t, bsz, t_len)
    y_s, states, sa_s, got_wo, got_wg =_rwkv_scan_fwd(*sk, sv, name="rwkv_scan", comm=("gather", send_later[:2]))
    wo, wg = rows_of(got_wo), rows_of(got_wg)
    y_t = _from_scan_v(y_s, bsz, t_len)
    y_a = _rwkv_post_fwd(y_t, r_t, k_t, v_t, g_t, ln_g, ln_b, r_k, "rwkv_post")
    y_b, h_lru = _lru_fwd(p_lru, prm_l, t_len, "lru_fwd")
    h1 = _mm(y_a, wo[:w], name="out_a", res=x2)
    h1 = _mm(y_b, wo[w:], name="out_b", res=h1)
    u2 = _rmsnorm_fwd(h1, norm_ffn_g, "norm_ffn")
    ffc = (1024, 256, 4096)
    gate, got_wu = _mm(u2, wg, name="ffn_gate", tb=True, caps=ffc, comm=("gather", send_later[2:3]))
    wu = rows_of(got_wu)
    up, act, got_wd = _mm(u2, wu, name="ffn_up", tb=True, caps=(1024, 256, 4096), comm=("gather", send_later[3:4]),
                          epi=(_swiglu_tile, [gate], [F32, BF16]))
    wd = rows_of(got_wd)
    h2 = _mm(act, wd, name="ffn_down", res=h1, caps=(1024, 256, 11008), resident="a")

    dh2, dh2b, g_norm_final, loss_vec = _loss_head(h2, g_final, loss_target.reshape(n, d), "loss_head")
    loss = lax.psum(loss_vec[0, 0], ("x", "y", "c"))
    dgate, dup = _mm(dh2b, wd, name="d_act", tb=True, caps=(1024, 256, 4096),
                     epi=(_swiglu_bwd_tile, [gate, up], [BF16, BF16]))
    shards = lambda g: g.reshape(4, 2, g.shape[0] // 8, g.shape[1])
    dwc = dict(ta=True, out_dtype=BF16, n_outer=True, caps=(256, 2048, 4096), resident="b")
    gw_down = _mm(act, dh2b, name="dw_down", **dwc)
    gw_gate = _mm(dgate, u2, name="dw_gate", **dwc)
    gw_up = _mm(dup, u2, name="dw_up", **dwc)
    g4f = [shards(g) for g in (gw_gate, gw_up, gw_down)]
    du2, *sib_f = _mm(dgate, wg, name="du2_gate", caps=(512, 256, 11008), comm=("swap_half", g4f))
    du2 = _mm(dup, wu, name="du2_up", res=du2, caps=(1024, 256, 11008), resident="a")
    dh1, dh1b, g_norm_ffn = _rmsnorm_bwd(du2, h1, norm_ffn_g, dh2, "norm_ffn_bwd")
    dcat = _mm(dh1b, wo, name="d_cat", tb=True)
    gw_out = jnp.concatenate([_mm(y_a, dh1b, name="dw_out_a", ta=True, out_dtype=BF16),
                              _mm(y_b, dh1b, name="dw_out_b", ta=True, out_dtype=BF16)], axis=0)
    g4o = [shards(gw_out)]
    sib_o = _sibling_swap(g4o, True, "grad_sibling_out")
    pair_a = [_pair_sum(a4, s, "grad_pair_sum_%d" % i) for i, (a4, s) in enumerate(zip(g4o + g4f, list(sib_o) + list(sib_f)))]
    (dp_lru, g_conv_w, g_conv_b, g_wr, g_br, g_wi, g_bi, g_lam, g_lng) = _lru_bwd(
        p_lru, h_lru, dcat, prm_l, t_len, "lru_bwd")
    dy_t, dr_p, dk_p, dv_p, dg_t, g_ln_g, g_ln_b, g_r_k = _rwkv_post_bwd(
        y_t, r_t, k_t, v_t, g_t, ln_g, ln_b, r_k, dcat, "rwkv_post_bwd")
    dr_s, dw_s, dk_s, da_s, db_s, dv_s, *parts_a = _rwkv_scan_bwd(
        *sk, sv, _to_scan_v(dy_t, bsz, t_len), states, sa_s, name="rwkv_scan_bwd", comm=("peer", pair_a))
    grads = [_from_scan_k(dr_s, bsz, t_len), dr_p, _from_scan_k(dw_s, bsz, t_len), _from_scan_k(dk_s, bsz, t_len),
             dk_p, _from_scan_v(dv_s, bsz, t_len), dv_p, _from_scan_k(da_s, bsz, t_len),
             _from_scan_k(db_s, bsz, t_len), dg_t]
    (dq_r, dq_l, g_mu_r, g_mu_l, g_w0, g_a0, g_kk, g_ka, g_w2, g_a2, g_g2) = _rwkv_prep_bwd(
        p_rkv, p_lora, prm_r, grads, t_len, "rwkv_prep_bwd")
    dp_rkv = _shift_combine(dq_r, prm_r["mu_rkv"], t_len, "shift_bwd_rkv")
    dp_lora = _shift_combine(dq_l, prm_r["mu_lora"], t_len, "shift_bwd_lora")
    gi_rkv = _mm(dp_rkv, u1, name="dw_in_rkv", ta=True, out_dtype=BF16)
    gi_lru = _mm(dp_lru, u1, name="dw_in_lru", ta=True, out_dtype=BF16)
    gi_lora = _mm(dp_lora, u1, name="dw_in_lora", ta=True, out_dtype=BF16)
    gw_in = jnp.concatenate([gi_rkv, gi_lora[:dl], gi_lora[dlp:dlp + al], gi_lora[dlp + alp:dlp + alp + gl], gi_lru],
                            axis=0)
    g4b = [shards(gw_in)]
    sib_b = _sibling_swap(g4b, True, "grad_sibling_in")
    pair_b = [_pair_sum(g4b[0], sib_b[0], "grad_pair_sum_in")]
    du1, *parts_b = _mm(dp_rkv, w_rkv, name="du1_rkv", comm=("peer", pair_b))
    du1 = _mm(dp_lru, w_lru, name="du1_lru", res=du1)
    du1 = _mm(dp_lora, w_lora, name="du1_lora", res=du1)
    gx, _, g_norm_mix = _rmsnorm_bwd(du1, x2, norm_mix_g, dh1, "norm_mix_bwd")

    pair, parts = pair_b + pair_a, list(parts_b) + list(parts_a)
    mine = [_peer_sum(own4, p3, "grad_chip_sum_%d" % i) for i, (own4, p3) in enumerate(zip(pair, parts))]
    theirs = _sibling_swap(mine, False, "grad_share")

    g_mu = jnp.concatenate([g_mu_r, g_mu_l[:, :dl], g_mu_l[:, dlp:dlp + al], g_mu_l[:, dlp + alp:dlp + alp + gl]],
                           axis=1)
    small = dict(norm_mix_g=g_norm_mix, mu_shift=g_mu, rwkv_w0=g_w0, rwkv_w2=g_w2[:dl], rwkv_a0=g_a0,
                 rwkv_a2=g_a2[:al], rwkv_g2=g_g2[:gl], rwkv_k_k=g_kk, rwkv_k_a=g_ka, rwkv_r_k=g_r_k,
                 rwkv_ln_g=g_ln_g, rwkv_ln_b=g_ln_b, conv_w=g_conv_w, conv_b=g_conv_b, lru_wr=g_wr, lru_br=g_br,
                 lru_wi=g_wi, lru_bi=g_bi, lru_lambda=g_lam, lru_norm_g=g_lng, norm_ffn_g=g_norm_ffn,
                 norm_final_g=g_norm_final)
    small_names = list(small)
    sizes = [small[k].size for k in small_names]
    total = sum(sizes)
    padded = _ceil_to(total, 512 * LANES)

    def pack(arrs):
        flat = jnp.concatenate([a.reshape(-1) for a in arrs] + [jnp.zeros((padded - sum(a.size for a in arrs),), F32)])
        return flat.reshape(padded // LANES, LANES)

    packed = pack([small[k] for k in small_names])
    other = _sibling_swap([packed], False, "small_sibling")[0]
    chip_sum = _add2(packed, other, "small_pair_sum")
    grad_w, delta_w, new_m, new_v = {}, {}, {}, {}
    for i, (k, g_mine, g_theirs) in enumerate(zip(big, mine, theirs)):
        res = _adamw_halves(g_mine, g_theirs, work(k, wts[k]), work(k, mom_m[k]), work(k, mom_v[k]), "adamw_" + k,
                            comm=("gather", [halves(chip_sum)]) if i == 0 else None)
        if i == 0:
            all4 = res[4].reshape((4,) + chip_sum.shape)
        grad_w[k], delta_w[k], new_m[k], new_v[k] = (unwork(k, t.reshape(2 * t.shape[1], t.shape[2]))
                                                     for t in res[:4])
    red = _chip_sum(all4, "small_chip_sum").reshape(-1)
    small_g = {}
    off = 0
    for k, sz in zip(small_names, sizes):
        full_g = red[off:off + sz].reshape(small[k].shape)
        off += sz
        if k in small_sh:
            cs = full_g.shape[1] // 4
            full_g = lax.dynamic_slice_in_dim(full_g, chip * cs, cs, axis=1)
        small_g[k] = full_g.reshape(wts[k].shape)

    lsizes = [small_g[k].size for k in small_names]
    lpad = _ceil_to(sum(lsizes), 128 * LANES)

    def lpack(tree):
        arrs = [tree[k].reshape(-1) for k in small_names]
        flat = jnp.concatenate(arrs + [jnp.zeros((lpad - sum(lsizes),), F32)])
        return flat.reshape(lpad // LANES, LANES)

    dlt, mn, vn = _adamw(lpack(small_g), lpack(wts), lpack(mom_m), lpack(mom_v), "adamw_small")
    off = 0
    for k, sz in zip(small_names, lsizes):
        shp = wts[k].shape
        grad_w[k] = small_g[k]
        delta_w[k] = dlt.reshape(-1)[off:off + sz].reshape(shp)
        new_m[k] = mn.reshape(-1)[off:off + sz].reshape(shp)
        new_v[k] = vn.reshape(-1)[off:off + sz].reshape(shp)
        off += sz

    return (loss, gx.reshape(bsz, t_len, d), *[grad_w[k] for k in names], *[delta_w[k] for k in names],
            *[new_m[k] for k in names], *[new_v[k] for k in names])
```
